```python
import jax, jax.numpy as jnp
from jax import lax
import numpy as np

D_MODEL = 2048
BATCH = 1
SEQ = 8192
DEPTH = 1

PLE_DIM = 256
MLA_HEADS = 8
QK_NOPE = 128
QK_ROPE = 64
QK_HEAD = QK_NOPE + QK_ROPE
V_HEAD = 128
Q_LORA = 512
KV_LORA = 256
ROPE_THETA = 10000.0
ATTN_BLOCK = 128
HG_HEADS = 8
HG_DK = 128
HG_DV = 128
HG_CHUNK = 64
D_MLA = MLA_HEADS * V_HEAD
D_HG = HG_HEADS * HG_DV
D_MIX = D_MLA + D_HG
IN_SPLITS = (Q_LORA, KV_LORA, QK_ROPE, HG_HEADS * HG_DK, HG_HEADS * HG_DK, D_HG, D_HG)
D_IN = Q_LORA + KV_LORA + QK_ROPE + 2 * HG_HEADS * HG_DK + 2 * D_HG
N_GROUPS = 8
EXPERTS_PER_GROUP = 8
N_EXPERTS = N_GROUPS * EXPERTS_PER_GROUP
TOP_K = 2
D_EXPERT = 512
MOE_BLOCK = 128
EPS = 1e-6

kernel_name = 'hymba_mla_hgrn2_hmoe_ple'


def rms_norm(x, gain):
    xf = x.astype(jnp.float32)
    y = xf * lax.rsqrt(jnp.mean(xf * xf, axis=-1, keepdims=True) + EPS)
    return (y * gain.astype(jnp.float32)).astype(x.dtype)


def rotary_tables(positions):
    inv_freq = 1.0 / (ROPE_THETA ** (jnp.arange(0, QK_ROPE, 2, dtype=jnp.float32) / QK_ROPE))
    ang = positions.astype(jnp.float32)[..., None] * inv_freq
    return jnp.cos(ang), jnp.sin(ang)


def apply_rope(x, cos, sin):
    xf = x.astype(jnp.float32)
    x1, x2 = jnp.split(xf, 2, axis=-1)
    return jnp.concatenate([x1 * cos - x2 * sin, x2 * cos + x1 * sin], axis=-1).astype(x.dtype)


def mla_group(c_q, c_kv, k_pe, cos, sin, q_norm, w_uq, kv_norm, w_ukv, mla_norm):
    B, S, _ = c_q.shape
    q = (rms_norm(c_q, q_norm) @ w_uq).reshape(B, S, MLA_HEADS, QK_HEAD)
    q = jnp.concatenate([q[..., :QK_NOPE], apply_rope(q[..., QK_NOPE:], cos[:, :, None], sin[:, :, None])], axis=-1)
    kv = (rms_norm(c_kv, kv_norm) @ w_ukv).reshape(B, S, MLA_HEADS, QK_NOPE + V_HEAD)
    k_pe = apply_rope(k_pe, cos, sin)
    k = jnp.concatenate([kv[..., :QK_NOPE], jnp.broadcast_to(k_pe[:, :, None], (B, S, MLA_HEADS, QK_ROPE))], axis=-1)
    v = kv[..., QK_NOPE:]
    scale = QK_HEAD ** -0.5
    kpos = jnp.arange(S)
    neg = jnp.finfo(jnp.float32).min

    def block(start):
        qb = lax.dynamic_slice_in_dim(q, start, ATTN_BLOCK, axis=1)
        s = jnp.einsum('bqhd,bkhd->bhqk', qb, k).astype(jnp.float32) * scale
        qpos = start + jnp.arange(ATTN_BLOCK)
        s = jnp.where(kpos[None, :] <= qpos[:, None], s, neg)
        pr = jax.nn.softmax(s, axis=-1).astype(v.dtype)
        return jnp.einsum('bhqk,bkhv->bqhv', pr, v)

    starts = jnp.arange(S // ATTN_BLOCK) * ATTN_BLOCK
    o = lax.map(block, starts)
    o = jnp.moveaxis(o, 0, 1).reshape(B, S, D_MLA)
    return rms_norm(o, mla_norm)


def hgrn2_group(q_in, f_in, i_in, g_in, lb, hg_norm):
    B, S, _ = q_in.shape
    NC = S // HG_CHUNK
    lb = lb.astype(jnp.float32).reshape(HG_HEADS, 1, HG_DK)

    def heads(t, d):
        return t.astype(jnp.float32).reshape(B, NC, HG_CHUNK, HG_HEADS, d).transpose(1, 0, 3, 2, 4)

    q = jax.nn.silu(heads(q_in, HG_DK))
    f = lb + (1.0 - lb) * jax.nn.sigmoid(heads(f_in, HG_DK))
    k = 1.0 - f
    b = jnp.cumsum(jnp.log(f), axis=3)
    i = heads(i_in, HG_DV)
    causal = jnp.tril(jnp.ones((HG_CHUNK, HG_CHUNK), dtype=bool))

    def step(state, xs):
        qc, kc, ic, bc = xs
        diff = bc[:, :, :, None, :] - bc[:, :, None, :, :]
        decay = jnp.where(causal[:, :, None], jnp.exp(jnp.minimum(diff, 0.0)), 0.0)
        attn = jnp.einsum('bhtk,bhtsk,bhsk->bhts', qc, decay, kc)
        o = attn @ ic + jnp.einsum('bhtk,bhkv->bhtv', qc * jnp.exp(bc), state)
        b_last = bc[:, :, -1:, :]
        state = jnp.exp(b_last[:, :, 0, :])[..., None] * state + jnp.einsum('bhsk,bhsv->bhkv', kc * jnp.exp(b_last - bc), ic)
        return state, o

    s0 = jnp.zeros((B, HG_HEADS, HG_DK, HG_DV), jnp.float32)
    _, o = lax.scan(step, s0, (q, k, i, b))
    o = o.transpose(1, 0, 3, 2, 4).reshape(B, S, HG_HEADS, HG_DV)
    o = rms_norm(o, hg_norm)
    gate = jax.nn.silu(g_in.astype(jnp.float32)).reshape(B, S, HG_HEADS, HG_DV)
    return (o * gate).reshape(B, S, D_HG).astype(q_in.dtype)


def hier_route(xt, w_rg, b_rg, w_re, b_re):
    T = xt.shape[0]
    pg = jax.nn.softmax((xt @ w_rg).astype(jnp.float32) + b_rg.astype(jnp.float32), axis=-1)
    g_w, g_idx = lax.top_k(pg, 1)
    el = ((xt @ w_re).astype(jnp.float32) + b_re.astype(jnp.float32)).reshape(T, N_GROUPS, EXPERTS_PER_GROUP)
    el = jnp.take_along_axis(el, g_idx[:, :, None], axis=1)[:, 0]
    top_p, top_i = lax.top_k(jax.nn.softmax(el, axis=-1), TOP_K)
    w = g_w * top_p / jnp.sum(top_p, axis=-1, keepdims=True)
    eid = g_idx * EXPERTS_PER_GROUP + top_i
    return eid, w


def moe_dispatch(xt, eid, ew, w_gate, w_up, w_down):
    T, D = xt.shape
    A = T * TOP_K
    e_flat = eid.reshape(-1).astype(jnp.int32)
    w_flat = ew.reshape(-1)
    t_flat = jnp.repeat(jnp.arange(T, dtype=jnp.int32), TOP_K)
    order = jnp.argsort(e_flat, stable=True)
    e_s, t_s, w_s = e_flat[order], t_flat[order], w_flat[order]
    counts = jnp.bincount(e_flat, length=N_EXPERTS).astype(jnp.int32)
    starts = jnp.cumsum(counts) - counts
    padded = ((counts + MOE_BLOCK - 1) // MOE_BLOCK) * MOE_BLOCK
    pends = jnp.cumsum(padded)
    pstarts = pends - padded
    slot = pstarts[e_s] + (jnp.arange(A, dtype=jnp.int32) - starts[e_s])
    n_blocks = -(-A // MOE_BLOCK) + N_EXPERTS
    slot_tok = jnp.full((n_blocks * MOE_BLOCK,), T, jnp.int32).at[slot].set(t_s)
    slot_w = jnp.zeros((n_blocks * MOE_BLOCK,), w_flat.dtype).at[slot].set(w_s)
    block_e = jnp.clip(jnp.searchsorted(pends, jnp.arange(n_blocks, dtype=jnp.int32) * MOE_BLOCK, side='right'), 0, N_EXPERTS - 1)
    x_pad = jnp.concatenate([xt, jnp.zeros((1, D), xt.dtype)], axis=0)

    def run(args):
        tok, wgt, e = args
        xb = x_pad[tok]
        hmid = jax.nn.silu(xb @ w_gate[e]) * (xb @ w_up[e])
        return (hmid @ w_down[e]) * wgt[:, None].astype(xt.dtype)

    yb = lax.map(run, (slot_tok.reshape(n_blocks, MOE_BLOCK), slot_w.reshape(n_blocks, MOE_BLOCK), block_e))
    y = jnp.zeros((T + 1, D), xt.dtype).at[slot_tok].add(yb.reshape(-1, D))
    return y[:T]


def setup_inputs(seed: int = 0) -> dict:
    key = jax.random.key(seed)
    ks = jax.random.split(key, 32)
    f32 = jnp.float32

    def normal(k, shape, scale):
        return jax.random.normal(k, shape, f32) * scale

    def gain(k, shape):
        return 1.0 + 0.05 * jax.random.normal(k, shape, f32)

    x = normal(ks[0], (BATCH, SEQ, D_MODEL), 1.0)
    p = normal(ks[1], (DEPTH, BATCH, SEQ, PLE_DIM), 1.0)
    positions = jax.random.randint(ks[2], (BATCH, 1), 0, 4096, dtype=jnp.int32) + jnp.arange(SEQ, dtype=jnp.int32)[None, :]
    return {
        'x': x,
        'p': p,
        'positions': positions,
        'attn_norm': gain(ks[3], (DEPTH, D_MODEL)),
        'w_in': normal(ks[4], (DEPTH, D_MODEL, D_IN), D_MODEL ** -0.5),
        'q_norm': gain(ks[5], (DEPTH, Q_LORA)),
        'w_uq': normal(ks[6], (DEPTH, Q_LORA, MLA_HEADS * QK_HEAD), Q_LORA ** -0.5),
        'kv_norm': gain(ks[7], (DEPTH, KV_LORA)),
        'w_ukv': normal(ks[8], (DEPTH, KV_LORA, MLA_HEADS * (QK_NOPE + V_HEAD)), KV_LORA ** -0.5),
        'mla_norm': gain(ks[9], (DEPTH, D_MLA)),
        'hg_lb_logits': normal(ks[10], (DEPTH + 1, HG_HEADS * HG_DK), 0.5),
        'hg_norm': gain(ks[11], (DEPTH, HG_HEADS, HG_DV)),
        'w_out': normal(ks[12], (DEPTH, D_MIX, D_MODEL), D_MIX ** -0.5),
        'ffn_norm': gain(ks[13], (DEPTH, D_MODEL)),
        'w_router_group': normal(ks[14], (DEPTH, D_MODEL, N_GROUPS), D_MODEL ** -0.5),
        'b_router_group': normal(ks[15], (DEPTH, N_GROUPS), 0.01),
        'w_router_expert': normal(ks[16], (DEPTH, D_MODEL, N_EXPERTS), D_MODEL ** -0.5),
        'b_router_expert': normal(ks[17], (DEPTH, N_EXPERTS), 0.01),
        'w_exp_gate': normal(ks[18], (DEPTH, N_EXPERTS, D_MODEL, D_EXPERT), D_MODEL ** -0.5),
        'w_exp_up': normal(ks[19], (DEPTH, N_EXPERTS, D_MODEL, D_EXPERT), D_MODEL ** -0.5),
        'w_exp_down': normal(ks[20], (DEPTH, N_EXPERTS, D_EXPERT, D_MODEL), D_EXPERT ** -0.5),
        'ple_norm': gain(ks[21], (DEPTH, D_MODEL)),
        'w_ple_gate': normal(ks[22], (DEPTH, D_MODEL, D_MODEL), D_MODEL ** -0.5),
        'b_ple_gate': normal(ks[23], (DEPTH, D_MODEL), 0.01),
        'w_ple_proj': normal(ks[24], (DEPTH, PLE_DIM, D_MODEL), PLE_DIM ** -0.5),
        'final_norm': gain(ks[25], (D_MODEL,)),
    }


def reference(x, p, positions, attn_norm, w_in, q_norm, w_uq, kv_norm, w_ukv, mla_norm,
              hg_lb_logits, hg_norm, w_out, ffn_norm, w_router_group, b_router_group,
              w_router_expert, b_router_expert, w_exp_gate, w_exp_up, w_exp_down,
              ple_norm, w_ple_gate, b_ple_gate, w_ple_proj, final_norm):
    B, S, D = x.shape
    cos, sin = rotary_tables(positions)
    lb_table = jnp.cumsum(jax.nn.softmax(hg_lb_logits.astype(jnp.float32), axis=0), axis=0)
    split_pts = np.cumsum(IN_SPLITS)[:-1].tolist()
    h = x
    for li in range(DEPTH):
        hn = rms_norm(h, attn_norm[li])
        proj = hn @ w_in[li]
        c_q, c_kv, k_pe, hq, hf, hi, hg = jnp.split(proj, split_pts, axis=-1)
        y_mla = mla_group(c_q, c_kv, k_pe, cos, sin, q_norm[li], w_uq[li], kv_norm[li], w_ukv[li], mla_norm[li])
        y_hg = hgrn2_group(hq, hf, hi, hg, lb_table[li], hg_norm[li])
        h = h + jnp.concatenate([y_mla, y_hg], axis=-1) @ w_out[li]
        hn = rms_norm(h, ffn_norm[li]).reshape(B * S, D)
        eid, ew = hier_route(hn, w_router_group[li], b_router_group[li], w_router_expert[li], b_router_expert[li])
        h = h + moe_dispatch(hn, eid, ew, w_exp_gate[li], w_exp_up[li], w_exp_down[li]).reshape(B, S, D)
        hn = rms_norm(h, ple_norm[li])
        gate = jax.nn.sigmoid(hn @ w_ple_gate[li] + b_ple_gate[li])
        h = h + gate * (p[li] @ w_ple_proj[li])
    return rms_norm(h, final_norm)
```

```python
import functools

import jax
import jax.numpy as jnp
import numpy as np
from jax import lax
from jax.experimental import pallas as pl
from jax.experimental.pallas import tpu as pltpu

F32 = jnp.float32
BF16 = jnp.bfloat16

D_MODEL = 2048
PLE_DIM = 256
MLA_HEADS = 8
QK_NOPE = 128
QK_ROPE = 64
QK_HEAD = QK_NOPE + QK_ROPE
QK_PAD = 256
V_HEAD = 128
Q_LORA = 512
KV_LORA = 256
ROPE_THETA = 10000.0
HG_HEADS = 8
HG_DK = 128
HG_DV = 128
HG_CHUNK = 64
D_MLA = MLA_HEADS * V_HEAD
D_HG = HG_HEADS * HG_DV
N_GROUPS = 8
EXPERTS_PER_GROUP = 8
N_EXPERTS = N_GROUPS * EXPERTS_PER_GROUP
TOP_K = 2
D_EXPERT = 512
EPS = 1e-6
LANES = 128
D_LAT = Q_LORA + KV_LORA + 2 * QK_ROPE
NEG_BIG = -1e30

MOE_BLOCK = 128
HG_ROWS = 256
HG_LEVELS = (32, 16, 8, 4, 2, 1)


def _cparams(sem, vmem_mb=None):
    kw = dict(dimension_semantics=sem)
    if vmem_mb is not None:
        kw["vmem_limit_bytes"] = vmem_mb * 1024 * 1024
    return pltpu.CompilerParams(**kw)


def _rms(x, g):
    ms = jnp.mean(x * x, axis=-1, keepdims=True)
    return x * lax.rsqrt(ms + EPS) * g


def _sigmoid(x):
    return 1.0 / (1.0 + jnp.exp(-x))


def _in_proj_kernel(x_ref, g_ref, w_ref, o_ref, xn_ref):
    @pl.when(pl.program_id(1) == 0)
    def _():
        xn_ref[...] = _rms(x_ref[...], g_ref[...]).astype(BF16)

    o_ref[...] = jnp.dot(xn_ref[...], w_ref[...], preferred_element_type=F32).astype(o_ref.dtype)


def _in_proj(x, gain, w, out_dtype, tm, tn, name):
    s, d = x.shape
    n = w.shape[1]
    return pl.pallas_call(
        _in_proj_kernel,
        grid=(s // tm, n // tn),
        in_specs=[
            pl.BlockSpec((tm, d), lambda i, j: (i, 0)),
            pl.BlockSpec((1, d), lambda i, j: (0, 0)),
            pl.BlockSpec((d, tn), lambda i, j: (0, j)),
        ],
        out_specs=pl.BlockSpec((tm, tn), lambda i, j: (i, j)),
        out_shape=jax.ShapeDtypeStruct((s, n), out_dtype),
        scratch_shapes=[pltpu.VMEM((tm, d), BF16)],
        compiler_params=_cparams(("parallel", "arbitrary"), 48),
        name=name,
    )(x, gain, w)


def _mla_up_kernel(lat_ref, qn_ref, kvn_ref, wq_ref, wkn_ref, wv_ref, cc_ref, ss_ref,
                   q_ref, k_ref, v_ref):
    lat = lat_ref[...]
    cq = _rms(lat[:, :Q_LORA], qn_ref[...]).astype(BF16)
    ckv = _rms(lat[:, Q_LORA:Q_LORA + KV_LORA], kvn_ref[...]).astype(BF16)
    kp = lat[:, Q_LORA + KV_LORA:]
    cc = cc_ref[...]
    ss = ss_ref[...]
    scale = QK_HEAD ** -0.5

    q = jnp.dot(cq, wq_ref[...], preferred_element_type=F32)
    for h in range(MLA_HEADS):
        base = h * QK_PAD
        q_ref[:, base:base + QK_NOPE] = (q[:, base:base + QK_NOPE] * scale).astype(BF16)
        r = q[:, base + QK_NOPE:base + QK_PAD]
        r = (r * cc + pltpu.roll(r, QK_ROPE, 1) * ss) * scale
        q_ref[:, base + QK_NOPE:base + QK_PAD] = r.astype(BF16)

    kpe = (kp * cc + pltpu.roll(kp, QK_ROPE, 1) * ss).astype(BF16)
    kn = jnp.dot(ckv, wkn_ref[...], preferred_element_type=F32).astype(BF16)
    for h in range(MLA_HEADS):
        base = h * QK_PAD
        k_ref[:, base:base + QK_NOPE] = kn[:, h * QK_NOPE:(h + 1) * QK_NOPE]
        k_ref[:, base + QK_NOPE:base + QK_PAD] = kpe
    v_ref[...] = jnp.dot(ckv, wv_ref[...], preferred_element_type=F32).astype(BF16)


def _mla_up(lat, qn, kvn, wq, wkn, wv, cc, ss, tm):
    s = lat.shape[0]
    row = lambda i: (i, 0)
    full = lambda i: (0, 0)
    return pl.pallas_call(
        _mla_up_kernel,
        grid=(s // tm,),
        in_specs=[
            pl.BlockSpec((tm, D_LAT), row),
            pl.BlockSpec((1, Q_LORA), full),
            pl.BlockSpec((1, KV_LORA), full),
            pl.BlockSpec(wq.shape, full),
            pl.BlockSpec(wkn.shape, full),
            pl.BlockSpec(wv.shape, full),
            pl.BlockSpec((tm, LANES), row),
            pl.BlockSpec((tm, LANES), row),
        ],
        out_specs=[
            pl.BlockSpec((tm, MLA_HEADS * QK_PAD), row),
            pl.BlockSpec((tm, MLA_HEADS * QK_PAD), row),
            pl.BlockSpec((tm, D_MLA), row),
        ],
        out_shape=[
            jax.ShapeDtypeStruct((s, MLA_HEADS * QK_PAD), BF16),
            jax.ShapeDtypeStruct((s, MLA_HEADS * QK_PAD), BF16),
            jax.ShapeDtypeStruct((s, D_MLA), BF16),
        ],
        compiler_params=_cparams(("parallel",), 48),
        name="mla_up",
    )(lat, qn, kvn, wq, wkn, wv, cc, ss)


def _attn_kernel(q_ref, k_ref, v_ref, g_ref, o_ref, acc_ref, m_ref, l_ref, *, tq, tk):
    i = pl.program_id(0)
    j = pl.program_id(1)

    @pl.when(j == 0)
    def _():
        m_ref[...] = jnp.full(m_ref.shape, NEG_BIG, F32)
        l_ref[...] = jnp.zeros(l_ref.shape, F32)
        acc_ref[...] = jnp.zeros(acc_ref.shape, F32)

    @pl.when(j * tk <= i * tq + (tq - 1))
    def _():
        row = i * tq + lax.broadcasted_iota(jnp.int32, (tq, tk), 0)
        col = j * tk + lax.broadcasted_iota(jnp.int32, (tq, tk), 1)
        mask = col <= row
        for h in range(MLA_HEADS):
            q = q_ref[:, h * QK_PAD:(h + 1) * QK_PAD]
            k = k_ref[:, h * QK_PAD:(h + 1) * QK_PAD]
            s = lax.dot_general(q, k, (((1,), (1,)), ((), ())), preferred_element_type=F32)
            s = jnp.where(mask, s, NEG_BIG)
            m_prev = m_ref[h]
            m_new = jnp.maximum(m_prev, jnp.max(s, axis=-1, keepdims=True))
            alpha = jnp.exp(m_prev - m_new)
            p = jnp.exp(s - m_new[:, :1])
            l_ref[h] = alpha * l_ref[h] + jnp.sum(p, axis=-1, keepdims=True)
            m_ref[h] = m_new
            pv = jnp.dot(p.astype(BF16), v_ref[:, h * V_HEAD:(h + 1) * V_HEAD],
                         preferred_element_type=F32)
            acc_ref[:, h * V_HEAD:(h + 1) * V_HEAD] = alpha * acc_ref[:, h * V_HEAD:(h + 1) * V_HEAD] + pv

    @pl.when(j == pl.num_programs(1) - 1)
    def _():
        for h in range(MLA_HEADS):
            acc_ref[:, h * V_HEAD:(h + 1) * V_HEAD] = acc_ref[:, h * V_HEAD:(h + 1) * V_HEAD] / l_ref[h]
        o_ref[...] = _rms(acc_ref[...], g_ref[...]).astype(o_ref.dtype)


def _attention(q, k, v, gain, tq, tk):
    s = q.shape[0]
    last = lambda i, j: (jnp.minimum(j, (i * tq + tq - 1) // tk), 0)
    return pl.pallas_call(
        functools.partial(_attn_kernel, tq=tq, tk=tk),
        grid=(s // tq, s // tk),
        in_specs=[
            pl.BlockSpec((tq, MLA_HEADS * QK_PAD), lambda i, j: (i, 0)),
            pl.BlockSpec((tk, MLA_HEADS * QK_PAD), last),
            pl.BlockSpec((tk, D_MLA), last),
            pl.BlockSpec((1, D_MLA), lambda i, j: (0, 0)),
        ],
        out_specs=pl.BlockSpec((tq, D_MLA), lambda i, j: (i, 0)),
        out_shape=jax.ShapeDtypeStruct((s, D_MLA), BF16),
        scratch_shapes=[
            pltpu.VMEM((tq, D_MLA), F32),
            pltpu.VMEM((MLA_HEADS, tq, LANES), F32),
            pltpu.VMEM((MLA_HEADS, tq, LANES), F32),
        ],
        compiler_params=_cparams(("parallel", "arbitrary"), 48),
        name="mla_attention",
    )(q, k, v, gain)


def _hgrn_tables():
    n = HG_ROWS
    r = np.arange(n)
    c = np.arange(n)
    same = (r[:, None] // HG_CHUNK) == (c[None, :] // HG_CHUNK)

    def rows_upto(idx):
        return (same & (c[None, :] <= idx[:, None])).astype(np.float32)

    blocks = [rows_upto(r)]
    for m in HG_LEVELS:
        blocks.append(rows_upto((r // (2 * m)) * (2 * m) + m))
    blocks.append(rows_upto((r // HG_CHUNK) * HG_CHUNK + HG_CHUNK - 1))
    mall = np.concatenate(blocks, axis=0)

    x = r[:, None] ^ c[None, :]
    lv = np.full((n, n), -1, np.int32)
    for li, m in enumerate(HG_LEVELS):
        lv = np.where(same & (r[:, None] > c[None, :]) & (x >= m) & (x < 2 * m), li, lv)
    lv = np.where(r[:, None] == c[None, :], len(HG_LEVELS), lv)
    return jnp.asarray(mall, BF16), jnp.asarray(lv, jnp.int32)


def _hgrn_kernel(q_ref, f_ref, i_ref, g_ref, lb_ref, gn_ref, mall_ref, lv_ref, o_ref, st_ref):
    t = pl.program_id(1)
    n = HG_ROWS

    @pl.when(t == 0)
    def _():
        st_ref[...] = jnp.zeros(st_ref.shape, F32)

    q_in = q_ref[...].astype(F32)
    qs = q_in * _sigmoid(q_in)
    lb = lb_ref[...]
    f = lb + (1.0 - lb) * _sigmoid(f_ref[...].astype(F32))
    kk = 1.0 - f
    logf = jnp.log(f)
    iv = i_ref[...]

    l1 = logf.astype(BF16)
    r1 = logf - l1.astype(F32)
    l2 = r1.astype(BF16)
    l3 = (r1 - l2.astype(F32)).astype(BF16)
    parts = jnp.dot(mall_ref[...], jnp.concatenate([l1, l2, l3], axis=1), preferred_element_type=F32)
    bc = parts[:, :HG_DK] + parts[:, HG_DK:2 * HG_DK] + parts[:, 2 * HG_DK:]
    b = bc[:n]

    rowid = lax.broadcasted_iota(jnp.int32, (n, 1), 0)
    lv = lv_ref[...]
    a = jnp.zeros((n, n), F32)
    for li, m in enumerate(HG_LEVELS):
        c = bc[(li + 1) * n:(li + 2) * n]
        e = jnp.exp(-jnp.abs(b - c))
        upper = (rowid & (2 * m - 1)) >= m
        x = (jnp.where(upper, qs, kk) * e).astype(BF16)
        p = lax.dot_general(x, x, (((1,), (1,)), ((), ())), preferred_element_type=F32)
        a = jnp.where(lv == li, p, a)
    a = jnp.where(lv == len(HG_LEVELS), jnp.sum(qs * kk, axis=-1, keepdims=True), a)
    o = jnp.dot(a.astype(BF16), iv, preferred_element_type=F32)

    blast = bc[(len(HG_LEVELS) + 1) * n:]
    qd = (qs * jnp.exp(b)).astype(BF16)
    kd = kk * jnp.exp(blast - b)
    ivt = iv.astype(F32).T.astype(BF16)
    st = st_ref[...]
    outs = []
    for ci in range(n // HG_CHUNK):
        lo = ci * HG_CHUNK
        inter = lax.dot_general(qd[lo:lo + HG_CHUNK], st.astype(BF16), (((1,), (1,)), ((), ())),
                                preferred_element_type=F32)
        outs.append(o[lo:lo + HG_CHUNK] + inter)
        kd_c = jnp.where((rowid >= lo) & (rowid < lo + HG_CHUNK), kd, 0.0).astype(BF16)
        upd = jnp.dot(ivt, kd_c, preferred_element_type=F32)
        st = jnp.exp(blast[lo:lo + 1]) * st + upd
    st_ref[...] = st
    o = jnp.concatenate(outs, axis=0)

    g_in = g_ref[...].astype(F32)
    o_ref[...] = (_rms(o, gn_ref[...]) * (g_in * _sigmoid(g_in))).astype(o_ref.dtype)


def _hgrn(hg4, lb, gn, mall, lv):
    s = hg4.shape[0]
    h = HG_HEADS

    def col(c):
        return lambda hh, t: (t, c * h + hh)

    head = lambda hh, t: (0, hh)
    const = lambda hh, t: (0, 0)
    return pl.pallas_call(
        _hgrn_kernel,
        grid=(h, s // HG_ROWS),
        in_specs=[
            pl.BlockSpec((HG_ROWS, HG_DK), col(0)),
            pl.BlockSpec((HG_ROWS, HG_DK), col(1)),
            pl.BlockSpec((HG_ROWS, HG_DV), col(2)),
            pl.BlockSpec((HG_ROWS, HG_DV), col(3)),
            pl.BlockSpec((1, HG_DK), head),
            pl.BlockSpec((1, HG_DV), head),
            pl.BlockSpec(mall.shape, const),
            pl.BlockSpec(lv.shape, const),
        ],
        out_specs=pl.BlockSpec((HG_ROWS, HG_DV), lambda hh, t: (t, hh)),
        out_shape=jax.ShapeDtypeStruct((s, D_HG), BF16),
        scratch_shapes=[pltpu.VMEM((HG_DV, HG_DK), F32)],
        compiler_params=_cparams(("parallel", "arbitrary"), 32),
        name="hgrn2",
    )(hg4, hg4, hg4, hg4, lb, gn, mall, lv)


def _out_route_kernel(x_ref, om_ref, oh_ref, wa_ref, wb_ref, g_ref, wr_ref, br_ref,
                      h_ref, hn_ref, rt_ref):
    h1 = (x_ref[...]
          + jnp.dot(om_ref[...], wa_ref[...], preferred_element_type=F32)
          + jnp.dot(oh_ref[...], wb_ref[...], preferred_element_type=F32))
    h_ref[...] = h1
    hn = _rms(h1, g_ref[...])
    hn_ref[...] = hn
    logits = jnp.dot(hn, wr_ref[...], preferred_element_type=F32,
                     precision=lax.Precision.HIGHEST) + br_ref[...]

    lane = lax.broadcasted_iota(jnp.int32, logits.shape, 1)
    lanef = lane.astype(F32)
    ninf = -jnp.inf
    big = float(LANES)

    is_g = lane < N_GROUPS
    gl = jnp.where(is_g, logits, ninf)
    gmax = jnp.max(gl, axis=-1, keepdims=True)
    gsum = jnp.sum(jnp.where(is_g, jnp.exp(gl - gmax), 0.0), axis=-1, keepdims=True)
    g_w = 1.0 / gsum
    g_idx = jnp.min(jnp.where(gl == gmax, lanef, big), axis=-1, keepdims=True)

    e_lane = lane - N_GROUPS
    in_grp = (e_lane >= 0) & (e_lane < N_EXPERTS) & ((e_lane >> 3).astype(F32) == g_idx)
    el = jnp.where(in_grp, logits, ninf)
    emax = jnp.max(el, axis=-1, keepdims=True)
    esum = jnp.sum(jnp.where(in_grp, jnp.exp(el - emax), 0.0), axis=-1, keepdims=True)
    i1 = jnp.min(jnp.where(el == emax, lanef, big), axis=-1, keepdims=True)
    el2 = jnp.where(lanef == i1, ninf, el)
    emax2 = jnp.max(el2, axis=-1, keepdims=True)
    i2 = jnp.min(jnp.where(el2 == emax2, lanef, big), axis=-1, keepdims=True)
    p1 = 1.0 / esum
    p2 = jnp.exp(emax2 - emax) / esum
    w1 = g_w * p1 / (p1 + p2)
    w2 = g_w * p2 / (p1 + p2)

    rt = jnp.where(lane == 0, i1 - N_GROUPS,
                   jnp.where(lane == 1, i2 - N_GROUPS,
                             jnp.where(lane == 2, w1, jnp.where(lane == 3, w2, 0.0))))
    rt_ref[...] = rt


def _out_route(x, o_mla, o_hg, wa, wb, gain, wr, br, tm):
    s, d = x.shape
    row = lambda i: (i, 0)
    full = lambda i: (0, 0)
    return pl.pallas_call(
        _out_route_kernel,
        grid=(s // tm,),
        in_specs=[
            pl.BlockSpec((tm, d), row),
            pl.BlockSpec((tm, D_MLA), row),
            pl.BlockSpec((tm, D_HG), row),
            pl.BlockSpec(wa.shape, full),
            pl.BlockSpec(wb.shape, full),
            pl.BlockSpec((1, d), full),
            pl.BlockSpec(wr.shape, full),
            pl.BlockSpec((1, LANES), full),
        ],
        out_specs=[
            pl.BlockSpec((tm, d), row),
            pl.BlockSpec((tm, d), row),
            pl.BlockSpec((tm, LANES), row),
        ],
        out_shape=[
            jax.ShapeDtypeStruct((s, d), F32),
            jax.ShapeDtypeStruct((s, d), F32),
            jax.ShapeDtypeStruct((s, LANES), F32),
        ],
        compiler_params=_cparams(("parallel",), 48),
        name="out_proj_route",
    )(x, o_mla, o_hg, wa, wb, gain, wr, br)


def _moe_kernel(be_ref, tok_ref, nu_ref, hn_hbm, w_ref, wg_ref, wu_ref, wd_ref, o_ref,
                xbuf, sem, wgb, wub, wdb):
    i = pl.program_id(0)
    nb = pl.num_programs(0)
    nused = nu_ref[0]
    slot = i % 2

    def row_copy(blk, r, sl):
        tok = tok_ref[blk * MOE_BLOCK + r]
        return pltpu.make_async_copy(hn_hbm.at[pl.ds(tok, 1), :], xbuf.at[sl, pl.ds(r, 1), :], sem.at[sl])

    def start_gather(blk, sl):
        def body(r, c):
            row_copy(blk, r, sl).start()
            return c
        lax.fori_loop(0, MOE_BLOCK, body, 0)

    def wait_gather(blk, sl):
        def body(r, c):
            row_copy(blk, r, sl).wait()
            return c
        lax.fori_loop(0, MOE_BLOCK, body, 0)

    @pl.when((i == 0) & (nused > 0))
    def _():
        start_gather(0, 0)

    @pl.when((i + 1 < nb) & (i + 1 < nused))
    def _():
        start_gather(i + 1, 1 - slot)

    @pl.when(i < nused)
    def _():
        prev = be_ref[jnp.maximum(i - 1, 0)]

        @pl.when((i == 0) | (be_ref[i] != prev))
        def _():
            wgb[...] = wg_ref[...].astype(BF16)
            wub[...] = wu_ref[...].astype(BF16)
            wdb[...] = wd_ref[...].astype(BF16)

        wait_gather(i, slot)
        x = xbuf[slot].astype(BF16)
        g = jnp.dot(x, wgb[...], preferred_element_type=F32)
        u = jnp.dot(x, wub[...], preferred_element_type=F32)
        hmid = (g * _sigmoid(g) * u).astype(BF16)
        o_ref[...] = jnp.dot(hmid, wdb[...], preferred_element_type=F32) * w_ref[...]

    @pl.when(i >= nused)
    def _():
        o_ref[...] = jnp.zeros(o_ref.shape, F32)


def _moe(block_e, slot_tok, nused, hn, slot_w, wg, wu, wd):
    nb = block_e.shape[0]
    d = hn.shape[1]
    grid_spec = pltpu.PrefetchScalarGridSpec(
        num_scalar_prefetch=3,
        grid=(nb,),
        in_specs=[
            pl.BlockSpec(memory_space=pl.ANY),
            pl.BlockSpec((MOE_BLOCK, 1), lambda i, be, tok, nu: (i, 0)),
            pl.BlockSpec((None, d, D_EXPERT), lambda i, be, tok, nu: (be[i], 0, 0)),
            pl.BlockSpec((None, d, D_EXPERT), lambda i, be, tok, nu: (be[i], 0, 0)),
            pl.BlockSpec((None, D_EXPERT, d), lambda i, be, tok, nu: (be[i], 0, 0)),
        ],
        out_specs=pl.BlockSpec((MOE_BLOCK, d), lambda i, be, tok, nu: (i, 0)),
        scratch_shapes=[
            pltpu.VMEM((2, MOE_BLOCK, d), F32),
            pltpu.SemaphoreType.DMA((2,)),
            pltpu.VMEM((d, D_EXPERT), BF16),
            pltpu.VMEM((d, D_EXPERT), BF16),
            pltpu.VMEM((D_EXPERT, d), BF16),
        ],
    )
    return pl.pallas_call(
        _moe_kernel,
        grid_spec=grid_spec,
        out_shape=jax.ShapeDtypeStruct((nb * MOE_BLOCK, d), F32),
        compiler_params=_cparams(("arbitrary",), 56),
        name="moe_experts",
    )(block_e, slot_tok, nused, hn, slot_w, wg, wu, wd)


def _ple_kernel(pos_ref, h_ref, y_hbm, p_ref, wg_ref, bg_ref, wp_ref, gp_ref, gf_ref, o_ref,
                ybuf, sem, *, tm):
    i = pl.program_id(0)

    def row_copy(r, kk):
        src = pos_ref[(i * tm + r) * TOP_K + kk]
        return pltpu.make_async_copy(y_hbm.at[pl.ds(src, 1), :], ybuf.at[kk, pl.ds(r, 1), :], sem.at[0])

    def start_body(r, c):
        row_copy(r, 0).start()
        row_copy(r, 1).start()
        return c

    def wait_body(r, c):
        row_copy(r, 0).wait()
        row_copy(r, 1).wait()
        return c

    lax.fori_loop(0, tm, start_body, 0)
    pe = jnp.dot(p_ref[...].astype(BF16), wp_ref[...], preferred_element_type=F32)
    lax.fori_loop(0, tm, wait_body, 0)

    h2 = h_ref[...] + (ybuf[0] + ybuf[1])
    hn = _rms(h2, gp_ref[...]).astype(BF16)
    gate = _sigmoid(jnp.dot(hn, wg_ref[...], preferred_element_type=F32) + bg_ref[...])
    h3 = h2 + gate * pe
    o_ref[...] = _rms(h3, gf_ref[...])


def _ple_final(pos, h1, y_slots, p, wg, bg, wp, gp, gf, tm):
    s, d = h1.shape
    row = lambda i, pos: (i, 0)
    full = lambda i, pos: (0, 0)
    grid_spec = pltpu.PrefetchScalarGridSpec(
        num_scalar_prefetch=1,
        grid=(s // tm,),
        in_specs=[
            pl.BlockSpec((tm, d), row),
            pl.BlockSpec(memory_space=pl.ANY),
            pl.BlockSpec((tm, PLE_DIM), row),
            pl.BlockSpec(wg.shape, full),
            pl.BlockSpec((1, d), full),
            pl.BlockSpec(wp.shape, full),
            pl.BlockSpec((1, d), full),
            pl.BlockSpec((1, d), full),
        ],
        out_specs=pl.BlockSpec((tm, d), row),
        scratch_shapes=[
            pltpu.VMEM((TOP_K, tm, d), F32),
            pltpu.SemaphoreType.DMA((1,)),
        ],
    )
    return pl.pallas_call(
        functools.partial(_ple_kernel, tm=tm),
        grid_spec=grid_spec,
        out_shape=jax.ShapeDtypeStruct((s, d), F32),
        compiler_params=_cparams(("arbitrary",), 48),
        name="ple_final",
    )(pos, h1, y_slots, p, wg, bg, wp, gp, gf)


def _dispatch(eid, ew, n_tok):
    a = n_tok * TOP_K
    nb = -(-a // MOE_BLOCK) + N_EXPERTS
    e_flat = eid.reshape(-1)
    w_flat = ew.reshape(-1)
    t_flat = jnp.repeat(jnp.arange(n_tok, dtype=jnp.int32), TOP_K)
    onehot = (e_flat[:, None] == jnp.arange(N_EXPERTS, dtype=jnp.int32)[None, :]).astype(jnp.int32)
    csum = jnp.cumsum(onehot, axis=0)
    rank = jnp.take_along_axis(csum, e_flat[:, None], axis=1)[:, 0] - 1
    counts = csum[-1]
    padded = ((counts + MOE_BLOCK - 1) // MOE_BLOCK) * MOE_BLOCK
    pends = jnp.cumsum(padded)
    pstarts = pends - padded
    slot = (pstarts[e_flat] + rank).astype(jnp.int32)
    slot_tok = jnp.zeros((nb * MOE_BLOCK,), jnp.int32).at[slot].set(t_flat)
    slot_w = jnp.zeros((nb * MOE_BLOCK,), F32).at[slot].set(w_flat)
    block_e = jnp.clip(
        jnp.searchsorted(pends, jnp.arange(nb, dtype=jnp.int32) * MOE_BLOCK, side="right"),
        0, N_EXPERTS - 1).astype(jnp.int32)
    nused = (pends[-1] // MOE_BLOCK).astype(jnp.int32).reshape(1)
    return block_e, slot_tok, nused, slot_w.reshape(-1, 1), slot


def kernel(x, p, positions, attn_norm, w_in, q_norm, w_uq, kv_norm, w_ukv, mla_norm, hg_lb_logits, hg_norm, w_out, ffn_norm, w_router_group, b_router_group, w_router_expert, b_router_expert, w_exp_gate, w_exp_up, w_exp_down, ple_norm, w_ple_gate, b_ple_gate, w_ple_proj, final_norm):
    bsz, s, d = x.shape
    assert bsz == 1 and w_in.shape[0] == 1
    xt = x[0]

    inv_freq = 1.0 / (ROPE_THETA ** (jnp.arange(0, QK_ROPE, 2, dtype=F32) / QK_ROPE))
    ang = positions[0].astype(F32)[:, None] * inv_freq
    cos, sin = jnp.cos(ang), jnp.sin(ang)
    zpad = jnp.zeros((s, LANES - QK_ROPE), F32)
    cc = jnp.concatenate([cos, cos, zpad], axis=1)
    ss = jnp.concatenate([-sin, sin, zpad], axis=1)

    lb = jnp.cumsum(jax.nn.softmax(hg_lb_logits.astype(F32), axis=0), axis=0)[0][None, :]

    wi = w_in[0]
    kr0 = Q_LORA + KV_LORA
    half = QK_ROPE // 2
    w_lat = jnp.concatenate(
        [wi[:, :kr0 + QK_ROPE], wi[:, kr0 + half:kr0 + QK_ROPE], wi[:, kr0:kr0 + half]], axis=1).astype(BF16)
    w_hg = wi[:, kr0 + QK_ROPE:].astype(BF16)
    wq3 = w_uq[0].reshape(Q_LORA, MLA_HEADS, QK_HEAD)
    wq_pad = jnp.concatenate(
        [wq3, wq3[:, :, QK_NOPE + half:], wq3[:, :, QK_NOPE:QK_NOPE + half]], axis=2
    ).reshape(Q_LORA, MLA_HEADS * QK_PAD).astype(BF16)
    wkv3 = w_ukv[0].reshape(KV_LORA, MLA_HEADS, QK_NOPE + V_HEAD)
    wkn = wkv3[:, :, :QK_NOPE].reshape(KV_LORA, MLA_HEADS * QK_NOPE).astype(BF16)
    wv = wkv3[:, :, QK_NOPE:].reshape(KV_LORA, D_MLA).astype(BF16)
    wo = w_out[0].astype(BF16)
    wr = jnp.concatenate(
        [w_router_group[0], w_router_expert[0], jnp.zeros((d, LANES - N_GROUPS - N_EXPERTS), F32)], axis=1)
    br = jnp.concatenate(
        [b_router_group[0], b_router_expert[0], jnp.zeros((LANES - N_GROUPS - N_EXPERTS,), F32)])[None, :]

    lat = _in_proj(xt, attn_norm, w_lat, F32, 512, D_LAT, "in_proj_latent")
    hg4 = _in_proj(xt, attn_norm, w_hg, BF16, 1024, 1024, "in_proj_hgrn")
    q, k, v = _mla_up(lat, q_norm, kv_norm, wq_pad, wkn, wv, cc, ss, 512)
    o_mla = _attention(q, k, v, mla_norm, 512, 512)
    mall, lv = _hgrn_tables()
    o_hg = _hgrn(hg4, lb, hg_norm[0].reshape(1, D_HG), mall, lv)

    h1, hn, rt = _out_route(xt, o_mla, o_hg, wo[:D_MLA], wo[D_MLA:], ffn_norm, wr, br, 256)
    eid = rt[:, :TOP_K].astype(jnp.int32)
    ew = rt[:, TOP_K:2 * TOP_K]
    block_e, slot_tok, nused, slot_w, pos = _dispatch(eid, ew, s)
    y_slots = _moe(block_e, slot_tok, nused, hn, slot_w, w_exp_gate[0], w_exp_up[0], w_exp_down[0])

    out = _ple_final(pos, h1, y_slots, p[0, 0], w_ple_gate[0].astype(BF16), b_ple_gate,
                     w_ple_proj[0].astype(BF16), ple_norm, final_norm[None, :], 256)
    return out[None]
```

```python
import functools

import jax
import jax.numpy as jnp
import numpy as np
from jax import lax
from jax.experimental import pallas as pl
from jax.experimental.pallas import tpu as pltpu

F32 = jnp.float32
BF16 = jnp.bfloat16

D_MODEL = 2048
PLE_DIM = 256
MLA_HEADS = 8
QK_NOPE = 128
QK_ROPE = 64
QK_HEAD = QK_NOPE + QK_ROPE
QK_PAD = 256
V_HEAD = 128
VT_ROWS = V_HEAD + 16
QK_AHEAD = 2
PV_LAG = 1
Q_LORA = 512
KV_LORA = 256
ROPE_THETA = 10000.0
HG_HEADS = 8
HG_DK = 128
HG_DV = 128
HG_CHUNK = 64
D_MLA = MLA_HEADS * V_HEAD
D_HG = HG_HEADS * HG_DV
N_GROUPS = 8
EXPERTS_PER_GROUP = 8
N_EXPERTS = N_GROUPS * EXPERTS_PER_GROUP
TOP_K = 2
D_EXPERT = 512
EPS = 1e-6
LANES = 128
D_LAT = Q_LORA + KV_LORA + 2 * QK_ROPE
NEG_BIG = -1e30
LOG2E = 1.4426950408889634

MOE_BLOCK = 128
HG_ROWS = 256
HG_LEVELS = (32, 16, 8, 4, 2, 1)


def _cparams(sem, vmem_mb=None):
    kw = dict(dimension_semantics=sem)
    if vmem_mb is not None:
        kw["vmem_limit_bytes"] = vmem_mb * 1024 * 1024
    return pltpu.CompilerParams(**kw)


def _rms(x, g):
    ms = jnp.mean(x * x, axis=-1, keepdims=True)
    return x * lax.rsqrt(ms + EPS) * g


def _sigmoid(x):
    return 1.0 / (1.0 + jnp.exp(-x))


def _in_proj_kernel(x_ref, g_ref, w_ref, o_ref, xn_ref):
    @pl.when(pl.program_id(1) == 0)
    def _():
        xn_ref[...] = _rms(x_ref[...], g_ref[...]).astype(BF16)

    o_ref[...] = jnp.dot(xn_ref[...], w_ref[...], preferred_element_type=F32).astype(o_ref.dtype)


def _in_proj(x, gain, w, out_dtype, tm, tn, name):
    s, d = x.shape
    n = w.shape[1]
    return pl.pallas_call(
        _in_proj_kernel,
        grid=(s // tm, n // tn),
        in_specs=[
            pl.BlockSpec((tm, d), lambda i, j: (i, 0)),
            pl.BlockSpec((1, d), lambda i, j: (0, 0)),
            pl.BlockSpec((d, tn), lambda i, j: (0, j)),
        ],
        out_specs=pl.BlockSpec((tm, tn), lambda i, j: (i, j)),
        out_shape=jax.ShapeDtypeStruct((s, n), out_dtype),
        scratch_shapes=[pltpu.VMEM((tm, d), BF16)],
        compiler_params=_cparams(("parallel", "arbitrary"), 48),
        name=name,
    )(x, gain, w)


def _mla_up_kernel(lat_ref, qn_ref, kvn_ref, wq_ref, wkn_ref, wvt_ref, cc_ref, ss_ref,
                   q_ref, k_ref, vt_ref):
    lat = lat_ref[...]
    cq = _rms(lat[:, :Q_LORA], qn_ref[...]).astype(BF16)
    ckv = _rms(lat[:, Q_LORA:Q_LORA + KV_LORA], kvn_ref[...]).astype(BF16)
    kp = lat[:, Q_LORA + KV_LORA:]
    cc = cc_ref[...]
    ss = ss_ref[...]
    scale = QK_HEAD ** -0.5 * LOG2E

    q = jnp.dot(cq, wq_ref[...], preferred_element_type=F32)
    for h in range(MLA_HEADS):
        base = h * QK_PAD
        q_ref[:, base:base + QK_NOPE] = (q[:, base:base + QK_NOPE] * scale).astype(BF16)
        r = q[:, base + QK_NOPE:base + QK_PAD]
        r = (r * cc + pltpu.roll(r, QK_ROPE, 1) * ss) * scale
        q_ref[:, base + QK_NOPE:base + QK_PAD] = r.astype(BF16)

    kpe = (kp * cc + pltpu.roll(kp, QK_ROPE, 1) * ss).astype(BF16)
    kn = jnp.dot(ckv, wkn_ref[...], preferred_element_type=F32).astype(BF16)
    for h in range(MLA_HEADS):
        base = h * QK_PAD
        k_ref[:, base:base + QK_NOPE] = kn[:, h * QK_NOPE:(h + 1) * QK_NOPE]
        k_ref[:, base + QK_NOPE:base + QK_PAD] = kpe
    vt = lax.dot_general(wvt_ref[...], ckv, (((1,), (1,)), ((), ())), preferred_element_type=F32).astype(BF16)
    ones = jnp.ones((VT_ROWS - V_HEAD, vt.shape[1]), BF16)
    for h in range(MLA_HEADS):
        vt_ref[h * VT_ROWS:h * VT_ROWS + V_HEAD, :] = vt[h * V_HEAD:(h + 1) * V_HEAD]
        vt_ref[h * VT_ROWS + V_HEAD:(h + 1) * VT_ROWS, :] = ones


def _mla_up(lat, qn, kvn, wq, wkn, wvt, cc, ss, tm):
    s = lat.shape[0]
    row = lambda i: (i, 0)
    full = lambda i: (0, 0)
    return pl.pallas_call(
        _mla_up_kernel,
        grid=(s // tm,),
        in_specs=[
            pl.BlockSpec((tm, D_LAT), row),
            pl.BlockSpec((1, Q_LORA), full),
            pl.BlockSpec((1, KV_LORA), full),
            pl.BlockSpec(wq.shape, full),
            pl.BlockSpec(wkn.shape, full),
            pl.BlockSpec(wvt.shape, full),
            pl.BlockSpec((tm, LANES), row),
            pl.BlockSpec((tm, LANES), row),
        ],
        out_specs=[
            pl.BlockSpec((tm, MLA_HEADS * QK_PAD), row),
            pl.BlockSpec((tm, MLA_HEADS * QK_PAD), row),
            pl.BlockSpec((MLA_HEADS * VT_ROWS, tm), lambda i: (0, i)),
        ],
        out_shape=[
            jax.ShapeDtypeStruct((s, MLA_HEADS * QK_PAD), BF16),
            jax.ShapeDtypeStruct((s, MLA_HEADS * QK_PAD), BF16),
            jax.ShapeDtypeStruct((MLA_HEADS * VT_ROWS, s), BF16),
        ],
        compiler_params=_cparams(("parallel",), 48),
        name="mla_up",
    )(lat, qn, kvn, wq, wkn, wvt, cc, ss)


def _attn_kernel(it_ref, jt_ref, q_ref, k_ref, vt_ref, g_ref, o_ref, acc_ref, m_ref, *, tq, tk, qc):
    t = pl.program_id(0)
    i = it_ref[t]
    j = jt_ref[t]

    @pl.when(j == 0)
    def _():
        m_ref[...] = jnp.full(m_ref.shape, NEG_BIG, F32)
        acc_ref[...] = jnp.zeros(acc_ref.shape, F32)

    units = [(h, c) for h in range(MLA_HEADS) for c in range(tq // qc)]

    def scores(u):
        h, c = u
        q = q_ref[c * qc:(c + 1) * qc, h * QK_PAD:(h + 1) * QK_PAD]
        k = k_ref[:, h * QK_PAD:(h + 1) * QK_PAD]
        return lax.dot_general(k, q, (((1,), (1,)), ((), ())), preferred_element_type=F32)

    def step(masked):
        if masked:
            diff = lax.broadcasted_iota(jnp.int32, (tk, qc), 0) - lax.broadcasted_iota(jnp.int32, (tk, qc), 1)
        def accumulate(h, cols, alpha, p):
            pv = jnp.dot(vt_ref[h * VT_ROWS:(h + 1) * VT_ROWS, :], p, preferred_element_type=F32)
            acc_ref[h, :, cols] = alpha * acc_ref[h, :, cols] + pv

        ahead = [scores(units[n]) for n in range(min(QK_AHEAD, len(units)))]
        pending = []
        for n, (h, c) in enumerate(units):
            s = ahead.pop(0)
            if n + QK_AHEAD < len(units):
                ahead.append(scores(units[n + QK_AHEAD]))
            cols = slice(c * qc, (c + 1) * qc)
            if masked:
                s = jnp.where(diff <= c * qc, s, NEG_BIG)
            m_prev = m_ref[h, :, cols]
            m_new = jnp.maximum(m_prev, jnp.max(s, axis=0, keepdims=True))
            alpha = jnp.exp2(m_prev - m_new)
            p = jnp.exp2(s - m_new).astype(BF16)
            m_ref[h, :, cols] = m_new
            pending.append((h, cols, alpha, p))
            if len(pending) > PV_LAG:
                accumulate(*pending.pop(0))
        for item in pending:
            accumulate(*item)

    @pl.when(j < i)
    def _():
        step(False)

    @pl.when(j == i)
    def _():
        step(True)
        outs = [acc_ref[h, :V_HEAD, :] * (1.0 / acc_ref[h, V_HEAD:V_HEAD + 1, :]) for h in range(MLA_HEADS)]
        o_ref[...] = _rms(jnp.concatenate(outs, axis=0).T, g_ref[...]).astype(o_ref.dtype)


def _attention(q, k, vt, gain, tb, qc):
    s = q.shape[0]
    nq = s // tb
    it = np.concatenate([np.full(i + 1, i, np.int32) for i in range(nq)])
    jt = np.concatenate([np.arange(i + 1, dtype=np.int32) for i in range(nq)])
    grid_spec = pltpu.PrefetchScalarGridSpec(
        num_scalar_prefetch=2,
        grid=(it.shape[0],),
        in_specs=[
            pl.BlockSpec((tb, MLA_HEADS * QK_PAD), lambda t, it, jt: (it[t], 0)),
            pl.BlockSpec((tb, MLA_HEADS * QK_PAD), lambda t, it, jt: (jt[t], 0)),
            pl.BlockSpec((MLA_HEADS * VT_ROWS, tb), lambda t, it, jt: (0, jt[t])),
            pl.BlockSpec((1, D_MLA), lambda t, it, jt: (0, 0)),
        ],
        out_specs=pl.BlockSpec((tb, D_MLA), lambda t, it, jt: (it[t], 0)),
        scratch_shapes=[
            pltpu.VMEM((MLA_HEADS, VT_ROWS, tb), F32),
            pltpu.VMEM((MLA_HEADS, 1, tb), F32),
        ],
    )
    return pl.pallas_call(
        functools.partial(_attn_kernel, tq=tb, tk=tb, qc=qc),
        grid_spec=grid_spec,
        out_shape=jax.ShapeDtypeStruct((s, D_MLA), BF16),
        compiler_params=_cparams(("arbitrary",), 48),
        name="mla_attention",
    )(jnp.asarray(it), jnp.asarray(jt), q, k, vt, gain)


def _hgrn_tables():
    n = HG_ROWS
    r = np.arange(n)
    c = np.arange(n)
    same = (r[:, None] // HG_CHUNK) == (c[None, :] // HG_CHUNK)

    def rows_upto(idx):
        return (same & (c[None, :] <= idx[:, None])).astype(np.float32)

    blocks = [rows_upto(r)]
    for m in HG_LEVELS:
        blocks.append(rows_upto((r // (2 * m)) * (2 * m) + m))
    blocks.append(rows_upto((r // HG_CHUNK) * HG_CHUNK + HG_CHUNK - 1))
    mall = np.concatenate(blocks, axis=0)

    x = r[:, None] ^ c[None, :]
    lv = np.full((n, n), -1, np.int32)
    for li, m in enumerate(HG_LEVELS):
        lv = np.where(same & (r[:, None] > c[None, :]) & (x >= m) & (x < 2 * m), li, lv)
    lv = np.where(r[:, None] == c[None, :], len(HG_LEVELS), lv)
    return jnp.asarray(mall, BF16), jnp.asarray(lv, jnp.int32)


def _hgrn_kernel(q_ref, f_ref, i_ref, g_ref, lb_ref, gn_ref, mall_ref, lv_ref, o_ref, st_ref):
    t = pl.program_id(1)
    n = HG_ROWS

    @pl.when(t == 0)
    def _():
        st_ref[...] = jnp.zeros(st_ref.shape, F32)

    q_in = q_ref[...].astype(F32)
    qs = q_in * _sigmoid(q_in)
    lb = lb_ref[...]
    f = lb + (1.0 - lb) * _sigmoid(f_ref[...].astype(F32))
    kk = 1.0 - f
    logf = jnp.log(f)
    iv = i_ref[...]

    l1 = logf.astype(BF16)
    r1 = logf - l1.astype(F32)
    l2 = r1.astype(BF16)
    l3 = (r1 - l2.astype(F32)).astype(BF16)
    parts = jnp.dot(mall_ref[...], jnp.concatenate([l1, l2, l3], axis=1), preferred_element_type=F32)
    bc = parts[:, :HG_DK] + parts[:, HG_DK:2 * HG_DK] + parts[:, 2 * HG_DK:]
    b = bc[:n]

    rowid = lax.broadcasted_iota(jnp.int32, (n, 1), 0)
    lv = lv_ref[...]
    a = jnp.zeros((n, n), F32)
    for li, m in enumerate(HG_LEVELS):
        c = bc[(li + 1) * n:(li + 2) * n]
        e = jnp.exp(-jnp.abs(b - c))
        upper = (rowid & (2 * m - 1)) >= m
        x = (jnp.where(upper, qs, kk) * e).astype(BF16)
        p = lax.dot_general(x, x, (((1,), (1,)), ((), ())), preferred_element_type=F32)
        a = jnp.where(lv == li, p, a)
    a = jnp.where(lv == len(HG_LEVELS), jnp.sum(qs * kk, axis=-1, keepdims=True), a)
    o = jnp.dot(a.astype(BF16), iv, preferred_element_type=F32)

    blast = bc[(len(HG_LEVELS) + 1) * n:]
    qd = (qs * jnp.exp(b)).astype(BF16)
    kd = kk * jnp.exp(blast - b)
    ivt = iv.astype(F32).T.astype(BF16)
    st = st_ref[...]
    outs = []
    for ci in range(n // HG_CHUNK):
        lo = ci * HG_CHUNK
        inter = lax.dot_general(qd[lo:lo + HG_CHUNK], st.astype(BF16), (((1,), (1,)), ((), ())),
                                preferred_element_type=F32)
        outs.append(o[lo:lo + HG_CHUNK] + inter)
        kd_c = jnp.where((rowid >= lo) & (rowid < lo + HG_CHUNK), kd, 0.0).astype(BF16)
        upd = jnp.dot(ivt, kd_c, preferred_element_type=F32)
        st = jnp.exp(blast[lo:lo + 1]) * st + upd
    st_ref[...] = st
    o = jnp.concatenate(outs, axis=0)

    g_in = g_ref[...].astype(F32)
    o_ref[...] = (_rms(o, gn_ref[...]) * (g_in * _sigmoid(g_in))).astype(o_ref.dtype)


def _hgrn(hg4, lb, gn, mall, lv):
    s = hg4.shape[0]
    h = HG_HEADS

    def col(c):
        return lambda hh, t: (t, c * h + hh)

    head = lambda hh, t: (0, hh)
    const = lambda hh, t: (0, 0)
    return pl.pallas_call(
        _hgrn_kernel,
        grid=(h, s // HG_ROWS),
        in_specs=[
            pl.BlockSpec((HG_ROWS, HG_DK), col(0)),
            pl.BlockSpec((HG_ROWS, HG_DK), col(1)),
            pl.BlockSpec((HG_ROWS, HG_DV), col(2)),
            pl.BlockSpec((HG_ROWS, HG_DV), col(3)),
            pl.BlockSpec((1, HG_DK), head),
            pl.BlockSpec((1, HG_DV), head),
            pl.BlockSpec(mall.shape, const),
            pl.BlockSpec(lv.shape, const),
        ],
        out_specs=pl.BlockSpec((HG_ROWS, HG_DV), lambda hh, t: (t, hh)),
        out_shape=jax.ShapeDtypeStruct((s, D_HG), BF16),
        scratch_shapes=[pltpu.VMEM((HG_DV, HG_DK), F32)],
        compiler_params=_cparams(("parallel", "arbitrary"), 32),
        name="hgrn2",
    )(hg4, hg4, hg4, hg4, lb, gn, mall, lv)


def _out_route_kernel(x_ref, om_ref, oh_ref, wa_ref, wb_ref, g_ref, wr_ref, br_ref,
                      h_ref, hn_ref, rt_ref):
    h1 = (x_ref[...]
          + jnp.dot(om_ref[...], wa_ref[...], preferred_element_type=F32)
          + jnp.dot(oh_ref[...], wb_ref[...], preferred_element_type=F32))
    h_ref[...] = h1
    hn = _rms(h1, g_ref[...])
    hn_ref[...] = hn
    logits = jnp.dot(hn, wr_ref[...], preferred_element_type=F32,
                     precision=lax.Precision.HIGHEST) + br_ref[...]

    lane = lax.broadcasted_iota(jnp.int32, logits.shape, 1)
    lanef = lane.astype(F32)
    ninf = -jnp.inf
    big = float(LANES)

    is_g = lane < N_GROUPS
    gl = jnp.where(is_g, logits, ninf)
    gmax = jnp.max(gl, axis=-1, keepdims=True)
    gsum = jnp.sum(jnp.where(is_g, jnp.exp(gl - gmax), 0.0), axis=-1, keepdims=True)
    g_w = 1.0 / gsum
    g_idx = jnp.min(jnp.where(gl == gmax, lanef, big), axis=-1, keepdims=True)

    e_lane = lane - N_GROUPS
    in_grp = (e_lane >= 0) & (e_lane < N_EXPERTS) & ((e_lane >> 3).astype(F32) == g_idx)
    el = jnp.where(in_grp, logits, ninf)
    emax = jnp.max(el, axis=-1, keepdims=True)
    esum = jnp.sum(jnp.where(in_grp, jnp.exp(el - emax), 0.0), axis=-1, keepdims=True)
    i1 = jnp.min(jnp.where(el == emax, lanef, big), axis=-1, keepdims=True)
    el2 = jnp.where(lanef == i1, ninf, el)
    emax2 = jnp.max(el2, axis=-1, keepdims=True)
    i2 = jnp.min(jnp.where(el2 == emax2, lanef, big), axis=-1, keepdims=True)
    p1 = 1.0 / esum
    p2 = jnp.exp(emax2 - emax) / esum
    w1 = g_w * p1 / (p1 + p2)
    w2 = g_w * p2 / (p1 + p2)

    rt = jnp.where(lane == 0, i1 - N_GROUPS,
                   jnp.where(lane == 1, i2 - N_GROUPS,
                             jnp.where(lane == 2, w1, jnp.where(lane == 3, w2, 0.0))))
    rt_ref[...] = rt


def _out_route(x, o_mla, o_hg, wa, wb, gain, wr, br, tm):
    s, d = x.shape
    row = lambda i: (i, 0)
    full = lambda i: (0, 0)
    return pl.pallas_call(
        _out_route_kernel,
        grid=(s // tm,),
        in_specs=[
            pl.BlockSpec((tm, d), row),
            pl.BlockSpec((tm, D_MLA), row),
            pl.BlockSpec((tm, D_HG), row),
            pl.BlockSpec(wa.shape, full),
            pl.BlockSpec(wb.shape, full),
            pl.BlockSpec((1, d), full),
            pl.BlockSpec(wr.shape, full),
            pl.BlockSpec((1, LANES), full),
        ],
        out_specs=[
            pl.BlockSpec((tm, d), row),
            pl.BlockSpec((tm, d), row),
            pl.BlockSpec((tm, LANES), row),
        ],
        out_shape=[
            jax.ShapeDtypeStruct((s, d), F32),
            jax.ShapeDtypeStruct((s, d), F32),
            jax.ShapeDtypeStruct((s, LANES), F32),
        ],
        compiler_params=_cparams(("parallel",), 48),
        name="out_proj_route",
    )(x, o_mla, o_hg, wa, wb, gain, wr, br)


def _moe_kernel(be_ref, tok_ref, nu_ref, hn_hbm, w_ref, wg_ref, wu_ref, wd_ref, o_ref,
                xbuf, sem, wgb, wub, wdb):
    i = pl.program_id(0)
    nb = pl.num_programs(0)
    nused = nu_ref[0]
    slot = i % 2

    def row_copy(blk, r, sl):
        tok = tok_ref[blk * MOE_BLOCK + r]
        return pltpu.make_async_copy(hn_hbm.at[pl.ds(tok, 1), :], xbuf.at[sl, pl.ds(r, 1), :], sem.at[sl])

    def start_gather(blk, sl):
        def body(r, c):
            row_copy(blk, r, sl).start()
            return c
        lax.fori_loop(0, MOE_BLOCK, body, 0)

    def wait_gather(blk, sl):
        def body(r, c):
            row_copy(blk, r, sl).wait()
            return c
        lax.fori_loop(0, MOE_BLOCK, body, 0)

    @pl.when((i == 0) & (nused > 0))
    def _():
        start_gather(0, 0)

    @pl.when((i + 1 < nb) & (i + 1 < nused))
    def _():
        start_gather(i + 1, 1 - slot)

    @pl.when(i < nused)
    def _():
        prev = be_ref[jnp.maximum(i - 1, 0)]

        @pl.when((i == 0) | (be_ref[i] != prev))
        def _():
            wgb[...] = wg_ref[...].astype(BF16)
            wub[...] = wu_ref[...].astype(BF16)
            wdb[...] = wd_ref[...].astype(BF16)

        wait_gather(i, slot)
        x = xbuf[slot].astype(BF16)
        g = jnp.dot(x, wgb[...], preferred_element_type=F32)
        u = jnp.dot(x, wub[...], preferred_element_type=F32)
        hmid = (g * _sigmoid(g) * u).astype(BF16)
        o_ref[...] = jnp.dot(hmid, wdb[...], preferred_element_type=F32) * w_ref[...]

    @pl.when(i >= nused)
    def _():
        o_ref[...] = jnp.zeros(o_ref.shape, F32)


def _moe(block_e, slot_tok, nused, hn, slot_w, wg, wu, wd):
    nb = block_e.shape[0]
    d = hn.shape[1]
    grid_spec = pltpu.PrefetchScalarGridSpec(
        num_scalar_prefetch=3,
        grid=(nb,),
        in_specs=[
            pl.BlockSpec(memory_space=pl.ANY),
            pl.BlockSpec((MOE_BLOCK, 1), lambda i, be, tok, nu: (i, 0)),
            pl.BlockSpec((None, d, D_EXPERT), lambda i, be, tok, nu: (be[i], 0, 0)),
            pl.BlockSpec((None, d, D_EXPERT), lambda i, be, tok, nu: (be[i], 0, 0)),
            pl.BlockSpec((None, D_EXPERT, d), lambda i, be, tok, nu: (be[i], 0, 0)),
        ],
        out_specs=pl.BlockSpec((MOE_BLOCK, d), lambda i, be, tok, nu: (i, 0)),
        scratch_shapes=[
            pltpu.VMEM((2, MOE_BLOCK, d), F32),
            pltpu.SemaphoreType.DMA((2,)),
            pltpu.VMEM((d, D_EXPERT), BF16),
            pltpu.VMEM((d, D_EXPERT), BF16),
            pltpu.VMEM((D_EXPERT, d), BF16),
        ],
    )
    return pl.pallas_call(
        _moe_kernel,
        grid_spec=grid_spec,
        out_shape=jax.ShapeDtypeStruct((nb * MOE_BLOCK, d), F32),
        compiler_params=_cparams(("arbitrary",), 56),
        name="moe_experts",
    )(block_e, slot_tok, nused, hn, slot_w, wg, wu, wd)


def _ple_kernel(pos_ref, h_ref, y_hbm, p_ref, wg_ref, bg_ref, wp_ref, gp_ref, gf_ref, o_ref,
                ybuf, sem, *, tm):
    i = pl.program_id(0)

    def row_copy(r, kk):
        src = pos_ref[(i * tm + r) * TOP_K + kk]
        return pltpu.make_async_copy(y_hbm.at[pl.ds(src, 1), :], ybuf.at[kk, pl.ds(r, 1), :], sem.at[0])

    def start_body(r, c):
        row_copy(r, 0).start()
        row_copy(r, 1).start()
        return c

    def wait_body(r, c):
        row_copy(r, 0).wait()
        row_copy(r, 1).wait()
        return c

    lax.fori_loop(0, tm, start_body, 0)
    pe = jnp.dot(p_ref[...].astype(BF16), wp_ref[...], preferred_element_type=F32)
    lax.fori_loop(0, tm, wait_body, 0)

    h2 = h_ref[...] + (ybuf[0] + ybuf[1])
    hn = _rms(h2, gp_ref[...]).astype(BF16)
    gate = _sigmoid(jnp.dot(hn, wg_ref[...], preferred_element_type=F32) + bg_ref[...])
    h3 = h2 + gate * pe
    o_ref[...] = _rms(h3, gf_ref[...])


def _ple_final(pos, h1, y_slots, p, wg, bg, wp, gp, gf, tm):
    s, d = h1.shape
    row = lambda i, pos: (i, 0)
    full = lambda i, pos: (0, 0)
    grid_spec = pltpu.PrefetchScalarGridSpec(
        num_scalar_prefetch=1,
        grid=(s // tm,),
        in_specs=[
            pl.BlockSpec((tm, d), row),
            pl.BlockSpec(memory_space=pl.ANY),
            pl.BlockSpec((tm, PLE_DIM), row),
            pl.BlockSpec(wg.shape, full),
            pl.BlockSpec((1, d), full),
            pl.BlockSpec(wp.shape, full),
            pl.BlockSpec((1, d), full),
            pl.BlockSpec((1, d), full),
        ],
        out_specs=pl.BlockSpec((tm, d), row),
        scratch_shapes=[
            pltpu.VMEM((TOP_K, tm, d), F32),
            pltpu.SemaphoreType.DMA((1,)),
        ],
    )
    return pl.pallas_call(
        functools.partial(_ple_kernel, tm=tm),
        grid_spec=grid_spec,
        out_shape=jax.ShapeDtypeStruct((s, d), F32),
        compiler_params=_cparams(("arbitrary",), 48),
        name="ple_final",
    )(pos, h1, y_slots, p, wg, bg, wp, gp, gf)


def _dispatch(eid, ew, n_tok):
    a = n_tok * TOP_K
    nb = -(-a // MOE_BLOCK) + N_EXPERTS
    e_flat = eid.reshape(-1)
    w_flat = ew.reshape(-1)
    t_flat = jnp.repeat(jnp.arange(n_tok, dtype=jnp.int32), TOP_K)
    onehot = (e_flat[:, None] == jnp.arange(N_EXPERTS, dtype=jnp.int32)[None, :]).astype(jnp.int32)
    csum = jnp.cumsum(onehot, axis=0)
    rank = jnp.take_along_axis(csum, e_flat[:, None], axis=1)[:, 0] - 1
    counts = csum[-1]
    padded = ((counts + MOE_BLOCK - 1) // MOE_BLOCK) * MOE_BLOCK
    pends = jnp.cumsum(padded)
    pstarts = pends - padded
    slot = (pstarts[e_flat] + rank).astype(jnp.int32)
    slot_tok = jnp.zeros((nb * MOE_BLOCK,), jnp.int32).at[slot].set(t_flat)
    slot_w = jnp.zeros((nb * MOE_BLOCK,), F32).at[slot].set(w_flat)
    block_e = jnp.clip(
        jnp.searchsorted(pends, jnp.arange(nb, dtype=jnp.int32) * MOE_BLOCK, side="right"),
        0, N_EXPERTS - 1).astype(jnp.int32)
    nused = (pends[-1] // MOE_BLOCK).astype(jnp.int32).reshape(1)
    return block_e, slot_tok, nused, slot_w.reshape(-1, 1), slot


def kernel(x, p, positions, attn_norm, w_in, q_norm, w_uq, kv_norm, w_ukv, mla_norm, hg_lb_logits, hg_norm, w_out, ffn_norm, w_router_group, b_router_group, w_router_expert, b_router_expert, w_exp_gate, w_exp_up, w_exp_down, ple_norm, w_ple_gate, b_ple_gate, w_ple_proj, final_norm):
    bsz, s, d = x.shape
    assert bsz == 1 and w_in.shape[0] == 1
    xt = x[0]

    inv_freq = 1.0 / (ROPE_THETA ** (jnp.arange(0, QK_ROPE, 2, dtype=F32) / QK_ROPE))
    ang = positions[0].astype(F32)[:, None] * inv_freq
    cos, sin = jnp.cos(ang), jnp.sin(ang)
    zpad = jnp.zeros((s, LANES - QK_ROPE), F32)
    cc = jnp.concatenate([cos, cos, zpad], axis=1)
    ss = jnp.concatenate([-sin, sin, zpad], axis=1)

    lb = jnp.cumsum(jax.nn.softmax(hg_lb_logits.astype(F32), axis=0), axis=0)[0][None, :]

    wi = w_in[0]
    kr0 = Q_LORA + KV_LORA
    half = QK_ROPE // 2
    w_lat = jnp.concatenate(
        [wi[:, :kr0 + QK_ROPE], wi[:, kr0 + half:kr0 + QK_ROPE], wi[:, kr0:kr0 + half]], axis=1).astype(BF16)
    w_hg = wi[:, kr0 + QK_ROPE:].astype(BF16)
    wq3 = w_uq[0].reshape(Q_LORA, MLA_HEADS, QK_HEAD)
    wq_pad = jnp.concatenate(
        [wq3, wq3[:, :, QK_NOPE + half:], wq3[:, :, QK_NOPE:QK_NOPE + half]], axis=2
    ).reshape(Q_LORA, MLA_HEADS * QK_PAD).astype(BF16)
    wkv3 = w_ukv[0].reshape(KV_LORA, MLA_HEADS, QK_NOPE + V_HEAD)
    wkn = wkv3[:, :, :QK_NOPE].reshape(KV_LORA, MLA_HEADS * QK_NOPE).astype(BF16)
    wvt = wkv3[:, :, QK_NOPE:].reshape(KV_LORA, D_MLA).T.astype(BF16)
    wo = w_out[0].astype(BF16)
    wr = jnp.concatenate(
        [w_router_group[0], w_router_expert[0], jnp.zeros((d, LANES - N_GROUPS - N_EXPERTS), F32)], axis=1)
    br = jnp.concatenate(
        [b_router_group[0], b_router_expert[0], jnp.zeros((LANES - N_GROUPS - N_EXPERTS,), F32)])[None, :]

    lat = _in_proj(xt, attn_norm, w_lat, F32, 512, D_LAT, "in_proj_latent")
    hg4 = _in_proj(xt, attn_norm, w_hg, BF16, 1024, 1024, "in_proj_hgrn")
    q, k, vt = _mla_up(lat, q_norm, kv_norm, wq_pad, wkn, wvt, cc, ss, 512)
    o_mla = _attention(q, k, vt, mla_norm, 512, 512)
    mall, lv = _hgrn_tables()
    o_hg = _hgrn(hg4, lb, hg_norm[0].reshape(1, D_HG), mall, lv)

    h1, hn, rt = _out_route(xt, o_mla, o_hg, wo[:D_MLA], wo[D_MLA:], ffn_norm, wr, br, 256)
    eid = rt[:, :TOP_K].astype(jnp.int32)
    ew = rt[:, TOP_K:2 * TOP_K]
    block_e, slot_tok, nused, slot_w, pos = _dispatch(eid, ew, s)
    y_slots = _moe(block_e, slot_tok, nused, hn, slot_w, w_exp_gate[0], w_exp_up[0], w_exp_down[0])

    out = _ple_final(pos, h1, y_slots, p[0, 0], w_ple_gate[0].astype(BF16), b_ple_gate,
                     w_ple_proj[0].astype(BF16), ple_norm, final_norm[None, :], 256)
    return out[None]
```

```python
import functools

import jax
import jax.numpy as jnp
import numpy as np
from jax import lax
from jax.experimental import pallas as pl
from jax.experimental.pallas import tpu as pltpu

F32 = jnp.float32
BF16 = jnp.bfloat16

D_MODEL = 2048
PLE_DIM = 256
MLA_HEADS = 8
QK_NOPE = 128
QK_ROPE = 64
QK_HEAD = QK_NOPE + QK_ROPE
QK_PAD = 256
V_HEAD = 128
VT_ROWS = V_HEAD + 16
QK_AHEAD = 2
PV_LAG = 1
Q_LORA = 512
KV_LORA = 256
ROPE_THETA = 10000.0
HG_HEADS = 8
HG_DK = 128
HG_DV = 128
HG_CHUNK = 64
D_MLA = MLA_HEADS * V_HEAD
D_HG = HG_HEADS * HG_DV
N_GROUPS = 8
EXPERTS_PER_GROUP = 8
N_EXPERTS = N_GROUPS * EXPERTS_PER_GROUP
TOP_K = 2
D_EXPERT = 512
EPS = 1e-6
LANES = 128
D_LAT = Q_LORA + KV_LORA + 2 * QK_ROPE
NEG_BIG = -1e30
LOG2E = 1.4426950408889634

MOE_BLOCK = 320
HG_ROWS = 256
HG_LEVELS = (32, 16, 8, 4, 2, 1)


def _cparams(sem, vmem_mb=None):
    kw = dict(dimension_semantics=sem)
    if vmem_mb is not None:
        kw["vmem_limit_bytes"] = vmem_mb * 1024 * 1024
    return pltpu.CompilerParams(**kw)


def _rms(x, g):
    ms = jnp.mean(x * x, axis=-1, keepdims=True)
    return x * lax.rsqrt(ms + EPS) * g


def _sigmoid(x):
    return 1.0 / (1.0 + jnp.exp(-x))


def _in_proj_kernel(x_ref, g_ref, w_ref, o_ref, xn_ref):
    @pl.when(pl.program_id(1) == 0)
    def _():
        xn_ref[...] = _rms(x_ref[...], g_ref[...]).astype(BF16)

    o_ref[...] = jnp.dot(xn_ref[...], w_ref[...], preferred_element_type=F32).astype(o_ref.dtype)


def _in_proj(x, gain, w, out_dtype, tm, tn, name):
    s, d = x.shape
    n = w.shape[1]
    return pl.pallas_call(
        _in_proj_kernel,
        grid=(s // tm, n // tn),
        in_specs=[
            pl.BlockSpec((tm, d), lambda i, j: (i, 0)),
            pl.BlockSpec((1, d), lambda i, j: (0, 0)),
            pl.BlockSpec((d, tn), lambda i, j: (0, j)),
        ],
        out_specs=pl.BlockSpec((tm, tn), lambda i, j: (i, j)),
        out_shape=jax.ShapeDtypeStruct((s, n), out_dtype),
        scratch_shapes=[pltpu.VMEM((tm, d), BF16)],
        compiler_params=_cparams(("parallel", "arbitrary"), 48),
        name=name,
    )(x, gain, w)


def _mla_up_kernel(lat_ref, qn_ref, kvn_ref, wq_ref, wkn_ref, wvt_ref, cc_ref, ss_ref,
                   q_ref, k_ref, vt_ref):
    lat = lat_ref[...]
    cq = _rms(lat[:, :Q_LORA], qn_ref[...]).astype(BF16)
    ckv = _rms(lat[:, Q_LORA:Q_LORA + KV_LORA], kvn_ref[...]).astype(BF16)
    kp = lat[:, Q_LORA + KV_LORA:]
    cc = cc_ref[...]
    ss = ss_ref[...]
    scale = QK_HEAD ** -0.5 * LOG2E

    q = jnp.dot(cq, wq_ref[...], preferred_element_type=F32)
    for h in range(MLA_HEADS):
        base = h * QK_PAD
        q_ref[:, base:base + QK_NOPE] = (q[:, base:base + QK_NOPE] * scale).astype(BF16)
        r = q[:, base + QK_NOPE:base + QK_PAD]
        r = (r * cc + pltpu.roll(r, QK_ROPE, 1) * ss) * scale
        q_ref[:, base + QK_NOPE:base + QK_PAD] = r.astype(BF16)

    kpe = (kp * cc + pltpu.roll(kp, QK_ROPE, 1) * ss).astype(BF16)
    kn = jnp.dot(ckv, wkn_ref[...], preferred_element_type=F32).astype(BF16)
    for h in range(MLA_HEADS):
        base = h * QK_PAD
        k_ref[:, base:base + QK_NOPE] = kn[:, h * QK_NOPE:(h + 1) * QK_NOPE]
        k_ref[:, base + QK_NOPE:base + QK_PAD] = kpe
    vt = lax.dot_general(wvt_ref[...], ckv, (((1,), (1,)), ((), ())), preferred_element_type=F32).astype(BF16)
    ones = jnp.ones((VT_ROWS - V_HEAD, vt.shape[1]), BF16)
    for h in range(MLA_HEADS):
        vt_ref[h * VT_ROWS:h * VT_ROWS + V_HEAD, :] = vt[h * V_HEAD:(h + 1) * V_HEAD]
        vt_ref[h * VT_ROWS + V_HEAD:(h + 1) * VT_ROWS, :] = ones


def _mla_up(lat, qn, kvn, wq, wkn, wvt, cc, ss, tm):
    s = lat.shape[0]
    row = lambda i: (i, 0)
    full = lambda i: (0, 0)
    return pl.pallas_call(
        _mla_up_kernel,
        grid=(s // tm,),
        in_specs=[
            pl.BlockSpec((tm, D_LAT), row),
            pl.BlockSpec((1, Q_LORA), full),
            pl.BlockSpec((1, KV_LORA), full),
            pl.BlockSpec(wq.shape, full),
            pl.BlockSpec(wkn.shape, full),
            pl.BlockSpec(wvt.shape, full),
            pl.BlockSpec((tm, LANES), row),
            pl.BlockSpec((tm, LANES), row),
        ],
        out_specs=[
            pl.BlockSpec((tm, MLA_HEADS * QK_PAD), row),
            pl.BlockSpec((tm, MLA_HEADS * QK_PAD), row),
            pl.BlockSpec((MLA_HEADS * VT_ROWS, tm), lambda i: (0, i)),
        ],
        out_shape=[
            jax.ShapeDtypeStruct((s, MLA_HEADS * QK_PAD), BF16),
            jax.ShapeDtypeStruct((s, MLA_HEADS * QK_PAD), BF16),
            jax.ShapeDtypeStruct((MLA_HEADS * VT_ROWS, s), BF16),
        ],
        compiler_params=_cparams(("parallel",), 48),
        name="mla_up",
    )(lat, qn, kvn, wq, wkn, wvt, cc, ss)


def _attn_kernel(it_ref, jt_ref, q_ref, k_ref, vt_ref, g_ref, o_ref, acc_ref, m_ref, *, tq, tk, qc):
    t = pl.program_id(0)
    i = it_ref[t]
    j = jt_ref[t]

    @pl.when(j == 0)
    def _():
        m_ref[...] = jnp.full(m_ref.shape, NEG_BIG, F32)
        acc_ref[...] = jnp.zeros(acc_ref.shape, F32)

    units = [(h, c) for h in range(MLA_HEADS) for c in range(tq // qc)]

    def scores(u):
        h, c = u
        q = q_ref[c * qc:(c + 1) * qc, h * QK_PAD:(h + 1) * QK_PAD]
        k = k_ref[:, h * QK_PAD:(h + 1) * QK_PAD]
        return lax.dot_general(k, q, (((1,), (1,)), ((), ())), preferred_element_type=F32)

    def step(masked):
        if masked:
            diff = lax.broadcasted_iota(jnp.int32, (tk, qc), 0) - lax.broadcasted_iota(jnp.int32, (tk, qc), 1)
        def accumulate(h, cols, alpha, p):
            pv = jnp.dot(vt_ref[h * VT_ROWS:(h + 1) * VT_ROWS, :], p, preferred_element_type=F32)
            acc_ref[h, :, cols] = alpha * acc_ref[h, :, cols] + pv

        ahead = [scores(units[n]) for n in range(min(QK_AHEAD, len(units)))]
        pending = []
        for n, (h, c) in enumerate(units):
            s = ahead.pop(0)
            if n + QK_AHEAD < len(units):
                ahead.append(scores(units[n + QK_AHEAD]))
            cols = slice(c * qc, (c + 1) * qc)
            if masked:
                s = jnp.where(diff <= c * qc, s, NEG_BIG)
            m_prev = m_ref[h, :, cols]
            m_new = jnp.maximum(m_prev, jnp.max(s, axis=0, keepdims=True))
            alpha = jnp.exp2(m_prev - m_new)
            p = jnp.exp2(s - m_new).astype(BF16)
            m_ref[h, :, cols] = m_new
            pending.append((h, cols, alpha, p))
            if len(pending) > PV_LAG:
                accumulate(*pending.pop(0))
        for item in pending:
            accumulate(*item)

    @pl.when(j < i)
    def _():
        step(False)

    @pl.when(j == i)
    def _():
        step(True)
        outs = [acc_ref[h, :V_HEAD, :] * (1.0 / acc_ref[h, V_HEAD:V_HEAD + 1, :]) for h in range(MLA_HEADS)]
        o_ref[...] = _rms(jnp.concatenate(outs, axis=0).T, g_ref[...]).astype(o_ref.dtype)


def _attention(q, k, vt, gain, tb, qc):
    s = q.shape[0]
    nq = s // tb
    it = np.concatenate([np.full(i + 1, i, np.int32) for i in range(nq)])
    jt = np.concatenate([np.arange(i + 1, dtype=np.int32) for i in range(nq)])
    grid_spec = pltpu.PrefetchScalarGridSpec(
        num_scalar_prefetch=2,
        grid=(it.shape[0],),
        in_specs=[
            pl.BlockSpec((tb, MLA_HEADS * QK_PAD), lambda t, it, jt: (it[t], 0)),
            pl.BlockSpec((tb, MLA_HEADS * QK_PAD), lambda t, it, jt: (jt[t], 0)),
            pl.BlockSpec((MLA_HEADS * VT_ROWS, tb), lambda t, it, jt: (0, jt[t])),
            pl.BlockSpec((1, D_MLA), lambda t, it, jt: (0, 0)),
        ],
        out_specs=pl.BlockSpec((tb, D_MLA), lambda t, it, jt: (it[t], 0)),
        scratch_shapes=[
            pltpu.VMEM((MLA_HEADS, VT_ROWS, tb), F32),
            pltpu.VMEM((MLA_HEADS, 1, tb), F32),
        ],
    )
    return pl.pallas_call(
        functools.partial(_attn_kernel, tq=tb, tk=tb, qc=qc),
        grid_spec=grid_spec,
        out_shape=jax.ShapeDtypeStruct((s, D_MLA), BF16),
        compiler_params=_cparams(("arbitrary",), 48),
        name="mla_attention",
    )(jnp.asarray(it), jnp.asarray(jt), q, k, vt, gain)


def _hgrn_tables():
    n = HG_ROWS
    r = np.arange(n)
    c = np.arange(n)
    same = (r[:, None] // HG_CHUNK) == (c[None, :] // HG_CHUNK)

    def rows_upto(idx):
        return (same & (c[None, :] <= idx[:, None])).astype(np.float32)

    blocks = [rows_upto(r)]
    for m in HG_LEVELS:
        blocks.append(rows_upto((r // (2 * m)) * (2 * m) + m))
    blocks.append(rows_upto((r // HG_CHUNK) * HG_CHUNK + HG_CHUNK - 1))
    mall = np.concatenate(blocks, axis=0)

    x = r[:, None] ^ c[None, :]
    lv = np.full((n, n), -1, np.int32)
    for li, m in enumerate(HG_LEVELS):
        lv = np.where(same & (r[:, None] > c[None, :]) & (x >= m) & (x < 2 * m), li, lv)
    lv = np.where(r[:, None] == c[None, :], len(HG_LEVELS), lv)
    return jnp.asarray(mall, BF16), jnp.asarray(lv, jnp.int32)


def _hgrn_kernel(q_ref, f_ref, i_ref, g_ref, lb_ref, gn_ref, mall_ref, lv_ref, o_ref, st_ref):
    t = pl.program_id(1)
    n = HG_ROWS

    @pl.when(t == 0)
    def _():
        st_ref[...] = jnp.zeros(st_ref.shape, F32)

    q_in = q_ref[...].astype(F32)
    qs = q_in * _sigmoid(q_in)
    lb = lb_ref[...]
    f = lb + (1.0 - lb) * _sigmoid(f_ref[...].astype(F32))
    kk = 1.0 - f
    logf = jnp.log(f)
    iv = i_ref[...]

    l1 = logf.astype(BF16)
    r1 = logf - l1.astype(F32)
    l2 = r1.astype(BF16)
    l3 = (r1 - l2.astype(F32)).astype(BF16)
    parts = jnp.dot(mall_ref[...], jnp.concatenate([l1, l2, l3], axis=1), preferred_element_type=F32)
    bc = parts[:, :HG_DK] + parts[:, HG_DK:2 * HG_DK] + parts[:, 2 * HG_DK:]
    b = bc[:n]

    rowid = lax.broadcasted_iota(jnp.int32, (n, 1), 0)
    lv = lv_ref[...]
    a = jnp.zeros((n, n), F32)
    for li, m in enumerate(HG_LEVELS):
        c = bc[(li + 1) * n:(li + 2) * n]
        e = jnp.exp(-jnp.abs(b - c))
        upper = (rowid & (2 * m - 1)) >= m
        x = (jnp.where(upper, qs, kk) * e).astype(BF16)
        p = lax.dot_general(x, x, (((1,), (1,)), ((), ())), preferred_element_type=F32)
        a = jnp.where(lv == li, p, a)
    a = jnp.where(lv == len(HG_LEVELS), jnp.sum(qs * kk, axis=-1, keepdims=True), a)
    o = jnp.dot(a.astype(BF16), iv, preferred_element_type=F32)

    blast = bc[(len(HG_LEVELS) + 1) * n:]
    qd = (qs * jnp.exp(b)).astype(BF16)
    kd = kk * jnp.exp(blast - b)
    ivt = iv.astype(F32).T.astype(BF16)
    st = st_ref[...]
    outs = []
    for ci in range(n // HG_CHUNK):
        lo = ci * HG_CHUNK
        inter = lax.dot_general(qd[lo:lo + HG_CHUNK], st.astype(BF16), (((1,), (1,)), ((), ())),
                                preferred_element_type=F32)
        outs.append(o[lo:lo + HG_CHUNK] + inter)
        kd_c = jnp.where((rowid >= lo) & (rowid < lo + HG_CHUNK), kd, 0.0).astype(BF16)
        upd = jnp.dot(ivt, kd_c, preferred_element_type=F32)
        st = jnp.exp(blast[lo:lo + 1]) * st + upd
    st_ref[...] = st
    o = jnp.concatenate(outs, axis=0)

    g_in = g_ref[...].astype(F32)
    o_ref[...] = (_rms(o, gn_ref[...]) * (g_in * _sigmoid(g_in))).astype(o_ref.dtype)


def _hgrn(hg4, lb, gn, mall, lv):
    s = hg4.shape[0]
    h = HG_HEADS

    def col(c):
        return lambda hh, t: (t, c * h + hh)

    head = lambda hh, t: (0, hh)
    const = lambda hh, t: (0, 0)
    return pl.pallas_call(
        _hgrn_kernel,
        grid=(h, s // HG_ROWS),
        in_specs=[
            pl.BlockSpec((HG_ROWS, HG_DK), col(0)),
            pl.BlockSpec((HG_ROWS, HG_DK), col(1)),
            pl.BlockSpec((HG_ROWS, HG_DV), col(2)),
            pl.BlockSpec((HG_ROWS, HG_DV), col(3)),
            pl.BlockSpec((1, HG_DK), head),
            pl.BlockSpec((1, HG_DV), head),
            pl.BlockSpec(mall.shape, const),
            pl.BlockSpec(lv.shape, const),
        ],
        out_specs=pl.BlockSpec((HG_ROWS, HG_DV), lambda hh, t: (t, hh)),
        out_shape=jax.ShapeDtypeStruct((s, D_HG), BF16),
        scratch_shapes=[pltpu.VMEM((HG_DV, HG_DK), F32)],
        compiler_params=_cparams(("parallel", "arbitrary"), 32),
        name="hgrn2",
    )(hg4, hg4, hg4, hg4, lb, gn, mall, lv)


def _out_route_kernel(x_ref, om_ref, oh_ref, wa_ref, wb_ref, g_ref, wr_ref, br_ref,
                      h_ref, hn_ref, rt_ref):
    h1 = (x_ref[...]
          + jnp.dot(om_ref[...], wa_ref[...], preferred_element_type=F32)
          + jnp.dot(oh_ref[...], wb_ref[...], preferred_element_type=F32))
    h_ref[...] = h1
    hn = _rms(h1, g_ref[...])
    hn_ref[...] = hn
    logits = jnp.dot(hn, wr_ref[...], preferred_element_type=F32,
                     precision=lax.Precision.HIGHEST) + br_ref[...]

    lane = lax.broadcasted_iota(jnp.int32, logits.shape, 1)
    lanef = lane.astype(F32)
    ninf = -jnp.inf
    big = float(LANES)

    is_g = lane < N_GROUPS
    gl = jnp.where(is_g, logits, ninf)
    gmax = jnp.max(gl, axis=-1, keepdims=True)
    gsum = jnp.sum(jnp.where(is_g, jnp.exp(gl - gmax), 0.0), axis=-1, keepdims=True)
    g_w = 1.0 / gsum
    g_idx = jnp.min(jnp.where(gl == gmax, lanef, big), axis=-1, keepdims=True)

    e_lane = lane - N_GROUPS
    in_grp = (e_lane >= 0) & (e_lane < N_EXPERTS) & ((e_lane >> 3).astype(F32) == g_idx)
    el = jnp.where(in_grp, logits, ninf)
    emax = jnp.max(el, axis=-1, keepdims=True)
    esum = jnp.sum(jnp.where(in_grp, jnp.exp(el - emax), 0.0), axis=-1, keepdims=True)
    i1 = jnp.min(jnp.where(el == emax, lanef, big), axis=-1, keepdims=True)
    el2 = jnp.where(lanef == i1, ninf, el)
    emax2 = jnp.max(el2, axis=-1, keepdims=True)
    i2 = jnp.min(jnp.where(el2 == emax2, lanef, big), axis=-1, keepdims=True)
    p1 = 1.0 / esum
    p2 = jnp.exp(emax2 - emax) / esum
    w1 = g_w * p1 / (p1 + p2)
    w2 = g_w * p2 / (p1 + p2)

    rt = jnp.where(lane == 0, i1 - N_GROUPS,
                   jnp.where(lane == 1, i2 - N_GROUPS,
                             jnp.where(lane == 2, w1, jnp.where(lane == 3, w2, 0.0))))
    rt_ref[...] = rt


def _out_route(x, o_mla, o_hg, wa, wb, gain, wr, br, tm):
    s, d = x.shape
    row = lambda i: (i, 0)
    full = lambda i: (0, 0)
    return pl.pallas_call(
        _out_route_kernel,
        grid=(s // tm,),
        in_specs=[
            pl.BlockSpec((tm, d), row),
            pl.BlockSpec((tm, D_MLA), row),
            pl.BlockSpec((tm, D_HG), row),
            pl.BlockSpec(wa.shape, full),
            pl.BlockSpec(wb.shape, full),
            pl.BlockSpec((1, d), full),
            pl.BlockSpec(wr.shape, full),
            pl.BlockSpec((1, LANES), full),
        ],
        out_specs=[
            pl.BlockSpec((tm, d), row),
            pl.BlockSpec((tm, d), row),
            pl.BlockSpec((tm, LANES), row),
        ],
        out_shape=[
            jax.ShapeDtypeStruct((s, d), F32),
            jax.ShapeDtypeStruct((s, d), F32),
            jax.ShapeDtypeStruct((s, LANES), F32),
        ],
        compiler_params=_cparams(("parallel",), 48),
        name="out_proj_route",
    )(x, o_mla, o_hg, wa, wb, gain, wr, br)


def _moe_kernel(be_ref, nv_ref, tok_ref, nu_ref, hn_hbm, wg_hbm, wu_hbm, wd_hbm, o_ref,
                xbuf, xsem, wgb, wub, wdb, wsem, wslot):
    i = pl.program_id(0)
    nused = nu_ref[0]
    e = be_ref[i]
    active = i < nused
    first = (i == 0) | (e != be_ref[jnp.maximum(i - 1, 0)])

    def row_copy(blk, r, sl):
        tok = tok_ref[blk * MOE_BLOCK + r]
        return pltpu.make_async_copy(hn_hbm.at[pl.ds(tok, 1), :], xbuf.at[sl, pl.ds(r, 1), :], xsem.at[sl])

    def gather(blk, sl, wait):
        def body(r, c):
            cp = row_copy(blk, r, sl)
            cp.wait() if wait else cp.start()
            return c
        lax.fori_loop(0, nv_ref[blk], body, 0)

    def weight_copies(ex, sl):
        return (pltpu.make_async_copy(wg_hbm.at[ex], wgb.at[sl], wsem.at[sl, 0]),
                pltpu.make_async_copy(wu_hbm.at[ex], wub.at[sl], wsem.at[sl, 1]),
                pltpu.make_async_copy(wd_hbm.at[ex], wdb.at[sl], wsem.at[sl, 2]))

    @pl.when((i == 0) & active)
    def _():
        xbuf[...] = jnp.zeros(xbuf.shape, F32)
        wslot[0] = 1
        for cp in weight_copies(e, 0):
            cp.start()
        gather(0, 0, wait=False)

    @pl.when(active & first)
    def _():
        sl = 1 - wslot[0]
        wslot[0] = sl
        nxt = lax.while_loop(lambda j: (j < nused) & (be_ref[jnp.minimum(j, nused - 1)] == e),
                             lambda j: j + 1, i + 1)

        @pl.when(nxt < nused)
        def _():
            for cp in weight_copies(be_ref[nxt], 1 - sl):
                cp.start()

        for cp in weight_copies(e, sl):
            cp.wait()

    @pl.when(active)
    def _():
        sl = wslot[0]
        xs = i % 2

        @pl.when(i + 1 < nused)
        def _():
            gather(i + 1, 1 - xs, wait=False)

        gather(i, xs, wait=True)
        x = xbuf[xs].astype(BF16)
        g = jnp.dot(x, wgb[sl].astype(BF16), preferred_element_type=F32)
        u = jnp.dot(x, wub[sl].astype(BF16), preferred_element_type=F32)
        hmid = (g * _sigmoid(g) * u).astype(BF16)
        o_ref[...] = jnp.dot(hmid, wdb[sl].astype(BF16), preferred_element_type=F32)

    @pl.when(jnp.logical_not(active))
    def _():
        o_ref[...] = jnp.zeros(o_ref.shape, F32)


def _moe(block_e, nvalid, slot_tok, nused, hn, wg, wu, wd):
    nb = block_e.shape[0]
    d = hn.shape[1]
    blk = lambda i, be, nv, tok, nu: (i, 0)
    grid_spec = pltpu.PrefetchScalarGridSpec(
        num_scalar_prefetch=4,
        grid=(nb,),
        in_specs=[pl.BlockSpec(memory_space=pl.ANY)] * 4,
        out_specs=pl.BlockSpec((MOE_BLOCK, d), blk),
        scratch_shapes=[
            pltpu.VMEM((2, MOE_BLOCK, d), F32),
            pltpu.SemaphoreType.DMA((2,)),
            pltpu.VMEM((2, d, D_EXPERT), F32),
            pltpu.VMEM((2, d, D_EXPERT), F32),
            pltpu.VMEM((2, D_EXPERT, d), F32),
            pltpu.SemaphoreType.DMA((2, 3)),
            pltpu.SMEM((1,), jnp.int32),
        ],
    )
    return pl.pallas_call(
        _moe_kernel,
        grid_spec=grid_spec,
        out_shape=jax.ShapeDtypeStruct((nb * MOE_BLOCK, d), F32),
        compiler_params=_cparams(("arbitrary",), 56),
        name="moe_experts",
    )(block_e, nvalid, slot_tok, nused, hn, wg, wu, wd)


def _ple_kernel(pos_ref, h_ref, rt_ref, y_hbm, p_ref, wg_ref, bg_ref, wp_ref, gp_ref, gf_ref, o_ref,
                ybuf, sem, *, tm):
    i = pl.program_id(0)
    n = pl.num_programs(0)

    def gather(step, sl, wait):
        def body(r, c):
            for kk in range(TOP_K):
                src = pos_ref[(step * tm + r) * TOP_K + kk]
                cp = pltpu.make_async_copy(y_hbm.at[pl.ds(src, 1), :], ybuf.at[sl, kk, pl.ds(r, 1), :], sem.at[sl])
                cp.wait() if wait else cp.start()
            return c
        lax.fori_loop(0, tm, body, 0)

    @pl.when(i == 0)
    def _():
        gather(0, 0, wait=False)

    @pl.when(i + 1 < n)
    def _():
        gather(i + 1, (i + 1) % 2, wait=False)

    pe = jnp.dot(p_ref[...].astype(BF16), wp_ref[...], preferred_element_type=F32)
    sl = i % 2
    gather(i, sl, wait=True)

    rt = rt_ref[...]
    h2 = h_ref[...] + (rt[:, TOP_K:TOP_K + 1] * ybuf[sl, 0] + rt[:, TOP_K + 1:TOP_K + 2] * ybuf[sl, 1])
    hn = _rms(h2, gp_ref[...]).astype(BF16)
    gate = _sigmoid(jnp.dot(hn, wg_ref[...], preferred_element_type=F32) + bg_ref[...])
    h3 = h2 + gate * pe
    o_ref[...] = _rms(h3, gf_ref[...])


def _ple_final(pos, h1, rt, y_slots, p, wg, bg, wp, gp, gf, tm):
    s, d = h1.shape
    row = lambda i, pos: (i, 0)
    full = lambda i, pos: (0, 0)
    grid_spec = pltpu.PrefetchScalarGridSpec(
        num_scalar_prefetch=1,
        grid=(s // tm,),
        in_specs=[
            pl.BlockSpec((tm, d), row),
            pl.BlockSpec((tm, LANES), row),
            pl.BlockSpec(memory_space=pl.ANY),
            pl.BlockSpec((tm, PLE_DIM), row),
            pl.BlockSpec(wg.shape, full),
            pl.BlockSpec((1, d), full),
            pl.BlockSpec(wp.shape, full),
            pl.BlockSpec((1, d), full),
            pl.BlockSpec((1, d), full),
        ],
        out_specs=pl.BlockSpec((tm, d), row),
        scratch_shapes=[
            pltpu.VMEM((2, TOP_K, tm, d), F32),
            pltpu.SemaphoreType.DMA((2,)),
        ],
    )
    return pl.pallas_call(
        functools.partial(_ple_kernel, tm=tm),
        grid_spec=grid_spec,
        out_shape=jax.ShapeDtypeStruct((s, d), F32),
        compiler_params=_cparams(("arbitrary",), 48),
        name="ple_final",
    )(pos, h1, rt, y_slots, p, wg, bg, wp, gp, gf)


def _slots_kernel(rt_ref, tri_ref, pos_ref, cnt_ref, carry_ref, base_ref):
    ps = pl.program_id(0)
    b = pl.program_id(1)
    rt = rt_ref[...]
    lane = lax.broadcasted_iota(jnp.int32, rt.shape, 1)
    lanef = lane.astype(F32)
    oh = [(lanef == rt[:, kk:kk + 1]).astype(F32) for kk in range(TOP_K)]
    both = oh[0] + oh[1]
    colsum = jnp.sum(both, axis=0, keepdims=True)

    @pl.when((ps == 0) & (b == 0))
    def _():
        cnt_ref[...] = jnp.zeros(cnt_ref.shape, F32)

    @pl.when(ps == 0)
    def _():
        cnt_ref[...] = cnt_ref[...] + colsum

    @pl.when((ps == 1) & (b == 0))
    def _():
        blocks = jnp.floor((cnt_ref[...] + (MOE_BLOCK - 0.5)) / MOE_BLOCK)
        r = lax.broadcasted_iota(jnp.int32, (LANES, LANES), 0)
        c = lax.broadcasted_iota(jnp.int32, (LANES, LANES), 1)
        before = (r < c).astype(F32)
        base_ref[...] = jnp.dot(blocks * MOE_BLOCK, before, preferred_element_type=F32,
                                precision=lax.Precision.HIGHEST)
        carry_ref[...] = jnp.zeros(carry_ref.shape, F32)

    @pl.when(ps == 1)
    def _():
        earlier = jnp.dot(tri_ref[...], both.astype(BF16), preferred_element_type=F32)
        row = earlier + (base_ref[0:1, :] + carry_ref[0:1, :])
        s0 = jnp.sum(oh[0] * row, axis=-1, keepdims=True)
        s1 = jnp.sum(oh[1] * row, axis=-1, keepdims=True)
        pos_ref[...] = jnp.where(lane == 0, s0, jnp.where(lane == 1, s1, 0.0))
        carry_ref[...] = carry_ref[...] + colsum


def _slots(rt, tb):
    s = rt.shape[0]
    tri = (np.arange(tb)[:, None] > np.arange(tb)[None, :]).astype(np.float32)
    return pl.pallas_call(
        _slots_kernel,
        grid=(2, s // tb),
        in_specs=[
            pl.BlockSpec((tb, LANES), lambda ps, b: (b, 0)),
            pl.BlockSpec((tb, tb), lambda ps, b: (0, 0)),
        ],
        out_specs=[
            pl.BlockSpec((tb, LANES), lambda ps, b: (b * ps, 0)),
            pl.BlockSpec((8, LANES), lambda ps, b: (0, 0)),
        ],
        out_shape=[
            jax.ShapeDtypeStruct((s, LANES), F32),
            jax.ShapeDtypeStruct((8, LANES), F32),
        ],
        scratch_shapes=[pltpu.VMEM((8, LANES), F32), pltpu.VMEM((8, LANES), F32)],
        compiler_params=_cparams(("arbitrary", "arbitrary")),
        name="moe_slots",
    )(rt, jnp.asarray(tri, BF16))


def _dispatch(rt, n_tok):
    a = n_tok * TOP_K
    nb = -(-a // MOE_BLOCK) + N_EXPERTS
    pos_f, cnt = _slots(rt, 512)
    pos = pos_f[:, :TOP_K].astype(jnp.int32).reshape(-1)
    counts = cnt[0, :N_EXPERTS].astype(jnp.int32)
    nblk = (counts + MOE_BLOCK - 1) // MOE_BLOCK
    bends = jnp.cumsum(nblk)
    bidx = jnp.arange(nb, dtype=jnp.int32)
    block_e = jnp.clip(jnp.searchsorted(bends, bidx, side="right"), 0, N_EXPERTS - 1).astype(jnp.int32)
    nused = bends[-1].astype(jnp.int32)
    within = bidx - (bends - nblk)[block_e]
    nvalid = jnp.where(bidx < nused, jnp.clip(counts[block_e] - within * MOE_BLOCK, 0, MOE_BLOCK), 0).astype(jnp.int32)
    t_flat = jnp.repeat(jnp.arange(n_tok, dtype=jnp.int32), TOP_K)
    slot_tok = jnp.zeros((nb * MOE_BLOCK,), jnp.int32).at[pos].set(t_flat)
    return block_e, nvalid, slot_tok, nused.reshape(1), pos


def kernel(x, p, positions, attn_norm, w_in, q_norm, w_uq, kv_norm, w_ukv, mla_norm, hg_lb_logits, hg_norm, w_out, ffn_norm, w_router_group, b_router_group, w_router_expert, b_router_expert, w_exp_gate, w_exp_up, w_exp_down, ple_norm, w_ple_gate, b_ple_gate, w_ple_proj, final_norm):
    bsz, s, d = x.shape
    assert bsz == 1 and w_in.shape[0] == 1
    xt = x[0]

    inv_freq = 1.0 / (ROPE_THETA ** (jnp.arange(0, QK_ROPE, 2, dtype=F32) / QK_ROPE))
    ang = positions[0].astype(F32)[:, None] * inv_freq
    cos, sin = jnp.cos(ang), jnp.sin(ang)
    zpad = jnp.zeros((s, LANES - QK_ROPE), F32)
    cc = jnp.concatenate([cos, cos, zpad], axis=1)
    ss = jnp.concatenate([-sin, sin, zpad], axis=1)

    lb = jnp.cumsum(jax.nn.softmax(hg_lb_logits.astype(F32), axis=0), axis=0)[0][None, :]

    wi = w_in[0]
    kr0 = Q_LORA + KV_LORA
    half = QK_ROPE // 2
    w_lat = jnp.concatenate(
        [wi[:, :kr0 + QK_ROPE], wi[:, kr0 + half:kr0 + QK_ROPE], wi[:, kr0:kr0 + half]], axis=1).astype(BF16)
    w_hg = wi[:, kr0 + QK_ROPE:].astype(BF16)
    wq3 = w_uq[0].reshape(Q_LORA, MLA_HEADS, QK_HEAD)
    wq_pad = jnp.concatenate(
        [wq3, wq3[:, :, QK_NOPE + half:], wq3[:, :, QK_NOPE:QK_NOPE + half]], axis=2
    ).reshape(Q_LORA, MLA_HEADS * QK_PAD).astype(BF16)
    wkv3 = w_ukv[0].reshape(KV_LORA, MLA_HEADS, QK_NOPE + V_HEAD)
    wkn = wkv3[:, :, :QK_NOPE].reshape(KV_LORA, MLA_HEADS * QK_NOPE).astype(BF16)
    wvt = wkv3[:, :, QK_NOPE:].reshape(KV_LORA, D_MLA).T.astype(BF16)
    wo = w_out[0].astype(BF16)
    wr = jnp.concatenate(
        [w_router_group[0], w_router_expert[0], jnp.zeros((d, LANES - N_GROUPS - N_EXPERTS), F32)], axis=1)
    br = jnp.concatenate(
        [b_router_group[0], b_router_expert[0], jnp.zeros((LANES - N_GROUPS - N_EXPERTS,), F32)])[None, :]

    lat = _in_proj(xt, attn_norm, w_lat, F32, 512, D_LAT, "in_proj_latent")
    hg4 = _in_proj(xt, attn_norm, w_hg, BF16, 1024, 1024, "in_proj_hgrn")
    q, k, vt = _mla_up(lat, q_norm, kv_norm, wq_pad, wkn, wvt, cc, ss, 512)
    o_mla = _attention(q, k, vt, mla_norm, 512, 512)
    mall, lv = _hgrn_tables()
    o_hg = _hgrn(hg4, lb, hg_norm[0].reshape(1, D_HG), mall, lv)

    h1, hn, rt = _out_route(xt, o_mla, o_hg, wo[:D_MLA], wo[D_MLA:], ffn_norm, wr, br, 256)
    block_e, nvalid, slot_tok, nused, pos = _dispatch(rt, s)
    y_slots = _moe(block_e, nvalid, slot_tok, nused, hn, w_exp_gate[0], w_exp_up[0], w_exp_down[0])

    out = _ple_final(pos, h1, rt, y_slots, p[0, 0], w_ple_gate[0].astype(BF16), b_ple_gate,
                     w_ple_proj[0].astype(BF16), ple_norm, final_norm[None, :], 256)
    return out[None]
```

```python
import functools

import jax
import jax.numpy as jnp
import numpy as np
from jax import lax
from jax.experimental import pallas as pl
from jax.experimental.pallas import tpu as pltpu

F32 = jnp.float32
BF16 = jnp.bfloat16

D_MODEL = 2048
PLE_DIM = 256
MLA_HEADS = 8
QK_NOPE = 128
QK_ROPE = 64
QK_HEAD = QK_NOPE + QK_ROPE
QK_PAD = 256
V_HEAD = 128
VT_ROWS = V_HEAD + 16
QK_AHEAD = 2
PV_LAG = 1
Q_LORA = 512
KV_LORA = 256
ROPE_THETA = 10000.0
HG_HEADS = 8
HG_DK = 128
HG_DV = 128
HG_CHUNK = 64
D_MLA = MLA_HEADS * V_HEAD
D_HG = HG_HEADS * HG_DV
N_GROUPS = 8
EXPERTS_PER_GROUP = 8
N_EXPERTS = N_GROUPS * EXPERTS_PER_GROUP
TOP_K = 2
D_EXPERT = 512
EPS = 1e-6
LANES = 128
D_LAT = Q_LORA + KV_LORA + 2 * QK_ROPE
NEG_BIG = -1e30
LOG2E = 1.4426950408889634

MOE_BLOCK = 320
GATHER_CHUNK = 32
HG_ROWS = 256
HG_LEVELS = (32, 16, 8, 4, 2, 1)


def _cparams(sem, vmem_mb=None):
    kw = dict(dimension_semantics=sem)
    if vmem_mb is not None:
        kw["vmem_limit_bytes"] = vmem_mb * 1024 * 1024
    return pltpu.CompilerParams(**kw)


def _rms(x, g):
    ms = jnp.mean(x * x, axis=-1, keepdims=True)
    return x * lax.rsqrt(ms + EPS) * g


def _sigmoid(x):
    return 1.0 / (1.0 + jnp.exp(-x))


def _in_proj_kernel(x_ref, g_ref, w_ref, o_ref, xn_ref):
    @pl.when(pl.program_id(1) == 0)
    def _():
        xn_ref[...] = _rms(x_ref[...], g_ref[...]).astype(BF16)

    o_ref[...] = jnp.dot(xn_ref[...], w_ref[...], preferred_element_type=F32).astype(o_ref.dtype)


def _in_proj(x, gain, w, out_dtype, tm, tn, name):
    s, d = x.shape
    n = w.shape[1]
    return pl.pallas_call(
        _in_proj_kernel,
        grid=(s // tm, n // tn),
        in_specs=[
            pl.BlockSpec((tm, d), lambda i, j: (i, 0)),
            pl.BlockSpec((1, d), lambda i, j: (0, 0)),
            pl.BlockSpec((d, tn), lambda i, j: (0, j)),
        ],
        out_specs=pl.BlockSpec((tm, tn), lambda i, j: (i, j)),
        out_shape=jax.ShapeDtypeStruct((s, n), out_dtype),
        scratch_shapes=[pltpu.VMEM((tm, d), BF16)],
        compiler_params=_cparams(("parallel", "arbitrary"), 48),
        name=name,
    )(x, gain, w)


def _mla_up_kernel(lat_ref, qn_ref, kvn_ref, wq_ref, wkn_ref, wvt_ref, cc_ref, ss_ref,
                   q_ref, k_ref, vt_ref):
    lat = lat_ref[...]
    cq = _rms(lat[:, :Q_LORA], qn_ref[...]).astype(BF16)
    ckv = _rms(lat[:, Q_LORA:Q_LORA + KV_LORA], kvn_ref[...]).astype(BF16)
    kp = lat[:, Q_LORA + KV_LORA:]
    cc = cc_ref[...]
    ss = ss_ref[...]
    scale = QK_HEAD ** -0.5 * LOG2E

    q = jnp.dot(cq, wq_ref[...], preferred_element_type=F32)
    for h in range(MLA_HEADS):
        base = h * QK_PAD
        q_ref[:, base:base + QK_NOPE] = (q[:, base:base + QK_NOPE] * scale).astype(BF16)
        r = q[:, base + QK_NOPE:base + QK_PAD]
        r = (r * cc + pltpu.roll(r, QK_ROPE, 1) * ss) * scale
        q_ref[:, base + QK_NOPE:base + QK_PAD] = r.astype(BF16)

    kpe = (kp * cc + pltpu.roll(kp, QK_ROPE, 1) * ss).astype(BF16)
    kn = jnp.dot(ckv, wkn_ref[...], preferred_element_type=F32).astype(BF16)
    for h in range(MLA_HEADS):
        base = h * QK_PAD
        k_ref[:, base:base + QK_NOPE] = kn[:, h * QK_NOPE:(h + 1) * QK_NOPE]
        k_ref[:, base + QK_NOPE:base + QK_PAD] = kpe
    vt = lax.dot_general(wvt_ref[...], ckv, (((1,), (1,)), ((), ())), preferred_element_type=F32).astype(BF16)
    ones = jnp.ones((VT_ROWS - V_HEAD, vt.shape[1]), BF16)
    for h in range(MLA_HEADS):
        vt_ref[h * VT_ROWS:h * VT_ROWS + V_HEAD, :] = vt[h * V_HEAD:(h + 1) * V_HEAD]
        vt_ref[h * VT_ROWS + V_HEAD:(h + 1) * VT_ROWS, :] = ones


def _mla_up(lat, qn, kvn, wq, wkn, wvt, cc, ss, tm):
    s = lat.shape[0]
    row = lambda i: (i, 0)
    full = lambda i: (0, 0)
    return pl.pallas_call(
        _mla_up_kernel,
        grid=(s // tm,),
        in_specs=[
            pl.BlockSpec((tm, D_LAT), row),
            pl.BlockSpec((1, Q_LORA), full),
            pl.BlockSpec((1, KV_LORA), full),
            pl.BlockSpec(wq.shape, full),
            pl.BlockSpec(wkn.shape, full),
            pl.BlockSpec(wvt.shape, full),
            pl.BlockSpec((tm, LANES), row),
            pl.BlockSpec((tm, LANES), row),
        ],
        out_specs=[
            pl.BlockSpec((tm, MLA_HEADS * QK_PAD), row),
            pl.BlockSpec((tm, MLA_HEADS * QK_PAD), row),
            pl.BlockSpec((MLA_HEADS * VT_ROWS, tm), lambda i: (0, i)),
        ],
        out_shape=[
            jax.ShapeDtypeStruct((s, MLA_HEADS * QK_PAD), BF16),
            jax.ShapeDtypeStruct((s, MLA_HEADS * QK_PAD), BF16),
            jax.ShapeDtypeStruct((MLA_HEADS * VT_ROWS, s), BF16),
        ],
        compiler_params=_cparams(("parallel",), 48),
        name="mla_up",
    )(lat, qn, kvn, wq, wkn, wvt, cc, ss)


def _attn_kernel(it_ref, jt_ref, q_ref, k_ref, vt_ref, g_ref, o_ref, acc_ref, m_ref, *, tq, tk, qc):
    t = pl.program_id(0)
    i = it_ref[t]
    j = jt_ref[t]

    @pl.when(j == 0)
    def _():
        m_ref[...] = jnp.full(m_ref.shape, NEG_BIG, F32)
        acc_ref[...] = jnp.zeros(acc_ref.shape, F32)

    units = [(h, c) for h in range(MLA_HEADS) for c in range(tq // qc)]

    def scores(u):
        h, c = u
        q = q_ref[c * qc:(c + 1) * qc, h * QK_PAD:(h + 1) * QK_PAD]
        k = k_ref[:, h * QK_PAD:(h + 1) * QK_PAD]
        return lax.dot_general(k, q, (((1,), (1,)), ((), ())), preferred_element_type=F32)

    def step(masked):
        if masked:
            diff = lax.broadcasted_iota(jnp.int32, (tk, qc), 0) - lax.broadcasted_iota(jnp.int32, (tk, qc), 1)
        def accumulate(h, cols, alpha, p):
            pv = jnp.dot(vt_ref[h * VT_ROWS:(h + 1) * VT_ROWS, :], p, preferred_element_type=F32)
            acc_ref[h, :, cols] = alpha * acc_ref[h, :, cols] + pv

        ahead = [scores(units[n]) for n in range(min(QK_AHEAD, len(units)))]
        pending = []
        for n, (h, c) in enumerate(units):
            s = ahead.pop(0)
            if n + QK_AHEAD < len(units):
                ahead.append(scores(units[n + QK_AHEAD]))
            cols = slice(c * qc, (c + 1) * qc)
            if masked:
                s = jnp.where(diff <= c * qc, s, NEG_BIG)
            m_prev = m_ref[h, :, cols]
            m_new = jnp.maximum(m_prev, jnp.max(s, axis=0, keepdims=True))
            alpha = jnp.exp2(m_prev - m_new)
            p = jnp.exp2(s - m_new).astype(BF16)
            m_ref[h, :, cols] = m_new
            pending.append((h, cols, alpha, p))
            if len(pending) > PV_LAG:
                accumulate(*pending.pop(0))
        for item in pending:
            accumulate(*item)

    @pl.when(j < i)
    def _():
        step(False)

    @pl.when(j == i)
    def _():
        step(True)
        outs = [acc_ref[h, :V_HEAD, :] * (1.0 / acc_ref[h, V_HEAD:V_HEAD + 1, :]) for h in range(MLA_HEADS)]
        o_ref[...] = _rms(jnp.concatenate(outs, axis=0).T, g_ref[...]).astype(o_ref.dtype)


def _attention(q, k, vt, gain, tb, qc):
    s = q.shape[0]
    nq = s // tb
    it = np.concatenate([np.full(i + 1, i, np.int32) for i in range(nq)])
    jt = np.concatenate([np.arange(i + 1, dtype=np.int32) for i in range(nq)])
    grid_spec = pltpu.PrefetchScalarGridSpec(
        num_scalar_prefetch=2,
        grid=(it.shape[0],),
        in_specs=[
            pl.BlockSpec((tb, MLA_HEADS * QK_PAD), lambda t, it, jt: (it[t], 0)),
            pl.BlockSpec((tb, MLA_HEADS * QK_PAD), lambda t, it, jt: (jt[t], 0)),
            pl.BlockSpec((MLA_HEADS * VT_ROWS, tb), lambda t, it, jt: (0, jt[t])),
            pl.BlockSpec((1, D_MLA), lambda t, it, jt: (0, 0)),
        ],
        out_specs=pl.BlockSpec((tb, D_MLA), lambda t, it, jt: (it[t], 0)),
        scratch_shapes=[
            pltpu.VMEM((MLA_HEADS, VT_ROWS, tb), F32),
            pltpu.VMEM((MLA_HEADS, 1, tb), F32),
        ],
    )
    return pl.pallas_call(
        functools.partial(_attn_kernel, tq=tb, tk=tb, qc=qc),
        grid_spec=grid_spec,
        out_shape=jax.ShapeDtypeStruct((s, D_MLA), BF16),
        compiler_params=_cparams(("arbitrary",), 48),
        name="mla_attention",
    )(jnp.asarray(it), jnp.asarray(jt), q, k, vt, gain)


def _hgrn_tables():
    n = HG_ROWS
    r = np.arange(n)
    c = np.arange(n)
    same = (r[:, None] // HG_CHUNK) == (c[None, :] // HG_CHUNK)

    def rows_upto(idx):
        return (same & (c[None, :] <= idx[:, None])).astype(np.float32)

    blocks = [rows_upto(r)]
    for m in HG_LEVELS:
        blocks.append(rows_upto((r // (2 * m)) * (2 * m) + m))
    blocks.append(rows_upto((r // HG_CHUNK) * HG_CHUNK + HG_CHUNK - 1))
    mall = np.concatenate(blocks, axis=0)

    x = r[:, None] ^ c[None, :]
    lv = np.full((n, n), -1, np.int32)
    for li, m in enumerate(HG_LEVELS):
        lv = np.where(same & (r[:, None] > c[None, :]) & (x >= m) & (x < 2 * m), li, lv)
    lv = np.where(r[:, None] == c[None, :], len(HG_LEVELS), lv)
    return jnp.asarray(mall, BF16), jnp.asarray(lv, jnp.int32)


def _hgrn_kernel(q_ref, f_ref, i_ref, g_ref, lb_ref, gn_ref, mall_ref, lv_ref, o_ref, st_ref):
    t = pl.program_id(1)
    n = HG_ROWS

    @pl.when(t == 0)
    def _():
        st_ref[...] = jnp.zeros(st_ref.shape, F32)

    q_in = q_ref[...].astype(F32)
    qs = q_in * _sigmoid(q_in)
    lb = lb_ref[...]
    f = lb + (1.0 - lb) * _sigmoid(f_ref[...].astype(F32))
    kk = 1.0 - f
    logf = jnp.log(f)
    iv = i_ref[...]

    l1 = logf.astype(BF16)
    r1 = logf - l1.astype(F32)
    l2 = r1.astype(BF16)
    l3 = (r1 - l2.astype(F32)).astype(BF16)
    parts = jnp.dot(mall_ref[...], jnp.concatenate([l1, l2, l3], axis=1), preferred_element_type=F32)
    bc = parts[:, :HG_DK] + parts[:, HG_DK:2 * HG_DK] + parts[:, 2 * HG_DK:]
    b = bc[:n]

    rowid = lax.broadcasted_iota(jnp.int32, (n, 1), 0)
    lv = lv_ref[...]
    a = jnp.zeros((n, n), F32)
    for li, m in enumerate(HG_LEVELS):
        c = bc[(li + 1) * n:(li + 2) * n]
        e = jnp.exp(-jnp.abs(b - c))
        upper = (rowid & (2 * m - 1)) >= m
        x = (jnp.where(upper, qs, kk) * e).astype(BF16)
        p = lax.dot_general(x, x, (((1,), (1,)), ((), ())), preferred_element_type=F32)
        a = jnp.where(lv == li, p, a)
    a = jnp.where(lv == len(HG_LEVELS), jnp.sum(qs * kk, axis=-1, keepdims=True), a)
    o = jnp.dot(a.astype(BF16), iv, preferred_element_type=F32)

    blast = bc[(len(HG_LEVELS) + 1) * n:]
    qd = (qs * jnp.exp(b)).astype(BF16)
    kd = kk * jnp.exp(blast - b)
    ivt = iv.astype(F32).T.astype(BF16)
    st = st_ref[...]
    outs = []
    for ci in range(n // HG_CHUNK):
        lo = ci * HG_CHUNK
        inter = lax.dot_general(qd[lo:lo + HG_CHUNK], st.astype(BF16), (((1,), (1,)), ((), ())),
                                preferred_element_type=F32)
        outs.append(o[lo:lo + HG_CHUNK] + inter)
        kd_c = jnp.where((rowid >= lo) & (rowid < lo + HG_CHUNK), kd, 0.0).astype(BF16)
        upd = jnp.dot(ivt, kd_c, preferred_element_type=F32)
        st = jnp.exp(blast[lo:lo + 1]) * st + upd
    st_ref[...] = st
    o = jnp.concatenate(outs, axis=0)

    g_in = g_ref[...].astype(F32)
    o_ref[...] = (_rms(o, gn_ref[...]) * (g_in * _sigmoid(g_in))).astype(o_ref.dtype)


def _hgrn(hg4, lb, gn, mall, lv):
    s = hg4.shape[0]
    h = HG_HEADS

    def col(c):
        return lambda hh, t: (t, c * h + hh)

    head = lambda hh, t: (0, hh)
    const = lambda hh, t: (0, 0)
    return pl.pallas_call(
        _hgrn_kernel,
        grid=(h, s // HG_ROWS),
        in_specs=[
            pl.BlockSpec((HG_ROWS, HG_DK), col(0)),
            pl.BlockSpec((HG_ROWS, HG_DK), col(1)),
            pl.BlockSpec((HG_ROWS, HG_DV), col(2)),
            pl.BlockSpec((HG_ROWS, HG_DV), col(3)),
            pl.BlockSpec((1, HG_DK), head),
            pl.BlockSpec((1, HG_DV), head),
            pl.BlockSpec(mall.shape, const),
            pl.BlockSpec(lv.shape, const),
        ],
        out_specs=pl.BlockSpec((HG_ROWS, HG_DV), lambda hh, t: (t, hh)),
        out_shape=jax.ShapeDtypeStruct((s, D_HG), BF16),
        scratch_shapes=[pltpu.VMEM((HG_DV, HG_DK), F32)],
        compiler_params=_cparams(("parallel", "arbitrary"), 32),
        name="hgrn2",
    )(hg4, hg4, hg4, hg4, lb, gn, mall, lv)


def _out_route_kernel(x_ref, om_ref, oh_ref, wa_ref, wb_ref, g_ref, wr_ref, br_ref,
                      h_ref, hn_ref, rt_ref):
    h1 = (x_ref[...]
          + jnp.dot(om_ref[...], wa_ref[...], preferred_element_type=F32)
          + jnp.dot(oh_ref[...], wb_ref[...], preferred_element_type=F32))
    h_ref[...] = h1
    hn = _rms(h1, g_ref[...])
    hn_ref[...] = hn
    logits = jnp.dot(hn, wr_ref[...], preferred_element_type=F32,
                     precision=lax.Precision.HIGHEST) + br_ref[...]

    lane = lax.broadcasted_iota(jnp.int32, logits.shape, 1)
    lanef = lane.astype(F32)
    ninf = -jnp.inf
    big = float(LANES)

    is_g = lane < N_GROUPS
    gl = jnp.where(is_g, logits, ninf)
    gmax = jnp.max(gl, axis=-1, keepdims=True)
    gsum = jnp.sum(jnp.where(is_g, jnp.exp(gl - gmax), 0.0), axis=-1, keepdims=True)
    g_w = 1.0 / gsum
    g_idx = jnp.min(jnp.where(gl == gmax, lanef, big), axis=-1, keepdims=True)

    e_lane = lane - N_GROUPS
    in_grp = (e_lane >= 0) & (e_lane < N_EXPERTS) & ((e_lane >> 3).astype(F32) == g_idx)
    el = jnp.where(in_grp, logits, ninf)
    emax = jnp.max(el, axis=-1, keepdims=True)
    esum = jnp.sum(jnp.where(in_grp, jnp.exp(el - emax), 0.0), axis=-1, keepdims=True)
    i1 = jnp.min(jnp.where(el == emax, lanef, big), axis=-1, keepdims=True)
    el2 = jnp.where(lanef == i1, ninf, el)
    emax2 = jnp.max(el2, axis=-1, keepdims=True)
    i2 = jnp.min(jnp.where(el2 == emax2, lanef, big), axis=-1, keepdims=True)
    p1 = 1.0 / esum
    p2 = jnp.exp(emax2 - emax) / esum
    w1 = g_w * p1 / (p1 + p2)
    w2 = g_w * p2 / (p1 + p2)

    rt = jnp.where(lane == 0, i1 - N_GROUPS,
                   jnp.where(lane == 1, i2 - N_GROUPS,
                             jnp.where(lane == 2, w1, jnp.where(lane == 3, w2, 0.0))))
    rt_ref[...] = rt


def _out_route(x, o_mla, o_hg, wa, wb, gain, wr, br, tm):
    s, d = x.shape
    row = lambda i: (i, 0)
    full = lambda i: (0, 0)
    return pl.pallas_call(
        _out_route_kernel,
        grid=(s // tm,),
        in_specs=[
            pl.BlockSpec((tm, d), row),
            pl.BlockSpec((tm, D_MLA), row),
            pl.BlockSpec((tm, D_HG), row),
            pl.BlockSpec(wa.shape, full),
            pl.BlockSpec(wb.shape, full),
            pl.BlockSpec((1, d), full),
            pl.BlockSpec(wr.shape, full),
            pl.BlockSpec((1, LANES), full),
        ],
        out_specs=[
            pl.BlockSpec((tm, d), row),
            pl.BlockSpec((tm, d), row),
            pl.BlockSpec((tm, LANES), row),
        ],
        out_shape=[
            jax.ShapeDtypeStruct((s, d), F32),
            jax.ShapeDtypeStruct((s, d), F32),
            jax.ShapeDtypeStruct((s, LANES), F32),
        ],
        compiler_params=_cparams(("parallel",), 48),
        name="out_proj_route",
    )(x, o_mla, o_hg, wa, wb, gain, wr, br)


def _moe_kernel(be_ref, nv_ref, tok_ref, nu_ref, hn_hbm, wg_hbm, wu_hbm, wd_hbm, o_ref,
                xbuf, xsem, wgb, wub, wdb, wsem, wslot):
    i = pl.program_id(0)
    nused = nu_ref[0]
    e = be_ref[i]
    active = i < nused
    first = (i == 0) | (e != be_ref[jnp.maximum(i - 1, 0)])

    def gather(blk, sl, wait):
        nv = nv_ref[blk]
        for lo in range(0, MOE_BLOCK, GATHER_CHUNK):
            @pl.when(lo < nv)
            def _():
                if wait:
                    pltpu.make_async_copy(hn_hbm.at[pl.ds(0, GATHER_CHUNK), :],
                                          xbuf.at[sl, pl.ds(lo, GATHER_CHUNK), :], xsem.at[sl]).wait()
                else:
                    for r in range(lo, lo + GATHER_CHUNK):
                        tok = tok_ref[blk * MOE_BLOCK + r]
                        pltpu.make_async_copy(hn_hbm.at[pl.ds(tok, 1), :], xbuf.at[sl, pl.ds(r, 1), :],
                                              xsem.at[sl]).start(priority=r % 2)

    def on_slot(val, fn):
        for s_ in (0, 1):
            pl.when(val == s_)(functools.partial(fn, s_))

    def weight_copies(ex, sl):
        return (pltpu.make_async_copy(wg_hbm.at[ex], wgb.at[sl], wsem.at[sl, 0]),
                pltpu.make_async_copy(wu_hbm.at[ex], wub.at[sl], wsem.at[sl, 1]),
                pltpu.make_async_copy(wd_hbm.at[ex], wdb.at[sl], wsem.at[sl, 2]))

    @pl.when((i == 0) & active)
    def _():
        xbuf[...] = jnp.zeros(xbuf.shape, F32)
        wslot[0] = 1
        for cp in weight_copies(e, 0):
            cp.start()
        gather(0, 0, wait=False)

    @pl.when(active & first)
    def _():
        sl = 1 - wslot[0]
        wslot[0] = sl
        nxt = lax.while_loop(lambda j: (j < nused) & (be_ref[jnp.minimum(j, nused - 1)] == e),
                             lambda j: j + 1, i + 1)

        @pl.when(nxt < nused)
        def _():
            for cp in weight_copies(be_ref[nxt], 1 - sl):
                cp.start()

        for cp in weight_copies(e, sl):
            cp.wait()

    @pl.when(active)
    def _():
        sl = wslot[0]
        xs = i % 2

        @pl.when(i + 1 < nused)
        def _():
            on_slot(xs, lambda s_: gather(i + 1, 1 - s_, wait=False))

        on_slot(xs, lambda s_: gather(i, s_, wait=True))
        x = xbuf[xs].astype(BF16)
        g = jnp.dot(x, wgb[sl].astype(BF16), preferred_element_type=F32)
        u = jnp.dot(x, wub[sl].astype(BF16), preferred_element_type=F32)
        hmid = (g * _sigmoid(g) * u).astype(BF16)
        o_ref[...] = jnp.dot(hmid, wdb[sl].astype(BF16), preferred_element_type=F32)

    @pl.when(jnp.logical_not(active))
    def _():
        o_ref[...] = jnp.zeros(o_ref.shape, F32)


def _moe(block_e, nvalid, slot_tok, nused, hn, wg, wu, wd):
    nb = block_e.shape[0]
    d = hn.shape[1]
    blk = lambda i, be, nv, tok, nu: (i, 0)
    grid_spec = pltpu.PrefetchScalarGridSpec(
        num_scalar_prefetch=4,
        grid=(nb,),
        in_specs=[pl.BlockSpec(memory_space=pl.ANY)] * 4,
        out_specs=pl.BlockSpec((MOE_BLOCK, d), blk),
        scratch_shapes=[
            pltpu.VMEM((2, MOE_BLOCK, d), F32),
            pltpu.SemaphoreType.DMA((2,)),
            pltpu.VMEM((2, d, D_EXPERT), F32),
            pltpu.VMEM((2, d, D_EXPERT), F32),
            pltpu.VMEM((2, D_EXPERT, d), F32),
            pltpu.SemaphoreType.DMA((2, 3)),
            pltpu.SMEM((1,), jnp.int32),
        ],
    )
    return pl.pallas_call(
        _moe_kernel,
        grid_spec=grid_spec,
        out_shape=jax.ShapeDtypeStruct((nb * MOE_BLOCK, d), F32),
        compiler_params=_cparams(("arbitrary",), 56),
        name="moe_experts",
    )(block_e, nvalid, slot_tok, nused, hn, wg, wu, wd)


def _ple_kernel(pos_ref, h_ref, rt_ref, y_hbm, p_ref, wg_ref, bg_ref, wp_ref, gp_ref, gf_ref, o_ref,
                ybuf, sem, *, tm):
    i = pl.program_id(0)
    n = pl.num_programs(0)

    def gather(step, sl, wait):
        for kk in range(TOP_K):
            if wait:
                pltpu.make_async_copy(y_hbm.at[pl.ds(0, tm), :], ybuf.at[sl, kk], sem.at[sl]).wait()
            else:
                for r in range(tm):
                    src = pos_ref[(step * tm + r) * TOP_K + kk]
                    pltpu.make_async_copy(y_hbm.at[pl.ds(src, 1), :], ybuf.at[sl, kk, pl.ds(r, 1), :],
                                          sem.at[sl]).start(priority=r % 2)

    def on_slot(val, fn):
        for s_ in (0, 1):
            pl.when(val == s_)(functools.partial(fn, s_))

    @pl.when(i == 0)
    def _():
        gather(0, 0, wait=False)

    sl = i % 2

    @pl.when(i + 1 < n)
    def _():
        on_slot(sl, lambda s_: gather(i + 1, 1 - s_, wait=False))

    pe = jnp.dot(p_ref[...].astype(BF16), wp_ref[...], preferred_element_type=F32)
    on_slot(sl, lambda s_: gather(i, s_, wait=True))

    rt = rt_ref[...]
    h2 = h_ref[...] + (rt[:, TOP_K:TOP_K + 1] * ybuf[sl, 0] + rt[:, TOP_K + 1:TOP_K + 2] * ybuf[sl, 1])
    hn = _rms(h2, gp_ref[...]).astype(BF16)
    gate = _sigmoid(jnp.dot(hn, wg_ref[...], preferred_element_type=F32) + bg_ref[...])
    h3 = h2 + gate * pe
    o_ref[...] = _rms(h3, gf_ref[...])


def _ple_final(pos, h1, rt, y_slots, p, wg, bg, wp, gp, gf, tm):
    s, d = h1.shape
    row = lambda i, pos: (i, 0)
    full = lambda i, pos: (0, 0)
    grid_spec = pltpu.PrefetchScalarGridSpec(
        num_scalar_prefetch=1,
        grid=(s // tm,),
        in_specs=[
            pl.BlockSpec((tm, d), row),
            pl.BlockSpec((tm, LANES), row),
            pl.BlockSpec(memory_space=pl.ANY),
            pl.BlockSpec((tm, PLE_DIM), row),
            pl.BlockSpec(wg.shape, full),
            pl.BlockSpec((1, d), full),
            pl.BlockSpec(wp.shape, full),
            pl.BlockSpec((1, d), full),
            pl.BlockSpec((1, d), full),
        ],
        out_specs=pl.BlockSpec((tm, d), row),
        scratch_shapes=[
            pltpu.VMEM((2, TOP_K, tm, d), F32),
            pltpu.SemaphoreType.DMA((2,)),
        ],
    )
    return pl.pallas_call(
        functools.partial(_ple_kernel, tm=tm),
        grid_spec=grid_spec,
        out_shape=jax.ShapeDtypeStruct((s, d), F32),
        compiler_params=_cparams(("arbitrary",), 48),
        name="ple_final",
    )(pos, h1, rt, y_slots, p, wg, bg, wp, gp, gf)


def _slots_kernel(rt_ref, tri_ref, pos_ref, cnt_ref, carry_ref, base_ref):
    ps = pl.program_id(0)
    b = pl.program_id(1)
    rt = rt_ref[...]
    lane = lax.broadcasted_iota(jnp.int32, rt.shape, 1)
    lanef = lane.astype(F32)
    oh = [(lanef == rt[:, kk:kk + 1]).astype(F32) for kk in range(TOP_K)]
    both = oh[0] + oh[1]
    colsum = jnp.sum(both, axis=0, keepdims=True)

    @pl.when((ps == 0) & (b == 0))
    def _():
        cnt_ref[...] = jnp.zeros(cnt_ref.shape, F32)

    @pl.when(ps == 0)
    def _():
        cnt_ref[...] = cnt_ref[...] + colsum

    @pl.when((ps == 1) & (b == 0))
    def _():
        blocks = jnp.floor((cnt_ref[...] + (MOE_BLOCK - 0.5)) / MOE_BLOCK)
        r = lax.broadcasted_iota(jnp.int32, (LANES, LANES), 0)
        c = lax.broadcasted_iota(jnp.int32, (LANES, LANES), 1)
        before = (r < c).astype(F32)
        base_ref[...] = jnp.dot(blocks * MOE_BLOCK, before, preferred_element_type=F32,
                                precision=lax.Precision.HIGHEST)
        carry_ref[...] = jnp.zeros(carry_ref.shape, F32)

    @pl.when(ps == 1)
    def _():
        earlier = jnp.dot(tri_ref[...], both.astype(BF16), preferred_element_type=F32)
        row = earlier + (base_ref[0:1, :] + carry_ref[0:1, :])
        s0 = jnp.sum(oh[0] * row, axis=-1, keepdims=True)
        s1 = jnp.sum(oh[1] * row, axis=-1, keepdims=True)
        pos_ref[...] = jnp.where(lane == 0, s0, jnp.where(lane == 1, s1, 0.0))
        carry_ref[...] = carry_ref[...] + colsum


def _slots(rt, tb):
    s = rt.shape[0]
    tri = (np.arange(tb)[:, None] > np.arange(tb)[None, :]).astype(np.float32)
    return pl.pallas_call(
        _slots_kernel,
        grid=(2, s // tb),
        in_specs=[
            pl.BlockSpec((tb, LANES), lambda ps, b: (b, 0)),
            pl.BlockSpec((tb, tb), lambda ps, b: (0, 0)),
        ],
        out_specs=[
            pl.BlockSpec((tb, LANES), lambda ps, b: (b * ps, 0)),
            pl.BlockSpec((8, LANES), lambda ps, b: (0, 0)),
        ],
        out_shape=[
            jax.ShapeDtypeStruct((s, LANES), F32),
            jax.ShapeDtypeStruct((8, LANES), F32),
        ],
        scratch_shapes=[pltpu.VMEM((8, LANES), F32), pltpu.VMEM((8, LANES), F32)],
        compiler_params=_cparams(("arbitrary", "arbitrary")),
        name="moe_slots",
    )(rt, jnp.asarray(tri, BF16))


def _dispatch(rt, n_tok):
    a = n_tok * TOP_K
    nb = -(-a // MOE_BLOCK) + N_EXPERTS
    pos_f, cnt = _slots(rt, 512)
    pos = pos_f[:, :TOP_K].astype(jnp.int32).reshape(-1)
    counts = cnt[0, :N_EXPERTS].astype(jnp.int32)
    nblk = (counts + MOE_BLOCK - 1) // MOE_BLOCK
    bends = jnp.cumsum(nblk)
    bidx = jnp.arange(nb, dtype=jnp.int32)
    block_e = jnp.clip(jnp.searchsorted(bends, bidx, side="right"), 0, N_EXPERTS - 1).astype(jnp.int32)
    nused = bends[-1].astype(jnp.int32)
    within = bidx - (bends - nblk)[block_e]
    nvalid = jnp.where(bidx < nused, jnp.clip(counts[block_e] - within * MOE_BLOCK, 0, MOE_BLOCK), 0).astype(jnp.int32)
    t_flat = jnp.repeat(jnp.arange(n_tok, dtype=jnp.int32), TOP_K)
    slot_tok = jnp.zeros((nb * MOE_BLOCK,), jnp.int32).at[pos].set(t_flat)
    return block_e, nvalid, slot_tok, nused.reshape(1), pos


def kernel(x, p, positions, attn_norm, w_in, q_norm, w_uq, kv_norm, w_ukv, mla_norm, hg_lb_logits, hg_norm, w_out, ffn_norm, w_router_group, b_router_group, w_router_expert, b_router_expert, w_exp_gate, w_exp_up, w_exp_down, ple_norm, w_ple_gate, b_ple_gate, w_ple_proj, final_norm):
    bsz, s, d = x.shape
    assert bsz == 1 and w_in.shape[0] == 1
    xt = x[0]

    inv_freq = 1.0 / (ROPE_THETA ** (jnp.arange(0, QK_ROPE, 2, dtype=F32) / QK_ROPE))
    ang = positions[0].astype(F32)[:, None] * inv_freq
    cos, sin = jnp.cos(ang), jnp.sin(ang)
    zpad = jnp.zeros((s, LANES - QK_ROPE), F32)
    cc = jnp.concatenate([cos, cos, zpad], axis=1)
    ss = jnp.concatenate([-sin, sin, zpad], axis=1)

    lb = jnp.cumsum(jax.nn.softmax(hg_lb_logits.astype(F32), axis=0), axis=0)[0][None, :]

    wi = w_in[0]
    kr0 = Q_LORA + KV_LORA
    half = QK_ROPE // 2
    w_lat = jnp.concatenate(
        [wi[:, :kr0 + QK_ROPE], wi[:, kr0 + half:kr0 + QK_ROPE], wi[:, kr0:kr0 + half]], axis=1).astype(BF16)
    w_hg = wi[:, kr0 + QK_ROPE:].astype(BF16)
    wq3 = w_uq[0].reshape(Q_LORA, MLA_HEADS, QK_HEAD)
    wq_pad = jnp.concatenate(
        [wq3, wq3[:, :, QK_NOPE + half:], wq3[:, :, QK_NOPE:QK_NOPE + half]], axis=2
    ).reshape(Q_LORA, MLA_HEADS * QK_PAD).astype(BF16)
    wkv3 = w_ukv[0].reshape(KV_LORA, MLA_HEADS, QK_NOPE + V_HEAD)
    wkn = wkv3[:, :, :QK_NOPE].reshape(KV_LORA, MLA_HEADS * QK_NOPE).astype(BF16)
    wvt = wkv3[:, :, QK_NOPE:].reshape(KV_LORA, D_MLA).T.astype(BF16)
    wo = w_out[0].astype(BF16)
    wr = jnp.concatenate(
        [w_router_group[0], w_router_expert[0], jnp.zeros((d, LANES - N_GROUPS - N_EXPERTS), F32)], axis=1)
    br = jnp.concatenate(
        [b_router_group[0], b_router_expert[0], jnp.zeros((LANES - N_GROUPS - N_EXPERTS,), F32)])[None, :]

    lat = _in_proj(xt, attn_norm, w_lat, F32, 512, D_LAT, "in_proj_latent")
    hg4 = _in_proj(xt, attn_norm, w_hg, BF16, 1024, 1024, "in_proj_hgrn")
    q, k, vt = _mla_up(lat, q_norm, kv_norm, wq_pad, wkn, wvt, cc, ss, 512)
    o_mla = _attention(q, k, vt, mla_norm, 512, 512)
    mall, lv = _hgrn_tables()
    o_hg = _hgrn(hg4, lb, hg_norm[0].reshape(1, D_HG), mall, lv)

    h1, hn, rt = _out_route(xt, o_mla, o_hg, wo[:D_MLA], wo[D_MLA:], ffn_norm, wr, br, 256)
    block_e, nvalid, slot_tok, nused, pos = _dispatch(rt, s)
    y_slots = _moe(block_e, nvalid, slot_tok, nused, hn, w_exp_gate[0], w_exp_up[0], w_exp_down[0])

    out = _ple_final(pos, h1, rt, y_slots, p[0, 0], w_ple_gate[0].astype(BF16), b_ple_gate,
                     w_ple_proj[0].astype(BF16), ple_norm, final_norm[None, :], 256)
    return out[None]
```

```python
import functools

import jax
import jax.numpy as jnp
import numpy as np
from jax import lax
from jax.experimental import pallas as pl
from jax.experimental.pallas import tpu as pltpu

F32 = jnp.float32
BF16 = jnp.bfloat16

D_MODEL = 2048
PLE_DIM = 256
MLA_HEADS = 8
QK_NOPE = 128
QK_ROPE = 64
QK_HEAD = QK_NOPE + QK_ROPE
QK_PAD = 256
V_HEAD = 128
VT_ROWS = V_HEAD + 16
QK_AHEAD = 2
PV_LAG = 1
Q_LORA = 512
KV_LORA = 256
ROPE_THETA = 10000.0
HG_HEADS = 8
HG_DK = 128
HG_DV = 128
HG_CHUNK = 64
D_MLA = MLA_HEADS * V_HEAD
D_HG = HG_HEADS * HG_DV
N_GROUPS = 8
EXPERTS_PER_GROUP = 8
N_EXPERTS = N_GROUPS * EXPERTS_PER_GROUP
TOP_K = 2
D_EXPERT = 512
EPS = 1e-6
LANES = 128
D_LAT = Q_LORA + KV_LORA + 2 * QK_ROPE
NEG_BIG = -1e30
LOG2E = 1.4426950408889634

MOE_BLOCK = 320
GATHER_CHUNK = 32
HG_ROWS = 256
HG_LEVELS = (32, 16, 8, 4, 2, 1)
HG_MM_LEVELS = (2, 1)
HG_NH = 4


def _cparams(sem, vmem_mb=None):
    kw = dict(dimension_semantics=sem)
    if vmem_mb is not None:
        kw["vmem_limit_bytes"] = vmem_mb * 1024 * 1024
    return pltpu.CompilerParams(**kw)


def _rms(x, g):
    ms = jnp.mean(x * x, axis=-1, keepdims=True)
    return x * lax.rsqrt(ms + EPS) * g


def _sigmoid(x):
    return 1.0 / (1.0 + jnp.exp(-x))


def _in_proj_kernel(x_ref, g_ref, w_ref, o_ref, xn_ref):
    @pl.when(pl.program_id(1) == 0)
    def _():
        xn_ref[...] = _rms(x_ref[...], g_ref[...]).astype(BF16)

    o_ref[...] = jnp.dot(xn_ref[...], w_ref[...], preferred_element_type=F32).astype(o_ref.dtype)


def _in_proj(x, gain, w, out_dtype, tm, tn, name):
    s, d = x.shape
    n = w.shape[1]
    return pl.pallas_call(
        _in_proj_kernel,
        grid=(s // tm, n // tn),
        in_specs=[
            pl.BlockSpec((tm, d), lambda i, j: (i, 0)),
            pl.BlockSpec((1, d), lambda i, j: (0, 0)),
            pl.BlockSpec((d, tn), lambda i, j: (0, j)),
        ],
        out_specs=pl.BlockSpec((tm, tn), lambda i, j: (i, j)),
        out_shape=jax.ShapeDtypeStruct((s, n), out_dtype),
        scratch_shapes=[pltpu.VMEM((tm, d), BF16)],
        compiler_params=_cparams(("parallel", "arbitrary"), 48),
        name=name,
    )(x, gain, w)


def _mla_up_kernel(lat_ref, qn_ref, kvn_ref, wq_ref, wkn_ref, wvt_ref, cc_ref, ss_ref,
                   q_ref, k_ref, vt_ref):
    lat = lat_ref[...]
    cq = _rms(lat[:, :Q_LORA], qn_ref[...]).astype(BF16)
    ckv = _rms(lat[:, Q_LORA:Q_LORA + KV_LORA], kvn_ref[...]).astype(BF16)
    kp = lat[:, Q_LORA + KV_LORA:]
    cc = cc_ref[...]
    ss = ss_ref[...]
    scale = QK_HEAD ** -0.5 * LOG2E

    q = jnp.dot(cq, wq_ref[...], preferred_element_type=F32)
    for h in range(MLA_HEADS):
        base = h * QK_PAD
        q_ref[:, base:base + QK_NOPE] = (q[:, base:base + QK_NOPE] * scale).astype(BF16)
        r = q[:, base + QK_NOPE:base + QK_PAD]
        r = (r * cc + pltpu.roll(r, QK_ROPE, 1) * ss) * scale
        q_ref[:, base + QK_NOPE:base + QK_PAD] = r.astype(BF16)

    kpe = (kp * cc + pltpu.roll(kp, QK_ROPE, 1) * ss).astype(BF16)
    kn = jnp.dot(ckv, wkn_ref[...], preferred_element_type=F32).astype(BF16)
    for h in range(MLA_HEADS):
        base = h * QK_PAD
        k_ref[:, base:base + QK_NOPE] = kn[:, h * QK_NOPE:(h + 1) * QK_NOPE]
        k_ref[:, base + QK_NOPE:base + QK_PAD] = kpe
    vt = lax.dot_general(wvt_ref[...], ckv, (((1,), (1,)), ((), ())), preferred_element_type=F32).astype(BF16)
    ones = jnp.ones((VT_ROWS - V_HEAD, vt.shape[1]), BF16)
    for h in range(MLA_HEADS):
        vt_ref[h * VT_ROWS:h * VT_ROWS + V_HEAD, :] = vt[h * V_HEAD:(h + 1) * V_HEAD]
        vt_ref[h * VT_ROWS + V_HEAD:(h + 1) * VT_ROWS, :] = ones


def _mla_up(lat, qn, kvn, wq, wkn, wvt, cc, ss, tm):
    s = lat.shape[0]
    row = lambda i: (i, 0)
    full = lambda i: (0, 0)
    return pl.pallas_call(
        _mla_up_kernel,
        grid=(s // tm,),
        in_specs=[
            pl.BlockSpec((tm, D_LAT), row),
            pl.BlockSpec((1, Q_LORA), full),
            pl.BlockSpec((1, KV_LORA), full),
            pl.BlockSpec(wq.shape, full),
            pl.BlockSpec(wkn.shape, full),
            pl.BlockSpec(wvt.shape, full),
            pl.BlockSpec((tm, LANES), row),
            pl.BlockSpec((tm, LANES), row),
        ],
        out_specs=[
            pl.BlockSpec((tm, MLA_HEADS * QK_PAD), row),
            pl.BlockSpec((tm, MLA_HEADS * QK_PAD), row),
            pl.BlockSpec((MLA_HEADS * VT_ROWS, tm), lambda i: (0, i)),
        ],
        out_shape=[
            jax.ShapeDtypeStruct((s, MLA_HEADS * QK_PAD), BF16),
            jax.ShapeDtypeStruct((s, MLA_HEADS * QK_PAD), BF16),
            jax.ShapeDtypeStruct((MLA_HEADS * VT_ROWS, s), BF16),
        ],
        compiler_params=_cparams(("parallel",), 48),
        name="mla_up",
    )(lat, qn, kvn, wq, wkn, wvt, cc, ss)


def _attn_kernel(it_ref, jt_ref, q_ref, k_ref, vt_ref, g_ref, o_ref, acc_ref, m_ref, *, tq, tk, qc):
    t = pl.program_id(0)
    i = it_ref[t]
    j = jt_ref[t]

    @pl.when(j == 0)
    def _():
        m_ref[...] = jnp.full(m_ref.shape, NEG_BIG, F32)
        acc_ref[...] = jnp.zeros(acc_ref.shape, F32)

    units = [(h, c) for h in range(MLA_HEADS) for c in range(tq // qc)]

    def scores(u):
        h, c = u
        q = q_ref[c * qc:(c + 1) * qc, h * QK_PAD:(h + 1) * QK_PAD]
        k = k_ref[:, h * QK_PAD:(h + 1) * QK_PAD]
        return lax.dot_general(k, q, (((1,), (1,)), ((), ())), preferred_element_type=F32)

    def step(masked):
        if masked:
            diff = lax.broadcasted_iota(jnp.int32, (tk, qc), 0) - lax.broadcasted_iota(jnp.int32, (tk, qc), 1)
        def accumulate(h, cols, alpha, p):
            pv = jnp.dot(vt_ref[h * VT_ROWS:(h + 1) * VT_ROWS, :], p, preferred_element_type=F32)
            acc_ref[h, :, cols] = alpha * acc_ref[h, :, cols] + pv

        ahead = [scores(units[n]) for n in range(min(QK_AHEAD, len(units)))]
        pending = []
        for n, (h, c) in enumerate(units):
            s = ahead.pop(0)
            if n + QK_AHEAD < len(units):
                ahead.append(scores(units[n + QK_AHEAD]))
            cols = slice(c * qc, (c + 1) * qc)
            if masked:
                s = jnp.where(diff <= c * qc, s, NEG_BIG)
            m_prev = m_ref[h, :, cols]
            m_new = jnp.maximum(m_prev, jnp.max(s, axis=0, keepdims=True))
            alpha = jnp.exp2(m_prev - m_new)
            p = jnp.exp2(s - m_new).astype(BF16)
            m_ref[h, :, cols] = m_new
            pending.append((h, cols, alpha, p))
            if len(pending) > PV_LAG:
                accumulate(*pending.pop(0))
        for item in pending:
            accumulate(*item)

    @pl.when(j < i)
    def _():
        step(False)

    @pl.when(j == i)
    def _():
        step(True)
        outs = [acc_ref[h, :V_HEAD, :] * (1.0 / acc_ref[h, V_HEAD:V_HEAD + 1, :]) for h in range(MLA_HEADS)]
        o_ref[...] = _rms(jnp.concatenate(outs, axis=0).T, g_ref[...]).astype(o_ref.dtype)


def _attention(q, k, vt, gain, tb, qc):
    s = q.shape[0]
    nq = s // tb
    it = np.concatenate([np.full(i + 1, i, np.int32) for i in range(nq)])
    jt = np.concatenate([np.arange(i + 1, dtype=np.int32) for i in range(nq)])
    grid_spec = pltpu.PrefetchScalarGridSpec(
        num_scalar_prefetch=2,
        grid=(it.shape[0],),
        in_specs=[
            pl.BlockSpec((tb, MLA_HEADS * QK_PAD), lambda t, it, jt: (it[t], 0)),
            pl.BlockSpec((tb, MLA_HEADS * QK_PAD), lambda t, it, jt: (jt[t], 0)),
            pl.BlockSpec((MLA_HEADS * VT_ROWS, tb), lambda t, it, jt: (0, jt[t])),
            pl.BlockSpec((1, D_MLA), lambda t, it, jt: (0, 0)),
        ],
        out_specs=pl.BlockSpec((tb, D_MLA), lambda t, it, jt: (it[t], 0)),
        scratch_shapes=[
            pltpu.VMEM((MLA_HEADS, VT_ROWS, tb), F32),
            pltpu.VMEM((MLA_HEADS, 1, tb), F32),
        ],
    )
    return pl.pallas_call(
        functools.partial(_attn_kernel, tq=tb, tk=tb, qc=qc),
        grid_spec=grid_spec,
        out_shape=jax.ShapeDtypeStruct((s, D_MLA), BF16),
        compiler_params=_cparams(("arbitrary",), 48),
        name="mla_attention",
    )(jnp.asarray(it), jnp.asarray(jt), q, k, vt, gain)


def _hgrn_tables():
    n = HG_ROWS
    r = np.arange(n)
    c = np.arange(n)
    same = (r[:, None] // HG_CHUNK) == (c[None, :] // HG_CHUNK)

    def rows_upto(idx):
        return (same & (c[None, :] <= idx[:, None])).astype(np.float32)

    blocks = [rows_upto(r)]
    for m in HG_MM_LEVELS:
        blocks.append(rows_upto((r // (2 * m)) * (2 * m) + m))
    mall = np.concatenate(blocks, axis=0)

    x = r[:, None] ^ c[None, :]
    lv = np.full((n, n), -1, np.int32)
    for li, m in enumerate(HG_LEVELS):
        lv = np.where(same & (r[:, None] > c[None, :]) & (x >= m) & (x < 2 * m), li, lv)
    lv = np.where(r[:, None] == c[None, :], len(HG_LEVELS), lv)
    return jnp.asarray(mall, BF16), jnp.asarray(lv, jnp.int32)


def _hgrn_kernel(q_ref, f_ref, i_ref, g_ref, lb_ref, gn_ref, mall_ref, lv_ref, o_ref, st_ref):
    t = pl.program_id(1)
    n = HG_ROWS

    @pl.when(t == 0)
    def _():
        st_ref[...] = jnp.zeros(st_ref.shape, F32)

    rowid = lax.broadcasted_iota(jnp.int32, (n, 1), 0)
    lv = lv_ref[...]

    def head(hh):
        cols = slice(hh * HG_DK, (hh + 1) * HG_DK)
        q_in = q_ref[:, cols].astype(F32)
        qs = q_in * _sigmoid(q_in)
        lb = lb_ref[:, cols]
        f = lb + (1.0 - lb) * _sigmoid(f_ref[:, cols].astype(F32))
        kk = 1.0 - f
        logf = jnp.log(f)
        iv = i_ref[:, cols]

        l1 = logf.astype(BF16)
        r1 = logf - l1.astype(F32)
        l2 = r1.astype(BF16)
        l3 = (r1 - l2.astype(F32)).astype(BF16)
        parts = jnp.dot(mall_ref[...], jnp.concatenate([l1, l2, l3], axis=1), preferred_element_type=F32)
        yield
        bc = parts[:, :HG_DK] + parts[:, HG_DK:2 * HG_DK] + parts[:, 2 * HG_DK:]
        b = bc[:n]

        def anchor(period, row):
            b3 = b.reshape(n // period, period, HG_DK)
            return jnp.broadcast_to(b3[:, row:row + 1, :], b3.shape).reshape(n, HG_DK)

        a = jnp.zeros((n, n), F32)
        for li, m in enumerate(HG_LEVELS):
            if m in HG_MM_LEVELS:
                k = 1 + HG_MM_LEVELS.index(m)
                c = bc[k * n:(k + 1) * n]
            else:
                c = anchor(2 * m, m)
            e = jnp.exp(-jnp.abs(b - c))
            upper = (rowid & (2 * m - 1)) >= m
            x = (jnp.where(upper, qs, kk) * e).astype(BF16)
            p = lax.dot_general(x, x, (((1,), (1,)), ((), ())), preferred_element_type=F32)
            yield
            a = jnp.where(lv == li, p, a)
        a = jnp.where(lv == len(HG_LEVELS), jnp.sum(qs * kk, axis=-1, keepdims=True), a)
        o = jnp.dot(a.astype(BF16), iv, preferred_element_type=F32)
        yield

        blast = anchor(HG_CHUNK, HG_CHUNK - 1)
        qd = (qs * jnp.exp(b)).astype(BF16)
        kd = kk * jnp.exp(blast - b)
        ivt = iv.astype(F32).T.astype(BF16)
        st = st_ref[hh]
        outs = []
        for ci in range(n // HG_CHUNK):
            lo = ci * HG_CHUNK
            inter = lax.dot_general(qd[lo:lo + HG_CHUNK], st.astype(BF16), (((1,), (1,)), ((), ())),
                                    preferred_element_type=F32)
            kd_c = jnp.where((rowid >= lo) & (rowid < lo + HG_CHUNK), kd, 0.0).astype(BF16)
            upd = jnp.dot(ivt, kd_c, preferred_element_type=F32)
            yield
            outs.append(o[lo:lo + HG_CHUNK] + inter)
            st = jnp.exp(blast[lo:lo + 1]) * st + upd
        st_ref[hh] = st
        o = jnp.concatenate(outs, axis=0)

        g_in = g_ref[:, cols].astype(F32)
        o_ref[:, cols] = (_rms(o, gn_ref[:, cols]) * (g_in * _sigmoid(g_in))).astype(o_ref.dtype)

    progs = [head(hh) for hh in range(HG_NH)]
    while progs:
        progs = [pr for pr in progs if next(pr, "done") != "done"]


def _hgrn(hg4, lb, gn, mall, lv):
    s = hg4.shape[0]
    groups = HG_HEADS // HG_NH
    width = HG_NH * HG_DK

    def col(c):
        return lambda hh, t: (t, c * groups + hh)

    head = lambda hh, t: (0, hh)
    const = lambda hh, t: (0, 0)
    return pl.pallas_call(
        _hgrn_kernel,
        grid=(groups, s // HG_ROWS),
        in_specs=[
            pl.BlockSpec((HG_ROWS, width), col(0)),
            pl.BlockSpec((HG_ROWS, width), col(1)),
            pl.BlockSpec((HG_ROWS, width), col(2)),
            pl.BlockSpec((HG_ROWS, width), col(3)),
            pl.BlockSpec((1, width), head),
            pl.BlockSpec((1, width), head),
            pl.BlockSpec(mall.shape, const),
            pl.BlockSpec(lv.shape, const),
        ],
        out_specs=pl.BlockSpec((HG_ROWS, width), lambda hh, t: (t, hh)),
        out_shape=jax.ShapeDtypeStruct((s, D_HG), BF16),
        scratch_shapes=[pltpu.VMEM((HG_NH, HG_DV, HG_DK), F32)],
        compiler_params=_cparams(("parallel", "arbitrary"), 32),
        name="hgrn2",
    )(hg4, hg4, hg4, hg4, lb, gn, mall, lv)


def _out_route_kernel(x_ref, om_ref, oh_ref, wa_ref, wb_ref, g_ref, wr_ref, br_ref,
                      h_ref, hn_ref, rt_ref):
    h1 = (x_ref[...]
          + jnp.dot(om_ref[...], wa_ref[...], preferred_element_type=F32)
          + jnp.dot(oh_ref[...], wb_ref[...], preferred_element_type=F32))
    h_ref[...] = h1
    hn = _rms(h1, g_ref[...])
    hn_ref[...] = hn
    hn_hi = hn.astype(BF16)
    hn_lo = (hn - hn_hi.astype(F32)).astype(BF16)
    hh = jnp.dot(hn_hi, wr_ref[...], preferred_element_type=F32)
    lh = jnp.dot(hn_lo, wr_ref[:, :LANES], preferred_element_type=F32)
    logits = hh[:, :LANES] + (hh[:, LANES:] + lh) + br_ref[...]

    lane = lax.broadcasted_iota(jnp.int32, logits.shape, 1)
    lanef = lane.astype(F32)
    ninf = -jnp.inf
    big = float(LANES)

    is_g = lane < N_GROUPS
    gl = jnp.where(is_g, logits, ninf)
    gmax = jnp.max(gl, axis=-1, keepdims=True)
    gsum = jnp.sum(jnp.where(is_g, jnp.exp(gl - gmax), 0.0), axis=-1, keepdims=True)
    g_w = 1.0 / gsum
    g_idx = jnp.min(jnp.where(gl == gmax, lanef, big), axis=-1, keepdims=True)

    e_lane = lane - N_GROUPS
    in_grp = (e_lane >= 0) & (e_lane < N_EXPERTS) & ((e_lane >> 3).astype(F32) == g_idx)
    el = jnp.where(in_grp, logits, ninf)
    emax = jnp.max(el, axis=-1, keepdims=True)
    esum = jnp.sum(jnp.where(in_grp, jnp.exp(el - emax), 0.0), axis=-1, keepdims=True)
    i1 = jnp.min(jnp.where(el == emax, lanef, big), axis=-1, keepdims=True)
    el2 = jnp.where(lanef == i1, ninf, el)
    emax2 = jnp.max(el2, axis=-1, keepdims=True)
    i2 = jnp.min(jnp.where(el2 == emax2, lanef, big), axis=-1, keepdims=True)
    p1 = 1.0 / esum
    p2 = jnp.exp(emax2 - emax) / esum
    w1 = g_w * p1 / (p1 + p2)
    w2 = g_w * p2 / (p1 + p2)

    rt = jnp.where(lane == 0, i1 - N_GROUPS,
                   jnp.where(lane == 1, i2 - N_GROUPS,
                             jnp.where(lane == 2, w1, jnp.where(lane == 3, w2, 0.0))))
    rt_ref[...] = rt


def _out_route(x, o_mla, o_hg, wa, wb, gain, wr, br, tm):
    s, d = x.shape
    row = lambda i: (i, 0)
    full = lambda i: (0, 0)
    return pl.pallas_call(
        _out_route_kernel,
        grid=(s // tm,),
        in_specs=[
            pl.BlockSpec((tm, d), row),
            pl.BlockSpec((tm, D_MLA), row),
            pl.BlockSpec((tm, D_HG), row),
            pl.BlockSpec(wa.shape, full, pipeline_mode=pl.Buffered(1)),
            pl.BlockSpec(wb.shape, full, pipeline_mode=pl.Buffered(1)),
            pl.BlockSpec((1, d), full),
            pl.BlockSpec(wr.shape, full, pipeline_mode=pl.Buffered(1)),
            pl.BlockSpec((1, LANES), full),
        ],
        out_specs=[
            pl.BlockSpec((tm, d), row),
            pl.BlockSpec((tm, d), row),
            pl.BlockSpec((tm, LANES), row),
        ],
        out_shape=[
            jax.ShapeDtypeStruct((s, d), F32),
            jax.ShapeDtypeStruct((s, d), F32),
            jax.ShapeDtypeStruct((s, LANES), F32),
        ],
        compiler_params=_cparams(("parallel",), 56),
        name="out_proj_route",
    )(x, o_mla, o_hg, wa, wb, gain, wr, br)


def _moe_kernel(be_ref, nv_ref, tok_ref, nu_ref, hn_hbm, wg_hbm, wu_hbm, wd_hbm, o_ref,
                xbuf, xsem, wgb, wub, wdb, wsem, wslot):
    i = pl.program_id(0)
    nused = nu_ref[0]
    e = be_ref[i]
    active = i < nused
    first = (i == 0) | (e != be_ref[jnp.maximum(i - 1, 0)])

    def gather(blk, sl, wait):
        nv = nv_ref[blk]
        for lo in range(0, MOE_BLOCK, GATHER_CHUNK):
            @pl.when(lo < nv)
            def _():
                if wait:
                    pltpu.make_async_copy(hn_hbm.at[pl.ds(0, GATHER_CHUNK), :],
                                          xbuf.at[sl, pl.ds(lo, GATHER_CHUNK), :], xsem.at[sl]).wait()
                else:
                    for r in range(lo, lo + GATHER_CHUNK):
                        tok = tok_ref[blk * MOE_BLOCK + r]
                        pltpu.make_async_copy(hn_hbm.at[pl.ds(tok, 1), :], xbuf.at[sl, pl.ds(r, 1), :],
                                              xsem.at[sl]).start(priority=r % 2)

    def on_slot(val, fn):
        for s_ in (0, 1):
            pl.when(val == s_)(functools.partial(fn, s_))

    def weight_copies(ex, sl):
        return (pltpu.make_async_copy(wg_hbm.at[ex], wgb.at[sl], wsem.at[sl, 0]),
                pltpu.make_async_copy(wu_hbm.at[ex], wub.at[sl], wsem.at[sl, 1]),
                pltpu.make_async_copy(wd_hbm.at[ex], wdb.at[sl], wsem.at[sl, 2]))

    @pl.when((i == 0) & active)
    def _():
        xbuf[...] = jnp.zeros(xbuf.shape, F32)
        wslot[0] = 1
        for cp in weight_copies(e, 0):
            cp.start()
        gather(0, 0, wait=False)

    @pl.when(active & first)
    def _():
        sl = 1 - wslot[0]
        wslot[0] = sl
        nxt = lax.while_loop(lambda j: (j < nused) & (be_ref[jnp.minimum(j, nused - 1)] == e),
                             lambda j: j + 1, i + 1)

        @pl.when(nxt < nused)
        def _():
            for cp in weight_copies(be_ref[nxt], 1 - sl):
                cp.start()

        for cp in weight_copies(e, sl):
            cp.wait()

    @pl.when(active)
    def _():
        sl = wslot[0]
        xs = i % 2

        @pl.when(i + 1 < nused)
        def _():
            on_slot(xs, lambda s_: gather(i + 1, 1 - s_, wait=False))

        on_slot(xs, lambda s_: gather(i, s_, wait=True))
        x = xbuf[xs].astype(BF16)
        g = jnp.dot(x, wgb[sl].astype(BF16), preferred_element_type=F32)
        u = jnp.dot(x, wub[sl].astype(BF16), preferred_element_type=F32)
        hmid = (g * _sigmoid(g) * u).astype(BF16)
        o_ref[...] = jnp.dot(hmid, wdb[sl].astype(BF16), preferred_element_type=F32)

    @pl.when(jnp.logical_not(active))
    def _():
        o_ref[...] = jnp.zeros(o_ref.shape, F32)


def _moe(block_e, nvalid, slot_tok, nused, hn, wg, wu, wd):
    nb = block_e.shape[0]
    d = hn.shape[1]
    blk = lambda i, be, nv, tok, nu: (i, 0)
    grid_spec = pltpu.PrefetchScalarGridSpec(
        num_scalar_prefetch=4,
        grid=(nb,),
        in_specs=[pl.BlockSpec(memory_space=pl.ANY)] * 4,
        out_specs=pl.BlockSpec((MOE_BLOCK, d), blk),
        scratch_shapes=[
            pltpu.VMEM((2, MOE_BLOCK, d), F32),
            pltpu.SemaphoreType.DMA((2,)),
            pltpu.VMEM((2, d, D_EXPERT), F32),
            pltpu.VMEM((2, d, D_EXPERT), F32),
            pltpu.VMEM((2, D_EXPERT, d), F32),
            pltpu.SemaphoreType.DMA((2, 3)),
            pltpu.SMEM((1,), jnp.int32),
        ],
    )
    return pl.pallas_call(
        _moe_kernel,
        grid_spec=grid_spec,
        out_shape=jax.ShapeDtypeStruct((nb * MOE_BLOCK, d), F32),
        compiler_params=_cparams(("arbitrary",), 56),
        name="moe_experts",
    )(block_e, nvalid, slot_tok, nused, hn, wg, wu, wd)


def _ple_kernel(pos_ref, h_ref, rt_ref, y_hbm, p_ref, wg_ref, bg_ref, wp_ref, gp_ref, gf_ref, o_ref,
                ybuf, sem, *, tm):
    i = pl.program_id(0)
    n = pl.num_programs(0)

    def gather(step, sl, wait):
        for kk in range(TOP_K):
            if wait:
                pltpu.make_async_copy(y_hbm.at[pl.ds(0, tm), :], ybuf.at[sl, kk], sem.at[sl]).wait()
            else:
                for r in range(tm):
                    src = pos_ref[(step * tm + r) * TOP_K + kk]
                    pltpu.make_async_copy(y_hbm.at[pl.ds(src, 1), :], ybuf.at[sl, kk, pl.ds(r, 1), :],
                                          sem.at[sl]).start(priority=r % 2)

    def on_slot(val, fn):
        for s_ in (0, 1):
            pl.when(val == s_)(functools.partial(fn, s_))

    @pl.when(i == 0)
    def _():
        gather(0, 0, wait=False)

    sl = i % 2

    @pl.when(i + 1 < n)
    def _():
        on_slot(sl, lambda s_: gather(i + 1, 1 - s_, wait=False))

    pe = jnp.dot(p_ref[...].astype(BF16), wp_ref[...], preferred_element_type=F32)
    on_slot(sl, lambda s_: gather(i, s_, wait=True))

    rt = rt_ref[...]
    h2 = h_ref[...] + (rt[:, TOP_K:TOP_K + 1] * ybuf[sl, 0] + rt[:, TOP_K + 1:TOP_K + 2] * ybuf[sl, 1])
    hn = _rms(h2, gp_ref[...]).astype(BF16)
    gate = _sigmoid(jnp.dot(hn, wg_ref[...], preferred_element_type=F32) + bg_ref[...])
    h3 = h2 + gate * pe
    o_ref[...] = _rms(h3, gf_ref[...])


def _ple_final(pos, h1, rt, y_slots, p, wg, bg, wp, gp, gf, tm):
    s, d = h1.shape
    row = lambda i, pos: (i, 0)
    full = lambda i, pos: (0, 0)
    grid_spec = pltpu.PrefetchScalarGridSpec(
        num_scalar_prefetch=1,
        grid=(s // tm,),
        in_specs=[
            pl.BlockSpec((tm, d), row),
            pl.BlockSpec((tm, LANES), row),
            pl.BlockSpec(memory_space=pl.ANY),
            pl.BlockSpec((tm, PLE_DIM), row),
            pl.BlockSpec(wg.shape, full),
            pl.BlockSpec((1, d), full),
            pl.BlockSpec(wp.shape, full),
            pl.BlockSpec((1, d), full),
            pl.BlockSpec((1, d), full),
        ],
        out_specs=pl.BlockSpec((tm, d), row),
        scratch_shapes=[
            pltpu.VMEM((2, TOP_K, tm, d), F32),
            pltpu.SemaphoreType.DMA((2,)),
        ],
    )
    return pl.pallas_call(
        functools.partial(_ple_kernel, tm=tm),
        grid_spec=grid_spec,
        out_shape=jax.ShapeDtypeStruct((s, d), F32),
        compiler_params=_cparams(("arbitrary",), 48),
        name="ple_final",
    )(pos, h1, rt, y_slots, p, wg, bg, wp, gp, gf)


def _slots_kernel(rt_ref, tri_ref, pos_ref, cnt_ref, carry_ref, base_ref):
    ps = pl.program_id(0)
    b = pl.program_id(1)
    rt = rt_ref[...]
    lane = lax.broadcasted_iota(jnp.int32, rt.shape, 1)
    lanef = lane.astype(F32)
    oh = [(lanef == rt[:, kk:kk + 1]).astype(F32) for kk in range(TOP_K)]
    both = oh[0] + oh[1]
    colsum = jnp.sum(both, axis=0, keepdims=True)

    @pl.when((ps == 0) & (b == 0))
    def _():
        cnt_ref[...] = jnp.zeros(cnt_ref.shape, F32)

    @pl.when(ps == 0)
    def _():
        cnt_ref[...] = cnt_ref[...] + colsum

    @pl.when((ps == 1) & (b == 0))
    def _():
        blocks = jnp.floor((cnt_ref[...] + (MOE_BLOCK - 0.5)) / MOE_BLOCK)
        r = lax.broadcasted_iota(jnp.int32, (LANES, LANES), 0)
        c = lax.broadcasted_iota(jnp.int32, (LANES, LANES), 1)
        before = (r < c).astype(F32)
        base_ref[...] = jnp.dot(blocks * MOE_BLOCK, before, preferred_element_type=F32,
                                precision=lax.Precision.HIGHEST)
        carry_ref[...] = jnp.zeros(carry_ref.shape, F32)

    @pl.when(ps == 1)
    def _():
        earlier = jnp.dot(tri_ref[...], both.astype(BF16), preferred_element_type=F32)
        row = earlier + (base_ref[0:1, :] + carry_ref[0:1, :])
        s0 = jnp.sum(oh[0] * row, axis=-1, keepdims=True)
        s1 = jnp.sum(oh[1] * row, axis=-1, keepdims=True)
        pos_ref[...] = jnp.where(lane == 0, s0, jnp.where(lane == 1, s1, 0.0))
        carry_ref[...] = carry_ref[...] + colsum


def _slots(rt, tb):
    s = rt.shape[0]
    tri = (np.arange(tb)[:, None] > np.arange(tb)[None, :]).astype(np.float32)
    return pl.pallas_call(
        _slots_kernel,
        grid=(2, s // tb),
        in_specs=[
            pl.BlockSpec((tb, LANES), lambda ps, b: (b, 0)),
            pl.BlockSpec((tb, tb), lambda ps, b: (0, 0)),
        ],
        out_specs=[
            pl.BlockSpec((tb, LANES), lambda ps, b: (b * ps, 0)),
            pl.BlockSpec((8, LANES), lambda ps, b: (0, 0)),
        ],
        out_shape=[
            jax.ShapeDtypeStruct((s, LANES), F32),
            jax.ShapeDtypeStruct((8, LANES), F32),
        ],
        scratch_shapes=[pltpu.VMEM((8, LANES), F32), pltpu.VMEM((8, LANES), F32)],
        compiler_params=_cparams(("arbitrary", "arbitrary")),
        name="moe_slots",
    )(rt, jnp.asarray(tri, BF16))


def _dispatch(rt, n_tok):
    a = n_tok * TOP_K
    nb = -(-a // MOE_BLOCK) + N_EXPERTS
    pos_f, cnt = _slots(rt, 512)
    pos = pos_f[:, :TOP_K].astype(jnp.int32).reshape(-1)
    counts = cnt[0, :N_EXPERTS].astype(jnp.int32)
    nblk = (counts + MOE_BLOCK - 1) // MOE_BLOCK
    bends = jnp.cumsum(nblk)
    bidx = jnp.arange(nb, dtype=jnp.int32)
    block_e = jnp.clip(jnp.searchsorted(bends, bidx, side="right"), 0, N_EXPERTS - 1).astype(jnp.int32)
    nused = bends[-1].astype(jnp.int32)
    within = bidx - (bends - nblk)[block_e]
    nvalid = jnp.where(bidx < nused, jnp.clip(counts[block_e] - within * MOE_BLOCK, 0, MOE_BLOCK), 0).astype(jnp.int32)
    t_flat = jnp.repeat(jnp.arange(n_tok, dtype=jnp.int32), TOP_K)
    slot_tok = jnp.zeros((nb * MOE_BLOCK,), jnp.int32).at[pos].set(t_flat)
    return block_e, nvalid, slot_tok, nused.reshape(1), pos


def kernel(x, p, positions, attn_norm, w_in, q_norm, w_uq, kv_norm, w_ukv, mla_norm, hg_lb_logits, hg_norm, w_out, ffn_norm, w_router_group, b_router_group, w_router_expert, b_router_expert, w_exp_gate, w_exp_up, w_exp_down, ple_norm, w_ple_gate, b_ple_gate, w_ple_proj, final_norm):
    bsz, s, d = x.shape
    assert bsz == 1 and w_in.shape[0] == 1
    xt = x[0]

    inv_freq = 1.0 / (ROPE_THETA ** (jnp.arange(0, QK_ROPE, 2, dtype=F32) / QK_ROPE))
    ang = positions[0].astype(F32)[:, None] * inv_freq
    cos, sin = jnp.cos(ang), jnp.sin(ang)
    zpad = jnp.zeros((s, LANES - QK_ROPE), F32)
    cc = jnp.concatenate([cos, cos, zpad], axis=1)
    ss = jnp.concatenate([-sin, sin, zpad], axis=1)

    lb = jnp.cumsum(jax.nn.softmax(hg_lb_logits.astype(F32), axis=0), axis=0)[0][None, :]

    wi = w_in[0]
    kr0 = Q_LORA + KV_LORA
    half = QK_ROPE // 2
    w_lat = jnp.concatenate(
        [wi[:, :kr0 + QK_ROPE], wi[:, kr0 + half:kr0 + QK_ROPE], wi[:, kr0:kr0 + half]], axis=1).astype(BF16)
    w_hg = wi[:, kr0 + QK_ROPE:].astype(BF16)
    wq3 = w_uq[0].reshape(Q_LORA, MLA_HEADS, QK_HEAD)
    wq_pad = jnp.concatenate(
        [wq3, wq3[:, :, QK_NOPE + half:], wq3[:, :, QK_NOPE:QK_NOPE + half]], axis=2
    ).reshape(Q_LORA, MLA_HEADS * QK_PAD).astype(BF16)
    wkv3 = w_ukv[0].reshape(KV_LORA, MLA_HEADS, QK_NOPE + V_HEAD)
    wkn = wkv3[:, :, :QK_NOPE].reshape(KV_LORA, MLA_HEADS * QK_NOPE).astype(BF16)
    wvt = wkv3[:, :, QK_NOPE:].reshape(KV_LORA, D_MLA).T.astype(BF16)
    wo = w_out[0].astype(BF16)
    wr = jnp.concatenate(
        [w_router_group[0], w_router_expert[0], jnp.zeros((d, LANES - N_GROUPS - N_EXPERTS), F32)], axis=1)
    wr_hi = wr.astype(BF16)
    wr = jnp.concatenate([wr_hi, (wr - wr_hi.astype(F32)).astype(BF16)], axis=1)
    br = jnp.concatenate(
        [b_router_group[0], b_router_expert[0], jnp.zeros((LANES - N_GROUPS - N_EXPERTS,), F32)])[None, :]

    lat = _in_proj(xt, attn_norm, w_lat, F32, 512, D_LAT, "in_proj_latent")
    hg4 = _in_proj(xt, attn_norm, w_hg, BF16, 1024, 1024, "in_proj_hgrn")
    q, k, vt = _mla_up(lat, q_norm, kv_norm, wq_pad, wkn, wvt, cc, ss, 512)
    o_mla = _attention(q, k, vt, mla_norm, 512, 512)
    mall, lv = _hgrn_tables()
    o_hg = _hgrn(hg4, lb, hg_norm[0].reshape(1, D_HG), mall, lv)

    h1, hn, rt = _out_route(xt, o_mla, o_hg, wo[:D_MLA], wo[D_MLA:], ffn_norm, wr, br, 512)
    block_e, nvalid, slot_tok, nused, pos = _dispatch(rt, s)
    y_slots = _moe(block_e, nvalid, slot_tok, nused, hn, w_exp_gate[0], w_exp_up[0], w_exp_down[0])

    out = _ple_final(pos, h1, rt, y_slots, p[0, 0], w_ple_gate[0].astype(BF16), b_ple_gate,
                     w_ple_proj[0].astype(BF16), ple_norm, final_norm[None, :], 256)
    return out[None]
```

```python
import functools

import jax
import jax.numpy as jnp
import numpy as np
from jax import lax
from jax.experimental import pallas as pl
from jax.experimental.pallas import tpu as pltpu

F32 = jnp.float32
BF16 = jnp.bfloat16

D_MODEL = 2048
PLE_DIM = 256
MLA_HEADS = 8
QK_NOPE = 128
QK_ROPE = 64
QK_HEAD = QK_NOPE + QK_ROPE
QK_PAD = 256
V_HEAD = 128
VT_ROWS = V_HEAD + 16
QK_AHEAD = 2
PV_LAG = 1
Q_LORA = 512
KV_LORA = 256
ROPE_THETA = 10000.0
HG_HEADS = 8
HG_DK = 128
HG_DV = 128
HG_CHUNK = 64
D_MLA = MLA_HEADS * V_HEAD
D_HG = HG_HEADS * HG_DV
N_GROUPS = 8
EXPERTS_PER_GROUP = 8
N_EXPERTS = N_GROUPS * EXPERTS_PER_GROUP
TOP_K = 2
D_EXPERT = 512
EPS = 1e-6
LANES = 128
D_LAT = Q_LORA + KV_LORA + 2 * QK_ROPE
NEG_BIG = -1e30
LOG2E = 1.4426950408889634

MOE_BLOCK = 320
GATHER_CHUNK = 32
HG_ROWS = 256
HG_LEVELS = (32, 16, 8, 4, 2, 1)
HG_MM_LEVELS = (2, 1)
HG_NH = 4


def _cparams(sem, vmem_mb=None):
    kw = dict(dimension_semantics=sem)
    if vmem_mb is not None:
        kw["vmem_limit_bytes"] = vmem_mb * 1024 * 1024
    return pltpu.CompilerParams(**kw)


def _rms(x, g):
    ms = jnp.mean(x * x, axis=-1, keepdims=True)
    return x * lax.rsqrt(ms + EPS) * g


def _sigmoid(x):
    return 1.0 / (1.0 + jnp.exp(-x))


def _in_proj_kernel(x_ref, g_ref, w_ref, o_ref, xn_ref):
    @pl.when(pl.program_id(1) == 0)
    def _():
        xn_ref[...] = _rms(x_ref[...], g_ref[...]).astype(BF16)

    o_ref[...] = jnp.dot(xn_ref[...], w_ref[...], preferred_element_type=F32).astype(o_ref.dtype)


def _in_proj(x, gain, w, out_dtype, tm, tn, name):
    s, d = x.shape
    n = w.shape[1]
    return pl.pallas_call(
        _in_proj_kernel,
        grid=(s // tm, n // tn),
        in_specs=[
            pl.BlockSpec((tm, d), lambda i, j: (i, 0)),
            pl.BlockSpec((1, d), lambda i, j: (0, 0)),
            pl.BlockSpec((d, tn), lambda i, j: (0, j)),
        ],
        out_specs=pl.BlockSpec((tm, tn), lambda i, j: (i, j)),
        out_shape=jax.ShapeDtypeStruct((s, n), out_dtype),
        scratch_shapes=[pltpu.VMEM((tm, d), BF16)],
        compiler_params=_cparams(("parallel", "arbitrary"), 48),
        name=name,
    )(x, gain, w)


def _mla_up_kernel(lat_ref, qn_ref, kvn_ref, wq_ref, wkn_ref, wvt_ref, cc_ref, ss_ref,
                   q_ref, k_ref, vt_ref):
    lat = lat_ref[...]
    cq = _rms(lat[:, :Q_LORA], qn_ref[...]).astype(BF16)
    ckv = _rms(lat[:, Q_LORA:Q_LORA + KV_LORA], kvn_ref[...]).astype(BF16)
    kp = lat[:, Q_LORA + KV_LORA:]
    cc = cc_ref[...]
    ss = ss_ref[...]
    scale = QK_HEAD ** -0.5 * LOG2E

    q = jnp.dot(cq, wq_ref[...], preferred_element_type=F32)
    for h in range(MLA_HEADS):
        base = h * QK_PAD
        q_ref[:, base:base + QK_NOPE] = (q[:, base:base + QK_NOPE] * scale).astype(BF16)
        r = q[:, base + QK_NOPE:base + QK_PAD]
        r = (r * cc + pltpu.roll(r, QK_ROPE, 1) * ss) * scale
        q_ref[:, base + QK_NOPE:base + QK_PAD] = r.astype(BF16)

    kpe = (kp * cc + pltpu.roll(kp, QK_ROPE, 1) * ss).astype(BF16)
    kn = jnp.dot(ckv, wkn_ref[...], preferred_element_type=F32).astype(BF16)
    for h in range(MLA_HEADS):
        base = h * QK_PAD
        k_ref[:, base:base + QK_NOPE] = kn[:, h * QK_NOPE:(h + 1) * QK_NOPE]
        k_ref[:, base + QK_NOPE:base + QK_PAD] = kpe
    vt = lax.dot_general(wvt_ref[...], ckv, (((1,), (1,)), ((), ())), preferred_element_type=F32).astype(BF16)
    ones = jnp.ones((VT_ROWS - V_HEAD, vt.shape[1]), BF16)
    for h in range(MLA_HEADS):
        vt_ref[h * VT_ROWS:h * VT_ROWS + V_HEAD, :] = vt[h * V_HEAD:(h + 1) * V_HEAD]
        vt_ref[h * VT_ROWS + V_HEAD:(h + 1) * VT_ROWS, :] = ones


def _mla_up(lat, qn, kvn, wq, wkn, wvt, cc, ss, tm):
    s = lat.shape[0]
    row = lambda i: (i, 0)
    full = lambda i: (0, 0)
    return pl.pallas_call(
        _mla_up_kernel,
        grid=(s // tm,),
        in_specs=[
            pl.BlockSpec((tm, D_LAT), row),
            pl.BlockSpec((1, Q_LORA), full),
            pl.BlockSpec((1, KV_LORA), full),
            pl.BlockSpec(wq.shape, full),
            pl.BlockSpec(wkn.shape, full),
            pl.BlockSpec(wvt.shape, full),
            pl.BlockSpec((tm, LANES), row),
            pl.BlockSpec((tm, LANES), row),
        ],
        out_specs=[
            pl.BlockSpec((tm, MLA_HEADS * QK_PAD), row),
            pl.BlockSpec((tm, MLA_HEADS * QK_PAD), row),
            pl.BlockSpec((MLA_HEADS * VT_ROWS, tm), lambda i: (0, i)),
        ],
        out_shape=[
            jax.ShapeDtypeStruct((s, MLA_HEADS * QK_PAD), BF16),
            jax.ShapeDtypeStruct((s, MLA_HEADS * QK_PAD), BF16),
            jax.ShapeDtypeStruct((MLA_HEADS * VT_ROWS, s), BF16),
        ],
        compiler_params=_cparams(("parallel",), 48),
        name="mla_up",
    )(lat, qn, kvn, wq, wkn, wvt, cc, ss)


def _attn_kernel(it_ref, jt_ref, q_ref, k_ref, vt_ref, g_ref, o_ref, acc_ref, m_ref, *, tq, tk, qc):
    t = pl.program_id(0)
    i = it_ref[t]
    j = jt_ref[t]

    @pl.when(j == 0)
    def _():
        m_ref[...] = jnp.full(m_ref.shape, NEG_BIG, F32)
        acc_ref[...] = jnp.zeros(acc_ref.shape, F32)

    units = [(h, c) for h in range(MLA_HEADS) for c in range(tq // qc)]

    def scores(u):
        h, c = u
        q = q_ref[c * qc:(c + 1) * qc, h * QK_PAD:(h + 1) * QK_PAD]
        k = k_ref[:, h * QK_PAD:(h + 1) * QK_PAD]
        return lax.dot_general(k, q, (((1,), (1,)), ((), ())), preferred_element_type=F32)

    def step(masked):
        if masked:
            diff = lax.broadcasted_iota(jnp.int32, (tk, qc), 0) - lax.broadcasted_iota(jnp.int32, (tk, qc), 1)
        def accumulate(h, cols, alpha, p):
            pv = jnp.dot(vt_ref[h * VT_ROWS:(h + 1) * VT_ROWS, :], p, preferred_element_type=F32)
            acc_ref[h, :, cols] = alpha * acc_ref[h, :, cols] + pv

        ahead = [scores(units[n]) for n in range(min(QK_AHEAD, len(units)))]
        pending = []
        for n, (h, c) in enumerate(units):
            s = ahead.pop(0)
            if n + QK_AHEAD < len(units):
                ahead.append(scores(units[n + QK_AHEAD]))
            cols = slice(c * qc, (c + 1) * qc)
            if masked:
                s = jnp.where(diff <= c * qc, s, NEG_BIG)
            m_prev = m_ref[h, :, cols]
            m_new = jnp.maximum(m_prev, jnp.max(s, axis=0, keepdims=True))
            alpha = jnp.exp2(m_prev - m_new)
            p = jnp.exp2(s - m_new).astype(BF16)
            m_ref[h, :, cols] = m_new
            pending.append((h, cols, alpha, p))
            if len(pending) > PV_LAG:
                accumulate(*pending.pop(0))
        for item in pending:
            accumulate(*item)

    @pl.when(j < i)
    def _():
        step(False)

    @pl.when(j == i)
    def _():
        step(True)
        outs = [acc_ref[h, :V_HEAD, :] * (1.0 / acc_ref[h, V_HEAD:V_HEAD + 1, :]) for h in range(MLA_HEADS)]
        o_ref[...] = _rms(jnp.concatenate(outs, axis=0).T, g_ref[...]).astype(o_ref.dtype)


def _attention(q, k, vt, gain, tb, qc):
    s = q.shape[0]
    nq = s // tb
    it = np.concatenate([np.full(i + 1, i, np.int32) for i in range(nq)])
    jt = np.concatenate([np.arange(i + 1, dtype=np.int32) for i in range(nq)])
    grid_spec = pltpu.PrefetchScalarGridSpec(
        num_scalar_prefetch=2,
        grid=(it.shape[0],),
        in_specs=[
            pl.BlockSpec((tb, MLA_HEADS * QK_PAD), lambda t, it, jt: (it[t], 0)),
            pl.BlockSpec((tb, MLA_HEADS * QK_PAD), lambda t, it, jt: (jt[t], 0)),
            pl.BlockSpec((MLA_HEADS * VT_ROWS, tb), lambda t, it, jt: (0, jt[t])),
            pl.BlockSpec((1, D_MLA), lambda t, it, jt: (0, 0)),
        ],
        out_specs=pl.BlockSpec((tb, D_MLA), lambda t, it, jt: (it[t], 0)),
        scratch_shapes=[
            pltpu.VMEM((MLA_HEADS, VT_ROWS, tb), F32),
            pltpu.VMEM((MLA_HEADS, 1, tb), F32),
        ],
    )
    return pl.pallas_call(
        functools.partial(_attn_kernel, tq=tb, tk=tb, qc=qc),
        grid_spec=grid_spec,
        out_shape=jax.ShapeDtypeStruct((s, D_MLA), BF16),
        compiler_params=_cparams(("arbitrary",), 48),
        name="mla_attention",
    )(jnp.asarray(it), jnp.asarray(jt), q, k, vt, gain)


def _hgrn_tables():
    n = HG_ROWS
    r = np.arange(n)
    c = np.arange(n)
    same = (r[:, None] // HG_CHUNK) == (c[None, :] // HG_CHUNK)

    def rows_upto(idx):
        return (same & (c[None, :] <= idx[:, None])).astype(np.float32)

    blocks = [rows_upto(r)]
    for m in HG_MM_LEVELS:
        blocks.append(rows_upto((r // (2 * m)) * (2 * m) + m))
    mall = np.concatenate(blocks, axis=0)

    x = r[:, None] ^ c[None, :]
    lv = np.full((n, n), -1, np.int32)
    for li, m in enumerate(HG_LEVELS):
        lv = np.where(same & (r[:, None] > c[None, :]) & (x >= m) & (x < 2 * m), li, lv)
    lv = np.where(r[:, None] == c[None, :], len(HG_LEVELS), lv)
    return jnp.asarray(mall, BF16), jnp.asarray(lv, jnp.int32)


def _hgrn_kernel(q_ref, f_ref, i_ref, g_ref, lb_ref, gn_ref, mall_ref, lv_ref, o_ref, st_ref):
    t = pl.program_id(1)
    n = HG_ROWS

    @pl.when(t == 0)
    def _():
        st_ref[...] = jnp.zeros(st_ref.shape, F32)

    rowid = lax.broadcasted_iota(jnp.int32, (n, 1), 0)
    lv = lv_ref[...]

    def head(hh):
        cols = slice(hh * HG_DK, (hh + 1) * HG_DK)
        q_in = q_ref[:, cols].astype(F32)
        qs = q_in * _sigmoid(q_in)
        lb = lb_ref[:, cols]
        f = lb + (1.0 - lb) * _sigmoid(f_ref[:, cols].astype(F32))
        kk = 1.0 - f
        logf = jnp.log(f)
        iv = i_ref[:, cols]

        l1 = logf.astype(BF16)
        r1 = logf - l1.astype(F32)
        l2 = r1.astype(BF16)
        l3 = (r1 - l2.astype(F32)).astype(BF16)
        parts = jnp.dot(mall_ref[...], jnp.concatenate([l1, l2, l3], axis=1), preferred_element_type=F32)
        yield
        bc = parts[:, :HG_DK] + parts[:, HG_DK:2 * HG_DK] + parts[:, 2 * HG_DK:]
        b = bc[:n]

        def anchor(period, row):
            b3 = b.reshape(n // period, period, HG_DK)
            return jnp.broadcast_to(b3[:, row:row + 1, :], b3.shape).reshape(n, HG_DK)

        a = jnp.zeros((n, n), F32)
        for li, m in enumerate(HG_LEVELS):
            if m in HG_MM_LEVELS:
                k = 1 + HG_MM_LEVELS.index(m)
                c = bc[k * n:(k + 1) * n]
            else:
                c = anchor(2 * m, m)
            e = jnp.exp(-jnp.abs(b - c))
            upper = (rowid & (2 * m - 1)) >= m
            x = (jnp.where(upper, qs, kk) * e).astype(BF16)
            p = lax.dot_general(x, x, (((1,), (1,)), ((), ())), preferred_element_type=F32)
            yield
            a = jnp.where(lv == li, p, a)
        a = jnp.where(lv == len(HG_LEVELS), jnp.sum(qs * kk, axis=-1, keepdims=True), a)
        o = jnp.dot(a.astype(BF16), iv, preferred_element_type=F32)
        yield

        blast = anchor(HG_CHUNK, HG_CHUNK - 1)
        qd = (qs * jnp.exp(b)).astype(BF16)
        kd = kk * jnp.exp(blast - b)
        ivt = iv.astype(F32).T.astype(BF16)
        st = st_ref[hh]
        outs = []
        for ci in range(n // HG_CHUNK):
            lo = ci * HG_CHUNK
            inter = lax.dot_general(qd[lo:lo + HG_CHUNK], st.astype(BF16), (((1,), (1,)), ((), ())),
                                    preferred_element_type=F32)
            kd_c = jnp.where((rowid >= lo) & (rowid < lo + HG_CHUNK), kd, 0.0).astype(BF16)
            upd = jnp.dot(ivt, kd_c, preferred_element_type=F32)
            yield
            outs.append(o[lo:lo + HG_CHUNK] + inter)
            st = jnp.exp(blast[lo:lo + 1]) * st + upd
        st_ref[hh] = st
        o = jnp.concatenate(outs, axis=0)

        g_in = g_ref[:, cols].astype(F32)
        o_ref[:, cols] = (_rms(o, gn_ref[:, cols]) * (g_in * _sigmoid(g_in))).astype(o_ref.dtype)

    progs = [head(hh) for hh in range(HG_NH)]
    while progs:
        progs = [pr for pr in progs if next(pr, "done") != "done"]


def _hgrn(hg4, lb, gn, mall, lv):
    s = hg4.shape[0]
    groups = HG_HEADS // HG_NH
    width = HG_NH * HG_DK

    def col(c):
        return lambda hh, t: (t, c * groups + hh)

    head = lambda hh, t: (0, hh)
    const = lambda hh, t: (0, 0)
    return pl.pallas_call(
        _hgrn_kernel,
        grid=(groups, s // HG_ROWS),
        in_specs=[
            pl.BlockSpec((HG_ROWS, width), col(0)),
            pl.BlockSpec((HG_ROWS, width), col(1)),
            pl.BlockSpec((HG_ROWS, width), col(2)),
            pl.BlockSpec((HG_ROWS, width), col(3)),
            pl.BlockSpec((1, width), head),
            pl.BlockSpec((1, width), head),
            pl.BlockSpec(mall.shape, const),
            pl.BlockSpec(lv.shape, const),
        ],
        out_specs=pl.BlockSpec((HG_ROWS, width), lambda hh, t: (t, hh)),
        out_shape=jax.ShapeDtypeStruct((s, D_HG), BF16),
        scratch_shapes=[pltpu.VMEM((HG_NH, HG_DV, HG_DK), F32)],
        compiler_params=_cparams(("parallel", "arbitrary"), 32),
        name="hgrn2",
    )(hg4, hg4, hg4, hg4, lb, gn, mall, lv)


def _out_route_kernel(x_ref, om_ref, oh_ref, wa_ref, wb_ref, g_ref, wr_ref, br_ref,
                      h_ref, hn_ref, rt_ref):
    h1 = (x_ref[...]
          + jnp.dot(om_ref[...], wa_ref[...], preferred_element_type=F32)
          + jnp.dot(oh_ref[...], wb_ref[...], preferred_element_type=F32))
    h_ref[...] = h1
    hn = _rms(h1, g_ref[...])
    hn_ref[...] = hn
    hn_hi = hn.astype(BF16)
    hn_lo = (hn - hn_hi.astype(F32)).astype(BF16)
    hh = jnp.dot(hn_hi, wr_ref[...], preferred_element_type=F32)
    lh = jnp.dot(hn_lo, wr_ref[:, :LANES], preferred_element_type=F32)
    logits = hh[:, :LANES] + (hh[:, LANES:] + lh) + br_ref[...]

    lane = lax.broadcasted_iota(jnp.int32, logits.shape, 1)
    lanef = lane.astype(F32)
    ninf = -jnp.inf
    big = float(LANES)

    is_g = lane < N_GROUPS
    gl = jnp.where(is_g, logits, ninf)
    gmax = jnp.max(gl, axis=-1, keepdims=True)
    gsum = jnp.sum(jnp.where(is_g, jnp.exp(gl - gmax), 0.0), axis=-1, keepdims=True)
    g_w = 1.0 / gsum
    g_idx = jnp.min(jnp.where(gl == gmax, lanef, big), axis=-1, keepdims=True)

    e_lane = lane - N_GROUPS
    in_grp = (e_lane >= 0) & (e_lane < N_EXPERTS) & ((e_lane >> 3).astype(F32) == g_idx)
    el = jnp.where(in_grp, logits, ninf)
    emax = jnp.max(el, axis=-1, keepdims=True)
    esum = jnp.sum(jnp.where(in_grp, jnp.exp(el - emax), 0.0), axis=-1, keepdims=True)
    i1 = jnp.min(jnp.where(el == emax, lanef, big), axis=-1, keepdims=True)
    el2 = jnp.where(lanef == i1, ninf, el)
    emax2 = jnp.max(el2, axis=-1, keepdims=True)
    i2 = jnp.min(jnp.where(el2 == emax2, lanef, big), axis=-1, keepdims=True)
    p1 = 1.0 / esum
    p2 = jnp.exp(emax2 - emax) / esum
    w1 = g_w * p1 / (p1 + p2)
    w2 = g_w * p2 / (p1 + p2)

    rt = jnp.where(lane == 0, i1 - N_GROUPS,
                   jnp.where(lane == 1, i2 - N_GROUPS,
                             jnp.where(lane == 2, w1, jnp.where(lane == 3, w2, 0.0))))
    rt_ref[...] = rt


def _out_route(x, o_mla, o_hg, wa, wb, gain, wr, br, tm):
    s, d = x.shape
    row = lambda i: (i, 0)
    full = lambda i: (0, 0)
    return pl.pallas_call(
        _out_route_kernel,
        grid=(s // tm,),
        in_specs=[
            pl.BlockSpec((tm, d), row),
            pl.BlockSpec((tm, D_MLA), row),
            pl.BlockSpec((tm, D_HG), row),
            pl.BlockSpec(wa.shape, full, pipeline_mode=pl.Buffered(1)),
            pl.BlockSpec(wb.shape, full, pipeline_mode=pl.Buffered(1)),
            pl.BlockSpec((1, d), full),
            pl.BlockSpec(wr.shape, full, pipeline_mode=pl.Buffered(1)),
            pl.BlockSpec((1, LANES), full),
        ],
        out_specs=[
            pl.BlockSpec((tm, d), row),
            pl.BlockSpec((tm, d), row),
            pl.BlockSpec((tm, LANES), row),
        ],
        out_shape=[
            jax.ShapeDtypeStruct((s, d), F32),
            jax.ShapeDtypeStruct((s, d), F32),
            jax.ShapeDtypeStruct((s, LANES), F32),
        ],
        compiler_params=_cparams(("parallel",), 56),
        name="out_proj_route",
    )(x, o_mla, o_hg, wa, wb, gain, wr, br)


def _moe_kernel(be_ref, nv_ref, dst_ref, nu_ref, hn_hbm, wg_hbm, wu_hbm, wd_hbm, y_hbm,
                xbuf, xsem, obuf, osem, wgb, wub, wdb, wsem, wslot, *, n_tok):
    i = pl.program_id(0)
    nused = nu_ref[0]
    e = be_ref[i]
    active = i < nused
    first = (i == 0) | (e != be_ref[jnp.maximum(i - 1, 0)])

    def gather(blk, sl, wait):
        nv = nv_ref[blk]
        for lo in range(0, MOE_BLOCK, GATHER_CHUNK):
            @pl.when(lo < nv)
            def _():
                if wait:
                    pltpu.make_async_copy(hn_hbm.at[pl.ds(0, GATHER_CHUNK), :],
                                          xbuf.at[sl, pl.ds(lo, GATHER_CHUNK), :], xsem.at[sl]).wait()
                else:
                    for r in range(lo, lo + GATHER_CHUNK):
                        tok = dst_ref[blk * MOE_BLOCK + r] & (n_tok - 1)
                        pltpu.make_async_copy(hn_hbm.at[pl.ds(tok, 1), :], xbuf.at[sl, pl.ds(r, 1), :],
                                              xsem.at[sl]).start(priority=r % 2)

    def scatter(blk, sl, wait):
        nv = nv_ref[blk]

        def row_copy(r):
            dst = dst_ref[blk * MOE_BLOCK + r]
            return pltpu.make_async_copy(obuf.at[sl, pl.ds(r, 1), :], y_hbm.at[pl.ds(dst, 1), :], osem.at[sl])

        for lo in range(0, MOE_BLOCK, GATHER_CHUNK):
            @pl.when(lo + GATHER_CHUNK <= nv)
            def _():
                if wait:
                    pltpu.make_async_copy(obuf.at[sl, pl.ds(lo, GATHER_CHUNK), :],
                                          y_hbm.at[pl.ds(0, GATHER_CHUNK), :], osem.at[sl]).wait()
                else:
                    for r in range(lo, lo + GATHER_CHUNK):
                        row_copy(r).start(priority=r % 2)

        def tail(r, c):
            cp = row_copy(r)
            cp.wait() if wait else cp.start()
            return c
        lax.fori_loop((nv // GATHER_CHUNK) * GATHER_CHUNK, nv, tail, 0)

    def on_slot(val, fn):
        for s_ in (0, 1):
            pl.when(val == s_)(functools.partial(fn, s_))

    def weight_copies(ex, sl):
        return (pltpu.make_async_copy(wg_hbm.at[ex], wgb.at[sl], wsem.at[sl, 0]),
                pltpu.make_async_copy(wu_hbm.at[ex], wub.at[sl], wsem.at[sl, 1]),
                pltpu.make_async_copy(wd_hbm.at[ex], wdb.at[sl], wsem.at[sl, 2]))

    @pl.when((i == 0) & active)
    def _():
        xbuf[...] = jnp.zeros(xbuf.shape, F32)
        wslot[0] = 1
        for n, cp in enumerate(weight_copies(e, 0)):
            cp.start(priority=n % 2)
        gather(0, 0, wait=False)

    @pl.when(active & first)
    def _():
        sl = 1 - wslot[0]
        wslot[0] = sl
        nxt = lax.while_loop(lambda j: (j < nused) & (be_ref[jnp.minimum(j, nused - 1)] == e),
                             lambda j: j + 1, i + 1)

        @pl.when(nxt < nused)
        def _():
            for n, cp in enumerate(weight_copies(be_ref[nxt], 1 - sl)):
                cp.start(priority=n % 2)

        for cp in weight_copies(e, sl):
            cp.wait()

    @pl.when(active)
    def _():
        sl = wslot[0]
        xs = i % 2

        @pl.when(i + 1 < nused)
        def _():
            on_slot(xs, lambda s_: gather(i + 1, 1 - s_, wait=False))

        @pl.when(i >= 2)
        def _():
            on_slot(xs, lambda s_: scatter(i - 2, s_, wait=True))

        on_slot(xs, lambda s_: gather(i, s_, wait=True))
        x = xbuf[xs].astype(BF16)
        g = jnp.dot(x, wgb[sl].astype(BF16), preferred_element_type=F32)
        u = jnp.dot(x, wub[sl].astype(BF16), preferred_element_type=F32)
        hmid = (g * _sigmoid(g) * u).astype(BF16)
        obuf[xs] = jnp.dot(hmid, wdb[sl].astype(BF16), preferred_element_type=F32)
        on_slot(xs, lambda s_: scatter(i, s_, wait=False))

        @pl.when(i == nused - 1)
        def _():
            @pl.when(i >= 1)
            def _():
                on_slot(xs, lambda s_: scatter(i - 1, 1 - s_, wait=True))

            on_slot(xs, lambda s_: scatter(i, s_, wait=True))


def _moe(block_e, nvalid, slot_dst, nused, hn, wg, wu, wd):
    nb = block_e.shape[0]
    n_tok, d = hn.shape
    assert n_tok & (n_tok - 1) == 0
    grid_spec = pltpu.PrefetchScalarGridSpec(
        num_scalar_prefetch=4,
        grid=(nb,),
        in_specs=[pl.BlockSpec(memory_space=pl.ANY)] * 4,
        out_specs=pl.BlockSpec(memory_space=pl.ANY),
        scratch_shapes=[
            pltpu.VMEM((2, MOE_BLOCK, d), F32),
            pltpu.SemaphoreType.DMA((2,)),
            pltpu.VMEM((2, MOE_BLOCK, d), F32),
            pltpu.SemaphoreType.DMA((2,)),
            pltpu.VMEM((2, d, D_EXPERT), F32),
            pltpu.VMEM((2, d, D_EXPERT), F32),
            pltpu.VMEM((2, D_EXPERT, d), F32),
            pltpu.SemaphoreType.DMA((2, 3)),
            pltpu.SMEM((1,), jnp.int32),
        ],
    )
    return pl.pallas_call(
        functools.partial(_moe_kernel, n_tok=n_tok),
        grid_spec=grid_spec,
        out_shape=jax.ShapeDtypeStruct((TOP_K * n_tok, d), F32),
        compiler_params=_cparams(("arbitrary",), 56),
        name="moe_experts",
    )(block_e, nvalid, slot_dst, nused, hn, wg, wu, wd)


def _ple_kernel(h_ref, rt_ref, y0_ref, y1_ref, p_ref, wg_ref, bg_ref, wp_ref, gp_ref, gf_ref, o_ref):
    rt = rt_ref[...]
    h2 = h_ref[...] + (rt[:, TOP_K:TOP_K + 1] * y0_ref[...] + rt[:, TOP_K + 1:TOP_K + 2] * y1_ref[...])
    hn = _rms(h2, gp_ref[...]).astype(BF16)
    gate = _sigmoid(jnp.dot(hn, wg_ref[...], preferred_element_type=F32) + bg_ref[...])
    pe = jnp.dot(p_ref[...].astype(BF16), wp_ref[...], preferred_element_type=F32)
    h3 = h2 + gate * pe
    o_ref[...] = _rms(h3, gf_ref[...])


def _ple_final(h1, rt, y, p, wg, bg, wp, gp, gf, tm):
    s, d = h1.shape
    nblk = s // tm
    row = lambda i: (i, 0)
    full = lambda i: (0, 0)
    return pl.pallas_call(
        _ple_kernel,
        grid=(nblk,),
        in_specs=[
            pl.BlockSpec((tm, d), row),
            pl.BlockSpec((tm, LANES), row),
            pl.BlockSpec((tm, d), row),
            pl.BlockSpec((tm, d), lambda i: (i + nblk, 0)),
            pl.BlockSpec((tm, PLE_DIM), row),
            pl.BlockSpec(wg.shape, full, pipeline_mode=pl.Buffered(1)),
            pl.BlockSpec((1, d), full),
            pl.BlockSpec(wp.shape, full, pipeline_mode=pl.Buffered(1)),
            pl.BlockSpec((1, d), full),
            pl.BlockSpec((1, d), full),
        ],
        out_specs=pl.BlockSpec((tm, d), row),
        out_shape=jax.ShapeDtypeStruct((s, d), F32),
        compiler_params=_cparams(("parallel",), 56),
        name="ple_final",
    )(h1, rt, y, y, p, wg, bg, wp, gp, gf)


def _slots_kernel(rt_ref, tri_ref, pos_ref, cnt_ref, carry_ref, base_ref):
    ps = pl.program_id(0)
    b = pl.program_id(1)
    rt = rt_ref[...]
    lane = lax.broadcasted_iota(jnp.int32, rt.shape, 1)
    lanef = lane.astype(F32)
    oh = [(lanef == rt[:, kk:kk + 1]).astype(F32) for kk in range(TOP_K)]
    both = oh[0] + oh[1]
    colsum = jnp.sum(both, axis=0, keepdims=True)

    @pl.when((ps == 0) & (b == 0))
    def _():
        cnt_ref[...] = jnp.zeros(cnt_ref.shape, F32)

    @pl.when(ps == 0)
    def _():
        cnt_ref[...] = cnt_ref[...] + colsum

    @pl.when((ps == 1) & (b == 0))
    def _():
        blocks = jnp.floor((cnt_ref[...] + (MOE_BLOCK - 0.5)) / MOE_BLOCK)
        r = lax.broadcasted_iota(jnp.int32, (LANES, LANES), 0)
        c = lax.broadcasted_iota(jnp.int32, (LANES, LANES), 1)
        before = (r < c).astype(F32)
        base_ref[...] = jnp.dot(blocks * MOE_BLOCK, before, preferred_element_type=F32,
                                precision=lax.Precision.HIGHEST)
        carry_ref[...] = jnp.zeros(carry_ref.shape, F32)

    @pl.when(ps == 1)
    def _():
        earlier = jnp.dot(tri_ref[...], both.astype(BF16), preferred_element_type=F32)
        row = earlier + (base_ref[0:1, :] + carry_ref[0:1, :])
        s0 = jnp.sum(oh[0] * row, axis=-1, keepdims=True)
        s1 = jnp.sum(oh[1] * row, axis=-1, keepdims=True)
        pos_ref[...] = jnp.where(lane == 0, s0, jnp.where(lane == 1, s1, 0.0))
        carry_ref[...] = carry_ref[...] + colsum


def _slots(rt, tb):
    s = rt.shape[0]
    tri = (np.arange(tb)[:, None] > np.arange(tb)[None, :]).astype(np.float32)
    return pl.pallas_call(
        _slots_kernel,
        grid=(2, s // tb),
        in_specs=[
            pl.BlockSpec((tb, LANES), lambda ps, b: (b, 0)),
            pl.BlockSpec((tb, tb), lambda ps, b: (0, 0)),
        ],
        out_specs=[
            pl.BlockSpec((tb, LANES), lambda ps, b: (b * ps, 0)),
            pl.BlockSpec((8, LANES), lambda ps, b: (0, 0)),
        ],
        out_shape=[
            jax.ShapeDtypeStruct((s, LANES), F32),
            jax.ShapeDtypeStruct((8, LANES), F32),
        ],
        scratch_shapes=[pltpu.VMEM((8, LANES), F32), pltpu.VMEM((8, LANES), F32)],
        compiler_params=_cparams(("arbitrary", "arbitrary")),
        name="moe_slots",
    )(rt, jnp.asarray(tri, BF16))


def _dispatch(rt, n_tok):
    a = n_tok * TOP_K
    nb = -(-a // MOE_BLOCK) + N_EXPERTS
    pos_f, cnt = _slots(rt, 512)
    pos = pos_f[:, :TOP_K].astype(jnp.int32).reshape(-1)
    counts = cnt[0, :N_EXPERTS].astype(jnp.int32)
    nblk = (counts + MOE_BLOCK - 1) // MOE_BLOCK
    bends = jnp.cumsum(nblk)
    bidx = jnp.arange(nb, dtype=jnp.int32)
    block_e = jnp.clip(jnp.searchsorted(bends, bidx, side="right"), 0, N_EXPERTS - 1).astype(jnp.int32)
    nused = bends[-1].astype(jnp.int32)
    within = bidx - (bends - nblk)[block_e]
    nvalid = jnp.where(bidx < nused, jnp.clip(counts[block_e] - within * MOE_BLOCK, 0, MOE_BLOCK), 0).astype(jnp.int32)
    a_idx = jnp.arange(a, dtype=jnp.int32)
    slot_dst = jnp.zeros((nb * MOE_BLOCK,), jnp.int32).at[pos].set((a_idx % TOP_K) * n_tok + a_idx // TOP_K)
    return block_e, nvalid, slot_dst, nused.reshape(1)


def kernel(x, p, positions, attn_norm, w_in, q_norm, w_uq, kv_norm, w_ukv, mla_norm, hg_lb_logits, hg_norm, w_out, ffn_norm, w_router_group, b_router_group, w_router_expert, b_router_expert, w_exp_gate, w_exp_up, w_exp_down, ple_norm, w_ple_gate, b_ple_gate, w_ple_proj, final_norm):
    bsz, s, d = x.shape
    assert bsz == 1 and w_in.shape[0] == 1
    xt = x[0]

    inv_freq = 1.0 / (ROPE_THETA ** (jnp.arange(0, QK_ROPE, 2, dtype=F32) / QK_ROPE))
    ang = positions[0].astype(F32)[:, None] * inv_freq
    cos, sin = jnp.cos(ang), jnp.sin(ang)
    zpad = jnp.zeros((s, LANES - QK_ROPE), F32)
    cc = jnp.concatenate([cos, cos, zpad], axis=1)
    ss = jnp.concatenate([-sin, sin, zpad], axis=1)

    lb = jnp.cumsum(jax.nn.softmax(hg_lb_logits.astype(F32), axis=0), axis=0)[0][None, :]

    wi = w_in[0]
    kr0 = Q_LORA + KV_LORA
    half = QK_ROPE // 2
    w_lat = jnp.concatenate(
        [wi[:, :kr0 + QK_ROPE], wi[:, kr0 + half:kr0 + QK_ROPE], wi[:, kr0:kr0 + half]], axis=1).astype(BF16)
    w_hg = wi[:, kr0 + QK_ROPE:].astype(BF16)
    wq3 = w_uq[0].reshape(Q_LORA, MLA_HEADS, QK_HEAD)
    wq_pad = jnp.concatenate(
        [wq3, wq3[:, :, QK_NOPE + half:], wq3[:, :, QK_NOPE:QK_NOPE + half]], axis=2
    ).reshape(Q_LORA, MLA_HEADS * QK_PAD).astype(BF16)
    wkv3 = w_ukv[0].reshape(KV_LORA, MLA_HEADS, QK_NOPE + V_HEAD)
    wkn = wkv3[:, :, :QK_NOPE].reshape(KV_LORA, MLA_HEADS * QK_NOPE).astype(BF16)
    wvt = wkv3[:, :, QK_NOPE:].reshape(KV_LORA, D_MLA).T.astype(BF16)
    wo = w_out[0].astype(BF16)
    wr = jnp.concatenate(
        [w_router_group[0], w_router_expert[0], jnp.zeros((d, LANES - N_GROUPS - N_EXPERTS), F32)], axis=1)
    wr_hi = wr.astype(BF16)
    wr = jnp.concatenate([wr_hi, (wr - wr_hi.astype(F32)).astype(BF16)], axis=1)
    br = jnp.concatenate(
        [b_router_group[0], b_router_expert[0], jnp.zeros((LANES - N_GROUPS - N_EXPERTS,), F32)])[None, :]

    lat = _in_proj(xt, attn_norm, w_lat, F32, 512, D_LAT, "in_proj_latent")
    hg4 = _in_proj(xt, attn_norm, w_hg, BF16, 1024, 1024, "in_proj_hgrn")
    q, k, vt = _mla_up(lat, q_norm, kv_norm, wq_pad, wkn, wvt, cc, ss, 512)
    o_mla = _attention(q, k, vt, mla_norm, 512, 512)
    mall, lv = _hgrn_tables()
    o_hg = _hgrn(hg4, lb, hg_norm[0].reshape(1, D_HG), mall, lv)

    h1, hn, rt = _out_route(xt, o_mla, o_hg, wo[:D_MLA], wo[D_MLA:], ffn_norm, wr, br, 512)
    block_e, nvalid, slot_dst, nused = _dispatch(rt, s)
    y = _moe(block_e, nvalid, slot_dst, nused, hn, w_exp_gate[0], w_exp_up[0], w_exp_down[0])

    out = _ple_final(h1, rt, y, p[0, 0], w_ple_gate[0].astype(BF16), b_ple_gate,
                     w_ple_proj[0].astype(BF16), ple_norm, final_norm[None, :], 512)
    return out[None]
```

```python
import functools

import jax
import jax.numpy as jnp
import numpy as np
from jax import lax
from jax.experimental import pallas as pl
from jax.experimental.pallas import tpu as pltpu

F32 = jnp.float32
BF16 = jnp.bfloat16

D_MODEL = 2048
PLE_DIM = 256
MLA_HEADS = 8
QK_NOPE = 128
QK_ROPE = 64
QK_HEAD = QK_NOPE + QK_ROPE
QK_PAD = 256
V_HEAD = 128
VT_ROWS = V_HEAD + 16
QK_AHEAD = 2
PV_LAG = 1
Q_LORA = 512
KV_LORA = 256
ROPE_THETA = 10000.0
HG_HEADS = 8
HG_DK = 128
HG_DV = 128
HG_CHUNK = 64
D_MLA = MLA_HEADS * V_HEAD
D_HG = HG_HEADS * HG_DV
N_GROUPS = 8
EXPERTS_PER_GROUP = 8
N_EXPERTS = N_GROUPS * EXPERTS_PER_GROUP
TOP_K = 2
D_EXPERT = 512
EPS = 1e-6
LANES = 128
D_LAT = Q_LORA + KV_LORA + 2 * QK_ROPE
NEG_BIG = -1e30
LOG2E = 1.4426950408889634

MOE_BLOCK = 320
GATHER_CHUNK = 32
HG_ROWS = 256
HG_LEVELS = (32, 16, 8, 4, 2, 1)
HG_MM_LEVELS = (2, 1)
HG_NH = 4


def _cparams(sem, vmem_mb=None, flags=None):
    kw = dict(dimension_semantics=sem)
    if vmem_mb is not None:
        kw["vmem_limit_bytes"] = vmem_mb * 1024 * 1024
    if flags is not None:
        kw["flags"] = flags
    return pltpu.CompilerParams(**kw)


def _rms(x, g):
    ms = jnp.mean(x * x, axis=-1, keepdims=True)
    return x * lax.rsqrt(ms + EPS) * g


def _sigmoid(x):
    return 1.0 / (1.0 + jnp.exp(-x))


def _pack_halves(x):
    n = x.shape[1] // 2
    bits = lax.bitcast_convert_type(x.astype(BF16).astype(F32), jnp.uint32)
    return bits[:, :n] | (bits[:, n:] >> 16)


def _unpack_halves(w):
    hi = lax.bitcast_convert_type(w & jnp.uint32(0xFFFF0000), F32)
    lo = lax.bitcast_convert_type(w << 16, F32)
    return jnp.concatenate([hi, lo], axis=1)


def _in_proj_kernel(x_ref, g_ref, w_ref, o_ref, xn_ref):
    @pl.when(pl.program_id(1) == 0)
    def _():
        xn_ref[...] = _rms(x_ref[...], g_ref[...]).astype(BF16)

    o_ref[...] = jnp.dot(xn_ref[...], w_ref[...], preferred_element_type=F32).astype(o_ref.dtype)


def _in_proj(x, gain, w, out_dtype, tm, tn, name):
    s, d = x.shape
    n = w.shape[1]
    return pl.pallas_call(
        _in_proj_kernel,
        grid=(s // tm, n // tn),
        in_specs=[
            pl.BlockSpec((tm, d), lambda i, j: (i, 0)),
            pl.BlockSpec((1, d), lambda i, j: (0, 0)),
            pl.BlockSpec((d, tn), lambda i, j: (0, j)),
        ],
        out_specs=pl.BlockSpec((tm, tn), lambda i, j: (i, j)),
        out_shape=jax.ShapeDtypeStruct((s, n), out_dtype),
        scratch_shapes=[pltpu.VMEM((tm, d), BF16)],
        compiler_params=_cparams(("parallel", "arbitrary"), 48),
        name=name,
    )(x, gain, w)


def _mla_up_kernel(lat_ref, qn_ref, kvn_ref, wq_ref, wkn_ref, wvt_ref, cc_ref, ss_ref,
                   q_ref, k_ref, vt_ref):
    lat = lat_ref[...]
    cq = _rms(lat[:, :Q_LORA], qn_ref[...]).astype(BF16)
    ckv = _rms(lat[:, Q_LORA:Q_LORA + KV_LORA], kvn_ref[...]).astype(BF16)
    kp = lat[:, Q_LORA + KV_LORA:]
    cc = cc_ref[...]
    ss = ss_ref[...]
    scale = QK_HEAD ** -0.5 * LOG2E

    q = jnp.dot(cq, wq_ref[...], preferred_element_type=F32)
    for h in range(MLA_HEADS):
        base = h * QK_PAD
        q_ref[:, base:base + QK_NOPE] = (q[:, base:base + QK_NOPE] * scale).astype(BF16)
        r = q[:, base + QK_NOPE:base + QK_PAD]
        r = (r * cc + pltpu.roll(r, QK_ROPE, 1) * ss) * scale
        q_ref[:, base + QK_NOPE:base + QK_PAD] = r.astype(BF16)

    kpe = (kp * cc + pltpu.roll(kp, QK_ROPE, 1) * ss).astype(BF16)
    kn = jnp.dot(ckv, wkn_ref[...], preferred_element_type=F32).astype(BF16)
    for h in range(MLA_HEADS):
        base = h * QK_PAD
        k_ref[:, base:base + QK_NOPE] = kn[:, h * QK_NOPE:(h + 1) * QK_NOPE]
        k_ref[:, base + QK_NOPE:base + QK_PAD] = kpe
    vt = lax.dot_general(wvt_ref[...], ckv, (((1,), (1,)), ((), ())), preferred_element_type=F32).astype(BF16)
    ones = jnp.ones((VT_ROWS - V_HEAD, vt.shape[1]), BF16)
    for h in range(MLA_HEADS):
        vt_ref[h * VT_ROWS:h * VT_ROWS + V_HEAD, :] = vt[h * V_HEAD:(h + 1) * V_HEAD]
        vt_ref[h * VT_ROWS + V_HEAD:(h + 1) * VT_ROWS, :] = ones


def _mla_up(lat, qn, kvn, wq, wkn, wvt, cc, ss, tm):
    s = lat.shape[0]
    row = lambda i: (i, 0)
    full = lambda i: (0, 0)
    return pl.pallas_call(
        _mla_up_kernel,
        grid=(s // tm,),
        in_specs=[
            pl.BlockSpec((tm, D_LAT), row),
            pl.BlockSpec((1, Q_LORA), full),
            pl.BlockSpec((1, KV_LORA), full),
            pl.BlockSpec(wq.shape, full),
            pl.BlockSpec(wkn.shape, full),
            pl.BlockSpec(wvt.shape, full),
            pl.BlockSpec((tm, LANES), row),
            pl.BlockSpec((tm, LANES), row),
        ],
        out_specs=[
            pl.BlockSpec((tm, MLA_HEADS * QK_PAD), row),
            pl.BlockSpec((tm, MLA_HEADS * QK_PAD), row),
            pl.BlockSpec((MLA_HEADS * VT_ROWS, tm), lambda i: (0, i)),
        ],
        out_shape=[
            jax.ShapeDtypeStruct((s, MLA_HEADS * QK_PAD), BF16),
            jax.ShapeDtypeStruct((s, MLA_HEADS * QK_PAD), BF16),
            jax.ShapeDtypeStruct((MLA_HEADS * VT_ROWS, s), BF16),
        ],
        compiler_params=_cparams(("parallel",), 48),
        name="mla_up",
    )(lat, qn, kvn, wq, wkn, wvt, cc, ss)


def _attn_kernel(it_ref, jt_ref, q_ref, k_ref, vt_ref, g_ref, o_ref, acc_ref, m_ref, *, tq, tk, qc):
    t = pl.program_id(0)
    i = it_ref[t]
    j = jt_ref[t]

    @pl.when(j == 0)
    def _():
        m_ref[...] = jnp.full(m_ref.shape, NEG_BIG, F32)
        acc_ref[...] = jnp.zeros(acc_ref.shape, F32)

    units = [(h, c) for h in range(MLA_HEADS) for c in range(tq // qc)]

    def scores(u):
        h, c = u
        q = q_ref[c * qc:(c + 1) * qc, h * QK_PAD:(h + 1) * QK_PAD]
        k = k_ref[:, h * QK_PAD:(h + 1) * QK_PAD]
        return lax.dot_general(k, q, (((1,), (1,)), ((), ())), preferred_element_type=F32)

    def step(masked):
        if masked:
            diff = lax.broadcasted_iota(jnp.int32, (tk, qc), 0) - lax.broadcasted_iota(jnp.int32, (tk, qc), 1)
        def accumulate(h, cols, alpha, p):
            pv = jnp.dot(vt_ref[h * VT_ROWS:(h + 1) * VT_ROWS, :], p, preferred_element_type=F32)
            acc_ref[h, :, cols] = alpha * acc_ref[h, :, cols] + pv

        ahead = [scores(units[n]) for n in range(min(QK_AHEAD, len(units)))]
        pending = []
        for n, (h, c) in enumerate(units):
            s = ahead.pop(0)
            if n + QK_AHEAD < len(units):
                ahead.append(scores(units[n + QK_AHEAD]))
            cols = slice(c * qc, (c + 1) * qc)
            if masked:
                s = jnp.where(diff <= c * qc, s, NEG_BIG)
            m_prev = m_ref[h, :, cols]
            m_new = jnp.maximum(m_prev, jnp.max(s, axis=0, keepdims=True))
            alpha = jnp.exp2(m_prev - m_new)
            p = jnp.exp2(s - m_new).astype(BF16)
            m_ref[h, :, cols] = m_new
            pending.append((h, cols, alpha, p))
            if len(pending) > PV_LAG:
                accumulate(*pending.pop(0))
        for item in pending:
            accumulate(*item)

    @pl.when(j < i)
    def _():
        step(False)

    @pl.when(j == i)
    def _():
        step(True)
        outs = [acc_ref[h, :V_HEAD, :] * (1.0 / acc_ref[h, V_HEAD:V_HEAD + 1, :]) for h in range(MLA_HEADS)]
        o_ref[...] = _rms(jnp.concatenate(outs, axis=0).T, g_ref[...]).astype(o_ref.dtype)


def _attention(q, k, vt, gain, tb, qc):
    s = q.shape[0]
    nq = s // tb
    it = np.concatenate([np.full(i + 1, i, np.int32) for i in range(nq)])
    jt = np.concatenate([np.arange(i + 1, dtype=np.int32) for i in range(nq)])
    grid_spec = pltpu.PrefetchScalarGridSpec(
        num_scalar_prefetch=2,
        grid=(it.shape[0],),
        in_specs=[
            pl.BlockSpec((tb, MLA_HEADS * QK_PAD), lambda t, it, jt: (it[t], 0)),
            pl.BlockSpec((tb, MLA_HEADS * QK_PAD), lambda t, it, jt: (jt[t], 0)),
            pl.BlockSpec((MLA_HEADS * VT_ROWS, tb), lambda t, it, jt: (0, jt[t])),
            pl.BlockSpec((1, D_MLA), lambda t, it, jt: (0, 0)),
        ],
        out_specs=pl.BlockSpec((tb, D_MLA), lambda t, it, jt: (it[t], 0)),
        scratch_shapes=[
            pltpu.VMEM((MLA_HEADS, VT_ROWS, tb), F32),
            pltpu.VMEM((MLA_HEADS, 1, tb), F32),
        ],
    )
    return pl.pallas_call(
        functools.partial(_attn_kernel, tq=tb, tk=tb, qc=qc),
        grid_spec=grid_spec,
        out_shape=jax.ShapeDtypeStruct((s, D_MLA), BF16),
        compiler_params=_cparams(("arbitrary",), 48),
        name="mla_attention",
    )(jnp.asarray(it), jnp.asarray(jt), q, k, vt, gain)


def _hgrn_tables():
    n = HG_ROWS
    r = np.arange(n)
    c = np.arange(n)
    same = (r[:, None] // HG_CHUNK) == (c[None, :] // HG_CHUNK)

    def rows_upto(idx):
        return (same & (c[None, :] <= idx[:, None])).astype(np.float32)

    blocks = [rows_upto(r)]
    for m in HG_MM_LEVELS:
        blocks.append(rows_upto((r // (2 * m)) * (2 * m) + m))
    mall = np.concatenate(blocks, axis=0)

    x = r[:, None] ^ c[None, :]
    lv = np.full((n, n), -1, np.int32)
    for li, m in enumerate(HG_LEVELS):
        lv = np.where(same & (r[:, None] > c[None, :]) & (x >= m) & (x < 2 * m), li, lv)
    lv = np.where(r[:, None] == c[None, :], len(HG_LEVELS), lv)
    return jnp.asarray(mall, BF16), jnp.asarray(lv, jnp.int32)


def _hgrn_kernel(q_ref, f_ref, i_ref, g_ref, lb_ref, gn_ref, mall_ref, lv_ref, o_ref, st_ref):
    t = pl.program_id(1)
    n = HG_ROWS

    @pl.when(t == 0)
    def _():
        st_ref[...] = jnp.zeros(st_ref.shape, F32)

    rowid = lax.broadcasted_iota(jnp.int32, (n, 1), 0)
    lv = lv_ref[...]

    def head(hh):
        cols = slice(hh * HG_DK, (hh + 1) * HG_DK)
        q_in = q_ref[:, cols].astype(F32)
        qs = q_in * _sigmoid(q_in)
        lb = lb_ref[:, cols]
        f = lb + (1.0 - lb) * _sigmoid(f_ref[:, cols].astype(F32))
        kk = 1.0 - f
        logf = jnp.log(f)
        iv = i_ref[:, cols]

        l1 = logf.astype(BF16)
        l2 = (logf - l1.astype(F32)).astype(BF16)
        parts = jnp.dot(mall_ref[...], jnp.concatenate([l1, l2], axis=1), preferred_element_type=F32)
        yield
        bc = parts[:, :HG_DK] + parts[:, HG_DK:]
        b = bc[:n]

        def anchor(period, row):
            b3 = b.reshape(n // period, period, HG_DK)
            return jnp.broadcast_to(b3[:, row:row + 1, :], b3.shape).reshape(n, HG_DK)

        a = jnp.zeros((n, n), F32)
        for li, m in enumerate(HG_LEVELS):
            if m in HG_MM_LEVELS:
                k = 1 + HG_MM_LEVELS.index(m)
                c = bc[k * n:(k + 1) * n]
            else:
                c = anchor(2 * m, m)
            e = jnp.exp(-jnp.abs(b - c))
            upper = (rowid & (2 * m - 1)) >= m
            x = (jnp.where(upper, qs, kk) * e).astype(BF16)
            p = lax.dot_general(x, x, (((1,), (1,)), ((), ())), preferred_element_type=F32)
            yield
            a = jnp.where(lv == li, p, a)
        a = jnp.where(lv == len(HG_LEVELS), jnp.sum(qs * kk, axis=-1, keepdims=True), a)
        o = jnp.dot(a.astype(BF16), iv, preferred_element_type=F32)
        yield

        blast = anchor(HG_CHUNK, HG_CHUNK - 1)
        qd = (qs * jnp.exp(b)).astype(BF16)
        kd = kk * jnp.exp(blast - b)
        ivt = iv.astype(F32).T.astype(BF16)
        st = st_ref[hh]
        outs = []
        for ci in range(n // HG_CHUNK):
            lo = ci * HG_CHUNK
            inter = lax.dot_general(qd[lo:lo + HG_CHUNK], st.astype(BF16), (((1,), (1,)), ((), ())),
                                    preferred_element_type=F32)
            kd_c = jnp.where((rowid >= lo) & (rowid < lo + HG_CHUNK), kd, 0.0).astype(BF16)
            upd = jnp.dot(ivt, kd_c, preferred_element_type=F32)
            yield
            outs.append(o[lo:lo + HG_CHUNK] + inter)
            st = jnp.exp(blast[lo:lo + 1]) * st + upd
        st_ref[hh] = st
        o = jnp.concatenate(outs, axis=0)

        g_in = g_ref[:, cols].astype(F32)
        o_ref[:, cols] = (_rms(o, gn_ref[:, cols]) * (g_in * _sigmoid(g_in))).astype(o_ref.dtype)

    progs = [head(hh) for hh in range(HG_NH)]
    while progs:
        progs = [pr for pr in progs if next(pr, "done") != "done"]


def _hgrn(hg4, lb, gn, mall, lv):
    s = hg4.shape[0]
    groups = HG_HEADS // HG_NH
    width = HG_NH * HG_DK

    def col(c):
        return lambda hh, t: (t, c * groups + hh)

    head = lambda hh, t: (0, hh)
    const = lambda hh, t: (0, 0)
    return pl.pallas_call(
        _hgrn_kernel,
        grid=(groups, s // HG_ROWS),
        in_specs=[
            pl.BlockSpec((HG_ROWS, width), col(0)),
            pl.BlockSpec((HG_ROWS, width), col(1)),
            pl.BlockSpec((HG_ROWS, width), col(2)),
            pl.BlockSpec((HG_ROWS, width), col(3)),
            pl.BlockSpec((1, width), head),
            pl.BlockSpec((1, width), head),
            pl.BlockSpec(mall.shape, const),
            pl.BlockSpec(lv.shape, const),
        ],
        out_specs=pl.BlockSpec((HG_ROWS, width), lambda hh, t: (t, hh)),
        out_shape=jax.ShapeDtypeStruct((s, D_HG), BF16),
        scratch_shapes=[pltpu.VMEM((HG_NH, HG_DV, HG_DK), F32)],
        compiler_params=_cparams(("parallel", "arbitrary"), 32),
        name="hgrn2",
    )(hg4, hg4, hg4, hg4, lb, gn, mall, lv)


def _out_route_kernel(x_ref, om_ref, oh_ref, wa_ref, wb_ref, g_ref, wr_ref, br_ref,
                      h_ref, hn_ref, rt_ref):
    h1 = (x_ref[...]
          + jnp.dot(om_ref[...], wa_ref[...], preferred_element_type=F32)
          + jnp.dot(oh_ref[...], wb_ref[...], preferred_element_type=F32))
    h_ref[...] = h1
    hn = _rms(h1, g_ref[...])
    hn_ref[...] = _pack_halves(hn)
    hn_hi = hn.astype(BF16)
    hn_lo = (hn - hn_hi.astype(F32)).astype(BF16)
    hh = jnp.dot(hn_hi, wr_ref[...], preferred_element_type=F32)
    lh = jnp.dot(hn_lo, wr_ref[:, :LANES], preferred_element_type=F32)
    logits = hh[:, :LANES] + (hh[:, LANES:] + lh) + br_ref[...]

    lane = lax.broadcasted_iota(jnp.int32, logits.shape, 1)
    lanef = lane.astype(F32)
    ninf = -jnp.inf
    big = float(LANES)

    is_g = lane < N_GROUPS
    gl = jnp.where(is_g, logits, ninf)
    gmax = jnp.max(gl, axis=-1, keepdims=True)
    gsum = jnp.sum(jnp.where(is_g, jnp.exp(gl - gmax), 0.0), axis=-1, keepdims=True)
    g_w = 1.0 / gsum
    g_idx = jnp.min(jnp.where(gl == gmax, lanef, big), axis=-1, keepdims=True)

    e_lane = lane - N_GROUPS
    in_grp = (e_lane >= 0) & (e_lane < N_EXPERTS) & ((e_lane >> 3).astype(F32) == g_idx)
    el = jnp.where(in_grp, logits, ninf)
    emax = jnp.max(el, axis=-1, keepdims=True)
    esum = jnp.sum(jnp.where(in_grp, jnp.exp(el - emax), 0.0), axis=-1, keepdims=True)
    i1 = jnp.min(jnp.where(el == emax, lanef, big), axis=-1, keepdims=True)
    el2 = jnp.where(lanef == i1, ninf, el)
    emax2 = jnp.max(el2, axis=-1, keepdims=True)
    i2 = jnp.min(jnp.where(el2 == emax2, lanef, big), axis=-1, keepdims=True)
    p1 = 1.0 / esum
    p2 = jnp.exp(emax2 - emax) / esum
    w1 = g_w * p1 / (p1 + p2)
    w2 = g_w * p2 / (p1 + p2)

    rt = jnp.where(lane == 0, i1 - N_GROUPS,
                   jnp.where(lane == 1, i2 - N_GROUPS,
                             jnp.where(lane == 2, w1, jnp.where(lane == 3, w2, 0.0))))
    rt_ref[...] = rt


def _out_route(x, o_mla, o_hg, wa, wb, gain, wr, br, tm):
    s, d = x.shape
    row = lambda i: (i, 0)
    full = lambda i: (0, 0)
    return pl.pallas_call(
        _out_route_kernel,
        grid=(s // tm,),
        in_specs=[
            pl.BlockSpec((tm, d), row),
            pl.BlockSpec((tm, D_MLA), row),
            pl.BlockSpec((tm, D_HG), row),
            pl.BlockSpec(wa.shape, full, pipeline_mode=pl.Buffered(1)),
            pl.BlockSpec(wb.shape, full, pipeline_mode=pl.Buffered(1)),
            pl.BlockSpec((1, d), full),
            pl.BlockSpec(wr.shape, full, pipeline_mode=pl.Buffered(1)),
            pl.BlockSpec((1, LANES), full),
        ],
        out_specs=[
            pl.BlockSpec((tm, d), row),
            pl.BlockSpec((tm, d // 2), row),
            pl.BlockSpec((tm, LANES), row),
        ],
        out_shape=[
            jax.ShapeDtypeStruct((s, d), F32),
            jax.ShapeDtypeStruct((s, d // 2), jnp.uint32),
            jax.ShapeDtypeStruct((s, LANES), F32),
        ],
        compiler_params=_cparams(("parallel",), 56),
        name="out_proj_route",
    )(x, o_mla, o_hg, wa, wb, gain, wr, br)


def _moe_kernel(be_ref, nv_ref, dst_ref, nu_ref, hn_hbm, wg_hbm, wu_hbm, wd_hbm, y_hbm,
                xbuf, xsem, obuf, osem, wgb, wub, wdb, wsem, wslot, *, n_tok):
    i = pl.program_id(0)
    nused = nu_ref[0]
    e = be_ref[i]
    active = i < nused
    first = (i == 0) | (e != be_ref[jnp.maximum(i - 1, 0)])

    def gather(blk, sl, wait):
        nv = nv_ref[blk]
        for lo in range(0, MOE_BLOCK, GATHER_CHUNK):
            @pl.when(lo < nv)
            def _():
                if wait:
                    pltpu.make_async_copy(hn_hbm.at[pl.ds(0, GATHER_CHUNK), :],
                                          xbuf.at[sl, pl.ds(lo, GATHER_CHUNK), :], xsem.at[sl]).wait()
                else:
                    for r in range(lo, lo + GATHER_CHUNK):
                        tok = dst_ref[blk * MOE_BLOCK + r] & (n_tok - 1)
                        pltpu.make_async_copy(hn_hbm.at[pl.ds(tok, 1), :], xbuf.at[sl, pl.ds(r, 1), :],
                                              xsem.at[sl]).start(priority=r % 2)

    def scatter(blk, sl, wait):
        nv = nv_ref[blk]

        def row_copy(r):
            dst = dst_ref[blk * MOE_BLOCK + r]
            return pltpu.make_async_copy(obuf.at[sl, pl.ds(r, 1), :], y_hbm.at[pl.ds(dst, 1), :], osem.at[sl])

        for lo in range(0, MOE_BLOCK, GATHER_CHUNK):
            @pl.when(lo + GATHER_CHUNK <= nv)
            def _():
                if wait:
                    pltpu.make_async_copy(obuf.at[sl, pl.ds(lo, GATHER_CHUNK), :],
                                          y_hbm.at[pl.ds(0, GATHER_CHUNK), :], osem.at[sl]).wait()
                else:
                    for r in range(lo, lo + GATHER_CHUNK):
                        row_copy(r).start(priority=r % 2)

        def tail(r, c):
            cp = row_copy(r)
            cp.wait() if wait else cp.start()
            return c
        lax.fori_loop((nv // GATHER_CHUNK) * GATHER_CHUNK, nv, tail, 0)

    def on_slot(val, fn):
        for s_ in (0, 1):
            pl.when(val == s_)(functools.partial(fn, s_))

    def weight_copies(ex, sl):
        return (pltpu.make_async_copy(wg_hbm.at[ex], wgb.at[sl], wsem.at[sl, 0]),
                pltpu.make_async_copy(wu_hbm.at[ex], wub.at[sl], wsem.at[sl, 1]),
                pltpu.make_async_copy(wd_hbm.at[ex], wdb.at[sl], wsem.at[sl, 2]))

    @pl.when((i == 0) & active)
    def _():
        xbuf[...] = jnp.zeros(xbuf.shape, jnp.uint32)
        wslot[0] = 1
        for n, cp in enumerate(weight_copies(e, 0)):
            cp.start(priority=n % 2)
        gather(0, 0, wait=False)

    @pl.when(active & first)
    def _():
        sl = 1 - wslot[0]
        wslot[0] = sl
        nxt = lax.while_loop(lambda j: (j < nused) & (be_ref[jnp.minimum(j, nused - 1)] == e),
                             lambda j: j + 1, i + 1)

        @pl.when(nxt < nused)
        def _():
            for n, cp in enumerate(weight_copies(be_ref[nxt], 1 - sl)):
                cp.start(priority=n % 2)

        for cp in weight_copies(e, sl):
            cp.wait()

    @pl.when(active)
    def _():
        sl = wslot[0]
        xs = i % 2

        @pl.when(i + 1 < nused)
        def _():
            on_slot(xs, lambda s_: gather(i + 1, 1 - s_, wait=False))

        @pl.when(i >= 2)
        def _():
            on_slot(xs, lambda s_: scatter(i - 2, s_, wait=True))

        on_slot(xs, lambda s_: gather(i, s_, wait=True))
        x = _unpack_halves(xbuf[xs]).astype(BF16)
        g = jnp.dot(x, wgb[sl].astype(BF16), preferred_element_type=F32)
        u = jnp.dot(x, wub[sl].astype(BF16), preferred_element_type=F32)
        hmid = (g * _sigmoid(g) * u).astype(BF16)
        obuf[xs] = _pack_halves(jnp.dot(hmid, wdb[sl].astype(BF16), preferred_element_type=F32))
        on_slot(xs, lambda s_: scatter(i, s_, wait=False))

        @pl.when(i == nused - 1)
        def _():
            @pl.when(i >= 1)
            def _():
                on_slot(xs, lambda s_: scatter(i - 1, 1 - s_, wait=True))

            on_slot(xs, lambda s_: scatter(i, s_, wait=True))


def _moe(block_e, nvalid, slot_dst, nused, hn, wg, wu, wd):
    nb = block_e.shape[0]
    n_tok, dw = hn.shape
    d = 2 * dw
    assert n_tok & (n_tok - 1) == 0
    grid_spec = pltpu.PrefetchScalarGridSpec(
        num_scalar_prefetch=4,
        grid=(nb,),
        in_specs=[pl.BlockSpec(memory_space=pl.ANY)] * 4,
        out_specs=pl.BlockSpec(memory_space=pl.ANY),
        scratch_shapes=[
            pltpu.VMEM((2, MOE_BLOCK, dw), jnp.uint32),
            pltpu.SemaphoreType.DMA((2,)),
            pltpu.VMEM((2, MOE_BLOCK, dw), jnp.uint32),
            pltpu.SemaphoreType.DMA((2,)),
            pltpu.VMEM((2, d, D_EXPERT), F32),
            pltpu.VMEM((2, d, D_EXPERT), F32),
            pltpu.VMEM((2, D_EXPERT, d), F32),
            pltpu.SemaphoreType.DMA((2, 3)),
            pltpu.SMEM((1,), jnp.int32),
        ],
    )
    return pl.pallas_call(
        functools.partial(_moe_kernel, n_tok=n_tok),
        grid_spec=grid_spec,
        out_shape=jax.ShapeDtypeStruct((TOP_K * n_tok, dw), jnp.uint32),
        compiler_params=_cparams(("arbitrary",), 56),
        name="moe_experts",
    )(block_e, nvalid, slot_dst, nused, hn, wg, wu, wd)


def _ple_kernel(h_ref, rt_ref, y0_ref, y1_ref, p_ref, wg_ref, bg_ref, wp_ref, gp_ref, gf_ref, o_ref):
    rt = rt_ref[...]
    h2 = h_ref[...] + (rt[:, TOP_K:TOP_K + 1] * _unpack_halves(y0_ref[...])
                       + rt[:, TOP_K + 1:TOP_K + 2] * _unpack_halves(y1_ref[...]))
    hn = _rms(h2, gp_ref[...]).astype(BF16)
    gate = _sigmoid(jnp.dot(hn, wg_ref[...], preferred_element_type=F32) + bg_ref[...])
    pe = jnp.dot(p_ref[...].astype(BF16), wp_ref[...], preferred_element_type=F32)
    h3 = h2 + gate * pe
    o_ref[...] = _rms(h3, gf_ref[...])


def _ple_final(h1, rt, y, p, wg, bg, wp, gp, gf, tm):
    s, d = h1.shape
    nblk = s // tm
    row = lambda i: (i, 0)
    full = lambda i: (0, 0)
    return pl.pallas_call(
        _ple_kernel,
        grid=(nblk,),
        in_specs=[
            pl.BlockSpec((tm, d), row),
            pl.BlockSpec((tm, LANES), row),
            pl.BlockSpec((tm, d // 2), row),
            pl.BlockSpec((tm, d // 2), lambda i: (i + nblk, 0)),
            pl.BlockSpec((tm, PLE_DIM), row),
            pl.BlockSpec(wg.shape, full, pipeline_mode=pl.Buffered(1)),
            pl.BlockSpec((1, d), full),
            pl.BlockSpec(wp.shape, full, pipeline_mode=pl.Buffered(1)),
            pl.BlockSpec((1, d), full),
            pl.BlockSpec((1, d), full),
        ],
        out_specs=pl.BlockSpec((tm, d), row),
        out_shape=jax.ShapeDtypeStruct((s, d), F32),
        compiler_params=_cparams(("parallel",), 56),
        name="ple_final",
    )(h1, rt, y, y, p, wg, bg, wp, gp, gf)


def _slots_kernel(rt_ref, tri_ref, pos_ref, cnt_ref, carry_ref, base_ref):
    ps = pl.program_id(0)
    b = pl.program_id(1)
    rt = rt_ref[...]
    lane = lax.broadcasted_iota(jnp.int32, rt.shape, 1)
    lanef = lane.astype(F32)
    oh = [(lanef == rt[:, kk:kk + 1]).astype(F32) for kk in range(TOP_K)]
    both = oh[0] + oh[1]
    colsum = jnp.sum(both, axis=0, keepdims=True)

    @pl.when((ps == 0) & (b == 0))
    def _():
        cnt_ref[...] = jnp.zeros(cnt_ref.shape, F32)

    @pl.when(ps == 0)
    def _():
        cnt_ref[...] = cnt_ref[...] + colsum

    @pl.when((ps == 1) & (b == 0))
    def _():
        blocks = jnp.floor((cnt_ref[...] + (MOE_BLOCK - 0.5)) / MOE_BLOCK)
        r = lax.broadcasted_iota(jnp.int32, (LANES, LANES), 0)
        c = lax.broadcasted_iota(jnp.int32, (LANES, LANES), 1)
        before = (r < c).astype(F32)
        base_ref[...] = jnp.dot(blocks * MOE_BLOCK, before, preferred_element_type=F32,
                                precision=lax.Precision.HIGHEST)
        carry_ref[...] = jnp.zeros(carry_ref.shape, F32)

    @pl.when(ps == 1)
    def _():
        earlier = jnp.dot(tri_ref[...], both.astype(BF16), preferred_element_type=F32)
        row = earlier + (base_ref[0:1, :] + carry_ref[0:1, :])
        s0 = jnp.sum(oh[0] * row, axis=-1, keepdims=True)
        s1 = jnp.sum(oh[1] * row, axis=-1, keepdims=True)
        pos_ref[...] = jnp.where(lane == 0, s0, jnp.where(lane == 1, s1, 0.0))
        carry_ref[...] = carry_ref[...] + colsum


def _slots(rt, tb):
    s = rt.shape[0]
    tri = (np.arange(tb)[:, None] > np.arange(tb)[None, :]).astype(np.float32)
    return pl.pallas_call(
        _slots_kernel,
        grid=(2, s // tb),
        in_specs=[
            pl.BlockSpec((tb, LANES), lambda ps, b: (b, 0)),
            pl.BlockSpec((tb, tb), lambda ps, b: (0, 0)),
        ],
        out_specs=[
            pl.BlockSpec((tb, LANES), lambda ps, b: (b * ps, 0)),
            pl.BlockSpec((8, LANES), lambda ps, b: (0, 0)),
        ],
        out_shape=[
            jax.ShapeDtypeStruct((s, LANES), F32),
            jax.ShapeDtypeStruct((8, LANES), F32),
        ],
        scratch_shapes=[pltpu.VMEM((8, LANES), F32), pltpu.VMEM((8, LANES), F32)],
        compiler_params=_cparams(("arbitrary", "arbitrary")),
        name="moe_slots",
    )(rt, jnp.asarray(tri, BF16))


def _dispatch(rt, n_tok):
    a = n_tok * TOP_K
    nb = -(-a // MOE_BLOCK) + N_EXPERTS
    pos_f, cnt = _slots(rt, 512)
    pos = pos_f[:, :TOP_K].astype(jnp.int32).reshape(-1)
    counts = cnt[0, :N_EXPERTS].astype(jnp.int32)
    nblk = (counts + MOE_BLOCK - 1) // MOE_BLOCK
    bends = jnp.cumsum(nblk)
    bidx = jnp.arange(nb, dtype=jnp.int32)
    block_e = jnp.clip(jnp.searchsorted(bends, bidx, side="right"), 0, N_EXPERTS - 1).astype(jnp.int32)
    nused = bends[-1].astype(jnp.int32)
    within = bidx - (bends - nblk)[block_e]
    nvalid = jnp.where(bidx < nused, jnp.clip(counts[block_e] - within * MOE_BLOCK, 0, MOE_BLOCK), 0).astype(jnp.int32)
    a_idx = jnp.arange(a, dtype=jnp.int32)
    slot_dst = jnp.zeros((nb * MOE_BLOCK,), jnp.int32).at[pos].set((a_idx % TOP_K) * n_tok + a_idx // TOP_K)
    return block_e, nvalid, slot_dst, nused.reshape(1)


def kernel(x, p, positions, attn_norm, w_in, q_norm, w_uq, kv_norm, w_ukv, mla_norm, hg_lb_logits, hg_norm, w_out, ffn_norm, w_router_group, b_router_group, w_router_expert, b_router_expert, w_exp_gate, w_exp_up, w_exp_down, ple_norm, w_ple_gate, b_ple_gate, w_ple_proj, final_norm):
    bsz, s, d = x.shape
    assert bsz == 1 and w_in.shape[0] == 1
    xt = x[0]

    inv_freq = 1.0 / (ROPE_THETA ** (jnp.arange(0, QK_ROPE, 2, dtype=F32) / QK_ROPE))
    ang = positions[0].astype(F32)[:, None] * inv_freq
    cos, sin = jnp.cos(ang), jnp.sin(ang)
    zpad = jnp.zeros((s, LANES - QK_ROPE), F32)
    cc = jnp.concatenate([cos, cos, zpad], axis=1)
    ss = jnp.concatenate([-sin, sin, zpad], axis=1)

    lb = jnp.cumsum(jax.nn.softmax(hg_lb_logits.astype(F32), axis=0), axis=0)[0][None, :]

    wi = w_in[0]
    kr0 = Q_LORA + KV_LORA
    half = QK_ROPE // 2
    w_lat = jnp.concatenate(
        [wi[:, :kr0 + QK_ROPE], wi[:, kr0 + half:kr0 + QK_ROPE], wi[:, kr0:kr0 + half]], axis=1).astype(BF16)
    w_hg = wi[:, kr0 + QK_ROPE:].astype(BF16)
    wq3 = w_uq[0].reshape(Q_LORA, MLA_HEADS, QK_HEAD)
    wq_pad = jnp.concatenate(
        [wq3, wq3[:, :, QK_NOPE + half:], wq3[:, :, QK_NOPE:QK_NOPE + half]], axis=2
    ).reshape(Q_LORA, MLA_HEADS * QK_PAD).astype(BF16)
    wkv3 = w_ukv[0].reshape(KV_LORA, MLA_HEADS, QK_NOPE + V_HEAD)
    wkn = wkv3[:, :, :QK_NOPE].reshape(KV_LORA, MLA_HEADS * QK_NOPE).astype(BF16)
    wvt = wkv3[:, :, QK_NOPE:].reshape(KV_LORA, D_MLA).T.astype(BF16)
    wo = w_out[0].astype(BF16)
    wr = jnp.concatenate(
        [w_router_group[0], w_router_expert[0], jnp.zeros((d, LANES - N_GROUPS - N_EXPERTS), F32)], axis=1)
    wr_hi = wr.astype(BF16)
    wr = jnp.concatenate([wr_hi, (wr - wr_hi.astype(F32)).astype(BF16)], axis=1)
    br = jnp.concatenate(
        [b_router_group[0], b_router_expert[0], jnp.zeros((LANES - N_GROUPS - N_EXPERTS,), F32)])[None, :]

    lat = _in_proj(xt, attn_norm, w_lat, F32, 512, D_LAT, "in_proj_latent")
    hg4 = _in_proj(xt, attn_norm, w_hg, BF16, 1024, 1024, "in_proj_hgrn")
    q, k, vt = _mla_up(lat, q_norm, kv_norm, wq_pad, wkn, wvt, cc, ss, 512)
    o_mla = _attention(q, k, vt, mla_norm, 512, 512)
    mall, lv = _hgrn_tables()
    o_hg = _hgrn(hg4, lb, hg_norm[0].reshape(1, D_HG), mall, lv)

    h1, hn, rt = _out_route(xt, o_mla, o_hg, wo[:D_MLA], wo[D_MLA:], ffn_norm, wr, br, 512)
    block_e, nvalid, slot_dst, nused = _dispatch(rt, s)
    y = _moe(block_e, nvalid, slot_dst, nused, hn, w_exp_gate[0], w_exp_up[0], w_exp_down[0])

    out = _ple_final(h1, rt, y, p[0, 0], w_ple_gate[0].astype(BF16), b_ple_gate,
                     w_ple_proj[0].astype(BF16), ple_norm, final_norm[None, :], 512)
    return out[None]
```

```python
import functools

import jax
import jax.numpy as jnp
import numpy as np
from jax import lax
from jax.experimental import pallas as pl
from jax.experimental.pallas import tpu as pltpu

F32 = jnp.float32
BF16 = jnp.bfloat16

D_MODEL = 2048
PLE_DIM = 256
MLA_HEADS = 8
QK_NOPE = 128
QK_ROPE = 64
QK_HEAD = QK_NOPE + QK_ROPE
QK_PAD = 256
V_HEAD = 128
VT_ROWS = V_HEAD + 16
QK_AHEAD = 2
PV_LAG = 1
Q_LORA = 512
KV_LORA = 256
ROPE_THETA = 10000.0
HG_HEADS = 8
HG_DK = 128
HG_DV = 128
HG_CHUNK = 64
D_MLA = MLA_HEADS * V_HEAD
D_HG = HG_HEADS * HG_DV
N_GROUPS = 8
EXPERTS_PER_GROUP = 8
N_EXPERTS = N_GROUPS * EXPERTS_PER_GROUP
TOP_K = 2
D_EXPERT = 512
EPS = 1e-6
LANES = 128
D_LAT = Q_LORA + KV_LORA + 2 * QK_ROPE
NEG_BIG = -1e30
LOG2E = 1.4426950408889634

MOE_BLOCK = 320
HG_ROWS = 256
HG_LEVELS = (32, 16, 8, 4, 2, 1)
HG_MM_LEVELS = (2, 1)
HG_NH = 4


def _cparams(sem, vmem_mb=None, flags=None):
    kw = dict(dimension_semantics=sem)
    if vmem_mb is not None:
        kw["vmem_limit_bytes"] = vmem_mb * 1024 * 1024
    if flags is not None:
        kw["flags"] = flags
    return pltpu.CompilerParams(**kw)


def _rms(x, g):
    ms = jnp.mean(x * x, axis=-1, keepdims=True)
    return x * lax.rsqrt(ms + EPS) * g


def _sigmoid(x):
    return 1.0 / (1.0 + jnp.exp(-x))


def _pack_halves(x):
    n = x.shape[1] // 2
    bits = lax.bitcast_convert_type(x.astype(BF16).astype(F32), jnp.uint32)
    return bits[:, :n] | (bits[:, n:] >> 16)


def _unpack_halves(w):
    hi = lax.bitcast_convert_type(w & jnp.uint32(0xFFFF0000), F32)
    lo = lax.bitcast_convert_type(w << 16, F32)
    return jnp.concatenate([hi, lo], axis=1)


def _in_proj_kernel(x_ref, g_ref, w_ref, o_ref, xn_ref):
    @pl.when(pl.program_id(1) == 0)
    def _():
        xn_ref[...] = _rms(x_ref[...], g_ref[...]).astype(BF16)

    o_ref[...] = jnp.dot(xn_ref[...], w_ref[...], preferred_element_type=F32).astype(o_ref.dtype)


def _in_proj(x, gain, w, out_dtype, tm, tn, name):
    s, d = x.shape
    n = w.shape[1]
    return pl.pallas_call(
        _in_proj_kernel,
        grid=(s // tm, n // tn),
        in_specs=[
            pl.BlockSpec((tm, d), lambda i, j: (i, 0)),
            pl.BlockSpec((1, d), lambda i, j: (0, 0)),
            pl.BlockSpec((d, tn), lambda i, j: (0, j)),
        ],
        out_specs=pl.BlockSpec((tm, tn), lambda i, j: (i, j)),
        out_shape=jax.ShapeDtypeStruct((s, n), out_dtype),
        scratch_shapes=[pltpu.VMEM((tm, d), BF16)],
        compiler_params=_cparams(("parallel", "arbitrary"), 48),
        name=name,
    )(x, gain, w)


def _mla_up_kernel(lat_ref, qn_ref, kvn_ref, wq_ref, wkn_ref, wvt_ref, cc_ref, ss_ref,
                   q_ref, k_ref, vt_ref):
    lat = lat_ref[...]
    cq = _rms(lat[:, :Q_LORA], qn_ref[...]).astype(BF16)
    ckv = _rms(lat[:, Q_LORA:Q_LORA + KV_LORA], kvn_ref[...]).astype(BF16)
    kp = lat[:, Q_LORA + KV_LORA:]
    cc = cc_ref[...]
    ss = ss_ref[...]
    scale = QK_HEAD ** -0.5 * LOG2E

    q = jnp.dot(cq, wq_ref[...], preferred_element_type=F32)
    for h in range(MLA_HEADS):
        base = h * QK_PAD
        q_ref[:, base:base + QK_NOPE] = (q[:, base:base + QK_NOPE] * scale).astype(BF16)
        r = q[:, base + QK_NOPE:base + QK_PAD]
        r = (r * cc + pltpu.roll(r, QK_ROPE, 1) * ss) * scale
        q_ref[:, base + QK_NOPE:base + QK_PAD] = r.astype(BF16)

    kpe = (kp * cc + pltpu.roll(kp, QK_ROPE, 1) * ss).astype(BF16)
    kn = jnp.dot(ckv, wkn_ref[...], preferred_element_type=F32).astype(BF16)
    for h in range(MLA_HEADS):
        base = h * QK_PAD
        k_ref[:, base:base + QK_NOPE] = kn[:, h * QK_NOPE:(h + 1) * QK_NOPE]
        k_ref[:, base + QK_NOPE:base + QK_PAD] = kpe
    vt = lax.dot_general(wvt_ref[...], ckv, (((1,), (1,)), ((), ())), preferred_element_type=F32).astype(BF16)
    ones = jnp.ones((VT_ROWS - V_HEAD, vt.shape[1]), BF16)
    for h in range(MLA_HEADS):
        vt_ref[h * VT_ROWS:h * VT_ROWS + V_HEAD, :] = vt[h * V_HEAD:(h + 1) * V_HEAD]
        vt_ref[h * VT_ROWS + V_HEAD:(h + 1) * VT_ROWS, :] = ones


def _mla_up(lat, qn, kvn, wq, wkn, wvt, cc, ss, tm):
    s = lat.shape[0]
    row = lambda i: (i, 0)
    full = lambda i: (0, 0)
    return pl.pallas_call(
        _mla_up_kernel,
        grid=(s // tm,),
        in_specs=[
            pl.BlockSpec((tm, D_LAT), row),
            pl.BlockSpec((1, Q_LORA), full),
            pl.BlockSpec((1, KV_LORA), full),
            pl.BlockSpec(wq.shape, full),
            pl.BlockSpec(wkn.shape, full),
            pl.BlockSpec(wvt.shape, full),
            pl.BlockSpec((tm, LANES), row),
            pl.BlockSpec((tm, LANES), row),
        ],
        out_specs=[
            pl.BlockSpec((tm, MLA_HEADS * QK_PAD), row),
            pl.BlockSpec((tm, MLA_HEADS * QK_PAD), row),
            pl.BlockSpec((MLA_HEADS * VT_ROWS, tm), lambda i: (0, i)),
        ],
        out_shape=[
            jax.ShapeDtypeStruct((s, MLA_HEADS * QK_PAD), BF16),
            jax.ShapeDtypeStruct((s, MLA_HEADS * QK_PAD), BF16),
            jax.ShapeDtypeStruct((MLA_HEADS * VT_ROWS, s), BF16),
        ],
        compiler_params=_cparams(("parallel",), 48),
        name="mla_up",
    )(lat, qn, kvn, wq, wkn, wvt, cc, ss)


def _attn_kernel(it_ref, jt_ref, q_ref, k_ref, vt_ref, g_ref, o_ref, acc_ref, m_ref, *, tq, tk, qc):
    t = pl.program_id(0)
    i = it_ref[t]
    j = jt_ref[t]

    @pl.when(j == 0)
    def _():
        m_ref[...] = jnp.full(m_ref.shape, NEG_BIG, F32)
        acc_ref[...] = jnp.zeros(acc_ref.shape, F32)

    units = [(h, c) for h in range(MLA_HEADS) for c in range(tq // qc)]

    def scores(u):
        h, c = u
        q = q_ref[c * qc:(c + 1) * qc, h * QK_PAD:(h + 1) * QK_PAD]
        k = k_ref[:, h * QK_PAD:(h + 1) * QK_PAD]
        return lax.dot_general(k, q, (((1,), (1,)), ((), ())), preferred_element_type=F32)

    def step(masked):
        if masked:
            diff = lax.broadcasted_iota(jnp.int32, (tk, qc), 0) - lax.broadcasted_iota(jnp.int32, (tk, qc), 1)
        def accumulate(h, cols, alpha, p):
            pv = jnp.dot(vt_ref[h * VT_ROWS:(h + 1) * VT_ROWS, :], p, preferred_element_type=F32)
            acc_ref[h, :, cols] = alpha * acc_ref[h, :, cols] + pv

        ahead = [scores(units[n]) for n in range(min(QK_AHEAD, len(units)))]
        pending = []
        for n, (h, c) in enumerate(units):
            s = ahead.pop(0)
            if n + QK_AHEAD < len(units):
                ahead.append(scores(units[n + QK_AHEAD]))
            cols = slice(c * qc, (c + 1) * qc)
            if masked:
                s = jnp.where(diff <= c * qc, s, NEG_BIG)
            m_prev = m_ref[h, :, cols]
            m_new = jnp.maximum(m_prev, jnp.max(s, axis=0, keepdims=True))
            alpha = jnp.exp2(m_prev - m_new)
            p = jnp.exp2(s - m_new).astype(BF16)
            m_ref[h, :, cols] = m_new
            pending.append((h, cols, alpha, p))
            if len(pending) > PV_LAG:
                accumulate(*pending.pop(0))
        for item in pending:
            accumulate(*item)

    @pl.when(j < i)
    def _():
        step(False)

    @pl.when(j == i)
    def _():
        step(True)
        outs = [acc_ref[h, :V_HEAD, :] * (1.0 / acc_ref[h, V_HEAD:V_HEAD + 1, :]) for h in range(MLA_HEADS)]
        o_ref[...] = _rms(jnp.concatenate(outs, axis=0).T, g_ref[...]).astype(o_ref.dtype)


def _attention(q, k, vt, gain, tb, qc):
    s = q.shape[0]
    nq = s // tb
    it = np.concatenate([np.full(i + 1, i, np.int32) for i in range(nq)])
    jt = np.concatenate([np.arange(i + 1, dtype=np.int32) for i in range(nq)])
    grid_spec = pltpu.PrefetchScalarGridSpec(
        num_scalar_prefetch=2,
        grid=(it.shape[0],),
        in_specs=[
            pl.BlockSpec((tb, MLA_HEADS * QK_PAD), lambda t, it, jt: (it[t], 0)),
            pl.BlockSpec((tb, MLA_HEADS * QK_PAD), lambda t, it, jt: (jt[t], 0)),
            pl.BlockSpec((MLA_HEADS * VT_ROWS, tb), lambda t, it, jt: (0, jt[t])),
            pl.BlockSpec((1, D_MLA), lambda t, it, jt: (0, 0)),
        ],
        out_specs=pl.BlockSpec((tb, D_MLA), lambda t, it, jt: (it[t], 0)),
        scratch_shapes=[
            pltpu.VMEM((MLA_HEADS, VT_ROWS, tb), F32),
            pltpu.VMEM((MLA_HEADS, 1, tb), F32),
        ],
    )
    return pl.pallas_call(
        functools.partial(_attn_kernel, tq=tb, tk=tb, qc=qc),
        grid_spec=grid_spec,
        out_shape=jax.ShapeDtypeStruct((s, D_MLA), BF16),
        compiler_params=_cparams(("arbitrary",), 48),
        name="mla_attention",
    )(jnp.asarray(it), jnp.asarray(jt), q, k, vt, gain)


def _hgrn_tables():
    n = HG_ROWS
    r = np.arange(n)
    c = np.arange(n)
    same = (r[:, None] // HG_CHUNK) == (c[None, :] // HG_CHUNK)

    def rows_upto(idx):
        return (same & (c[None, :] <= idx[:, None])).astype(np.float32)

    blocks = [rows_upto(r)]
    for m in HG_MM_LEVELS:
        blocks.append(rows_upto((r // (2 * m)) * (2 * m) + m))
    mall = np.concatenate(blocks, axis=0)

    x = r[:, None] ^ c[None, :]
    lv = np.full((n, n), -1, np.int32)
    for li, m in enumerate(HG_LEVELS):
        lv = np.where(same & (r[:, None] > c[None, :]) & (x >= m) & (x < 2 * m), li, lv)
    lv = np.where(r[:, None] == c[None, :], len(HG_LEVELS), lv)
    return jnp.asarray(mall, BF16), jnp.asarray(lv, jnp.int32)


def _hgrn_kernel(q_ref, f_ref, i_ref, g_ref, lb_ref, gn_ref, mall_ref, lv_ref, o_ref, st_ref):
    t = pl.program_id(1)
    n = HG_ROWS

    @pl.when(t == 0)
    def _():
        st_ref[...] = jnp.zeros(st_ref.shape, F32)

    rowid = lax.broadcasted_iota(jnp.int32, (n, 1), 0)
    lv = lv_ref[...]

    def head(hh):
        cols = slice(hh * HG_DK, (hh + 1) * HG_DK)
        q_in = q_ref[:, cols].astype(F32)
        qs = q_in * _sigmoid(q_in)
        lb = lb_ref[:, cols]
        f = lb + (1.0 - lb) * _sigmoid(f_ref[:, cols].astype(F32))
        kk = 1.0 - f
        logf = jnp.log(f)
        iv = i_ref[:, cols]

        l1 = logf.astype(BF16)
        l2 = (logf - l1.astype(F32)).astype(BF16)
        parts = jnp.dot(mall_ref[...], jnp.concatenate([l1, l2], axis=1), preferred_element_type=F32)
        yield
        bc = parts[:, :HG_DK] + parts[:, HG_DK:]
        b = bc[:n]

        def anchor(period, row):
            b3 = b.reshape(n // period, period, HG_DK)
            return jnp.broadcast_to(b3[:, row:row + 1, :], b3.shape).reshape(n, HG_DK)

        a = jnp.zeros((n, n), F32)
        for li, m in enumerate(HG_LEVELS):
            if m in HG_MM_LEVELS:
                k = 1 + HG_MM_LEVELS.index(m)
                c = bc[k * n:(k + 1) * n]
            else:
                c = anchor(2 * m, m)
            e = jnp.exp(-jnp.abs(b - c))
            upper = (rowid & (2 * m - 1)) >= m
            x = (jnp.where(upper, qs, kk) * e).astype(BF16)
            p = lax.dot_general(x, x, (((1,), (1,)), ((), ())), preferred_element_type=F32)
            yield
            a = jnp.where(lv == li, p, a)
        a = jnp.where(lv == len(HG_LEVELS), jnp.sum(qs * kk, axis=-1, keepdims=True), a)
        o = jnp.dot(a.astype(BF16), iv, preferred_element_type=F32)
        yield

        blast = anchor(HG_CHUNK, HG_CHUNK - 1)
        qd = (qs * jnp.exp(b)).astype(BF16)
        kd = kk * jnp.exp(blast - b)
        ivt = iv.astype(F32).T.astype(BF16)
        st = st_ref[hh]
        outs = []
        for ci in range(n // HG_CHUNK):
            lo = ci * HG_CHUNK
            inter = lax.dot_general(qd[lo:lo + HG_CHUNK], st.astype(BF16), (((1,), (1,)), ((), ())),
                                    preferred_element_type=F32)
            kd_c = jnp.where((rowid >= lo) & (rowid < lo + HG_CHUNK), kd, 0.0).astype(BF16)
            upd = jnp.dot(ivt, kd_c, preferred_element_type=F32)
            yield
            outs.append(o[lo:lo + HG_CHUNK] + inter)
            st = jnp.exp(blast[lo:lo + 1]) * st + upd
        st_ref[hh] = st
        o = jnp.concatenate(outs, axis=0)

        g_in = g_ref[:, cols].astype(F32)
        o_ref[:, cols] = (_rms(o, gn_ref[:, cols]) * (g_in * _sigmoid(g_in))).astype(o_ref.dtype)

    progs = [head(hh) for hh in range(HG_NH)]
    while progs:
        progs = [pr for pr in progs if next(pr, "done") != "done"]


def _hgrn(hg4, lb, gn, mall, lv):
    s = hg4.shape[0]
    groups = HG_HEADS // HG_NH
    width = HG_NH * HG_DK

    def col(c):
        return lambda hh, t: (t, c * groups + hh)

    head = lambda hh, t: (0, hh)
    const = lambda hh, t: (0, 0)
    return pl.pallas_call(
        _hgrn_kernel,
        grid=(groups, s // HG_ROWS),
        in_specs=[
            pl.BlockSpec((HG_ROWS, width), col(0)),
            pl.BlockSpec((HG_ROWS, width), col(1)),
            pl.BlockSpec((HG_ROWS, width), col(2)),
            pl.BlockSpec((HG_ROWS, width), col(3)),
            pl.BlockSpec((1, width), head),
            pl.BlockSpec((1, width), head),
            pl.BlockSpec(mall.shape, const),
            pl.BlockSpec(lv.shape, const),
        ],
        out_specs=pl.BlockSpec((HG_ROWS, width), lambda hh, t: (t, hh)),
        out_shape=jax.ShapeDtypeStruct((s, D_HG), BF16),
        scratch_shapes=[pltpu.VMEM((HG_NH, HG_DV, HG_DK), F32)],
        compiler_params=_cparams(("parallel", "arbitrary"), 32),
        name="hgrn2",
    )(hg4, hg4, hg4, hg4, lb, gn, mall, lv)


def _out_route_kernel(x_ref, om_ref, oh_ref, wa_ref, wb_ref, g_ref, wr_ref, br_ref,
                      h_ref, hn_ref, rt_ref):
    h1 = (x_ref[...]
          + jnp.dot(om_ref[...], wa_ref[...], preferred_element_type=F32)
          + jnp.dot(oh_ref[...], wb_ref[...], preferred_element_type=F32))
    h_ref[...] = h1
    hn = _rms(h1, g_ref[...])
    hn_ref[...] = _pack_halves(hn)
    hn_hi = hn.astype(BF16)
    hn_lo = (hn - hn_hi.astype(F32)).astype(BF16)
    hh = jnp.dot(hn_hi, wr_ref[...], preferred_element_type=F32)
    lh = jnp.dot(hn_lo, wr_ref[:, :LANES], preferred_element_type=F32)
    logits = hh[:, :LANES] + (hh[:, LANES:] + lh) + br_ref[...]

    lane = lax.broadcasted_iota(jnp.int32, logits.shape, 1)
    lanef = lane.astype(F32)
    ninf = -jnp.inf
    big = float(LANES)

    is_g = lane < N_GROUPS
    gl = jnp.where(is_g, logits, ninf)
    gmax = jnp.max(gl, axis=-1, keepdims=True)
    gsum = jnp.sum(jnp.where(is_g, jnp.exp(gl - gmax), 0.0), axis=-1, keepdims=True)
    g_w = 1.0 / gsum
    g_idx = jnp.min(jnp.where(gl == gmax, lanef, big), axis=-1, keepdims=True)

    e_lane = lane - N_GROUPS
    in_grp = (e_lane >= 0) & (e_lane < N_EXPERTS) & ((e_lane >> 3).astype(F32) == g_idx)
    el = jnp.where(in_grp, logits, ninf)
    emax = jnp.max(el, axis=-1, keepdims=True)
    esum = jnp.sum(jnp.where(in_grp, jnp.exp(el - emax), 0.0), axis=-1, keepdims=True)
    i1 = jnp.min(jnp.where(el == emax, lanef, big), axis=-1, keepdims=True)
    el2 = jnp.where(lanef == i1, ninf, el)
    emax2 = jnp.max(el2, axis=-1, keepdims=True)
    i2 = jnp.min(jnp.where(el2 == emax2, lanef, big), axis=-1, keepdims=True)
    p1 = 1.0 / esum
    p2 = jnp.exp(emax2 - emax) / esum
    w1 = g_w * p1 / (p1 + p2)
    w2 = g_w * p2 / (p1 + p2)

    rt = jnp.where(lane == 0, i1 - N_GROUPS,
                   jnp.where(lane == 1, i2 - N_GROUPS,
                             jnp.where(lane == 2, w1, jnp.where(lane == 3, w2, 0.0))))
    rt_ref[...] = rt


def _out_route(x, o_mla, o_hg, wa, wb, gain, wr, br, tm):
    s, d = x.shape
    row = lambda i: (i, 0)
    full = lambda i: (0, 0)
    return pl.pallas_call(
        _out_route_kernel,
        grid=(s // tm,),
        in_specs=[
            pl.BlockSpec((tm, d), row),
            pl.BlockSpec((tm, D_MLA), row),
            pl.BlockSpec((tm, D_HG), row),
            pl.BlockSpec(wa.shape, full, pipeline_mode=pl.Buffered(1)),
            pl.BlockSpec(wb.shape, full, pipeline_mode=pl.Buffered(1)),
            pl.BlockSpec((1, d), full),
            pl.BlockSpec(wr.shape, full, pipeline_mode=pl.Buffered(1)),
            pl.BlockSpec((1, LANES), full),
        ],
        out_specs=[
            pl.BlockSpec((tm, d), row),
            pl.BlockSpec((tm, d // 2), row),
            pl.BlockSpec((tm, LANES), row),
        ],
        out_shape=[
            jax.ShapeDtypeStruct((s, d), F32),
            jax.ShapeDtypeStruct((s, d // 2), jnp.uint32),
            jax.ShapeDtypeStruct((s, LANES), F32),
        ],
        compiler_params=_cparams(("parallel",), 56),
        name="out_proj_route",
    )(x, o_mla, o_hg, wa, wb, gain, wr, br)


def _moe_kernel(be_ref, dst_ref, nu_ref, hn_hbm, wg_hbm, wu_hbm, wd_hbm, y_hbm,
                xbuf, xsem, obuf, osem, wgb, wub, wdb, wsem, wslot, *, n_tok):
    i = pl.program_id(0)
    nused = nu_ref[0]
    e = be_ref[i]
    active = i < nused
    first = (i == 0) | (e != be_ref[jnp.maximum(i - 1, 0)])

    def gather_start(blk, sl):
        for r in range(MOE_BLOCK):
            tok = dst_ref[blk * MOE_BLOCK + r] & (n_tok - 1)
            pltpu.make_async_copy(hn_hbm.at[pl.ds(tok, 1), :], xbuf.at[sl, pl.ds(r, 1), :],
                                  xsem.at[sl]).start(priority=r % 2)

    def gather_wait(sl):
        pltpu.make_async_copy(hn_hbm.at[pl.ds(0, MOE_BLOCK), :], xbuf.at[sl], xsem.at[sl]).wait()

    def scatter_start(blk, sl):
        for r in range(MOE_BLOCK):
            dst = dst_ref[blk * MOE_BLOCK + r]
            pltpu.make_async_copy(obuf.at[sl, pl.ds(r, 1), :], y_hbm.at[pl.ds(dst, 1), :],
                                  osem.at[sl]).start(priority=r % 2)

    def scatter_wait(sl):
        pltpu.make_async_copy(obuf.at[sl], y_hbm.at[pl.ds(0, MOE_BLOCK), :], osem.at[sl]).wait()

    def spare_copy(sl):
        return pltpu.make_async_copy(obuf.at[sl], y_hbm.at[pl.ds(TOP_K * n_tok + sl * MOE_BLOCK, MOE_BLOCK), :],
                                     osem.at[sl])

    def weight_copies(ex, sl):
        return (pltpu.make_async_copy(wg_hbm.at[ex], wgb.at[sl], wsem.at[sl, 0]),
                pltpu.make_async_copy(wu_hbm.at[ex], wub.at[sl], wsem.at[sl, 1]),
                pltpu.make_async_copy(wd_hbm.at[ex], wdb.at[sl], wsem.at[sl, 2]))

    @pl.when((i == 0) & active)
    def _():
        wslot[0] = 1
        for n, cp in enumerate(weight_copies(e, 0)):
            cp.start(priority=n % 2)
        gather_start(0, 0)
        obuf[...] = jnp.zeros(obuf.shape, jnp.uint32)
        for sl in (0, 1):
            spare_copy(sl).start()
        for sl in (0, 1):
            spare_copy(sl).wait()

    @pl.when(active & first)
    def _():
        sl = 1 - wslot[0]
        wslot[0] = sl
        nxt = lax.while_loop(lambda j: (j < nused) & (be_ref[jnp.minimum(j, nused - 1)] == e),
                             lambda j: j + 1, i + 1)

        @pl.when(nxt < nused)
        def _():
            for n, cp in enumerate(weight_copies(be_ref[nxt], 1 - sl)):
                cp.start(priority=n % 2)

        for cp in weight_copies(e, sl):
            cp.wait()

    def block(xs):
        sl = wslot[0]

        @pl.when(i >= 2)
        def _():
            scatter_wait(xs)

        gather_wait(xs)
        gather_start(jnp.minimum(i + 1, nused - 1), 1 - xs)
        x = _unpack_halves(xbuf[xs]).astype(BF16)
        g = jnp.dot(x, wgb[sl].astype(BF16), preferred_element_type=F32)
        u = jnp.dot(x, wub[sl].astype(BF16), preferred_element_type=F32)
        hmid = (g * _sigmoid(g) * u).astype(BF16)
        obuf[xs] = _pack_halves(jnp.dot(hmid, wdb[sl].astype(BF16), preferred_element_type=F32))
        scatter_start(i, xs)

        @pl.when(i == nused - 1)
        def _():
            gather_wait(1 - xs)

            @pl.when(i >= 1)
            def _():
                scatter_wait(1 - xs)

            scatter_wait(xs)

    for s_ in (0, 1):
        pl.when(active & (i % 2 == s_))(functools.partial(block, s_))


def _moe(block_e, slot_dst, nused, hn, wg, wu, wd):
    nb = block_e.shape[0]
    n_tok, dw = hn.shape
    d = 2 * dw
    assert n_tok & (n_tok - 1) == 0
    grid_spec = pltpu.PrefetchScalarGridSpec(
        num_scalar_prefetch=3,
        grid=(nb,),
        in_specs=[pl.BlockSpec(memory_space=pl.ANY)] * 4,
        out_specs=pl.BlockSpec(memory_space=pl.ANY),
        scratch_shapes=[
            pltpu.VMEM((2, MOE_BLOCK, dw), jnp.uint32),
            pltpu.SemaphoreType.DMA((2,)),
            pltpu.VMEM((2, MOE_BLOCK, dw), jnp.uint32),
            pltpu.SemaphoreType.DMA((2,)),
            pltpu.VMEM((2, d, D_EXPERT), F32),
            pltpu.VMEM((2, d, D_EXPERT), F32),
            pltpu.VMEM((2, D_EXPERT, d), F32),
            pltpu.SemaphoreType.DMA((2, 3)),
            pltpu.SMEM((1,), jnp.int32),
        ],
    )
    return pl.pallas_call(
        functools.partial(_moe_kernel, n_tok=n_tok),
        grid_spec=grid_spec,
        out_shape=jax.ShapeDtypeStruct((TOP_K * n_tok + 2 * MOE_BLOCK, dw), jnp.uint32),
        compiler_params=_cparams(("arbitrary",), 56),
        name="moe_experts",
    )(block_e, slot_dst, nused, hn, wg, wu, wd)


def _ple_kernel(h_ref, rt_ref, y0_ref, y1_ref, p_ref, wg_ref, bg_ref, wp_ref, gp_ref, gf_ref, o_ref):
    rt = rt_ref[...]
    h2 = h_ref[...] + (rt[:, TOP_K:TOP_K + 1] * _unpack_halves(y0_ref[...])
                       + rt[:, TOP_K + 1:TOP_K + 2] * _unpack_halves(y1_ref[...]))
    hn = _rms(h2, gp_ref[...]).astype(BF16)
    gate = _sigmoid(jnp.dot(hn, wg_ref[...], preferred_element_type=F32) + bg_ref[...])
    pe = jnp.dot(p_ref[...].astype(BF16), wp_ref[...], preferred_element_type=F32)
    h3 = h2 + gate * pe
    o_ref[...] = _rms(h3, gf_ref[...])


def _ple_final(h1, rt, y, p, wg, bg, wp, gp, gf, tm):
    s, d = h1.shape
    nblk = s // tm
    row = lambda i: (i, 0)
    full = lambda i: (0, 0)
    return pl.pallas_call(
        _ple_kernel,
        grid=(nblk,),
        in_specs=[
            pl.BlockSpec((tm, d), row),
            pl.BlockSpec((tm, LANES), row),
            pl.BlockSpec((tm, d // 2), row),
            pl.BlockSpec((tm, d // 2), lambda i: (i + nblk, 0)),
            pl.BlockSpec((tm, PLE_DIM), row),
            pl.BlockSpec(wg.shape, full, pipeline_mode=pl.Buffered(1)),
            pl.BlockSpec((1, d), full),
            pl.BlockSpec(wp.shape, full, pipeline_mode=pl.Buffered(1)),
            pl.BlockSpec((1, d), full),
            pl.BlockSpec((1, d), full),
        ],
        out_specs=pl.BlockSpec((tm, d), row),
        out_shape=jax.ShapeDtypeStruct((s, d), F32),
        compiler_params=_cparams(("parallel",), 56),
        name="ple_final",
    )(h1, rt, y, y, p, wg, bg, wp, gp, gf)


def _slots_kernel(rt_ref, tri_ref, pos_ref, cnt_ref, carry_ref, base_ref):
    ps = pl.program_id(0)
    b = pl.program_id(1)
    rt = rt_ref[...]
    lane = lax.broadcasted_iota(jnp.int32, rt.shape, 1)
    lanef = lane.astype(F32)
    oh = [(lanef == rt[:, kk:kk + 1]).astype(F32) for kk in range(TOP_K)]
    both = oh[0] + oh[1]
    colsum = jnp.sum(both, axis=0, keepdims=True)

    @pl.when((ps == 0) & (b == 0))
    def _():
        cnt_ref[...] = jnp.zeros(cnt_ref.shape, F32)

    @pl.when(ps == 0)
    def _():
        cnt_ref[...] = cnt_ref[...] + colsum

    @pl.when((ps == 1) & (b == 0))
    def _():
        blocks = jnp.floor((cnt_ref[...] + (MOE_BLOCK - 0.5)) / MOE_BLOCK)
        r = lax.broadcasted_iota(jnp.int32, (LANES, LANES), 0)
        c = lax.broadcasted_iota(jnp.int32, (LANES, LANES), 1)
        before = (r < c).astype(F32)
        base_ref[...] = jnp.dot(blocks * MOE_BLOCK, before, preferred_element_type=F32,
                                precision=lax.Precision.HIGHEST)
        carry_ref[...] = jnp.zeros(carry_ref.shape, F32)

    @pl.when(ps == 1)
    def _():
        earlier = jnp.dot(tri_ref[...], both.astype(BF16), preferred_element_type=F32)
        row = earlier + (base_ref[0:1, :] + carry_ref[0:1, :])
        s0 = jnp.sum(oh[0] * row, axis=-1, keepdims=True)
        s1 = jnp.sum(oh[1] * row, axis=-1, keepdims=True)
        pos_ref[...] = jnp.where(lane == 0, s0, jnp.where(lane == 1, s1, 0.0))
        carry_ref[...] = carry_ref[...] + colsum


def _slots(rt, tb):
    s = rt.shape[0]
    tri = (np.arange(tb)[:, None] > np.arange(tb)[None, :]).astype(np.float32)
    return pl.pallas_call(
        _slots_kernel,
        grid=(2, s // tb),
        in_specs=[
            pl.BlockSpec((tb, LANES), lambda ps, b: (b, 0)),
            pl.BlockSpec((tb, tb), lambda ps, b: (0, 0)),
        ],
        out_specs=[
            pl.BlockSpec((tb, LANES), lambda ps, b: (b * ps, 0)),
            pl.BlockSpec((8, LANES), lambda ps, b: (0, 0)),
        ],
        out_shape=[
            jax.ShapeDtypeStruct((s, LANES), F32),
            jax.ShapeDtypeStruct((8, LANES), F32),
        ],
        scratch_shapes=[pltpu.VMEM((8, LANES), F32), pltpu.VMEM((8, LANES), F32)],
        compiler_params=_cparams(("arbitrary", "arbitrary")),
        name="moe_slots",
    )(rt, jnp.asarray(tri, BF16))


def _dispatch(rt, n_tok):
    a = n_tok * TOP_K
    nb = -(-a // MOE_BLOCK) + N_EXPERTS
    pos_f, cnt = _slots(rt, 512)
    pos = pos_f[:, :TOP_K].astype(jnp.int32).reshape(-1)
    counts = cnt[0, :N_EXPERTS].astype(jnp.int32)
    nblk = (counts + MOE_BLOCK - 1) // MOE_BLOCK
    bends = jnp.cumsum(nblk)
    bidx = jnp.arange(nb, dtype=jnp.int32)
    block_e = jnp.clip(jnp.searchsorted(bends, bidx, side="right"), 0, N_EXPERTS - 1).astype(jnp.int32)
    nused = bends[-1].astype(jnp.int32)
    a_idx = jnp.arange(a, dtype=jnp.int32)
    sidx = jnp.arange(nb * MOE_BLOCK, dtype=jnp.int32)
    spare = a + ((sidx // MOE_BLOCK) % 2) * MOE_BLOCK + sidx % MOE_BLOCK
    slot_dst = spare.at[pos].set((a_idx % TOP_K) * n_tok + a_idx // TOP_K)
    return block_e, slot_dst, nused.reshape(1)


def kernel(x, p, positions, attn_norm, w_in, q_norm, w_uq, kv_norm, w_ukv, mla_norm, hg_lb_logits, hg_norm, w_out, ffn_norm, w_router_group, b_router_group, w_router_expert, b_router_expert, w_exp_gate, w_exp_up, w_exp_down, ple_norm, w_ple_gate, b_ple_gate, w_ple_proj, final_norm):
    bsz, s, d = x.shape
    assert bsz == 1 and w_in.shape[0] == 1
    xt = x[0]

    inv_freq = 1.0 / (ROPE_THETA ** (jnp.arange(0, QK_ROPE, 2, dtype=F32) / QK_ROPE))
    ang = positions[0].astype(F32)[:, None] * inv_freq
    cos, sin = jnp.cos(ang), jnp.sin(ang)
    zpad = jnp.zeros((s, LANES - QK_ROPE), F32)
    cc = jnp.concatenate([cos, cos, zpad], axis=1)
    ss = jnp.concatenate([-sin, sin, zpad], axis=1)

    lb = jnp.cumsum(jax.nn.softmax(hg_lb_logits.astype(F32), axis=0), axis=0)[0][None, :]

    wi = w_in[0]
    kr0 = Q_LORA + KV_LORA
    half = QK_ROPE // 2
    w_lat = jnp.concatenate(
        [wi[:, :kr0 + QK_ROPE], wi[:, kr0 + half:kr0 + QK_ROPE], wi[:, kr0:kr0 + half]], axis=1).astype(BF16)
    w_hg = wi[:, kr0 + QK_ROPE:].astype(BF16)
    wq3 = w_uq[0].reshape(Q_LORA, MLA_HEADS, QK_HEAD)
    wq_pad = jnp.concatenate(
        [wq3, wq3[:, :, QK_NOPE + half:], wq3[:, :, QK_NOPE:QK_NOPE + half]], axis=2
    ).reshape(Q_LORA, MLA_HEADS * QK_PAD).astype(BF16)
    wkv3 = w_ukv[0].reshape(KV_LORA, MLA_HEADS, QK_NOPE + V_HEAD)
    wkn = wkv3[:, :, :QK_NOPE].reshape(KV_LORA, MLA_HEADS * QK_NOPE).astype(BF16)
    wvt = wkv3[:, :, QK_NOPE:].reshape(KV_LORA, D_MLA).T.astype(BF16)
    wo = w_out[0].astype(BF16)
    wr = jnp.concatenate(
        [w_router_group[0], w_router_expert[0], jnp.zeros((d, LANES - N_GROUPS - N_EXPERTS), F32)], axis=1)
    wr_hi = wr.astype(BF16)
    wr = jnp.concatenate([wr_hi, (wr - wr_hi.astype(F32)).astype(BF16)], axis=1)
    br = jnp.concatenate(
        [b_router_group[0], b_router_expert[0], jnp.zeros((LANES - N_GROUPS - N_EXPERTS,), F32)])[None, :]

    lat = _in_proj(xt, attn_norm, w_lat, F32, 512, D_LAT, "in_proj_latent")
    hg4 = _in_proj(xt, attn_norm, w_hg, BF16, 1024, 1024, "in_proj_hgrn")
    q, k, vt = _mla_up(lat, q_norm, kv_norm, wq_pad, wkn, wvt, cc, ss, 512)
    o_mla = _attention(q, k, vt, mla_norm, 512, 512)
    mall, lv = _hgrn_tables()
    o_hg = _hgrn(hg4, lb, hg_norm[0].reshape(1, D_HG), mall, lv)

    h1, hn, rt = _out_route(xt, o_mla, o_hg, wo[:D_MLA], wo[D_MLA:], ffn_norm, wr, br, 512)
    block_e, slot_dst, nused = _dispatch(rt, s)
    y = _moe(block_e, slot_dst, nused, hn, w_exp_gate[0], w_exp_up[0], w_exp_down[0])

    out = _ple_final(h1, rt, y, p[0, 0], w_ple_gate[0].astype(BF16), b_ple_gate,
                     w_ple_proj[0].astype(BF16), ple_norm, final_norm[None, :], 512)
    return out[None]
```

```python
import functools

import jax
import jax.numpy as jnp
import numpy as np
from jax import lax
from jax.experimental import pallas as pl
from jax.experimental.pallas import tpu as pltpu

F32 = jnp.float32
BF16 = jnp.bfloat16

D_MODEL = 2048
PLE_DIM = 256
MLA_HEADS = 8
QK_NOPE = 128
QK_ROPE = 64
QK_HEAD = QK_NOPE + QK_ROPE
QK_PAD = 256
V_HEAD = 128
VT_ROWS = V_HEAD + 16
QK_AHEAD = 2
PV_LAG = 1
Q_LORA = 512
KV_LORA = 256
ROPE_THETA = 10000.0
HG_HEADS = 8
HG_DK = 128
HG_DV = 128
HG_CHUNK = 64
D_MLA = MLA_HEADS * V_HEAD
D_HG = HG_HEADS * HG_DV
N_GROUPS = 8
EXPERTS_PER_GROUP = 8
N_EXPERTS = N_GROUPS * EXPERTS_PER_GROUP
TOP_K = 2
D_EXPERT = 512
EPS = 1e-6
LANES = 128
D_LAT = Q_LORA + KV_LORA + 2 * QK_ROPE
NEG_BIG = -1e30
LOG2E = 1.4426950408889634

MOE_BLOCK = 320
ROW_DMA_PRIORITY = 0
HG_ROWS = 256
HG_LEVELS = (32, 16, 8, 4, 2, 1)
HG_MM_LEVELS = (2, 1)
HG_NH = 4


def _cparams(sem, vmem_mb=None, flags=None):
    kw = dict(dimension_semantics=sem)
    if vmem_mb is not None:
        kw["vmem_limit_bytes"] = vmem_mb * 1024 * 1024
    if flags is not None:
        kw["flags"] = flags
    return pltpu.CompilerParams(**kw)


def _rms(x, g):
    ms = jnp.mean(x * x, axis=-1, keepdims=True)
    return x * lax.rsqrt(ms + EPS) * g


def _sigmoid(x):
    return 1.0 / (1.0 + jnp.exp(-x))


def _pack_halves(x):
    n = x.shape[1] // 2
    bits = lax.bitcast_convert_type(x.astype(BF16).astype(F32), jnp.uint32)
    return bits[:, :n] | (bits[:, n:] >> 16)


def _unpack_halves(w):
    hi = lax.bitcast_convert_type(w & jnp.uint32(0xFFFF0000), F32)
    lo = lax.bitcast_convert_type(w << 16, F32)
    return jnp.concatenate([hi, lo], axis=1)


def _in_proj_kernel(x_ref, g_ref, w_ref, o_ref, xn_ref):
    @pl.when(pl.program_id(1) == 0)
    def _():
        xn_ref[...] = _rms(x_ref[...], g_ref[...]).astype(BF16)

    o_ref[...] = jnp.dot(xn_ref[...], w_ref[...], preferred_element_type=F32).astype(o_ref.dtype)


def _in_proj(x, gain, w, out_dtype, tm, tn, name):
    s, d = x.shape
    n = w.shape[1]
    return pl.pallas_call(
        _in_proj_kernel,
        grid=(s // tm, n // tn),
        in_specs=[
            pl.BlockSpec((tm, d), lambda i, j: (i, 0)),
            pl.BlockSpec((1, d), lambda i, j: (0, 0)),
            pl.BlockSpec((d, tn), lambda i, j: (0, j)),
        ],
        out_specs=pl.BlockSpec((tm, tn), lambda i, j: (i, j)),
        out_shape=jax.ShapeDtypeStruct((s, n), out_dtype),
        scratch_shapes=[pltpu.VMEM((tm, d), BF16)],
        compiler_params=_cparams(("parallel", "arbitrary"), 48),
        name=name,
    )(x, gain, w)


def _mla_up_kernel(lat_ref, qn_ref, kvn_ref, wq_ref, wkn_ref, wvt_ref, cc_ref, ss_ref,
                   q_ref, k_ref, vt_ref):
    lat = lat_ref[...]
    cq = _rms(lat[:, :Q_LORA], qn_ref[...]).astype(BF16)
    ckv = _rms(lat[:, Q_LORA:Q_LORA + KV_LORA], kvn_ref[...]).astype(BF16)
    kp = lat[:, Q_LORA + KV_LORA:]
    cc = cc_ref[...]
    ss = ss_ref[...]
    scale = QK_HEAD ** -0.5 * LOG2E

    q = jnp.dot(cq, wq_ref[...], preferred_element_type=F32)
    for h in range(MLA_HEADS):
        base = h * QK_PAD
        q_ref[:, base:base + QK_NOPE] = (q[:, base:base + QK_NOPE] * scale).astype(BF16)
        r = q[:, base + QK_NOPE:base + QK_PAD]
        r = (r * cc + pltpu.roll(r, QK_ROPE, 1) * ss) * scale
        q_ref[:, base + QK_NOPE:base + QK_PAD] = r.astype(BF16)

    kpe = (kp * cc + pltpu.roll(kp, QK_ROPE, 1) * ss).astype(BF16)
    kn = jnp.dot(ckv, wkn_ref[...], preferred_element_type=F32).astype(BF16)
    for h in range(MLA_HEADS):
        base = h * QK_PAD
        k_ref[:, base:base + QK_NOPE] = kn[:, h * QK_NOPE:(h + 1) * QK_NOPE]
        k_ref[:, base + QK_NOPE:base + QK_PAD] = kpe
    vt = lax.dot_general(wvt_ref[...], ckv, (((1,), (1,)), ((), ())), preferred_element_type=F32).astype(BF16)
    ones = jnp.ones((VT_ROWS - V_HEAD, vt.shape[1]), BF16)
    for h in range(MLA_HEADS):
        vt_ref[h * VT_ROWS:h * VT_ROWS + V_HEAD, :] = vt[h * V_HEAD:(h + 1) * V_HEAD]
        vt_ref[h * VT_ROWS + V_HEAD:(h + 1) * VT_ROWS, :] = ones


def _mla_up(lat, qn, kvn, wq, wkn, wvt, cc, ss, tm):
    s = lat.shape[0]
    row = lambda i: (i, 0)
    full = lambda i: (0, 0)
    return pl.pallas_call(
        _mla_up_kernel,
        grid=(s // tm,),
        in_specs=[
            pl.BlockSpec((tm, D_LAT), row),
            pl.BlockSpec((1, Q_LORA), full),
            pl.BlockSpec((1, KV_LORA), full),
            pl.BlockSpec(wq.shape, full),
            pl.BlockSpec(wkn.shape, full),
            pl.BlockSpec(wvt.shape, full),
            pl.BlockSpec((tm, LANES), row),
            pl.BlockSpec((tm, LANES), row),
        ],
        out_specs=[
            pl.BlockSpec((tm, MLA_HEADS * QK_PAD), row),
            pl.BlockSpec((tm, MLA_HEADS * QK_PAD), row),
            pl.BlockSpec((MLA_HEADS * VT_ROWS, tm), lambda i: (0, i)),
        ],
        out_shape=[
            jax.ShapeDtypeStruct((s, MLA_HEADS * QK_PAD), BF16),
            jax.ShapeDtypeStruct((s, MLA_HEADS * QK_PAD), BF16),
            jax.ShapeDtypeStruct((MLA_HEADS * VT_ROWS, s), BF16),
        ],
        compiler_params=_cparams(("parallel",), 48),
        name="mla_up",
    )(lat, qn, kvn, wq, wkn, wvt, cc, ss)


def _attn_kernel(it_ref, jt_ref, q_ref, k_ref, vt_ref, g_ref, o_ref, acc_ref, m_ref, *, tq, tk, qc):
    t = pl.program_id(0)
    i = it_ref[t]
    j = jt_ref[t]

    @pl.when(j == 0)
    def _():
        m_ref[...] = jnp.full(m_ref.shape, NEG_BIG, F32)
        acc_ref[...] = jnp.zeros(acc_ref.shape, F32)

    units = [(h, c) for h in range(MLA_HEADS) for c in range(tq // qc)]

    def scores(u):
        h, c = u
        q = q_ref[c * qc:(c + 1) * qc, h * QK_PAD:(h + 1) * QK_PAD]
        k = k_ref[:, h * QK_PAD:(h + 1) * QK_PAD]
        return lax.dot_general(k, q, (((1,), (1,)), ((), ())), preferred_element_type=F32)

    def step(masked):
        if masked:
            diff = lax.broadcasted_iota(jnp.int32, (tk, qc), 0) - lax.broadcasted_iota(jnp.int32, (tk, qc), 1)
        def accumulate(h, cols, alpha, p):
            pv = jnp.dot(vt_ref[h * VT_ROWS:(h + 1) * VT_ROWS, :], p, preferred_element_type=F32)
            acc_ref[h, :, cols] = alpha * acc_ref[h, :, cols] + pv

        ahead = [scores(units[n]) for n in range(min(QK_AHEAD, len(units)))]
        pending = []
        for n, (h, c) in enumerate(units):
            s = ahead.pop(0)
            if n + QK_AHEAD < len(units):
                ahead.append(scores(units[n + QK_AHEAD]))
            cols = slice(c * qc, (c + 1) * qc)
            if masked:
                s = jnp.where(diff <= c * qc, s, NEG_BIG)
            m_prev = m_ref[h, :, cols]
            m_new = jnp.maximum(m_prev, jnp.max(s, axis=0, keepdims=True))
            alpha = jnp.exp2(m_prev - m_new)
            p = jnp.exp2(s - m_new).astype(BF16)
            m_ref[h, :, cols] = m_new
            pending.append((h, cols, alpha, p))
            if len(pending) > PV_LAG:
                accumulate(*pending.pop(0))
        for item in pending:
            accumulate(*item)

    @pl.when(j < i)
    def _():
        step(False)

    @pl.when(j == i)
    def _():
        step(True)
        outs = [acc_ref[h, :V_HEAD, :] * (1.0 / acc_ref[h, V_HEAD:V_HEAD + 1, :]) for h in range(MLA_HEADS)]
        o_ref[...] = _rms(jnp.concatenate(outs, axis=0).T, g_ref[...]).astype(o_ref.dtype)


def _attention(q, k, vt, gain, tb, qc):
    s = q.shape[0]
    nq = s // tb
    it = np.concatenate([np.full(i + 1, i, np.int32) for i in range(nq)])
    jt = np.concatenate([np.arange(i + 1, dtype=np.int32) for i in range(nq)])
    grid_spec = pltpu.PrefetchScalarGridSpec(
        num_scalar_prefetch=2,
        grid=(it.shape[0],),
        in_specs=[
            pl.BlockSpec((tb, MLA_HEADS * QK_PAD), lambda t, it, jt: (it[t], 0)),
            pl.BlockSpec((tb, MLA_HEADS * QK_PAD), lambda t, it, jt: (jt[t], 0)),
            pl.BlockSpec((MLA_HEADS * VT_ROWS, tb), lambda t, it, jt: (0, jt[t])),
            pl.BlockSpec((1, D_MLA), lambda t, it, jt: (0, 0)),
        ],
        out_specs=pl.BlockSpec((tb, D_MLA), lambda t, it, jt: (it[t], 0)),
        scratch_shapes=[
            pltpu.VMEM((MLA_HEADS, VT_ROWS, tb), F32),
            pltpu.VMEM((MLA_HEADS, 1, tb), F32),
        ],
    )
    return pl.pallas_call(
        functools.partial(_attn_kernel, tq=tb, tk=tb, qc=qc),
        grid_spec=grid_spec,
        out_shape=jax.ShapeDtypeStruct((s, D_MLA), BF16),
        compiler_params=_cparams(("arbitrary",), 48),
        name="mla_attention",
    )(jnp.asarray(it), jnp.asarray(jt), q, k, vt, gain)


def _hgrn_tables():
    n = HG_ROWS
    r = np.arange(n)
    c = np.arange(n)
    same = (r[:, None] // HG_CHUNK) == (c[None, :] // HG_CHUNK)

    def rows_upto(idx):
        return (same & (c[None, :] <= idx[:, None])).astype(np.float32)

    blocks = [rows_upto(r)]
    for m in HG_MM_LEVELS:
        blocks.append(rows_upto((r // (2 * m)) * (2 * m) + m))
    mall = np.concatenate(blocks, axis=0)

    x = r[:, None] ^ c[None, :]
    lv = np.full((n, n), -1, np.int32)
    for li, m in enumerate(HG_LEVELS):
        lv = np.where(same & (r[:, None] > c[None, :]) & (x >= m) & (x < 2 * m), li, lv)
    lv = np.where(r[:, None] == c[None, :], len(HG_LEVELS), lv)
    return jnp.asarray(mall, BF16), jnp.asarray(lv, jnp.int32)


def _hgrn_kernel(q_ref, f_ref, i_ref, g_ref, lb_ref, gn_ref, mall_ref, lv_ref, o_ref, st_ref):
    t = pl.program_id(1)
    n = HG_ROWS

    @pl.when(t == 0)
    def _():
        st_ref[...] = jnp.zeros(st_ref.shape, F32)

    rowid = lax.broadcasted_iota(jnp.int32, (n, 1), 0)
    lv = lv_ref[...]

    def head(hh):
        cols = slice(hh * HG_DK, (hh + 1) * HG_DK)
        q_in = q_ref[:, cols].astype(F32)
        qs = q_in * _sigmoid(q_in)
        lb = lb_ref[:, cols]
        f = lb + (1.0 - lb) * _sigmoid(f_ref[:, cols].astype(F32))
        kk = 1.0 - f
        logf = jnp.log(f)
        iv = i_ref[:, cols]

        l1 = logf.astype(BF16)
        l2 = (logf - l1.astype(F32)).astype(BF16)
        parts = jnp.dot(mall_ref[...], jnp.concatenate([l1, l2], axis=1), preferred_element_type=F32)
        yield
        bc = parts[:, :HG_DK] + parts[:, HG_DK:]
        b = bc[:n]

        def anchor(period, row):
            b3 = b.reshape(n // period, period, HG_DK)
            return jnp.broadcast_to(b3[:, row:row + 1, :], b3.shape).reshape(n, HG_DK)

        a = jnp.zeros((n, n), F32)
        for li, m in enumerate(HG_LEVELS):
            if m in HG_MM_LEVELS:
                k = 1 + HG_MM_LEVELS.index(m)
                c = bc[k * n:(k + 1) * n]
            else:
                c = anchor(2 * m, m)
            e = jnp.exp(-jnp.abs(b - c))
            upper = (rowid & (2 * m - 1)) >= m
            x = (jnp.where(upper, qs, kk) * e).astype(BF16)
            p = lax.dot_general(x, x, (((1,), (1,)), ((), ())), preferred_element_type=F32)
            yield
            a = jnp.where(lv == li, p, a)
        a = jnp.where(lv == len(HG_LEVELS), jnp.sum(qs * kk, axis=-1, keepdims=True), a)
        o = jnp.dot(a.astype(BF16), iv, preferred_element_type=F32)
        yield

        blast = anchor(HG_CHUNK, HG_CHUNK - 1)
        qd = (qs * jnp.exp(b)).astype(BF16)
        kd = kk * jnp.exp(blast - b)
        ivt = iv.astype(F32).T.astype(BF16)
        st = st_ref[hh]
        outs = []
        for ci in range(n // HG_CHUNK):
            lo = ci * HG_CHUNK
            inter = lax.dot_general(qd[lo:lo + HG_CHUNK], st.astype(BF16), (((1,), (1,)), ((), ())),
                                    preferred_element_type=F32)
            kd_c = jnp.where((rowid >= lo) & (rowid < lo + HG_CHUNK), kd, 0.0).astype(BF16)
            upd = jnp.dot(ivt, kd_c, preferred_element_type=F32)
            yield
            outs.append(o[lo:lo + HG_CHUNK] + inter)
            st = jnp.exp(blast[lo:lo + 1]) * st + upd
        st_ref[hh] = st
        o = jnp.concatenate(outs, axis=0)

        g_in = g_ref[:, cols].astype(F32)
        o_ref[:, cols] = (_rms(o, gn_ref[:, cols]) * (g_in * _sigmoid(g_in))).astype(o_ref.dtype)

    progs = [head(hh) for hh in range(HG_NH)]
    while progs:
        progs = [pr for pr in progs if next(pr, "done") != "done"]


def _hgrn(hg4, lb, gn, mall, lv):
    s = hg4.shape[0]
    groups = HG_HEADS // HG_NH
    width = HG_NH * HG_DK

    def col(c):
        return lambda hh, t: (t, c * groups + hh)

    head = lambda hh, t: (0, hh)
    const = lambda hh, t: (0, 0)
    return pl.pallas_call(
        _hgrn_kernel,
        grid=(groups, s // HG_ROWS),
        in_specs=[
            pl.BlockSpec((HG_ROWS, width), col(0)),
            pl.BlockSpec((HG_ROWS, width), col(1)),
            pl.BlockSpec((HG_ROWS, width), col(2)),
            pl.BlockSpec((HG_ROWS, width), col(3)),
            pl.BlockSpec((1, width), head),
            pl.BlockSpec((1, width), head),
            pl.BlockSpec(mall.shape, const),
            pl.BlockSpec(lv.shape, const),
        ],
        out_specs=pl.BlockSpec((HG_ROWS, width), lambda hh, t: (t, hh)),
        out_shape=jax.ShapeDtypeStruct((s, D_HG), BF16),
        scratch_shapes=[pltpu.VMEM((HG_NH, HG_DV, HG_DK), F32)],
        compiler_params=_cparams(("parallel", "arbitrary"), 32),
        name="hgrn2",
    )(hg4, hg4, hg4, hg4, lb, gn, mall, lv)


def _out_route_kernel(x_ref, om_ref, oh_ref, wa_ref, wb_ref, g_ref, wr_ref, br_ref,
                      h_ref, hn_ref, rt_ref):
    h1 = (x_ref[...]
          + jnp.dot(om_ref[...], wa_ref[...], preferred_element_type=F32)
          + jnp.dot(oh_ref[...], wb_ref[...], preferred_element_type=F32))
    h_ref[...] = h1
    hn = _rms(h1, g_ref[...])
    hn_ref[...] = _pack_halves(hn)
    hn_hi = hn.astype(BF16)
    hn_lo = (hn - hn_hi.astype(F32)).astype(BF16)
    hh = jnp.dot(hn_hi, wr_ref[...], preferred_element_type=F32)
    lh = jnp.dot(hn_lo, wr_ref[:, :LANES], preferred_element_type=F32)
    logits = hh[:, :LANES] + (hh[:, LANES:] + lh) + br_ref[...]

    lane = lax.broadcasted_iota(jnp.int32, logits.shape, 1)
    lanef = lane.astype(F32)
    ninf = -jnp.inf
    big = float(LANES)

    is_g = lane < N_GROUPS
    gl = jnp.where(is_g, logits, ninf)
    gmax = jnp.max(gl, axis=-1, keepdims=True)
    gsum = jnp.sum(jnp.where(is_g, jnp.exp(gl - gmax), 0.0), axis=-1, keepdims=True)
    g_w = 1.0 / gsum
    g_idx = jnp.min(jnp.where(gl == gmax, lanef, big), axis=-1, keepdims=True)

    e_lane = lane - N_GROUPS
    in_grp = (e_lane >= 0) & (e_lane < N_EXPERTS) & ((e_lane >> 3).astype(F32) == g_idx)
    el = jnp.where(in_grp, logits, ninf)
    emax = jnp.max(el, axis=-1, keepdims=True)
    esum = jnp.sum(jnp.where(in_grp, jnp.exp(el - emax), 0.0), axis=-1, keepdims=True)
    i1 = jnp.min(jnp.where(el == emax, lanef, big), axis=-1, keepdims=True)
    el2 = jnp.where(lanef == i1, ninf, el)
    emax2 = jnp.max(el2, axis=-1, keepdims=True)
    i2 = jnp.min(jnp.where(el2 == emax2, lanef, big), axis=-1, keepdims=True)
    p1 = 1.0 / esum
    p2 = jnp.exp(emax2 - emax) / esum
    w1 = g_w * p1 / (p1 + p2)
    w2 = g_w * p2 / (p1 + p2)

    rt = jnp.where(lane == 0, i1 - N_GROUPS,
                   jnp.where(lane == 1, i2 - N_GROUPS,
                             jnp.where(lane == 2, w1, jnp.where(lane == 3, w2, 0.0))))
    rt_ref[...] = rt


def _out_route(x, o_mla, o_hg, wa, wb, gain, wr, br, tm):
    s, d = x.shape
    row = lambda i: (i, 0)
    full = lambda i: (0, 0)
    return pl.pallas_call(
        _out_route_kernel,
        grid=(s // tm,),
        in_specs=[
            pl.BlockSpec((tm, d), row),
            pl.BlockSpec((tm, D_MLA), row),
            pl.BlockSpec((tm, D_HG), row),
            pl.BlockSpec(wa.shape, full, pipeline_mode=pl.Buffered(1)),
            pl.BlockSpec(wb.shape, full, pipeline_mode=pl.Buffered(1)),
            pl.BlockSpec((1, d), full),
            pl.BlockSpec(wr.shape, full, pipeline_mode=pl.Buffered(1)),
            pl.BlockSpec((1, LANES), full),
        ],
        out_specs=[
            pl.BlockSpec((tm, d), row),
            pl.BlockSpec((tm, d // 2), row),
            pl.BlockSpec((tm, LANES), row),
        ],
        out_shape=[
            jax.ShapeDtypeStruct((s, d), F32),
            jax.ShapeDtypeStruct((s, d // 2), jnp.uint32),
            jax.ShapeDtypeStruct((s, LANES), F32),
        ],
        compiler_params=_cparams(("parallel",), 56),
        name="out_proj_route",
    )(x, o_mla, o_hg, wa, wb, gain, wr, br)


def _moe_kernel(be_ref, dst_ref, nu_ref, hn_hbm, wg_hbm, wu_hbm, wd_hbm, y_hbm,
                xbuf, xsem, obuf, osem, wgb, wub, wdb, wsem, wslot, *, n_tok):
    i = pl.program_id(0)
    nused = nu_ref[0]
    e = be_ref[i]
    active = i < nused
    first = (i == 0) | (e != be_ref[jnp.maximum(i - 1, 0)])

    def gather_start(blk, sl):
        for r in range(MOE_BLOCK):
            tok = dst_ref[blk * MOE_BLOCK + r] & (n_tok - 1)
            pltpu.make_async_copy(hn_hbm.at[pl.ds(tok, 1), :], xbuf.at[sl, pl.ds(r, 1), :],
                                  xsem.at[sl]).start(priority=ROW_DMA_PRIORITY)

    def gather_wait(sl):
        pltpu.make_async_copy(hn_hbm.at[pl.ds(0, MOE_BLOCK), :], xbuf.at[sl], xsem.at[sl]).wait()

    def scatter_start(blk, sl):
        for r in range(MOE_BLOCK):
            dst = dst_ref[blk * MOE_BLOCK + r]
            pltpu.make_async_copy(obuf.at[sl, pl.ds(r, 1), :], y_hbm.at[pl.ds(dst, 1), :],
                                  osem.at[sl]).start(priority=ROW_DMA_PRIORITY)

    def scatter_wait(sl):
        pltpu.make_async_copy(obuf.at[sl], y_hbm.at[pl.ds(0, MOE_BLOCK), :], osem.at[sl]).wait()

    def spare_copy(sl):
        return pltpu.make_async_copy(obuf.at[sl], y_hbm.at[pl.ds(TOP_K * n_tok + sl * MOE_BLOCK, MOE_BLOCK), :],
                                     osem.at[sl])

    def weight_copies(ex, sl):
        return (pltpu.make_async_copy(wg_hbm.at[ex], wgb.at[sl], wsem.at[sl, 0]),
                pltpu.make_async_copy(wu_hbm.at[ex], wub.at[sl], wsem.at[sl, 1]),
                pltpu.make_async_copy(wd_hbm.at[ex], wdb.at[sl], wsem.at[sl, 2]))

    @pl.when((i == 0) & active)
    def _():
        wslot[0] = 1
        for n, cp in enumerate(weight_copies(e, 0)):
            cp.start(priority=1 - ROW_DMA_PRIORITY)
        gather_start(0, 0)
        obuf[...] = jnp.zeros(obuf.shape, jnp.uint32)
        for sl in (0, 1):
            spare_copy(sl).start()
        for sl in (0, 1):
            spare_copy(sl).wait()

    @pl.when(active & first)
    def _():
        sl = 1 - wslot[0]
        wslot[0] = sl
        nxt = lax.while_loop(lambda j: (j < nused) & (be_ref[jnp.minimum(j, nused - 1)] == e),
                             lambda j: j + 1, i + 1)

        @pl.when(nxt < nused)
        def _():
            for n, cp in enumerate(weight_copies(be_ref[nxt], 1 - sl)):
                cp.start(priority=1 - ROW_DMA_PRIORITY)

        for cp in weight_copies(e, sl):
            cp.wait()

    def block(xs):
        sl = wslot[0]

        @pl.when(i >= 2)
        def _():
            scatter_wait(xs)

        gather_wait(xs)
        gather_start(jnp.minimum(i + 1, nused - 1), 1 - xs)
        x = _unpack_halves(xbuf[xs]).astype(BF16)
        g = jnp.dot(x, wgb[sl].astype(BF16), preferred_element_type=F32)
        u = jnp.dot(x, wub[sl].astype(BF16), preferred_element_type=F32)
        hmid = (g * _sigmoid(g) * u).astype(BF16)
        obuf[xs] = _pack_halves(jnp.dot(hmid, wdb[sl].astype(BF16), preferred_element_type=F32))
        scatter_start(i, xs)

        @pl.when(i == nused - 1)
        def _():
            gather_wait(1 - xs)

            @pl.when(i >= 1)
            def _():
                scatter_wait(1 - xs)

            scatter_wait(xs)

    for s_ in (0, 1):
        pl.when(active & (i % 2 == s_))(functools.partial(block, s_))


def _moe(block_e, slot_dst, nused, hn, wg, wu, wd):
    nb = block_e.shape[0]
    n_tok, dw = hn.shape
    d = 2 * dw
    assert n_tok & (n_tok - 1) == 0
    grid_spec = pltpu.PrefetchScalarGridSpec(
        num_scalar_prefetch=3,
        grid=(nb,),
        in_specs=[pl.BlockSpec(memory_space=pl.ANY)] * 4,
        out_specs=pl.BlockSpec(memory_space=pl.ANY),
        scratch_shapes=[
            pltpu.VMEM((2, MOE_BLOCK, dw), jnp.uint32),
            pltpu.SemaphoreType.DMA((2,)),
            pltpu.VMEM((2, MOE_BLOCK, dw), jnp.uint32),
            pltpu.SemaphoreType.DMA((2,)),
            pltpu.VMEM((2, d, D_EXPERT), F32),
            pltpu.VMEM((2, d, D_EXPERT), F32),
            pltpu.VMEM((2, D_EXPERT, d), F32),
            pltpu.SemaphoreType.DMA((2, 3)),
            pltpu.SMEM((1,), jnp.int32),
        ],
    )
    return pl.pallas_call(
        functools.partial(_moe_kernel, n_tok=n_tok),
        grid_spec=grid_spec,
        out_shape=jax.ShapeDtypeStruct((TOP_K * n_tok + 2 * MOE_BLOCK, dw), jnp.uint32),
        compiler_params=_cparams(("arbitrary",), 56),
        name="moe_experts",
    )(block_e, slot_dst, nused, hn, wg, wu, wd)


def _ple_kernel(h_ref, rt_ref, y0_ref, y1_ref, p_ref, wg_ref, bg_ref, wp_ref, gp_ref, gf_ref, o_ref):
    rt = rt_ref[...]
    h2 = h_ref[...] + (rt[:, TOP_K:TOP_K + 1] * _unpack_halves(y0_ref[...])
                       + rt[:, TOP_K + 1:TOP_K + 2] * _unpack_halves(y1_ref[...]))
    hn = _rms(h2, gp_ref[...]).astype(BF16)
    gate = _sigmoid(jnp.dot(hn, wg_ref[...], preferred_element_type=F32) + bg_ref[...])
    pe = jnp.dot(p_ref[...].astype(BF16), wp_ref[...], preferred_element_type=F32)
    h3 = h2 + gate * pe
    o_ref[...] = _rms(h3, gf_ref[...])


def _ple_final(h1, rt, y, p, wg, bg, wp, gp, gf, tm):
    s, d = h1.shape
    nblk = s // tm
    row = lambda i: (i, 0)
    full = lambda i: (0, 0)
    return pl.pallas_call(
        _ple_kernel,
        grid=(nblk,),
        in_specs=[
            pl.BlockSpec((tm, d), row),
            pl.BlockSpec((tm, LANES), row),
            pl.BlockSpec((tm, d // 2), row),
            pl.BlockSpec((tm, d // 2), lambda i: (i + nblk, 0)),
            pl.BlockSpec((tm, PLE_DIM), row),
            pl.BlockSpec(wg.shape, full, pipeline_mode=pl.Buffered(1)),
            pl.BlockSpec((1, d), full),
            pl.BlockSpec(wp.shape, full, pipeline_mode=pl.Buffered(1)),
            pl.BlockSpec((1, d), full),
            pl.BlockSpec((1, d), full),
        ],
        out_specs=pl.BlockSpec((tm, d), row),
        out_shape=jax.ShapeDtypeStruct((s, d), F32),
        compiler_params=_cparams(("parallel",), 56),
        name="ple_final",
    )(h1, rt, y, y, p, wg, bg, wp, gp, gf)


def _slots_kernel(rt_ref, tri_ref, pos_ref, cnt_ref, carry_ref, base_ref):
    ps = pl.program_id(0)
    b = pl.program_id(1)
    rt = rt_ref[...]
    lane = lax.broadcasted_iota(jnp.int32, rt.shape, 1)
    lanef = lane.astype(F32)
    oh = [(lanef == rt[:, kk:kk + 1]).astype(F32) for kk in range(TOP_K)]
    both = oh[0] + oh[1]
    colsum = jnp.sum(both, axis=0, keepdims=True)

    @pl.when((ps == 0) & (b == 0))
    def _():
        cnt_ref[...] = jnp.zeros(cnt_ref.shape, F32)

    @pl.when(ps == 0)
    def _():
        cnt_ref[...] = cnt_ref[...] + colsum

    @pl.when((ps == 1) & (b == 0))
    def _():
        blocks = jnp.floor((cnt_ref[...] + (MOE_BLOCK - 0.5)) / MOE_BLOCK)
        r = lax.broadcasted_iota(jnp.int32, (LANES, LANES), 0)
        c = lax.broadcasted_iota(jnp.int32, (LANES, LANES), 1)
        before = (r < c).astype(F32)
        base_ref[...] = jnp.dot(blocks * MOE_BLOCK, before, preferred_element_type=F32,
                                precision=lax.Precision.HIGHEST)
        carry_ref[...] = jnp.zeros(carry_ref.shape, F32)

    @pl.when(ps == 1)
    def _():
        earlier = jnp.dot(tri_ref[...], both.astype(BF16), preferred_element_type=F32)
        row = earlier + (base_ref[0:1, :] + carry_ref[0:1, :])
        s0 = jnp.sum(oh[0] * row, axis=-1, keepdims=True)
        s1 = jnp.sum(oh[1] * row, axis=-1, keepdims=True)
        pos_ref[...] = jnp.where(lane == 0, s0, jnp.where(lane == 1, s1, 0.0))
        carry_ref[...] = carry_ref[...] + colsum


def _slots(rt, tb):
    s = rt.shape[0]
    tri = (np.arange(tb)[:, None] > np.arange(tb)[None, :]).astype(np.float32)
    return pl.pallas_call(
        _slots_kernel,
        grid=(2, s // tb),
        in_specs=[
            pl.BlockSpec((tb, LANES), lambda ps, b: (b, 0)),
            pl.BlockSpec((tb, tb), lambda ps, b: (0, 0)),
        ],
        out_specs=[
            pl.BlockSpec((tb, LANES), lambda ps, b: (b * ps, 0)),
            pl.BlockSpec((8, LANES), lambda ps, b: (0, 0)),
        ],
        out_shape=[
            jax.ShapeDtypeStruct((s, LANES), F32),
            jax.ShapeDtypeStruct((8, LANES), F32),
        ],
        scratch_shapes=[pltpu.VMEM((8, LANES), F32), pltpu.VMEM((8, LANES), F32)],
        compiler_params=_cparams(("arbitrary", "arbitrary")),
        name="moe_slots",
    )(rt, jnp.asarray(tri, BF16))


def _dispatch(rt, n_tok):
    a = n_tok * TOP_K
    nb = -(-a // MOE_BLOCK) + N_EXPERTS
    pos_f, cnt = _slots(rt, 512)
    pos = pos_f[:, :TOP_K].astype(jnp.int32).reshape(-1)
    counts = cnt[0, :N_EXPERTS].astype(jnp.int32)
    nblk = (counts + MOE_BLOCK - 1) // MOE_BLOCK
    bends = jnp.cumsum(nblk)
    bidx = jnp.arange(nb, dtype=jnp.int32)
    block_e = jnp.clip(jnp.searchsorted(bends, bidx, side="right"), 0, N_EXPERTS - 1).astype(jnp.int32)
    nused = bends[-1].astype(jnp.int32)
    a_idx = jnp.arange(a, dtype=jnp.int32)
    sidx = jnp.arange(nb * MOE_BLOCK, dtype=jnp.int32)
    spare = a + ((sidx // MOE_BLOCK) % 2) * MOE_BLOCK + sidx % MOE_BLOCK
    slot_dst = spare.at[pos].set((a_idx % TOP_K) * n_tok + a_idx // TOP_K)
    return block_e, slot_dst, nused.reshape(1)


def kernel(x, p, positions, attn_norm, w_in, q_norm, w_uq, kv_norm, w_ukv, mla_norm, hg_lb_logits, hg_norm, w_out, ffn_norm, w_router_group, b_router_group, w_router_expert, b_router_expert, w_exp_gate, w_exp_up, w_exp_down, ple_norm, w_ple_gate, b_ple_gate, w_ple_proj, final_norm):
    bsz, s, d = x.shape
    assert bsz == 1 and w_in.shape[0] == 1
    xt = x[0]

    inv_freq = 1.0 / (ROPE_THETA ** (jnp.arange(0, QK_ROPE, 2, dtype=F32) / QK_ROPE))
    ang = positions[0].astype(F32)[:, None] * inv_freq
    cos, sin = jnp.cos(ang), jnp.sin(ang)
    zpad = jnp.zeros((s, LANES - QK_ROPE), F32)
    cc = jnp.concatenate([cos, cos, zpad], axis=1)
    ss = jnp.concatenate([-sin, sin, zpad], axis=1)

    lb = jnp.cumsum(jax.nn.softmax(hg_lb_logits.astype(F32), axis=0), axis=0)[0][None, :]

    wi = w_in[0]
    kr0 = Q_LORA + KV_LORA
    half = QK_ROPE // 2
    w_lat = jnp.concatenate(
        [wi[:, :kr0 + QK_ROPE], wi[:, kr0 + half:kr0 + QK_ROPE], wi[:, kr0:kr0 + half]], axis=1).astype(BF16)
    w_hg = wi[:, kr0 + QK_ROPE:].astype(BF16)
    wq3 = w_uq[0].reshape(Q_LORA, MLA_HEADS, QK_HEAD)
    wq_pad = jnp.concatenate(
        [wq3, wq3[:, :, QK_NOPE + half:], wq3[:, :, QK_NOPE:QK_NOPE + half]], axis=2
    ).reshape(Q_LORA, MLA_HEADS * QK_PAD).astype(BF16)
    wkv3 = w_ukv[0].reshape(KV_LORA, MLA_HEADS, QK_NOPE + V_HEAD)
    wkn = wkv3[:, :, :QK_NOPE].reshape(KV_LORA, MLA_HEADS * QK_NOPE).astype(BF16)
    wvt = wkv3[:, :, QK_NOPE:].reshape(KV_LORA, D_MLA).T.astype(BF16)
    wo = w_out[0].astype(BF16)
    wr = jnp.concatenate(
        [w_router_group[0], w_router_expert[0], jnp.zeros((d, LANES - N_GROUPS - N_EXPERTS), F32)], axis=1)
    wr_hi = wr.astype(BF16)
    wr = jnp.concatenate([wr_hi, (wr - wr_hi.astype(F32)).astype(BF16)], axis=1)
    br = jnp.concatenate(
        [b_router_group[0], b_router_expert[0], jnp.zeros((LANES - N_GROUPS - N_EXPERTS,), F32)])[None, :]

    lat = _in_proj(xt, attn_norm, w_lat, F32, 512, D_LAT, "in_proj_latent")
    hg4 = _in_proj(xt, attn_norm, w_hg, BF16, 1024, 1024, "in_proj_hgrn")
    q, k, vt = _mla_up(lat, q_norm, kv_norm, wq_pad, wkn, wvt, cc, ss, 512)
    o_mla = _attention(q, k, vt, mla_norm, 512, 512)
    mall, lv = _hgrn_tables()
    o_hg = _hgrn(hg4, lb, hg_norm[0].reshape(1, D_HG), mall, lv)

    h1, hn, rt = _out_route(xt, o_mla, o_hg, wo[:D_MLA], wo[D_MLA:], ffn_norm, wr, br, 512)
    block_e, slot_dst, nused = _dispatch(rt, s)
    y = _moe(block_e, slot_dst, nused, hn, w_exp_gate[0], w_exp_up[0], w_exp_down[0])

    out = _ple_final(h1, rt, y, p[0, 0], w_ple_gate[0].astype(BF16), b_ple_gate,
                     w_ple_proj[0].astype(BF16), ple_norm, final_norm[None, :], 512)
    return out[None]
```

```python
import functools

import jax
import jax.numpy as jnp
import numpy as np
from jax import lax
from jax.experimental import pallas as pl
from jax.experimental.pallas import tpu as pltpu

F32 = jnp.float32
BF16 = jnp.bfloat16

D_MODEL = 2048
PLE_DIM = 256
MLA_HEADS = 8
QK_NOPE = 128
QK_ROPE = 64
QK_HEAD = QK_NOPE + QK_ROPE
QK_PAD = 256
V_HEAD = 128
V_PAD = 2 * V_HEAD
QK_AHEAD = 3
PV_LAG = 2
LAG_LIMIT = 60.0
Q_LORA = 512
KV_LORA = 256
ROPE_THETA = 10000.0
HG_HEADS = 8
HG_DK = 128
HG_DV = 128
HG_CHUNK = 64
D_MLA = MLA_HEADS * V_HEAD
D_HG = HG_HEADS * HG_DV
N_GROUPS = 8
EXPERTS_PER_GROUP = 8
N_EXPERTS = N_GROUPS * EXPERTS_PER_GROUP
TOP_K = 2
D_EXPERT = 512
EPS = 1e-6
LANES = 128
D_LAT = Q_LORA + KV_LORA + 2 * QK_ROPE
NEG_BIG = -1e30
LOG2E = 1.4426950408889634

MOE_BLOCK = 320
ROW_DMA_PRIORITY = 0
HG_ROWS = 256
HG_LEVELS = (32, 16, 8, 4, 2, 1)
HG_MM_LEVELS = (2, 1)
HG_NH = 4


def _cparams(sem, vmem_mb=None, flags=None):
    kw = dict(dimension_semantics=sem)
    if vmem_mb is not None:
        kw["vmem_limit_bytes"] = vmem_mb * 1024 * 1024
    if flags is not None:
        kw["flags"] = flags
    return pltpu.CompilerParams(**kw)


def _rms(x, g):
    ms = jnp.mean(x * x, axis=-1, keepdims=True)
    return x * lax.rsqrt(ms + EPS) * g


def _sigmoid(x):
    return 1.0 / (1.0 + jnp.exp(-x))


def _pack_halves(x):
    n = x.shape[1] // 2
    bits = lax.bitcast_convert_type(x.astype(BF16).astype(F32), jnp.uint32)
    return bits[:, :n] | (bits[:, n:] >> 16)


def _unpack_halves(w):
    hi = lax.bitcast_convert_type(w & jnp.uint32(0xFFFF0000), F32)
    lo = lax.bitcast_convert_type(w << 16, F32)
    return jnp.concatenate([hi, lo], axis=1)


def _in_proj_kernel(x_ref, g_ref, w_ref, o_ref, xn_ref):
    @pl.when(pl.program_id(1) == 0)
    def _():
        xn_ref[...] = _rms(x_ref[...], g_ref[...]).astype(BF16)

    o_ref[...] = jnp.dot(xn_ref[...], w_ref[...], preferred_element_type=F32).astype(o_ref.dtype)


def _in_proj(x, gain, w, out_dtype, tm, tn, name):
    s, d = x.shape
    n = w.shape[1]
    return pl.pallas_call(
        _in_proj_kernel,
        grid=(s // tm, n // tn),
        in_specs=[
            pl.BlockSpec((tm, d), lambda i, j: (i, 0)),
            pl.BlockSpec((1, d), lambda i, j: (0, 0)),
            pl.BlockSpec((d, tn), lambda i, j: (0, j)),
        ],
        out_specs=pl.BlockSpec((tm, tn), lambda i, j: (i, j)),
        out_shape=jax.ShapeDtypeStruct((s, n), out_dtype),
        scratch_shapes=[pltpu.VMEM((tm, d), BF16)],
        compiler_params=_cparams(("parallel", "arbitrary"), 48),
        name=name,
    )(x, gain, w)


def _mla_up_kernel(lat_ref, qn_ref, kvn_ref, wq_ref, wknt_ref, wv_ref, cc_ref, ss_ref,
                   q_ref, kt_ref, v_ref):
    lat = lat_ref[...]
    cq = _rms(lat[:, :Q_LORA], qn_ref[...]).astype(BF16)
    ckv = _rms(lat[:, Q_LORA:Q_LORA + KV_LORA], kvn_ref[...]).astype(BF16)
    kp = lat[:, Q_LORA + KV_LORA:]
    cc = cc_ref[...]
    ss = ss_ref[...]
    scale = QK_HEAD ** -0.5 * LOG2E

    q = jnp.dot(cq, wq_ref[...], preferred_element_type=F32)
    for h in range(MLA_HEADS):
        base = h * QK_PAD
        q_ref[:, base:base + QK_NOPE] = (q[:, base:base + QK_NOPE] * scale).astype(BF16)
        r = q[:, base + QK_NOPE:base + QK_PAD]
        r = (r * cc + pltpu.roll(r, QK_ROPE, 1) * ss) * scale
        q_ref[:, base + QK_NOPE:base + QK_PAD] = r.astype(BF16)

    kpe_t = (kp * cc + pltpu.roll(kp, QK_ROPE, 1) * ss).T.astype(BF16)
    kn_t = lax.dot_general(wknt_ref[...], ckv, (((1,), (1,)), ((), ())), preferred_element_type=F32).astype(BF16)
    for h in range(MLA_HEADS):
        base = h * QK_PAD
        kt_ref[base:base + QK_NOPE, :] = kn_t[h * QK_NOPE:(h + 1) * QK_NOPE]
        kt_ref[base + QK_NOPE:base + QK_PAD, :] = kpe_t
    v = jnp.dot(ckv, wv_ref[...], preferred_element_type=F32).astype(BF16)
    ones = jnp.ones((v.shape[0], V_PAD - V_HEAD), BF16)
    for h in range(MLA_HEADS):
        v_ref[:, h * V_PAD:h * V_PAD + V_HEAD] = v[:, h * V_HEAD:(h + 1) * V_HEAD]
        v_ref[:, h * V_PAD + V_HEAD:(h + 1) * V_PAD] = ones


def _mla_up(lat, qn, kvn, wq, wknt, wv, cc, ss, tm):
    s = lat.shape[0]
    row = lambda i: (i, 0)
    full = lambda i: (0, 0)
    return pl.pallas_call(
        _mla_up_kernel,
        grid=(s // tm,),
        in_specs=[
            pl.BlockSpec((tm, D_LAT), row),
            pl.BlockSpec((1, Q_LORA), full),
            pl.BlockSpec((1, KV_LORA), full),
            pl.BlockSpec(wq.shape, full),
            pl.BlockSpec(wknt.shape, full),
            pl.BlockSpec(wv.shape, full),
            pl.BlockSpec((tm, LANES), row),
            pl.BlockSpec((tm, LANES), row),
        ],
        out_specs=[
            pl.BlockSpec((tm, MLA_HEADS * QK_PAD), row),
            pl.BlockSpec((MLA_HEADS * QK_PAD, tm), lambda i: (0, i)),
            pl.BlockSpec((tm, MLA_HEADS * V_PAD), row),
        ],
        out_shape=[
            jax.ShapeDtypeStruct((s, MLA_HEADS * QK_PAD), BF16),
            jax.ShapeDtypeStruct((MLA_HEADS * QK_PAD, s), BF16),
            jax.ShapeDtypeStruct((s, MLA_HEADS * V_PAD), BF16),
        ],
        compiler_params=_cparams(("parallel",), 48),
        name="mla_up",
    )(lat, qn, kvn, wq, wknt, wv, cc, ss)


def _attn_kernel(it_ref, jt_ref, q_ref, kt_ref, v_ref, g_ref, o_ref, acc_ref, m_ref, redo_ref, *, tb):
    t = pl.program_id(0)
    i = it_ref[t]
    j = jt_ref[t]
    src = t % 2
    dst = 1 - src
    ntile = tb // LANES

    @pl.when(j == 0)
    def _():
        m_ref[...] = jnp.full(m_ref.shape, NEG_BIG, F32)
        acc_ref[src] = jnp.zeros(acc_ref.shape[1:], F32)
        redo_ref[0] = 0

    def scores(h):
        return jnp.dot(q_ref[:, h * QK_PAD:(h + 1) * QK_PAD], kt_ref[h * QK_PAD:(h + 1) * QK_PAD, :],
                       preferred_element_type=F32)

    def values(h):
        return v_ref[:, h * V_PAD:(h + 1) * V_PAD]

    def key_minus_query():
        return (lax.broadcasted_iota(jnp.int32, (tb, tb), 1) - lax.broadcasted_iota(jnp.int32, (tb, tb), 0))

    def lagged(masked):
        if masked:
            keep = key_minus_query() <= 0
        ahead = [scores(h) for h in range(min(QK_AHEAD, MLA_HEADS))]
        pending = []
        worst = None

        def accumulate(h, p):
            acc_ref[dst, h] = acc_ref[src, h] + jnp.dot(p, values(h), preferred_element_type=F32)

        for h in range(MLA_HEADS):
            s = ahead.pop(0)
            if h + QK_AHEAD < MLA_HEADS:
                ahead.append(scores(h + QK_AHEAD))
            d = s - jnp.concatenate([m_ref[h]] * ntile, axis=1)
            if masked:
                d = jnp.where(keep, d, NEG_BIG)
            top = jnp.max(d.reshape(tb // 8, 8, tb), axis=0)
            worst = top if worst is None else jnp.maximum(worst, top)
            pending.append((h, jnp.exp2(d).astype(BF16)))
            if len(pending) > PV_LAG:
                accumulate(*pending.pop(0))
        for item in pending:
            accumulate(*item)
        redo_ref[0] = (jnp.max(worst) > LAG_LIMIT).astype(jnp.int32)

    def exact():
        keep = key_minus_query() <= jnp.where(j == i, 0, tb)
        for h in range(MLA_HEADS):
            s = jnp.where(keep, scores(h), NEG_BIG)
            m_prev = m_ref[h]
            m_new = jnp.maximum(m_prev, jnp.max(s, axis=1, keepdims=True))
            alpha = jnp.exp2(m_prev - m_new)
            p = jnp.exp2(s - jnp.concatenate([m_new] * ntile, axis=1)).astype(BF16)
            m_ref[h] = m_new
            acc_ref[dst, h] = (acc_ref[src, h] * jnp.concatenate([alpha] * (V_PAD // LANES), axis=1)
                               + jnp.dot(p, values(h), preferred_element_type=F32))

    @pl.when((j > 0) & (j < i))
    def _():
        lagged(False)

    @pl.when((j > 0) & (j == i))
    def _():
        lagged(True)

    @pl.when((j == 0) | (redo_ref[0] != 0))
    def _():
        exact()

    @pl.when(j == i)
    def _():
        outs = [acc_ref[dst, h, :, :V_HEAD] / acc_ref[dst, h, :, V_HEAD:] for h in range(MLA_HEADS)]
        o_ref[...] = _rms(jnp.concatenate(outs, axis=1), g_ref[...]).astype(o_ref.dtype)


def _attention(q, kt, v, gain, tb):
    s = q.shape[0]
    nq = s // tb
    it = np.concatenate([np.full(i + 1, i, np.int32) for i in range(nq)])
    jt = np.concatenate([np.arange(i + 1, dtype=np.int32) for i in range(nq)])
    grid_spec = pltpu.PrefetchScalarGridSpec(
        num_scalar_prefetch=2,
        grid=(it.shape[0],),
        in_specs=[
            pl.BlockSpec((tb, MLA_HEADS * QK_PAD), lambda t, it, jt: (it[t], 0)),
            pl.BlockSpec((MLA_HEADS * QK_PAD, tb), lambda t, it, jt: (0, jt[t])),
            pl.BlockSpec((tb, MLA_HEADS * V_PAD), lambda t, it, jt: (jt[t], 0)),
            pl.BlockSpec((1, D_MLA), lambda t, it, jt: (0, 0)),
        ],
        out_specs=pl.BlockSpec((tb, D_MLA), lambda t, it, jt: (it[t], 0)),
        scratch_shapes=[
            pltpu.VMEM((2, MLA_HEADS, tb, V_PAD), F32),
            pltpu.VMEM((MLA_HEADS, tb, LANES), F32),
            pltpu.SMEM((1,), jnp.int32),
        ],
    )
    return pl.pallas_call(
        functools.partial(_attn_kernel, tb=tb),
        grid_spec=grid_spec,
        out_shape=jax.ShapeDtypeStruct((s, D_MLA), BF16),
        compiler_params=_cparams(("arbitrary",), 48),
        name="mla_attention",
    )(jnp.asarray(it), jnp.asarray(jt), q, kt, v, gain)


def _hgrn_tables():
    n = HG_ROWS
    r = np.arange(n)
    c = np.arange(n)
    same = (r[:, None] // HG_CHUNK) == (c[None, :] // HG_CHUNK)

    def rows_upto(idx):
        return (same & (c[None, :] <= idx[:, None])).astype(np.float32)

    blocks = [rows_upto(r)]
    for m in HG_MM_LEVELS:
        blocks.append(rows_upto((r // (2 * m)) * (2 * m) + m))
    mall = np.concatenate(blocks, axis=0)

    x = r[:, None] ^ c[None, :]
    lv = np.full((n, n), -1, np.int32)
    for li, m in enumerate(HG_LEVELS):
        lv = np.where(same & (r[:, None] > c[None, :]) & (x >= m) & (x < 2 * m), li, lv)
    lv = np.where(r[:, None] == c[None, :], len(HG_LEVELS), lv)
    return jnp.asarray(mall, BF16), jnp.asarray(lv, jnp.int32)


def _hgrn_kernel(q_ref, f_ref, i_ref, g_ref, lb_ref, gn_ref, mall_ref, lv_ref, o_ref, st_ref):
    t = pl.program_id(1)
    n = HG_ROWS

    @pl.when(t == 0)
    def _():
        st_ref[...] = jnp.zeros(st_ref.shape, F32)

    rowid = lax.broadcasted_iota(jnp.int32, (n, 1), 0)
    lv = lv_ref[...]

    def head(hh):
        cols = slice(hh * HG_DK, (hh + 1) * HG_DK)
        q_in = q_ref[:, cols].astype(F32)
        qs = q_in * _sigmoid(q_in)
        lb = lb_ref[:, cols]
        f = lb + (1.0 - lb) * _sigmoid(f_ref[:, cols].astype(F32))
        kk = 1.0 - f
        logf = jnp.log(f)
        iv = i_ref[:, cols]

        l1 = logf.astype(BF16)
        l2 = (logf - l1.astype(F32)).astype(BF16)
        parts = jnp.dot(mall_ref[...], jnp.concatenate([l1, l2], axis=1), preferred_element_type=F32)
        yield
        bc = parts[:, :HG_DK] + parts[:, HG_DK:]
        b = bc[:n]

        def anchor(period, row):
            b3 = b.reshape(n // period, period, HG_DK)
            return jnp.broadcast_to(b3[:, row:row + 1, :], b3.shape).reshape(n, HG_DK)

        a = jnp.zeros((n, n), F32)
        for li, m in enumerate(HG_LEVELS):
            if m in HG_MM_LEVELS:
                k = 1 + HG_MM_LEVELS.index(m)
                c = bc[k * n:(k + 1) * n]
            else:
                c = anchor(2 * m, m)
            e = jnp.exp(-jnp.abs(b - c))
            upper = (rowid & (2 * m - 1)) >= m
            x = (jnp.where(upper, qs, kk) * e).astype(BF16)
            p = lax.dot_general(x, x, (((1,), (1,)), ((), ())), preferred_element_type=F32)
            yield
            a = jnp.where(lv == li, p, a)
        a = jnp.where(lv == len(HG_LEVELS), jnp.sum(qs * kk, axis=-1, keepdims=True), a)
        o = jnp.dot(a.astype(BF16), iv, preferred_element_type=F32)
        yield

        blast = anchor(HG_CHUNK, HG_CHUNK - 1)
        qd = (qs * jnp.exp(b)).astype(BF16)
        kd = kk * jnp.exp(blast - b)
        ivt = iv.astype(F32).T.astype(BF16)
        st = st_ref[hh]
        outs = []
        for ci in range(n // HG_CHUNK):
            lo = ci * HG_CHUNK
            inter = lax.dot_general(qd[lo:lo + HG_CHUNK], st.astype(BF16), (((1,), (1,)), ((), ())),
                                    preferred_element_type=F32)
            kd_c = jnp.where((rowid >= lo) & (rowid < lo + HG_CHUNK), kd, 0.0).astype(BF16)
            upd = jnp.dot(ivt, kd_c, preferred_element_type=F32)
            yield
            outs.append(o[lo:lo + HG_CHUNK] + inter)
            st = jnp.exp(blast[lo:lo + 1]) * st + upd
        st_ref[hh] = st
        o = jnp.concatenate(outs, axis=0)

        g_in = g_ref[:, cols].astype(F32)
        o_ref[:, cols] = (_rms(o, gn_ref[:, cols]) * (g_in * _sigmoid(g_in))).astype(o_ref.dtype)

    progs = [head(hh) for hh in range(HG_NH)]
    while progs:
        progs = [pr for pr in progs if next(pr, "done") != "done"]


def _hgrn(hg4, lb, gn, mall, lv):
    s = hg4.shape[0]
    groups = HG_HEADS // HG_NH
    width = HG_NH * HG_DK

    def col(c):
        return lambda hh, t: (t, c * groups + hh)

    head = lambda hh, t: (0, hh)
    const = lambda hh, t: (0, 0)
    return pl.pallas_call(
        _hgrn_kernel,
        grid=(groups, s // HG_ROWS),
        in_specs=[
            pl.BlockSpec((HG_ROWS, width), col(0)),
            pl.BlockSpec((HG_ROWS, width), col(1)),
            pl.BlockSpec((HG_ROWS, width), col(2)),
            pl.BlockSpec((HG_ROWS, width), col(3)),
            pl.BlockSpec((1, width), head),
            pl.BlockSpec((1, width), head),
            pl.BlockSpec(mall.shape, const),
            pl.BlockSpec(lv.shape, const),
        ],
        out_specs=pl.BlockSpec((HG_ROWS, width), lambda hh, t: (t, hh)),
        out_shape=jax.ShapeDtypeStruct((s, D_HG), BF16),
        scratch_shapes=[pltpu.VMEM((HG_NH, HG_DV, HG_DK), F32)],
        compiler_params=_cparams(("parallel", "arbitrary"), 32),
        name="hgrn2",
    )(hg4, hg4, hg4, hg4, lb, gn, mall, lv)


def _out_route_kernel(x_ref, om_ref, oh_ref, wa_ref, wb_ref, g_ref, wr_ref, br_ref,
                      h_ref, hn_ref, rt_ref):
    h1 = (x_ref[...]
          + jnp.dot(om_ref[...], wa_ref[...], preferred_element_type=F32)
          + jnp.dot(oh_ref[...], wb_ref[...], preferred_element_type=F32))
    h_ref[...] = h1
    hn = _rms(h1, g_ref[...])
    hn_ref[...] = _pack_halves(hn)
    hn_hi = hn.astype(BF16)
    hn_lo = (hn - hn_hi.astype(F32)).astype(BF16)
    hh = jnp.dot(hn_hi, wr_ref[...], preferred_element_type=F32)
    lh = jnp.dot(hn_lo, wr_ref[:, :LANES], preferred_element_type=F32)
    logits = hh[:, :LANES] + (hh[:, LANES:] + lh) + br_ref[...]

    lane = lax.broadcasted_iota(jnp.int32, logits.shape, 1)
    lanef = lane.astype(F32)
    ninf = -jnp.inf
    big = float(LANES)

    is_g = lane < N_GROUPS
    gl = jnp.where(is_g, logits, ninf)
    gmax = jnp.max(gl, axis=-1, keepdims=True)
    gsum = jnp.sum(jnp.where(is_g, jnp.exp(gl - gmax), 0.0), axis=-1, keepdims=True)
    g_w = 1.0 / gsum
    g_idx = jnp.min(jnp.where(gl == gmax, lanef, big), axis=-1, keepdims=True)

    e_lane = lane - N_GROUPS
    in_grp = (e_lane >= 0) & (e_lane < N_EXPERTS) & ((e_lane >> 3).astype(F32) == g_idx)
    el = jnp.where(in_grp, logits, ninf)
    emax = jnp.max(el, axis=-1, keepdims=True)
    esum = jnp.sum(jnp.where(in_grp, jnp.exp(el - emax), 0.0), axis=-1, keepdims=True)
    i1 = jnp.min(jnp.where(el == emax, lanef, big), axis=-1, keepdims=True)
    el2 = jnp.where(lanef == i1, ninf, el)
    emax2 = jnp.max(el2, axis=-1, keepdims=True)
    i2 = jnp.min(jnp.where(el2 == emax2, lanef, big), axis=-1, keepdims=True)
    p1 = 1.0 / esum
    p2 = jnp.exp(emax2 - emax) / esum
    w1 = g_w * p1 / (p1 + p2)
    w2 = g_w * p2 / (p1 + p2)

    rt = jnp.where(lane == 0, i1 - N_GROUPS,
                   jnp.where(lane == 1, i2 - N_GROUPS,
                             jnp.where(lane == 2, w1, jnp.where(lane == 3, w2, 0.0))))
    rt_ref[...] = rt


def _out_route(x, o_mla, o_hg, wa, wb, gain, wr, br, tm):
    s, d = x.shape
    row = lambda i: (i, 0)
    full = lambda i: (0, 0)
    return pl.pallas_call(
        _out_route_kernel,
        grid=(s // tm,),
        in_specs=[
            pl.BlockSpec((tm, d), row),
            pl.BlockSpec((tm, D_MLA), row),
            pl.BlockSpec((tm, D_HG), row),
            pl.BlockSpec(wa.shape, full, pipeline_mode=pl.Buffered(1)),
            pl.BlockSpec(wb.shape, full, pipeline_mode=pl.Buffered(1)),
            pl.BlockSpec((1, d), full),
            pl.BlockSpec(wr.shape, full, pipeline_mode=pl.Buffered(1)),
            pl.BlockSpec((1, LANES), full),
        ],
        out_specs=[
            pl.BlockSpec((tm, d), row),
            pl.BlockSpec((tm, d // 2), row),
            pl.BlockSpec((tm, LANES), row),
        ],
        out_shape=[
            jax.ShapeDtypeStruct((s, d), F32),
            jax.ShapeDtypeStruct((s, d // 2), jnp.uint32),
            jax.ShapeDtypeStruct((s, LANES), F32),
        ],
        compiler_params=_cparams(("parallel",), 56),
        name="out_proj_route",
    )(x, o_mla, o_hg, wa, wb, gain, wr, br)


def _moe_kernel(be_ref, dst_ref, nu_ref, hn_hbm, wg_hbm, wu_hbm, wd_hbm, y_hbm,
                xbuf, xsem, obuf, osem, wgb, wub, wdb, wsem, wslot, *, n_tok):
    i = pl.program_id(0)
    nused = nu_ref[0]
    e = be_ref[i]
    active = i < nused
    first = (i == 0) | (e != be_ref[jnp.maximum(i - 1, 0)])

    def gather_start(blk, sl):
        for r in range(MOE_BLOCK):
            tok = dst_ref[blk * MOE_BLOCK + r] & (n_tok - 1)
            pltpu.make_async_copy(hn_hbm.at[pl.ds(tok, 1), :], xbuf.at[sl, pl.ds(r, 1), :],
                                  xsem.at[sl]).start(priority=ROW_DMA_PRIORITY)

    def gather_wait(sl):
        pltpu.make_async_copy(hn_hbm.at[pl.ds(0, MOE_BLOCK), :], xbuf.at[sl], xsem.at[sl]).wait()

    def scatter_start(blk, sl):
        for r in range(MOE_BLOCK):
            dst = dst_ref[blk * MOE_BLOCK + r]
            pltpu.make_async_copy(obuf.at[sl, pl.ds(r, 1), :], y_hbm.at[pl.ds(dst, 1), :],
                                  osem.at[sl]).start(priority=ROW_DMA_PRIORITY)

    def scatter_wait(sl):
        pltpu.make_async_copy(obuf.at[sl], y_hbm.at[pl.ds(0, MOE_BLOCK), :], osem.at[sl]).wait()

    def spare_copy(sl):
        return pltpu.make_async_copy(obuf.at[sl], y_hbm.at[pl.ds(TOP_K * n_tok + sl * MOE_BLOCK, MOE_BLOCK), :],
                                     osem.at[sl])

    def weight_copies(ex, sl):
        return (pltpu.make_async_copy(wg_hbm.at[ex], wgb.at[sl], wsem.at[sl, 0]),
                pltpu.make_async_copy(wu_hbm.at[ex], wub.at[sl], wsem.at[sl, 1]),
                pltpu.make_async_copy(wd_hbm.at[ex], wdb.at[sl], wsem.at[sl, 2]))

    @pl.when((i == 0) & active)
    def _():
        wslot[0] = 1
        for n, cp in enumerate(weight_copies(e, 0)):
            cp.start(priority=1 - ROW_DMA_PRIORITY)
        gather_start(0, 0)
        obuf[...] = jnp.zeros(obuf.shape, jnp.uint32)
        for sl in (0, 1):
            spare_copy(sl).start()
        for sl in (0, 1):
            spare_copy(sl).wait()

    @pl.when(active & first)
    def _():
        sl = 1 - wslot[0]
        wslot[0] = sl
        nxt = lax.while_loop(lambda j: (j < nused) & (be_ref[jnp.minimum(j, nused - 1)] == e),
                             lambda j: j + 1, i + 1)

        @pl.when(nxt < nused)
        def _():
            for n, cp in enumerate(weight_copies(be_ref[nxt], 1 - sl)):
                cp.start(priority=1 - ROW_DMA_PRIORITY)

        for cp in weight_copies(e, sl):
            cp.wait()

    def block(xs):
        sl = wslot[0]

        @pl.when(i >= 2)
        def _():
            scatter_wait(xs)

        gather_wait(xs)
        gather_start(jnp.minimum(i + 1, nused - 1), 1 - xs)
        x = _unpack_halves(xbuf[xs]).astype(BF16)
        g = jnp.dot(x, wgb[sl].astype(BF16), preferred_element_type=F32)
        u = jnp.dot(x, wub[sl].astype(BF16), preferred_element_type=F32)
        hmid = (g * _sigmoid(g) * u).astype(BF16)
        obuf[xs] = _pack_halves(jnp.dot(hmid, wdb[sl].astype(BF16), preferred_element_type=F32))
        scatter_start(i, xs)

        @pl.when(i == nused - 1)
        def _():
            gather_wait(1 - xs)

            @pl.when(i >= 1)
            def _():
                scatter_wait(1 - xs)

            scatter_wait(xs)

    for s_ in (0, 1):
        pl.when(active & (i % 2 == s_))(functools.partial(block, s_))


def _moe(block_e, slot_dst, nused, hn, wg, wu, wd):
    nb = block_e.shape[0]
    n_tok, dw = hn.shape
    d = 2 * dw
    assert n_tok & (n_tok - 1) == 0
    grid_spec = pltpu.PrefetchScalarGridSpec(
        num_scalar_prefetch=3,
        grid=(nb,),
        in_specs=[pl.BlockSpec(memory_space=pl.ANY)] * 4,
        out_specs=pl.BlockSpec(memory_space=pl.ANY),
        scratch_shapes=[
            pltpu.VMEM((2, MOE_BLOCK, dw), jnp.uint32),
            pltpu.SemaphoreType.DMA((2,)),
            pltpu.VMEM((2, MOE_BLOCK, dw), jnp.uint32),
            pltpu.SemaphoreType.DMA((2,)),
            pltpu.VMEM((2, d, D_EXPERT), F32),
            pltpu.VMEM((2, d, D_EXPERT), F32),
            pltpu.VMEM((2, D_EXPERT, d), F32),
            pltpu.SemaphoreType.DMA((2, 3)),
            pltpu.SMEM((1,), jnp.int32),
        ],
    )
    return pl.pallas_call(
        functools.partial(_moe_kernel, n_tok=n_tok),
        grid_spec=grid_spec,
        out_shape=jax.ShapeDtypeStruct((TOP_K * n_tok + 2 * MOE_BLOCK, dw), jnp.uint32),
        compiler_params=_cparams(("arbitrary",), 56),
        name="moe_experts",
    )(block_e, slot_dst, nused, hn, wg, wu, wd)


def _ple_kernel(h_ref, rt_ref, y0_ref, y1_ref, p_ref, wg_ref, bg_ref, wp_ref, gp_ref, gf_ref, o_ref):
    rt = rt_ref[...]
    h2 = h_ref[...] + (rt[:, TOP_K:TOP_K + 1] * _unpack_halves(y0_ref[...])
                       + rt[:, TOP_K + 1:TOP_K + 2] * _unpack_halves(y1_ref[...]))
    hn = _rms(h2, gp_ref[...]).astype(BF16)
    gate = _sigmoid(jnp.dot(hn, wg_ref[...], preferred_element_type=F32) + bg_ref[...])
    pe = jnp.dot(p_ref[...].astype(BF16), wp_ref[...], preferred_element_type=F32)
    h3 = h2 + gate * pe
    o_ref[...] = _rms(h3, gf_ref[...])


def _ple_final(h1, rt, y, p, wg, bg, wp, gp, gf, tm):
    s, d = h1.shape
    nblk = s // tm
    row = lambda i: (i, 0)
    full = lambda i: (0, 0)
    return pl.pallas_call(
        _ple_kernel,
        grid=(nblk,),
        in_specs=[
            pl.BlockSpec((tm, d), row),
            pl.BlockSpec((tm, LANES), row),
            pl.BlockSpec((tm, d // 2), row),
            pl.BlockSpec((tm, d // 2), lambda i: (i + nblk, 0)),
            pl.BlockSpec((tm, PLE_DIM), row),
            pl.BlockSpec(wg.shape, full, pipeline_mode=pl.Buffered(1)),
            pl.BlockSpec((1, d), full),
            pl.BlockSpec(wp.shape, full, pipeline_mode=pl.Buffered(1)),
            pl.BlockSpec((1, d), full),
            pl.BlockSpec((1, d), full),
        ],
        out_specs=pl.BlockSpec((tm, d), row),
        out_shape=jax.ShapeDtypeStruct((s, d), F32),
        compiler_params=_cparams(("parallel",), 56),
        name="ple_final",
    )(h1, rt, y, y, p, wg, bg, wp, gp, gf)


def _slots_kernel(rt_ref, tri_ref, pos_ref, cnt_ref, carry_ref, base_ref):
    ps = pl.program_id(0)
    b = pl.program_id(1)
    rt = rt_ref[...]
    lane = lax.broadcasted_iota(jnp.int32, rt.shape, 1)
    lanef = lane.astype(F32)
    oh = [(lanef == rt[:, kk:kk + 1]).astype(F32) for kk in range(TOP_K)]
    both = oh[0] + oh[1]
    colsum = jnp.sum(both, axis=0, keepdims=True)

    @pl.when((ps == 0) & (b == 0))
    def _():
        cnt_ref[...] = jnp.zeros(cnt_ref.shape, F32)

    @pl.when(ps == 0)
    def _():
        cnt_ref[...] = cnt_ref[...] + colsum

    @pl.when((ps == 1) & (b == 0))
    def _():
        blocks = jnp.floor((cnt_ref[...] + (MOE_BLOCK - 0.5)) / MOE_BLOCK)
        r = lax.broadcasted_iota(jnp.int32, (LANES, LANES), 0)
        c = lax.broadcasted_iota(jnp.int32, (LANES, LANES), 1)
        before = (r < c).astype(F32)
        base_ref[...] = jnp.dot(blocks * MOE_BLOCK, before, preferred_element_type=F32,
                                precision=lax.Precision.HIGHEST)
        carry_ref[...] = jnp.zeros(carry_ref.shape, F32)

    @pl.when(ps == 1)
    def _():
        earlier = jnp.dot(tri_ref[...], both.astype(BF16), preferred_element_type=F32)
        row = earlier + (base_ref[0:1, :] + carry_ref[0:1, :])
        s0 = jnp.sum(oh[0] * row, axis=-1, keepdims=True)
        s1 = jnp.sum(oh[1] * row, axis=-1, keepdims=True)
        pos_ref[...] = jnp.where(lane == 0, s0, jnp.where(lane == 1, s1, 0.0))
        carry_ref[...] = carry_ref[...] + colsum


def _slots(rt, tb):
    s = rt.shape[0]
    tri = (np.arange(tb)[:, None] > np.arange(tb)[None, :]).astype(np.float32)
    return pl.pallas_call(
        _slots_kernel,
        grid=(2, s // tb),
        in_specs=[
            pl.BlockSpec((tb, LANES), lambda ps, b: (b, 0)),
            pl.BlockSpec((tb, tb), lambda ps, b: (0, 0)),
        ],
        out_specs=[
            pl.BlockSpec((tb, LANES), lambda ps, b: (b * ps, 0)),
            pl.BlockSpec((8, LANES), lambda ps, b: (0, 0)),
        ],
        out_shape=[
            jax.ShapeDtypeStruct((s, LANES), F32),
            jax.ShapeDtypeStruct((8, LANES), F32),
        ],
        scratch_shapes=[pltpu.VMEM((8, LANES), F32), pltpu.VMEM((8, LANES), F32)],
        compiler_params=_cparams(("arbitrary", "arbitrary")),
        name="moe_slots",
    )(rt, jnp.asarray(tri, BF16))


def _dispatch(rt, n_tok):
    a = n_tok * TOP_K
    nb = -(-a // MOE_BLOCK) + N_EXPERTS
    pos_f, cnt = _slots(rt, 512)
    pos = pos_f[:, :TOP_K].astype(jnp.int32).reshape(-1)
    counts = cnt[0, :N_EXPERTS].astype(jnp.int32)
    nblk = (counts + MOE_BLOCK - 1) // MOE_BLOCK
    bends = jnp.cumsum(nblk)
    bidx = jnp.arange(nb, dtype=jnp.int32)
    block_e = jnp.clip(jnp.searchsorted(bends, bidx, side="right"), 0, N_EXPERTS - 1).astype(jnp.int32)
    nused = bends[-1].astype(jnp.int32)
    a_idx = jnp.arange(a, dtype=jnp.int32)
    sidx = jnp.arange(nb * MOE_BLOCK, dtype=jnp.int32)
    spare = a + ((sidx // MOE_BLOCK) % 2) * MOE_BLOCK + sidx % MOE_BLOCK
    slot_dst = spare.at[pos].set((a_idx % TOP_K) * n_tok + a_idx // TOP_K)
    return block_e, slot_dst, nused.reshape(1)


def kernel(x, p, positions, attn_norm, w_in, q_norm, w_uq, kv_norm, w_ukv, mla_norm, hg_lb_logits, hg_norm, w_out, ffn_norm, w_router_group, b_router_group, w_router_expert, b_router_expert, w_exp_gate, w_exp_up, w_exp_down, ple_norm, w_ple_gate, b_ple_gate, w_ple_proj, final_norm):
    bsz, s, d = x.shape
    assert bsz == 1 and w_in.shape[0] == 1
    xt = x[0]

    inv_freq = 1.0 / (ROPE_THETA ** (jnp.arange(0, QK_ROPE, 2, dtype=F32) / QK_ROPE))
    ang = positions[0].astype(F32)[:, None] * inv_freq
    cos, sin = jnp.cos(ang), jnp.sin(ang)
    zpad = jnp.zeros((s, LANES - QK_ROPE), F32)
    cc = jnp.concatenate([cos, cos, zpad], axis=1)
    ss = jnp.concatenate([-sin, sin, zpad], axis=1)

    lb = jnp.cumsum(jax.nn.softmax(hg_lb_logits.astype(F32), axis=0), axis=0)[0][None, :]

    wi = w_in[0]
    kr0 = Q_LORA + KV_LORA
    half = QK_ROPE // 2
    w_lat = jnp.concatenate(
        [wi[:, :kr0 + QK_ROPE], wi[:, kr0 + half:kr0 + QK_ROPE], wi[:, kr0:kr0 + half]], axis=1).astype(BF16)
    w_hg = wi[:, kr0 + QK_ROPE:].astype(BF16)
    wq3 = w_uq[0].reshape(Q_LORA, MLA_HEADS, QK_HEAD)
    wq_pad = jnp.concatenate(
        [wq3, wq3[:, :, QK_NOPE + half:], wq3[:, :, QK_NOPE:QK_NOPE + half]], axis=2
    ).reshape(Q_LORA, MLA_HEADS * QK_PAD).astype(BF16)
    wkv3 = w_ukv[0].reshape(KV_LORA, MLA_HEADS, QK_NOPE + V_HEAD)
    wknt = wkv3[:, :, :QK_NOPE].reshape(KV_LORA, MLA_HEADS * QK_NOPE).T.astype(BF16)
    wv = wkv3[:, :, QK_NOPE:].reshape(KV_LORA, D_MLA).astype(BF16)
    wo = w_out[0].astype(BF16)
    wr = jnp.concatenate(
        [w_router_group[0], w_router_expert[0], jnp.zeros((d, LANES - N_GROUPS - N_EXPERTS), F32)], axis=1)
    wr_hi = wr.astype(BF16)
    wr = jnp.concatenate([wr_hi, (wr - wr_hi.astype(F32)).astype(BF16)], axis=1)
    br = jnp.concatenate(
        [b_router_group[0], b_router_expert[0], jnp.zeros((LANES - N_GROUPS - N_EXPERTS,), F32)])[None, :]

    lat = _in_proj(xt, attn_norm, w_lat, F32, 512, D_LAT, "in_proj_latent")
    hg4 = _in_proj(xt, attn_norm, w_hg, BF16, 1024, 1024, "in_proj_hgrn")
    q, kt, v = _mla_up(lat, q_norm, kv_norm, wq_pad, wknt, wv, cc, ss, 512)
    o_mla = _attention(q, kt, v, mla_norm, 512)
    mall, lv = _hgrn_tables()
    o_hg = _hgrn(hg4, lb, hg_norm[0].reshape(1, D_HG), mall, lv)

    h1, hn, rt = _out_route(xt, o_mla, o_hg, wo[:D_MLA], wo[D_MLA:], ffn_norm, wr, br, 512)
    block_e, slot_dst, nused = _dispatch(rt, s)
    y = _moe(block_e, slot_dst, nused, hn, w_exp_gate[0], w_exp_up[0], w_exp_down[0])

    out = _ple_final(h1, rt, y, p[0, 0], w_ple_gate[0].astype(BF16), b_ple_gate,
                     w_ple_proj[0].astype(BF16), ple_norm, final_norm[None, :], 512)
    return out[None]
```

```python
import functools

import jax
import jax.numpy as jnp
import numpy as np
from jax import lax
from jax.experimental import pallas as pl
from jax.experimental.pallas import tpu as pltpu

F32 = jnp.float32
BF16 = jnp.bfloat16

D_MODEL = 2048
PLE_DIM = 256
MLA_HEADS = 8
QK_NOPE = 128
QK_ROPE = 64
QK_HEAD = QK_NOPE + QK_ROPE
QK_PAD = 256
V_HEAD = 128
V_PAD = 2 * V_HEAD
QK_AHEAD = 3
PV_LAG = 2
LAG_LIMIT = 60.0
Q_LORA = 512
KV_LORA = 256
ROPE_THETA = 10000.0
HG_HEADS = 8
HG_DK = 128
HG_DV = 128
HG_CHUNK = 64
D_MLA = MLA_HEADS * V_HEAD
D_HG = HG_HEADS * HG_DV
N_GROUPS = 8
EXPERTS_PER_GROUP = 8
N_EXPERTS = N_GROUPS * EXPERTS_PER_GROUP
TOP_K = 2
D_EXPERT = 512
EPS = 1e-6
LANES = 128
D_LAT = Q_LORA + KV_LORA + 2 * QK_ROPE
NEG_BIG = -1e30
LOG2E = 1.4426950408889634

MOE_BLOCK = 320
ROW_CHUNK = 64
ROW_DMA_PRIORITY = 0
HG_ROWS = 256
HG_LEVELS = (32, 16, 8, 4, 2, 1)
HG_MM_LEVELS = (2, 1)
HG_NH = 4


def _cparams(sem, vmem_mb=None, flags=None):
    kw = dict(dimension_semantics=sem)
    if vmem_mb is not None:
        kw["vmem_limit_bytes"] = vmem_mb * 1024 * 1024
    if flags is not None:
        kw["flags"] = flags
    return pltpu.CompilerParams(**kw)


def _rms(x, g):
    ms = jnp.mean(x * x, axis=-1, keepdims=True)
    return x * lax.rsqrt(ms + EPS) * g


def _sigmoid(x):
    return 1.0 / (1.0 + jnp.exp(-x))


def _pack_halves(x):
    n = x.shape[1] // 2
    bits = lax.bitcast_convert_type(x.astype(BF16).astype(F32), jnp.uint32)
    return bits[:, :n] | (bits[:, n:] >> 16)


def _unpack_halves(w):
    hi = lax.bitcast_convert_type(w & jnp.uint32(0xFFFF0000), F32)
    lo = lax.bitcast_convert_type(w << 16, F32)
    return jnp.concatenate([hi, lo], axis=1)


def _in_proj_kernel(x_ref, g_ref, w_ref, o_ref, xn_ref):
    @pl.when(pl.program_id(1) == 0)
    def _():
        xn_ref[...] = _rms(x_ref[...], g_ref[...]).astype(BF16)

    o_ref[...] = jnp.dot(xn_ref[...], w_ref[...], preferred_element_type=F32).astype(o_ref.dtype)


def _in_proj(x, gain, w, out_dtype, tm, tn, name):
    s, d = x.shape
    n = w.shape[1]
    return pl.pallas_call(
        _in_proj_kernel,
        grid=(s // tm, n // tn),
        in_specs=[
            pl.BlockSpec((tm, d), lambda i, j: (i, 0)),
            pl.BlockSpec((1, d), lambda i, j: (0, 0)),
            pl.BlockSpec((d, tn), lambda i, j: (0, j)),
        ],
        out_specs=pl.BlockSpec((tm, tn), lambda i, j: (i, j)),
        out_shape=jax.ShapeDtypeStruct((s, n), out_dtype),
        scratch_shapes=[pltpu.VMEM((tm, d), BF16)],
        compiler_params=_cparams(("parallel", "arbitrary"), 48),
        name=name,
    )(x, gain, w)


def _mla_up_kernel(lat_ref, qn_ref, kvn_ref, wq_ref, wknt_ref, wv_ref, cc_ref, ss_ref,
                   q_ref, kt_ref, v_ref):
    lat = lat_ref[...]
    cq = _rms(lat[:, :Q_LORA], qn_ref[...]).astype(BF16)
    ckv = _rms(lat[:, Q_LORA:Q_LORA + KV_LORA], kvn_ref[...]).astype(BF16)
    kp = lat[:, Q_LORA + KV_LORA:]
    cc = cc_ref[...]
    ss = ss_ref[...]
    scale = QK_HEAD ** -0.5 * LOG2E

    q = jnp.dot(cq, wq_ref[...], preferred_element_type=F32)
    for h in range(MLA_HEADS):
        base = h * QK_PAD
        q_ref[:, base:base + QK_NOPE] = (q[:, base:base + QK_NOPE] * scale).astype(BF16)
        r = q[:, base + QK_NOPE:base + QK_PAD]
        r = (r * cc + pltpu.roll(r, QK_ROPE, 1) * ss) * scale
        q_ref[:, base + QK_NOPE:base + QK_PAD] = r.astype(BF16)

    kpe_t = (kp * cc + pltpu.roll(kp, QK_ROPE, 1) * ss).T.astype(BF16)
    kn_t = lax.dot_general(wknt_ref[...], ckv, (((1,), (1,)), ((), ())), preferred_element_type=F32).astype(BF16)
    for h in range(MLA_HEADS):
        base = h * QK_PAD
        kt_ref[base:base + QK_NOPE, :] = kn_t[h * QK_NOPE:(h + 1) * QK_NOPE]
        kt_ref[base + QK_NOPE:base + QK_PAD, :] = kpe_t
    v = jnp.dot(ckv, wv_ref[...], preferred_element_type=F32).astype(BF16)
    ones = jnp.ones((v.shape[0], V_PAD - V_HEAD), BF16)
    for h in range(MLA_HEADS):
        v_ref[:, h * V_PAD:h * V_PAD + V_HEAD] = v[:, h * V_HEAD:(h + 1) * V_HEAD]
        v_ref[:, h * V_PAD + V_HEAD:(h + 1) * V_PAD] = ones


def _mla_up(lat, qn, kvn, wq, wknt, wv, cc, ss, tm):
    s = lat.shape[0]
    row = lambda i: (i, 0)
    full = lambda i: (0, 0)
    return pl.pallas_call(
        _mla_up_kernel,
        grid=(s // tm,),
        in_specs=[
            pl.BlockSpec((tm, D_LAT), row),
            pl.BlockSpec((1, Q_LORA), full),
            pl.BlockSpec((1, KV_LORA), full),
            pl.BlockSpec(wq.shape, full),
            pl.BlockSpec(wknt.shape, full),
            pl.BlockSpec(wv.shape, full),
            pl.BlockSpec((tm, LANES), row),
            pl.BlockSpec((tm, LANES), row),
        ],
        out_specs=[
            pl.BlockSpec((tm, MLA_HEADS * QK_PAD), row),
            pl.BlockSpec((MLA_HEADS * QK_PAD, tm), lambda i: (0, i)),
            pl.BlockSpec((tm, MLA_HEADS * V_PAD), row),
        ],
        out_shape=[
            jax.ShapeDtypeStruct((s, MLA_HEADS * QK_PAD), BF16),
            jax.ShapeDtypeStruct((MLA_HEADS * QK_PAD, s), BF16),
            jax.ShapeDtypeStruct((s, MLA_HEADS * V_PAD), BF16),
        ],
        compiler_params=_cparams(("parallel",), 48),
        name="mla_up",
    )(lat, qn, kvn, wq, wknt, wv, cc, ss)


def _attn_kernel(it_ref, jt_ref, q_ref, kt_ref, v_ref, g_ref, o_ref, acc_ref, m_ref, redo_ref, *, tb):
    t = pl.program_id(0)
    i = it_ref[t]
    j = jt_ref[t]
    src = t % 2
    dst = 1 - src
    ntile = tb // LANES

    @pl.when(j == 0)
    def _():
        m_ref[...] = jnp.full(m_ref.shape, NEG_BIG, F32)
        acc_ref[src] = jnp.zeros(acc_ref.shape[1:], F32)
        redo_ref[0] = 0

    def scores(h):
        return jnp.dot(q_ref[:, h * QK_PAD:(h + 1) * QK_PAD], kt_ref[h * QK_PAD:(h + 1) * QK_PAD, :],
                       preferred_element_type=F32)

    def values(h):
        return v_ref[:, h * V_PAD:(h + 1) * V_PAD]

    def key_minus_query():
        return (lax.broadcasted_iota(jnp.int32, (tb, tb), 1) - lax.broadcasted_iota(jnp.int32, (tb, tb), 0))

    def lagged(masked):
        if masked:
            keep = key_minus_query() <= 0
        ahead = [scores(h) for h in range(min(QK_AHEAD, MLA_HEADS))]
        pending = []
        worst = None

        def accumulate(h, p):
            acc_ref[dst, h] = acc_ref[src, h] + jnp.dot(p, values(h), preferred_element_type=F32)

        for h in range(MLA_HEADS):
            s = ahead.pop(0)
            if h + QK_AHEAD < MLA_HEADS:
                ahead.append(scores(h + QK_AHEAD))
            d = s - jnp.concatenate([m_ref[h]] * ntile, axis=1)
            if masked:
                d = jnp.where(keep, d, NEG_BIG)
            top = jnp.max(d.reshape(tb // 8, 8, tb), axis=0)
            worst = top if worst is None else jnp.maximum(worst, top)
            pending.append((h, jnp.exp2(d).astype(BF16)))
            if len(pending) > PV_LAG:
                accumulate(*pending.pop(0))
        for item in pending:
            accumulate(*item)
        redo_ref[0] = (jnp.max(worst) > LAG_LIMIT).astype(jnp.int32)

    def exact():
        keep = key_minus_query() <= jnp.where(j == i, 0, tb)
        ahead = [scores(h) for h in range(min(QK_AHEAD, MLA_HEADS))]
        for h in range(MLA_HEADS):
            s = jnp.where(keep, ahead.pop(0), NEG_BIG)
            if h + QK_AHEAD < MLA_HEADS:
                ahead.append(scores(h + QK_AHEAD))
            m_prev = m_ref[h]
            m_new = jnp.maximum(m_prev, jnp.max(s, axis=1, keepdims=True))
            alpha = jnp.exp2(m_prev - m_new)
            p = jnp.exp2(s - jnp.concatenate([m_new] * ntile, axis=1)).astype(BF16)
            m_ref[h] = m_new
            acc_ref[dst, h] = (acc_ref[src, h] * jnp.concatenate([alpha] * (V_PAD // LANES), axis=1)
                               + jnp.dot(p, values(h), preferred_element_type=F32))

    @pl.when((j > 0) & (j < i))
    def _():
        lagged(False)

    @pl.when((j > 0) & (j == i))
    def _():
        lagged(True)

    @pl.when((j == 0) | (redo_ref[0] != 0))
    def _():
        exact()

    @pl.when(j == i)
    def _():
        outs = [acc_ref[dst, h, :, :V_HEAD] / acc_ref[dst, h, :, V_HEAD:] for h in range(MLA_HEADS)]
        o_ref[...] = _rms(jnp.concatenate(outs, axis=1), g_ref[...]).astype(o_ref.dtype)


def _attention(q, kt, v, gain, tb):
    s = q.shape[0]
    nq = s // tb
    it = np.concatenate([np.full(i + 1, i, np.int32) for i in range(nq)])
    jt = np.concatenate([np.arange(i + 1, dtype=np.int32) for i in range(nq)])
    grid_spec = pltpu.PrefetchScalarGridSpec(
        num_scalar_prefetch=2,
        grid=(it.shape[0],),
        in_specs=[
            pl.BlockSpec((tb, MLA_HEADS * QK_PAD), lambda t, it, jt: (it[t], 0)),
            pl.BlockSpec((MLA_HEADS * QK_PAD, tb), lambda t, it, jt: (0, jt[t])),
            pl.BlockSpec((tb, MLA_HEADS * V_PAD), lambda t, it, jt: (jt[t], 0)),
            pl.BlockSpec((1, D_MLA), lambda t, it, jt: (0, 0)),
        ],
        out_specs=pl.BlockSpec((tb, D_MLA), lambda t, it, jt: (it[t], 0)),
        scratch_shapes=[
            pltpu.VMEM((2, MLA_HEADS, tb, V_PAD), F32),
            pltpu.VMEM((MLA_HEADS, tb, LANES), F32),
            pltpu.SMEM((1,), jnp.int32),
        ],
    )
    return pl.pallas_call(
        functools.partial(_attn_kernel, tb=tb),
        grid_spec=grid_spec,
        out_shape=jax.ShapeDtypeStruct((s, D_MLA), BF16),
        compiler_params=_cparams(("arbitrary",), 48),
        name="mla_attention",
    )(jnp.asarray(it), jnp.asarray(jt), q, kt, v, gain)


def _hgrn_tables():
    n = HG_ROWS
    r = np.arange(n)
    c = np.arange(n)
    same = (r[:, None] // HG_CHUNK) == (c[None, :] // HG_CHUNK)

    def rows_upto(idx):
        return (same & (c[None, :] <= idx[:, None])).astype(np.float32)

    blocks = [rows_upto(r)]
    for m in HG_MM_LEVELS:
        blocks.append(rows_upto((r // (2 * m)) * (2 * m) + m))
    mall = np.concatenate(blocks, axis=0)

    x = r[:, None] ^ c[None, :]
    lv = np.full((n, n), -1, np.int32)
    for li, m in enumerate(HG_LEVELS):
        lv = np.where(same & (r[:, None] > c[None, :]) & (x >= m) & (x < 2 * m), li, lv)
    lv = np.where(r[:, None] == c[None, :], len(HG_LEVELS), lv)
    return jnp.asarray(mall, BF16), jnp.asarray(lv, jnp.int32)


def _hgrn_kernel(q_ref, f_ref, i_ref, g_ref, lb_ref, gn_ref, mall_ref, lv_ref, o_ref, st_ref):
    t = pl.program_id(1)
    n = HG_ROWS

    @pl.when(t == 0)
    def _():
        st_ref[...] = jnp.zeros(st_ref.shape, F32)

    rowid = lax.broadcasted_iota(jnp.int32, (n, 1), 0)
    lv = lv_ref[...]

    def head(hh):
        cols = slice(hh * HG_DK, (hh + 1) * HG_DK)
        q_in = q_ref[:, cols].astype(F32)
        qs = q_in * _sigmoid(q_in)
        lb = lb_ref[:, cols]
        f = lb + (1.0 - lb) * _sigmoid(f_ref[:, cols].astype(F32))
        kk = 1.0 - f
        logf = jnp.log(f)
        iv = i_ref[:, cols]

        l1 = logf.astype(BF16)
        l2 = (logf - l1.astype(F32)).astype(BF16)
        parts = jnp.dot(mall_ref[...], jnp.concatenate([l1, l2], axis=1), preferred_element_type=F32)
        yield
        bc = parts[:, :HG_DK] + parts[:, HG_DK:]
        b = bc[:n]

        def anchor(period, row):
            b3 = b.reshape(n // period, period, HG_DK)
            return jnp.broadcast_to(b3[:, row:row + 1, :], b3.shape).reshape(n, HG_DK)

        a = jnp.zeros((n, n), F32)
        for li, m in enumerate(HG_LEVELS):
            if m in HG_MM_LEVELS:
                k = 1 + HG_MM_LEVELS.index(m)
                c = bc[k * n:(k + 1) * n]
            else:
                c = anchor(2 * m, m)
            e = jnp.exp(-jnp.abs(b - c))
            upper = (rowid & (2 * m - 1)) >= m
            x = (jnp.where(upper, qs, kk) * e).astype(BF16)
            p = lax.dot_general(x, x, (((1,), (1,)), ((), ())), preferred_element_type=F32)
            yield
            a = jnp.where(lv == li, p, a)
        a = jnp.where(lv == len(HG_LEVELS), jnp.sum(qs * kk, axis=-1, keepdims=True), a)
        o = jnp.dot(a.astype(BF16), iv, preferred_element_type=F32)
        yield

        blast = anchor(HG_CHUNK, HG_CHUNK - 1)
        qd = (qs * jnp.exp(b)).astype(BF16)
        kd = kk * jnp.exp(blast - b)
        ivt = iv.astype(F32).T.astype(BF16)
        st = st_ref[hh]
        outs = []
        for ci in range(n // HG_CHUNK):
            lo = ci * HG_CHUNK
            inter = lax.dot_general(qd[lo:lo + HG_CHUNK], st.astype(BF16), (((1,), (1,)), ((), ())),
                                    preferred_element_type=F32)
            kd_c = jnp.where((rowid >= lo) & (rowid < lo + HG_CHUNK), kd, 0.0).astype(BF16)
            upd = jnp.dot(ivt, kd_c, preferred_element_type=F32)
            yield
            outs.append(o[lo:lo + HG_CHUNK] + inter)
            st = jnp.exp(blast[lo:lo + 1]) * st + upd
        st_ref[hh] = st
        o = jnp.concatenate(outs, axis=0)

        g_in = g_ref[:, cols].astype(F32)
        o_ref[:, cols] = (_rms(o, gn_ref[:, cols]) * (g_in * _sigmoid(g_in))).astype(o_ref.dtype)

    progs = [head(hh) for hh in range(HG_NH)]
    while progs:
        progs = [pr for pr in progs if next(pr, "done") != "done"]


def _hgrn(hg4, lb, gn, mall, lv):
    s = hg4.shape[0]
    groups = HG_HEADS // HG_NH
    width = HG_NH * HG_DK

    def col(c):
        return lambda hh, t: (t, c * groups + hh)

    head = lambda hh, t: (0, hh)
    const = lambda hh, t: (0, 0)
    return pl.pallas_call(
        _hgrn_kernel,
        grid=(groups, s // HG_ROWS),
        in_specs=[
            pl.BlockSpec((HG_ROWS, width), col(0)),
            pl.BlockSpec((HG_ROWS, width), col(1)),
            pl.BlockSpec((HG_ROWS, width), col(2)),
            pl.BlockSpec((HG_ROWS, width), col(3)),
            pl.BlockSpec((1, width), head),
            pl.BlockSpec((1, width), head),
            pl.BlockSpec(mall.shape, const),
            pl.BlockSpec(lv.shape, const),
        ],
        out_specs=pl.BlockSpec((HG_ROWS, width), lambda hh, t: (t, hh)),
        out_shape=jax.ShapeDtypeStruct((s, D_HG), BF16),
        scratch_shapes=[pltpu.VMEM((HG_NH, HG_DV, HG_DK), F32)],
        compiler_params=_cparams(("parallel", "arbitrary"), 32),
        name="hgrn2",
    )(hg4, hg4, hg4, hg4, lb, gn, mall, lv)


def _out_route_kernel(x_ref, om_ref, oh_ref, wa_ref, wb_ref, g_ref, wr_ref, br_ref,
                      h_ref, hn_ref, rt_ref):
    h1 = (x_ref[...]
          + jnp.dot(om_ref[...], wa_ref[...], preferred_element_type=F32)
          + jnp.dot(oh_ref[...], wb_ref[...], preferred_element_type=F32))
    h_ref[...] = h1
    hn = _rms(h1, g_ref[...])
    hn_ref[...] = _pack_halves(hn)
    hn_hi = hn.astype(BF16)
    hn_lo = (hn - hn_hi.astype(F32)).astype(BF16)
    hh = jnp.dot(hn_hi, wr_ref[...], preferred_element_type=F32)
    lh = jnp.dot(hn_lo, wr_ref[:, :LANES], preferred_element_type=F32)
    logits = hh[:, :LANES] + (hh[:, LANES:] + lh) + br_ref[...]

    lane = lax.broadcasted_iota(jnp.int32, logits.shape, 1)
    lanef = lane.astype(F32)
    ninf = -jnp.inf
    big = float(LANES)

    is_g = lane < N_GROUPS
    gl = jnp.where(is_g, logits, ninf)
    gmax = jnp.max(gl, axis=-1, keepdims=True)
    gsum = jnp.sum(jnp.where(is_g, jnp.exp(gl - gmax), 0.0), axis=-1, keepdims=True)
    g_w = 1.0 / gsum
    g_idx = jnp.min(jnp.where(gl == gmax, lanef, big), axis=-1, keepdims=True)

    e_lane = lane - N_GROUPS
    in_grp = (e_lane >= 0) & (e_lane < N_EXPERTS) & ((e_lane >> 3).astype(F32) == g_idx)
    el = jnp.where(in_grp, logits, ninf)
    emax = jnp.max(el, axis=-1, keepdims=True)
    esum = jnp.sum(jnp.where(in_grp, jnp.exp(el - emax), 0.0), axis=-1, keepdims=True)
    i1 = jnp.min(jnp.where(el == emax, lanef, big), axis=-1, keepdims=True)
    el2 = jnp.where(lanef == i1, ninf, el)
    emax2 = jnp.max(el2, axis=-1, keepdims=True)
    i2 = jnp.min(jnp.where(el2 == emax2, lanef, big), axis=-1, keepdims=True)
    p1 = 1.0 / esum
    p2 = jnp.exp(emax2 - emax) / esum
    w1 = g_w * p1 / (p1 + p2)
    w2 = g_w * p2 / (p1 + p2)

    rt = jnp.where(lane == 0, i1 - N_GROUPS,
                   jnp.where(lane == 1, i2 - N_GROUPS,
                             jnp.where(lane == 2, w1, jnp.where(lane == 3, w2, 0.0))))
    rt_ref[...] = rt


def _out_route(x, o_mla, o_hg, wa, wb, gain, wr, br, tm):
    s, d = x.shape
    row = lambda i: (i, 0)
    full = lambda i: (0, 0)
    return pl.pallas_call(
        _out_route_kernel,
        grid=(s // tm,),
        in_specs=[
            pl.BlockSpec((tm, d), row),
            pl.BlockSpec((tm, D_MLA), row),
            pl.BlockSpec((tm, D_HG), row),
            pl.BlockSpec(wa.shape, full, pipeline_mode=pl.Buffered(1)),
            pl.BlockSpec(wb.shape, full, pipeline_mode=pl.Buffered(1)),
            pl.BlockSpec((1, d), full),
            pl.BlockSpec(wr.shape, full, pipeline_mode=pl.Buffered(1)),
            pl.BlockSpec((1, LANES), full),
        ],
        out_specs=[
            pl.BlockSpec((tm, d), row),
            pl.BlockSpec((tm, d // 2), row),
            pl.BlockSpec((tm, LANES), row),
        ],
        out_shape=[
            jax.ShapeDtypeStruct((s, d), F32),
            jax.ShapeDtypeStruct((s, d // 2), jnp.uint32),
            jax.ShapeDtypeStruct((s, LANES), F32),
        ],
        compiler_params=_cparams(("parallel",), 56),
        name="out_proj_route",
    )(x, o_mla, o_hg, wa, wb, gain, wr, br)


def _moe_kernel(be_ref, nv_ref, dst_ref, nu_ref, hn_hbm, wg_hbm, wu_hbm, wd_hbm, y_hbm,
                xbuf, xsem, obuf, osem, wgb, wub, wdb, wsem, wslot, *, n_tok):
    i = pl.program_id(0)
    nused = nu_ref[0]
    e = be_ref[i]
    active = i < nused
    first = (i == 0) | (e != be_ref[jnp.maximum(i - 1, 0)])

    def chunks(blk, fn):
        nv = nv_ref[blk]
        for lo in range(0, MOE_BLOCK, ROW_CHUNK):
            pl.when(lo < nv)(functools.partial(fn, lo))

    def gather_start(blk, sl):
        def issue(lo):
            for r in range(lo, lo + ROW_CHUNK):
                tok = dst_ref[blk * MOE_BLOCK + r] & (n_tok - 1)
                pltpu.make_async_copy(hn_hbm.at[pl.ds(tok, 1), :], xbuf.at[sl, pl.ds(r, 1), :],
                                      xsem.at[sl]).start(priority=ROW_DMA_PRIORITY)
        chunks(blk, issue)

    def gather_wait(blk, sl):
        chunks(blk, lambda lo: pltpu.make_async_copy(
            hn_hbm.at[pl.ds(0, ROW_CHUNK), :], xbuf.at[sl, pl.ds(lo, ROW_CHUNK), :], xsem.at[sl]).wait())

    def scatter_start(blk, sl):
        def issue(lo):
            for r in range(lo, lo + ROW_CHUNK):
                dst = dst_ref[blk * MOE_BLOCK + r]
                pltpu.make_async_copy(obuf.at[sl, pl.ds(r, 1), :], y_hbm.at[pl.ds(dst, 1), :],
                                      osem.at[sl]).start(priority=ROW_DMA_PRIORITY)
        chunks(blk, issue)

    def scatter_wait(blk, sl):
        chunks(blk, lambda lo: pltpu.make_async_copy(
            obuf.at[sl, pl.ds(lo, ROW_CHUNK), :], y_hbm.at[pl.ds(0, ROW_CHUNK), :], osem.at[sl]).wait())

    def spare_copy(sl):
        return pltpu.make_async_copy(obuf.at[sl], y_hbm.at[pl.ds(TOP_K * n_tok + sl * MOE_BLOCK, MOE_BLOCK), :],
                                     osem.at[sl])

    def weight_copies(ex, sl):
        return (pltpu.make_async_copy(wg_hbm.at[ex], wgb.at[sl], wsem.at[sl, 0]),
                pltpu.make_async_copy(wu_hbm.at[ex], wub.at[sl], wsem.at[sl, 1]),
                pltpu.make_async_copy(wd_hbm.at[ex], wdb.at[sl], wsem.at[sl, 2]))

    @pl.when((i == 0) & active)
    def _():
        wslot[0] = 1
        for cp in weight_copies(e, 0):
            cp.start(priority=1 - ROW_DMA_PRIORITY)
        xbuf[...] = jnp.zeros(xbuf.shape, jnp.uint32)
        gather_start(0, 0)
        obuf[...] = jnp.zeros(obuf.shape, jnp.uint32)
        for sl in (0, 1):
            spare_copy(sl).start()
        for sl in (0, 1):
            spare_copy(sl).wait()

    @pl.when(active & first)
    def _():
        sl = 1 - wslot[0]
        wslot[0] = sl
        nxt = lax.while_loop(lambda j: (j < nused) & (be_ref[jnp.minimum(j, nused - 1)] == e),
                             lambda j: j + 1, i + 1)

        @pl.when(nxt < nused)
        def _():
            for cp in weight_copies(be_ref[nxt], 1 - sl):
                cp.start(priority=1 - ROW_DMA_PRIORITY)

        for cp in weight_copies(e, sl):
            cp.wait()

    def block(xs):
        sl = wslot[0]

        @pl.when(i >= 2)
        def _():
            scatter_wait(i - 2, xs)

        gather_wait(i, xs)

        @pl.when(i + 1 < nused)
        def _():
            gather_start(i + 1, 1 - xs)

        x = _unpack_halves(xbuf[xs]).astype(BF16)
        g = jnp.dot(x, wgb[sl].astype(BF16), preferred_element_type=F32)
        u = jnp.dot(x, wub[sl].astype(BF16), preferred_element_type=F32)
        hmid = (g * _sigmoid(g) * u).astype(BF16)
        obuf[xs] = _pack_halves(jnp.dot(hmid, wdb[sl].astype(BF16), preferred_element_type=F32))
        scatter_start(i, xs)

        @pl.when(i == nused - 1)
        def _():
            @pl.when(i >= 1)
            def _():
                scatter_wait(i - 1, 1 - xs)

            scatter_wait(i, xs)

    for s_ in (0, 1):
        pl.when(active & (i % 2 == s_))(functools.partial(block, s_))


def _moe(block_e, nvalid, slot_dst, nused, hn, wg, wu, wd):
    nb = block_e.shape[0]
    n_tok, dw = hn.shape
    d = 2 * dw
    assert n_tok & (n_tok - 1) == 0
    grid_spec = pltpu.PrefetchScalarGridSpec(
        num_scalar_prefetch=4,
        grid=(nb,),
        in_specs=[pl.BlockSpec(memory_space=pl.ANY)] * 4,
        out_specs=pl.BlockSpec(memory_space=pl.ANY),
        scratch_shapes=[
            pltpu.VMEM((2, MOE_BLOCK, dw), jnp.uint32),
            pltpu.SemaphoreType.DMA((2,)),
            pltpu.VMEM((2, MOE_BLOCK, dw), jnp.uint32),
            pltpu.SemaphoreType.DMA((2,)),
            pltpu.VMEM((2, d, D_EXPERT), F32),
            pltpu.VMEM((2, d, D_EXPERT), F32),
            pltpu.VMEM((2, D_EXPERT, d), F32),
            pltpu.SemaphoreType.DMA((2, 3)),
            pltpu.SMEM((1,), jnp.int32),
        ],
    )
    return pl.pallas_call(
        functools.partial(_moe_kernel, n_tok=n_tok),
        grid_spec=grid_spec,
        out_shape=jax.ShapeDtypeStruct((TOP_K * n_tok + 2 * MOE_BLOCK, dw), jnp.uint32),
        compiler_params=_cparams(("arbitrary",), 56),
        name="moe_experts",
    )(block_e, nvalid, slot_dst, nused, hn, wg, wu, wd)


def _ple_kernel(h_ref, rt_ref, y0_ref, y1_ref, p_ref, wg_ref, bg_ref, wp_ref, gp_ref, gf_ref, o_ref):
    rt = rt_ref[...]
    h2 = h_ref[...] + (rt[:, TOP_K:TOP_K + 1] * _unpack_halves(y0_ref[...])
                       + rt[:, TOP_K + 1:TOP_K + 2] * _unpack_halves(y1_ref[...]))
    hn = _rms(h2, gp_ref[...]).astype(BF16)
    gate = _sigmoid(jnp.dot(hn, wg_ref[...], preferred_element_type=F32) + bg_ref[...])
    pe = jnp.dot(p_ref[...].astype(BF16), wp_ref[...], preferred_element_type=F32)
    h3 = h2 + gate * pe
    o_ref[...] = _rms(h3, gf_ref[...])


def _ple_final(h1, rt, y, p, wg, bg, wp, gp, gf, tm):
    s, d = h1.shape
    nblk = s // tm
    row = lambda i: (i, 0)
    full = lambda i: (0, 0)
    return pl.pallas_call(
        _ple_kernel,
        grid=(nblk,),
        in_specs=[
            pl.BlockSpec((tm, d), row),
            pl.BlockSpec((tm, LANES), row),
            pl.BlockSpec((tm, d // 2), row),
            pl.BlockSpec((tm, d // 2), lambda i: (i + nblk, 0)),
            pl.BlockSpec((tm, PLE_DIM), row),
            pl.BlockSpec(wg.shape, full, pipeline_mode=pl.Buffered(1)),
            pl.BlockSpec((1, d), full),
            pl.BlockSpec(wp.shape, full, pipeline_mode=pl.Buffered(1)),
            pl.BlockSpec((1, d), full),
            pl.BlockSpec((1, d), full),
        ],
        out_specs=pl.BlockSpec((tm, d), row),
        out_shape=jax.ShapeDtypeStruct((s, d), F32),
        compiler_params=_cparams(("parallel",), 56),
        name="ple_final",
    )(h1, rt, y, y, p, wg, bg, wp, gp, gf)


def _slots_kernel(rt_ref, tri_ref, pos_ref, cnt_ref, carry_ref, base_ref):
    ps = pl.program_id(0)
    b = pl.program_id(1)
    rt = rt_ref[...]
    lane = lax.broadcasted_iota(jnp.int32, rt.shape, 1)
    lanef = lane.astype(F32)
    oh = [(lanef == rt[:, kk:kk + 1]).astype(F32) for kk in range(TOP_K)]
    both = oh[0] + oh[1]
    colsum = jnp.sum(both, axis=0, keepdims=True)

    @pl.when((ps == 0) & (b == 0))
    def _():
        cnt_ref[...] = jnp.zeros(cnt_ref.shape, F32)

    @pl.when(ps == 0)
    def _():
        cnt_ref[...] = cnt_ref[...] + colsum

    @pl.when((ps == 1) & (b == 0))
    def _():
        blocks = jnp.floor((cnt_ref[...] + (MOE_BLOCK - 0.5)) / MOE_BLOCK)
        r = lax.broadcasted_iota(jnp.int32, (LANES, LANES), 0)
        c = lax.broadcasted_iota(jnp.int32, (LANES, LANES), 1)
        before = (r < c).astype(F32)
        base_ref[...] = jnp.dot(blocks * MOE_BLOCK, before, preferred_element_type=F32,
                                precision=lax.Precision.HIGHEST)
        carry_ref[...] = jnp.zeros(carry_ref.shape, F32)

    @pl.when(ps == 1)
    def _():
        earlier = jnp.dot(tri_ref[...], both.astype(BF16), preferred_element_type=F32)
        row = earlier + (base_ref[0:1, :] + carry_ref[0:1, :])
        s0 = jnp.sum(oh[0] * row, axis=-1, keepdims=True)
        s1 = jnp.sum(oh[1] * row, axis=-1, keepdims=True)
        pos_ref[...] = jnp.where(lane == 0, s0, jnp.where(lane == 1, s1, 0.0))
        carry_ref[...] = carry_ref[...] + colsum


def _slots(rt, tb):
    s = rt.shape[0]
    tri = (np.arange(tb)[:, None] > np.arange(tb)[None, :]).astype(np.float32)
    return pl.pallas_call(
        _slots_kernel,
        grid=(2, s // tb),
        in_specs=[
            pl.BlockSpec((tb, LANES), lambda ps, b: (b, 0)),
            pl.BlockSpec((tb, tb), lambda ps, b: (0, 0)),
        ],
        out_specs=[
            pl.BlockSpec((tb, LANES), lambda ps, b: (b * ps, 0)),
            pl.BlockSpec((8, LANES), lambda ps, b: (0, 0)),
        ],
        out_shape=[
            jax.ShapeDtypeStruct((s, LANES), F32),
            jax.ShapeDtypeStruct((8, LANES), F32),
        ],
        scratch_shapes=[pltpu.VMEM((8, LANES), F32), pltpu.VMEM((8, LANES), F32)],
        compiler_params=_cparams(("arbitrary", "arbitrary")),
        name="moe_slots",
    )(rt, jnp.asarray(tri, BF16))


def _dispatch(rt, n_tok):
    a = n_tok * TOP_K
    nb = -(-a // MOE_BLOCK) + N_EXPERTS
    pos_f, cnt = _slots(rt, 512)
    pos = pos_f[:, :TOP_K].astype(jnp.int32).reshape(-1)
    counts = cnt[0, :N_EXPERTS].astype(jnp.int32)
    nblk = (counts + MOE_BLOCK - 1) // MOE_BLOCK
    bends = jnp.cumsum(nblk)
    bidx = jnp.arange(nb, dtype=jnp.int32)
    block_e = jnp.minimum(jnp.sum((bends[None, :] <= bidx[:, None]).astype(jnp.int32), axis=1), N_EXPERTS - 1)
    nused = bends[-1].astype(jnp.int32)
    nvalid = jnp.clip(counts[block_e] - (bidx - (bends - nblk)[block_e]) * MOE_BLOCK, 0, MOE_BLOCK)
    nvalid = jnp.where(bidx < nused, nvalid, 0).astype(jnp.int32)
    a_idx = jnp.arange(a, dtype=jnp.int32)
    sidx = jnp.arange(nb * MOE_BLOCK, dtype=jnp.int32)
    spare = a + ((sidx // MOE_BLOCK) % 2) * MOE_BLOCK + sidx % MOE_BLOCK
    slot_dst = spare.at[pos].set((a_idx % TOP_K) * n_tok + a_idx // TOP_K)
    return block_e, nvalid, slot_dst, nused.reshape(1)


def kernel(x, p, positions, attn_norm, w_in, q_norm, w_uq, kv_norm, w_ukv, mla_norm, hg_lb_logits, hg_norm, w_out, ffn_norm, w_router_group, b_router_group, w_router_expert, b_router_expert, w_exp_gate, w_exp_up, w_exp_down, ple_norm, w_ple_gate, b_ple_gate, w_ple_proj, final_norm):
    bsz, s, d = x.shape
    assert bsz == 1 and w_in.shape[0] == 1
    xt = x[0]

    inv_freq = 1.0 / (ROPE_THETA ** (jnp.arange(0, QK_ROPE, 2, dtype=F32) / QK_ROPE))
    ang = positions[0].astype(F32)[:, None] * inv_freq
    cos, sin = jnp.cos(ang), jnp.sin(ang)
    zpad = jnp.zeros((s, LANES - QK_ROPE), F32)
    cc = jnp.concatenate([cos, cos, zpad], axis=1)
    ss = jnp.concatenate([-sin, sin, zpad], axis=1)

    lb = jnp.cumsum(jax.nn.softmax(hg_lb_logits.astype(F32), axis=0), axis=0)[0][None, :]

    wi = w_in[0]
    kr0 = Q_LORA + KV_LORA
    half = QK_ROPE // 2
    w_lat = jnp.concatenate(
        [wi[:, :kr0 + QK_ROPE], wi[:, kr0 + half:kr0 + QK_ROPE], wi[:, kr0:kr0 + half]], axis=1).astype(BF16)
    w_hg = wi[:, kr0 + QK_ROPE:].astype(BF16)
    wq3 = w_uq[0].reshape(Q_LORA, MLA_HEADS, QK_HEAD)
    wq_pad = jnp.concatenate(
        [wq3, wq3[:, :, QK_NOPE + half:], wq3[:, :, QK_NOPE:QK_NOPE + half]], axis=2
    ).reshape(Q_LORA, MLA_HEADS * QK_PAD).astype(BF16)
    wkv3 = w_ukv[0].reshape(KV_LORA, MLA_HEADS, QK_NOPE + V_HEAD)
    wknt = wkv3[:, :, :QK_NOPE].reshape(KV_LORA, MLA_HEADS * QK_NOPE).T.astype(BF16)
    wv = wkv3[:, :, QK_NOPE:].reshape(KV_LORA, D_MLA).astype(BF16)
    wo = w_out[0].astype(BF16)
    wr = jnp.concatenate(
        [w_router_group[0], w_router_expert[0], jnp.zeros((d, LANES - N_GROUPS - N_EXPERTS), F32)], axis=1)
    wr_hi = wr.astype(BF16)
    wr = jnp.concatenate([wr_hi, (wr - wr_hi.astype(F32)).astype(BF16)], axis=1)
    br = jnp.concatenate(
        [b_router_group[0], b_router_expert[0], jnp.zeros((LANES - N_GROUPS - N_EXPERTS,), F32)])[None, :]

    lat = _in_proj(xt, attn_norm, w_lat, F32, 512, D_LAT, "in_proj_latent")
    hg4 = _in_proj(xt, attn_norm, w_hg, BF16, 1024, 1024, "in_proj_hgrn")
    q, kt, v = _mla_up(lat, q_norm, kv_norm, wq_pad, wknt, wv, cc, ss, 512)
    o_mla = _attention(q, kt, v, mla_norm, 512)
    mall, lv = _hgrn_tables()
    o_hg = _hgrn(hg4, lb, hg_norm[0].reshape(1, D_HG), mall, lv)

    h1, hn, rt = _out_route(xt, o_mla, o_hg, wo[:D_MLA], wo[D_MLA:], ffn_norm, wr, br, 512)
    block_e, nvalid, slot_dst, nused = _dispatch(rt, s)
    y = _moe(block_e, nvalid, slot_dst, nused, hn, w_exp_gate[0], w_exp_up[0], w_exp_down[0])

    out = _ple_final(h1, rt, y, p[0, 0], w_ple_gate[0].astype(BF16), b_ple_gate,
                     w_ple_proj[0].astype(BF16), ple_norm, final_norm[None, :], 512)
    return out[None]
```

```python
import functools

import jax
import jax.numpy as jnp
import numpy as np
from jax import lax
from jax.experimental import pallas as pl
from jax.experimental.pallas import tpu as pltpu

F32 = jnp.float32
BF16 = jnp.bfloat16

D_MODEL = 2048
PLE_DIM = 256
MLA_HEADS = 8
QK_NOPE = 128
QK_ROPE = 64
QK_HEAD = QK_NOPE + QK_ROPE
QK_PAD = 256
V_HEAD = 128
V_PAD = 2 * V_HEAD
QK_AHEAD = 3
PV_LAG = 2
LAG_LIMIT = 60.0
Q_LORA = 512
KV_LORA = 256
ROPE_THETA = 10000.0
HG_HEADS = 8
HG_DK = 128
HG_DV = 128
HG_CHUNK = 64
D_MLA = MLA_HEADS * V_HEAD
D_HG = HG_HEADS * HG_DV
N_GROUPS = 8
EXPERTS_PER_GROUP = 8
N_EXPERTS = N_GROUPS * EXPERTS_PER_GROUP
TOP_K = 2
D_EXPERT = 512
EPS = 1e-6
LANES = 128
D_LAT = Q_LORA + KV_LORA + 2 * QK_ROPE
IN_TILE = 1024
NEG_BIG = -1e30
LOG2E = 1.4426950408889634

MOE_BLOCK = 320
ROW_CHUNK = 64
ROW_DMA_PRIORITY = 0
HG_ROWS = 256
HG_LEVELS = (32, 16, 8, 4, 2, 1)
HG_MM_LEVELS = (2, 1)
HG_NH = 4


def _cparams(sem, vmem_mb=None, flags=None):
    kw = dict(dimension_semantics=sem)
    if vmem_mb is not None:
        kw["vmem_limit_bytes"] = vmem_mb * 1024 * 1024
    if flags is not None:
        kw["flags"] = flags
    return pltpu.CompilerParams(**kw)


def _rms(x, g):
    ms = jnp.mean(x * x, axis=-1, keepdims=True)
    return x * lax.rsqrt(ms + EPS) * g


def _sigmoid(x):
    return 1.0 / (1.0 + jnp.exp(-x))


def _pack_halves(x):
    n = x.shape[1] // 2
    bits = lax.bitcast_convert_type(x.astype(BF16).astype(F32), jnp.uint32)
    return bits[:, :n] | (bits[:, n:] >> 16)


def _unpack_halves(w):
    hi = lax.bitcast_convert_type(w & jnp.uint32(0xFFFF0000), F32)
    lo = lax.bitcast_convert_type(w << 16, F32)
    return jnp.concatenate([hi, lo], axis=1)


def _in_proj_kernel(x_ref, g_ref, w_ref, hg_ref, lat_ref, xn_ref):
    j = pl.program_id(1)
    last = pl.num_programs(1) - 1

    @pl.when(j == 0)
    def _():
        xn_ref[...] = _rms(x_ref[...], g_ref[...]).astype(BF16)

    @pl.when(j < last)
    def _():
        hg_ref[...] = jnp.dot(xn_ref[...], w_ref[...], preferred_element_type=F32).astype(hg_ref.dtype)

    @pl.when(j == last)
    def _():
        lat_ref[...] = jnp.dot(xn_ref[...], w_ref[...], preferred_element_type=F32)


def _in_proj(x, gain, w, tm, tn):
    s, d = x.shape
    nt = w.shape[1] // tn
    return pl.pallas_call(
        _in_proj_kernel,
        grid=(s // tm, nt),
        in_specs=[
            pl.BlockSpec((tm, d), lambda i, j: (i, 0)),
            pl.BlockSpec((1, d), lambda i, j: (0, 0)),
            pl.BlockSpec((d, tn), lambda i, j: (0, j)),
        ],
        out_specs=[
            pl.BlockSpec((tm, tn), lambda i, j: (i, jnp.minimum(j, nt - 2))),
            pl.BlockSpec((tm, tn), lambda i, j: (i, 0)),
        ],
        out_shape=[
            jax.ShapeDtypeStruct((s, (nt - 1) * tn), BF16),
            jax.ShapeDtypeStruct((s, tn), F32),
        ],
        scratch_shapes=[pltpu.VMEM((tm, d), BF16)],
        compiler_params=_cparams(("parallel", "arbitrary"), 56),
        name="in_proj",
    )(x, gain, w)


def _mla_up_kernel(lat_ref, qn_ref, kvn_ref, wq_ref, wknt_ref, wv_ref, cc_ref, ss_ref,
                   q_ref, kt_ref, v_ref):
    lat = lat_ref[...]
    cq = _rms(lat[:, :Q_LORA], qn_ref[...]).astype(BF16)
    ckv = _rms(lat[:, Q_LORA:Q_LORA + KV_LORA], kvn_ref[...]).astype(BF16)
    kp = lat[:, Q_LORA + KV_LORA:D_LAT]
    cc = cc_ref[...]
    ss = ss_ref[...]
    scale = QK_HEAD ** -0.5 * LOG2E

    q = jnp.dot(cq, wq_ref[...], preferred_element_type=F32)
    for h in range(MLA_HEADS):
        base = h * QK_PAD
        q_ref[:, base:base + QK_NOPE] = (q[:, base:base + QK_NOPE] * scale).astype(BF16)
        r = q[:, base + QK_NOPE:base + QK_PAD]
        r = (r * cc + pltpu.roll(r, QK_ROPE, 1) * ss) * scale
        q_ref[:, base + QK_NOPE:base + QK_PAD] = r.astype(BF16)

    kpe_t = (kp * cc + pltpu.roll(kp, QK_ROPE, 1) * ss).T.astype(BF16)
    kn_t = lax.dot_general(wknt_ref[...], ckv, (((1,), (1,)), ((), ())), preferred_element_type=F32).astype(BF16)
    for h in range(MLA_HEADS):
        base = h * QK_PAD
        kt_ref[base:base + QK_NOPE, :] = kn_t[h * QK_NOPE:(h + 1) * QK_NOPE]
        kt_ref[base + QK_NOPE:base + QK_PAD, :] = kpe_t
    v = jnp.dot(ckv, wv_ref[...], preferred_element_type=F32).astype(BF16)
    ones = jnp.ones((v.shape[0], V_PAD - V_HEAD), BF16)
    for h in range(MLA_HEADS):
        v_ref[:, h * V_PAD:h * V_PAD + V_HEAD] = v[:, h * V_HEAD:(h + 1) * V_HEAD]
        v_ref[:, h * V_PAD + V_HEAD:(h + 1) * V_PAD] = ones


def _mla_up(lat, qn, kvn, wq, wknt, wv, cc, ss, tm):
    s = lat.shape[0]
    row = lambda i: (i, 0)
    full = lambda i: (0, 0)
    return pl.pallas_call(
        _mla_up_kernel,
        grid=(s // tm,),
        in_specs=[
            pl.BlockSpec((tm, IN_TILE), row),
            pl.BlockSpec((1, Q_LORA), full),
            pl.BlockSpec((1, KV_LORA), full),
            pl.BlockSpec(wq.shape, full),
            pl.BlockSpec(wknt.shape, full),
            pl.BlockSpec(wv.shape, full),
            pl.BlockSpec((tm, LANES), row),
            pl.BlockSpec((tm, LANES), row),
        ],
        out_specs=[
            pl.BlockSpec((tm, MLA_HEADS * QK_PAD), row),
            pl.BlockSpec((MLA_HEADS * QK_PAD, tm), lambda i: (0, i)),
            pl.BlockSpec((tm, MLA_HEADS * V_PAD), row),
        ],
        out_shape=[
            jax.ShapeDtypeStruct((s, MLA_HEADS * QK_PAD), BF16),
            jax.ShapeDtypeStruct((MLA_HEADS * QK_PAD, s), BF16),
            jax.ShapeDtypeStruct((s, MLA_HEADS * V_PAD), BF16),
        ],
        compiler_params=_cparams(("parallel",), 48),
        name="mla_up",
    )(lat, qn, kvn, wq, wknt, wv, cc, ss)


def _attn_kernel(it_ref, jt_ref, q_ref, kt_ref, v_ref, g_ref, o_ref, acc_ref, m_ref, redo_ref, *, tb):
    t = pl.program_id(0)
    i = it_ref[t]
    j = jt_ref[t]
    src = t % 2
    dst = 1 - src
    ntile = tb // LANES

    @pl.when(j == 0)
    def _():
        m_ref[...] = jnp.full(m_ref.shape, NEG_BIG, F32)
        acc_ref[src] = jnp.zeros(acc_ref.shape[1:], F32)
        redo_ref[0] = 0

    def scores(h):
        return jnp.dot(q_ref[:, h * QK_PAD:(h + 1) * QK_PAD], kt_ref[h * QK_PAD:(h + 1) * QK_PAD, :],
                       preferred_element_type=F32)

    def values(h):
        return v_ref[:, h * V_PAD:(h + 1) * V_PAD]

    def key_minus_query():
        return (lax.broadcasted_iota(jnp.int32, (tb, tb), 1) - lax.broadcasted_iota(jnp.int32, (tb, tb), 0))

    def lagged(masked):
        if masked:
            keep = key_minus_query() <= 0
        ahead = [scores(h) for h in range(min(QK_AHEAD, MLA_HEADS))]
        pending = []
        worst = None

        def accumulate(h, p):
            acc_ref[dst, h] = acc_ref[src, h] + jnp.dot(p, values(h), preferred_element_type=F32)

        for h in range(MLA_HEADS):
            s = ahead.pop(0)
            if h + QK_AHEAD < MLA_HEADS:
                ahead.append(scores(h + QK_AHEAD))
            d = s - jnp.concatenate([m_ref[h]] * ntile, axis=1)
            if masked:
                d = jnp.where(keep, d, NEG_BIG)
            top = jnp.max(d.reshape(tb // 8, 8, tb), axis=0)
            worst = top if worst is None else jnp.maximum(worst, top)
            pending.append((h, jnp.exp2(d).astype(BF16)))
            if len(pending) > PV_LAG:
                accumulate(*pending.pop(0))
        for item in pending:
            accumulate(*item)
        redo_ref[0] = (jnp.max(worst) > LAG_LIMIT).astype(jnp.int32)

    def exact():
        keep = key_minus_query() <= jnp.where(j == i, 0, tb)
        ahead = [scores(h) for h in range(min(QK_AHEAD, MLA_HEADS))]
        for h in range(MLA_HEADS):
            s = jnp.where(keep, ahead.pop(0), NEG_BIG)
            if h + QK_AHEAD < MLA_HEADS:
                ahead.append(scores(h + QK_AHEAD))
            m_prev = m_ref[h]
            m_new = jnp.maximum(m_prev, jnp.max(s, axis=1, keepdims=True))
            alpha = jnp.exp2(m_prev - m_new)
            p = jnp.exp2(s - jnp.concatenate([m_new] * ntile, axis=1)).astype(BF16)
            m_ref[h] = m_new
            acc_ref[dst, h] = (acc_ref[src, h] * jnp.concatenate([alpha] * (V_PAD // LANES), axis=1)
                               + jnp.dot(p, values(h), preferred_element_type=F32))

    @pl.when((j > 0) & (j < i))
    def _():
        lagged(False)

    @pl.when((j > 0) & (j == i))
    def _():
        lagged(True)

    @pl.when((j == 0) | (redo_ref[0] != 0))
    def _():
        exact()

    @pl.when(j == i)
    def _():
        outs = [acc_ref[dst, h, :, :V_HEAD] / acc_ref[dst, h, :, V_HEAD:] for h in range(MLA_HEADS)]
        o_ref[...] = _rms(jnp.concatenate(outs, axis=1), g_ref[...]).astype(o_ref.dtype)


def _attention(q, kt, v, gain, tb):
    s = q.shape[0]
    nq = s // tb
    it = np.concatenate([np.full(i + 1, i, np.int32) for i in range(nq)])
    jt = np.concatenate([np.arange(i + 1, dtype=np.int32) for i in range(nq)])
    grid_spec = pltpu.PrefetchScalarGridSpec(
        num_scalar_prefetch=2,
        grid=(it.shape[0],),
        in_specs=[
            pl.BlockSpec((tb, MLA_HEADS * QK_PAD), lambda t, it, jt: (it[t], 0)),
            pl.BlockSpec((MLA_HEADS * QK_PAD, tb), lambda t, it, jt: (0, jt[t])),
            pl.BlockSpec((tb, MLA_HEADS * V_PAD), lambda t, it, jt: (jt[t], 0)),
            pl.BlockSpec((1, D_MLA), lambda t, it, jt: (0, 0)),
        ],
        out_specs=pl.BlockSpec((tb, D_MLA), lambda t, it, jt: (it[t], 0)),
        scratch_shapes=[
            pltpu.VMEM((2, MLA_HEADS, tb, V_PAD), F32),
            pltpu.VMEM((MLA_HEADS, tb, LANES), F32),
            pltpu.SMEM((1,), jnp.int32),
        ],
    )
    return pl.pallas_call(
        functools.partial(_attn_kernel, tb=tb),
        grid_spec=grid_spec,
        out_shape=jax.ShapeDtypeStruct((s, D_MLA), BF16),
        compiler_params=_cparams(("arbitrary",), 48),
        name="mla_attention",
    )(jnp.asarray(it), jnp.asarray(jt), q, kt, v, gain)


def _hgrn_tables():
    n = HG_ROWS
    r = np.arange(n)
    c = np.arange(n)
    same = (r[:, None] // HG_CHUNK) == (c[None, :] // HG_CHUNK)

    def rows_upto(idx):
        return (same & (c[None, :] <= idx[:, None])).astype(np.float32)

    blocks = [rows_upto(r)]
    for m in HG_MM_LEVELS:
        blocks.append(rows_upto((r // (2 * m)) * (2 * m) + m))
    mall = np.concatenate(blocks, axis=0)

    x = r[:, None] ^ c[None, :]
    lv = np.full((n, n), -1, np.int32)
    for li, m in enumerate(HG_LEVELS):
        lv = np.where(same & (r[:, None] > c[None, :]) & (x >= m) & (x < 2 * m), li, lv)
    lv = np.where(r[:, None] == c[None, :], len(HG_LEVELS), lv)
    return jnp.asarray(mall, BF16), jnp.asarray(lv, jnp.int32)


def _hgrn_kernel(q_ref, f_ref, i_ref, g_ref, lb_ref, gn_ref, mall_ref, lv_ref, o_ref, st_ref):
    t = pl.program_id(1)
    n = HG_ROWS

    @pl.when(t == 0)
    def _():
        st_ref[...] = jnp.zeros(st_ref.shape, F32)

    rowid = lax.broadcasted_iota(jnp.int32, (n, 1), 0)
    lv = lv_ref[...]

    def head(hh):
        cols = slice(hh * HG_DK, (hh + 1) * HG_DK)
        q_in = q_ref[:, cols].astype(F32)
        qs = q_in * _sigmoid(q_in)
        lb = lb_ref[:, cols]
        f = lb + (1.0 - lb) * _sigmoid(f_ref[:, cols].astype(F32))
        kk = 1.0 - f
        logf = jnp.log(f)
        iv = i_ref[:, cols]

        l1 = logf.astype(BF16)
        l2 = (logf - l1.astype(F32)).astype(BF16)
        parts = jnp.dot(mall_ref[...], jnp.concatenate([l1, l2], axis=1), preferred_element_type=F32)
        yield
        bc = parts[:, :HG_DK] + parts[:, HG_DK:]
        b = bc[:n]

        def anchor(period, row):
            b3 = b.reshape(n // period, period, HG_DK)
            return jnp.broadcast_to(b3[:, row:row + 1, :], b3.shape).reshape(n, HG_DK)

        a = jnp.zeros((n, n), F32)
        for li, m in enumerate(HG_LEVELS):
            if m in HG_MM_LEVELS:
                k = 1 + HG_MM_LEVELS.index(m)
                c = bc[k * n:(k + 1) * n]
            else:
                c = anchor(2 * m, m)
            e = jnp.exp(-jnp.abs(b - c))
            upper = (rowid & (2 * m - 1)) >= m
            x = (jnp.where(upper, qs, kk) * e).astype(BF16)
            p = lax.dot_general(x, x, (((1,), (1,)), ((), ())), preferred_element_type=F32)
            yield
            a = jnp.where(lv == li, p, a)
        a = jnp.where(lv == len(HG_LEVELS), jnp.sum(qs * kk, axis=-1, keepdims=True), a)
        o = jnp.dot(a.astype(BF16), iv, preferred_element_type=F32)
        yield

        blast = anchor(HG_CHUNK, HG_CHUNK - 1)
        qd = (qs * jnp.exp(b)).astype(BF16)
        kd = kk * jnp.exp(blast - b)
        ivt = iv.astype(F32).T.astype(BF16)
        st = st_ref[hh]
        outs = []
        for ci in range(n // HG_CHUNK):
            lo = ci * HG_CHUNK
            inter = lax.dot_general(qd[lo:lo + HG_CHUNK], st.astype(BF16), (((1,), (1,)), ((), ())),
                                    preferred_element_type=F32)
            kd_c = jnp.where((rowid >= lo) & (rowid < lo + HG_CHUNK), kd, 0.0).astype(BF16)
            upd = jnp.dot(ivt, kd_c, preferred_element_type=F32)
            yield
            outs.append(o[lo:lo + HG_CHUNK] + inter)
            st = jnp.exp(blast[lo:lo + 1]) * st + upd
        st_ref[hh] = st
        o = jnp.concatenate(outs, axis=0)

        g_in = g_ref[:, cols].astype(F32)
        o_ref[:, cols] = (_rms(o, gn_ref[:, cols]) * (g_in * _sigmoid(g_in))).astype(o_ref.dtype)

    progs = [head(hh) for hh in range(HG_NH)]
    while progs:
        progs = [pr for pr in progs if next(pr, "done") != "done"]


def _hgrn(hg4, lb, gn, mall, lv):
    s = hg4.shape[0]
    groups = HG_HEADS // HG_NH
    width = HG_NH * HG_DK

    def col(c):
        return lambda hh, t: (t, c * groups + hh)

    head = lambda hh, t: (0, hh)
    const = lambda hh, t: (0, 0)
    return pl.pallas_call(
        _hgrn_kernel,
        grid=(groups, s // HG_ROWS),
        in_specs=[
            pl.BlockSpec((HG_ROWS, width), col(0)),
            pl.BlockSpec((HG_ROWS, width), col(1)),
            pl.BlockSpec((HG_ROWS, width), col(2)),
            pl.BlockSpec((HG_ROWS, width), col(3)),
            pl.BlockSpec((1, width), head),
            pl.BlockSpec((1, width), head),
            pl.BlockSpec(mall.shape, const),
            pl.BlockSpec(lv.shape, const),
        ],
        out_specs=pl.BlockSpec((HG_ROWS, width), lambda hh, t: (t, hh)),
        out_shape=jax.ShapeDtypeStruct((s, D_HG), BF16),
        scratch_shapes=[pltpu.VMEM((HG_NH, HG_DV, HG_DK), F32)],
        compiler_params=_cparams(("parallel", "arbitrary"), 32),
        name="hgrn2",
    )(hg4, hg4, hg4, hg4, lb, gn, mall, lv)


def _out_route_kernel(x_ref, om_ref, oh_ref, wa_ref, wb_ref, g_ref, wr_ref, br_ref,
                      h_ref, hn_ref, rt_ref):
    h1 = (x_ref[...]
          + jnp.dot(om_ref[...], wa_ref[...], preferred_element_type=F32)
          + jnp.dot(oh_ref[...], wb_ref[...], preferred_element_type=F32))
    h_ref[...] = h1
    hn = _rms(h1, g_ref[...])
    hn_ref[...] = _pack_halves(hn)
    hn_hi = hn.astype(BF16)
    hn_lo = (hn - hn_hi.astype(F32)).astype(BF16)
    hh = jnp.dot(hn_hi, wr_ref[...], preferred_element_type=F32)
    lh = jnp.dot(hn_lo, wr_ref[:, :LANES], preferred_element_type=F32)
    logits = hh[:, :LANES] + (hh[:, LANES:] + lh) + br_ref[...]

    lane = lax.broadcasted_iota(jnp.int32, logits.shape, 1)
    lanef = lane.astype(F32)
    ninf = -jnp.inf
    big = float(LANES)

    is_g = lane < N_GROUPS
    gl = jnp.where(is_g, logits, ninf)
    gmax = jnp.max(gl, axis=-1, keepdims=True)
    gsum = jnp.sum(jnp.where(is_g, jnp.exp(gl - gmax), 0.0), axis=-1, keepdims=True)
    g_w = 1.0 / gsum
    g_idx = jnp.min(jnp.where(gl == gmax, lanef, big), axis=-1, keepdims=True)

    e_lane = lane - N_GROUPS
    in_grp = (e_lane >= 0) & (e_lane < N_EXPERTS) & ((e_lane >> 3).astype(F32) == g_idx)
    el = jnp.where(in_grp, logits, ninf)
    emax = jnp.max(el, axis=-1, keepdims=True)
    esum = jnp.sum(jnp.where(in_grp, jnp.exp(el - emax), 0.0), axis=-1, keepdims=True)
    i1 = jnp.min(jnp.where(el == emax, lanef, big), axis=-1, keepdims=True)
    el2 = jnp.where(lanef == i1, ninf, el)
    emax2 = jnp.max(el2, axis=-1, keepdims=True)
    i2 = jnp.min(jnp.where(el2 == emax2, lanef, big), axis=-1, keepdims=True)
    p1 = 1.0 / esum
    p2 = jnp.exp(emax2 - emax) / esum
    w1 = g_w * p1 / (p1 + p2)
    w2 = g_w * p2 / (p1 + p2)

    rt = jnp.where(lane == 0, i1 - N_GROUPS,
                   jnp.where(lane == 1, i2 - N_GROUPS,
                             jnp.where(lane == 2, w1, jnp.where(lane == 3, w2, 0.0))))
    rt_ref[...] = rt


def _out_route(x, o_mla, o_hg, wa, wb, gain, wr, br, tm):
    s, d = x.shape
    row = lambda i: (i, 0)
    full = lambda i: (0, 0)
    return pl.pallas_call(
        _out_route_kernel,
        grid=(s // tm,),
        in_specs=[
            pl.BlockSpec((tm, d), row),
            pl.BlockSpec((tm, D_MLA), row),
            pl.BlockSpec((tm, D_HG), row),
            pl.BlockSpec(wa.shape, full, pipeline_mode=pl.Buffered(1)),
            pl.BlockSpec(wb.shape, full, pipeline_mode=pl.Buffered(1)),
            pl.BlockSpec((1, d), full),
            pl.BlockSpec(wr.shape, full, pipeline_mode=pl.Buffered(1)),
            pl.BlockSpec((1, LANES), full),
        ],
        out_specs=[
            pl.BlockSpec((tm, d), row),
            pl.BlockSpec((tm, d // 2), row),
            pl.BlockSpec((tm, LANES), row),
        ],
        out_shape=[
            jax.ShapeDtypeStruct((s, d), F32),
            jax.ShapeDtypeStruct((s, d // 2), jnp.uint32),
            jax.ShapeDtypeStruct((s, LANES), F32),
        ],
        compiler_params=_cparams(("parallel",), 56),
        name="out_proj_route",
    )(x, o_mla, o_hg, wa, wb, gain, wr, br)


def _moe_kernel(be_ref, nv_ref, dst_ref, nu_ref, hn_hbm, wg_hbm, wu_hbm, wd_hbm, y_hbm,
                xbuf, xsem, obuf, osem, wgb, wub, wdb, wsem, wslot, *, n_tok):
    i = pl.program_id(0)
    nused = nu_ref[0]
    e = be_ref[i]
    active = i < nused
    first = (i == 0) | (e != be_ref[jnp.maximum(i - 1, 0)])

    def chunks(blk, fn):
        nv = nv_ref[blk]
        for lo in range(0, MOE_BLOCK, ROW_CHUNK):
            pl.when(lo < nv)(functools.partial(fn, lo))

    def gather_start(blk, sl):
        def issue(lo):
            for r in range(lo, lo + ROW_CHUNK):
                tok = dst_ref[blk * MOE_BLOCK + r] & (n_tok - 1)
                pltpu.make_async_copy(hn_hbm.at[pl.ds(tok, 1), :], xbuf.at[sl, pl.ds(r, 1), :],
                                      xsem.at[sl]).start(priority=ROW_DMA_PRIORITY)
        chunks(blk, issue)

    def gather_wait(blk, sl):
        chunks(blk, lambda lo: pltpu.make_async_copy(
            hn_hbm.at[pl.ds(0, ROW_CHUNK), :], xbuf.at[sl, pl.ds(lo, ROW_CHUNK), :], xsem.at[sl]).wait())

    def scatter_start(blk, sl):
        def issue(lo):
            for r in range(lo, lo + ROW_CHUNK):
                dst = dst_ref[blk * MOE_BLOCK + r]
                pltpu.make_async_copy(obuf.at[sl, pl.ds(r, 1), :], y_hbm.at[pl.ds(dst, 1), :],
                                      osem.at[sl]).start(priority=ROW_DMA_PRIORITY)
        chunks(blk, issue)

    def scatter_wait(blk, sl):
        chunks(blk, lambda lo: pltpu.make_async_copy(
            obuf.at[sl, pl.ds(lo, ROW_CHUNK), :], y_hbm.at[pl.ds(0, ROW_CHUNK), :], osem.at[sl]).wait())

    def spare_copy(sl):
        return pltpu.make_async_copy(obuf.at[sl], y_hbm.at[pl.ds(TOP_K * n_tok + sl * MOE_BLOCK, MOE_BLOCK), :],
                                     osem.at[sl])

    def weight_copies(ex, sl):
        return (pltpu.make_async_copy(wg_hbm.at[ex], wgb.at[sl], wsem.at[sl, 0]),
                pltpu.make_async_copy(wu_hbm.at[ex], wub.at[sl], wsem.at[sl, 1]),
                pltpu.make_async_copy(wd_hbm.at[ex], wdb.at[sl], wsem.at[sl, 2]))

    @pl.when((i == 0) & active)
    def _():
        wslot[0] = 1
        for cp in weight_copies(e, 0):
            cp.start(priority=1 - ROW_DMA_PRIORITY)
        xbuf[...] = jnp.zeros(xbuf.shape, jnp.uint32)
        gather_start(0, 0)
        obuf[...] = jnp.zeros(obuf.shape, jnp.uint32)
        for sl in (0, 1):
            spare_copy(sl).start()
        for sl in (0, 1):
            spare_copy(sl).wait()

    @pl.when(active & first)
    def _():
        sl = 1 - wslot[0]
        wslot[0] = sl
        nxt = lax.while_loop(lambda j: (j < nused) & (be_ref[jnp.minimum(j, nused - 1)] == e),
                             lambda j: j + 1, i + 1)

        @pl.when(nxt < nused)
        def _():
            for cp in weight_copies(be_ref[nxt], 1 - sl):
                cp.start(priority=1 - ROW_DMA_PRIORITY)

        for cp in weight_copies(e, sl):
            cp.wait()

    def block(xs):
        sl = wslot[0]

        @pl.when(i >= 2)
        def _():
            scatter_wait(i - 2, xs)

        gather_wait(i, xs)

        @pl.when(i + 1 < nused)
        def _():
            gather_start(i + 1, 1 - xs)

        x = _unpack_halves(xbuf[xs]).astype(BF16)
        g = jnp.dot(x, wgb[sl].astype(BF16), preferred_element_type=F32)
        u = jnp.dot(x, wub[sl].astype(BF16), preferred_element_type=F32)
        hmid = (g * _sigmoid(g) * u).astype(BF16)
        obuf[xs] = _pack_halves(jnp.dot(hmid, wdb[sl].astype(BF16), preferred_element_type=F32))
        scatter_start(i, xs)

        @pl.when(i == nused - 1)
        def _():
            @pl.when(i >= 1)
            def _():
                scatter_wait(i - 1, 1 - xs)

            scatter_wait(i, xs)

    for s_ in (0, 1):
        pl.when(active & (i % 2 == s_))(functools.partial(block, s_))


def _moe(block_e, nvalid, slot_dst, nused, hn, wg, wu, wd):
    nb = block_e.shape[0]
    n_tok, dw = hn.shape
    d = 2 * dw
    assert n_tok & (n_tok - 1) == 0
    grid_spec = pltpu.PrefetchScalarGridSpec(
        num_scalar_prefetch=4,
        grid=(nb,),
        in_specs=[pl.BlockSpec(memory_space=pl.ANY)] * 4,
        out_specs=pl.BlockSpec(memory_space=pl.ANY),
        scratch_shapes=[
            pltpu.VMEM((2, MOE_BLOCK, dw), jnp.uint32),
            pltpu.SemaphoreType.DMA((2,)),
            pltpu.VMEM((2, MOE_BLOCK, dw), jnp.uint32),
            pltpu.SemaphoreType.DMA((2,)),
            pltpu.VMEM((2, d, D_EXPERT), F32),
            pltpu.VMEM((2, d, D_EXPERT), F32),
            pltpu.VMEM((2, D_EXPERT, d), F32),
            pltpu.SemaphoreType.DMA((2, 3)),
            pltpu.SMEM((1,), jnp.int32),
        ],
    )
    return pl.pallas_call(
        functools.partial(_moe_kernel, n_tok=n_tok),
        grid_spec=grid_spec,
        out_shape=jax.ShapeDtypeStruct((TOP_K * n_tok + 2 * MOE_BLOCK, dw), jnp.uint32),
        compiler_params=_cparams(("arbitrary",), 56),
        name="moe_experts",
    )(block_e, nvalid, slot_dst, nused, hn, wg, wu, wd)


def _ple_kernel(h_ref, rt_ref, y0_ref, y1_ref, p_ref, wg_ref, bg_ref, wp_ref, gp_ref, gf_ref, o_ref):
    rt = rt_ref[...]
    h2 = h_ref[...] + (rt[:, TOP_K:TOP_K + 1] * _unpack_halves(y0_ref[...])
                       + rt[:, TOP_K + 1:TOP_K + 2] * _unpack_halves(y1_ref[...]))
    hn = _rms(h2, gp_ref[...]).astype(BF16)
    gate = _sigmoid(jnp.dot(hn, wg_ref[...], preferred_element_type=F32) + bg_ref[...])
    pe = jnp.dot(p_ref[...].astype(BF16), wp_ref[...], preferred_element_type=F32)
    h3 = h2 + gate * pe
    o_ref[...] = _rms(h3, gf_ref[...])


def _ple_final(h1, rt, y, p, wg, bg, wp, gp, gf, tm):
    s, d = h1.shape
    nblk = s // tm
    row = lambda i: (i, 0)
    full = lambda i: (0, 0)
    return pl.pallas_call(
        _ple_kernel,
        grid=(nblk,),
        in_specs=[
            pl.BlockSpec((tm, d), row),
            pl.BlockSpec((tm, LANES), row),
            pl.BlockSpec((tm, d // 2), row),
            pl.BlockSpec((tm, d // 2), lambda i: (i + nblk, 0)),
            pl.BlockSpec((tm, PLE_DIM), row),
            pl.BlockSpec(wg.shape, full, pipeline_mode=pl.Buffered(1)),
            pl.BlockSpec((1, d), full),
            pl.BlockSpec(wp.shape, full, pipeline_mode=pl.Buffered(1)),
            pl.BlockSpec((1, d), full),
            pl.BlockSpec((1, d), full),
        ],
        out_specs=pl.BlockSpec((tm, d), row),
        out_shape=jax.ShapeDtypeStruct((s, d), F32),
        compiler_params=_cparams(("parallel",), 56),
        name="ple_final",
    )(h1, rt, y, y, p, wg, bg, wp, gp, gf)


def _slots_kernel(rt_ref, tri_ref, pos_ref, cnt_ref, carry_ref, base_ref):
    ps = pl.program_id(0)
    b = pl.program_id(1)
    rt = rt_ref[...]
    lane = lax.broadcasted_iota(jnp.int32, rt.shape, 1)
    lanef = lane.astype(F32)
    oh = [(lanef == rt[:, kk:kk + 1]).astype(F32) for kk in range(TOP_K)]
    both = oh[0] + oh[1]
    colsum = jnp.sum(both, axis=0, keepdims=True)

    @pl.when((ps == 0) & (b == 0))
    def _():
        cnt_ref[...] = jnp.zeros(cnt_ref.shape, F32)

    @pl.when(ps == 0)
    def _():
        cnt_ref[...] = cnt_ref[...] + colsum

    @pl.when((ps == 1) & (b == 0))
    def _():
        blocks = jnp.floor((cnt_ref[...] + (MOE_BLOCK - 0.5)) / MOE_BLOCK)
        r = lax.broadcasted_iota(jnp.int32, (LANES, LANES), 0)
        c = lax.broadcasted_iota(jnp.int32, (LANES, LANES), 1)
        before = (r < c).astype(F32)
        base_ref[...] = jnp.dot(blocks * MOE_BLOCK, before, preferred_element_type=F32,
                                precision=lax.Precision.HIGHEST)
        carry_ref[...] = jnp.zeros(carry_ref.shape, F32)

    @pl.when(ps == 1)
    def _():
        earlier = jnp.dot(tri_ref[...], both.astype(BF16), preferred_element_type=F32)
        row = earlier + (base_ref[0:1, :] + carry_ref[0:1, :])
        s0 = jnp.sum(oh[0] * row, axis=-1, keepdims=True)
        s1 = jnp.sum(oh[1] * row, axis=-1, keepdims=True)
        pos_ref[...] = jnp.where(lane == 0, s0, jnp.where(lane == 1, s1, 0.0))
        carry_ref[...] = carry_ref[...] + colsum


def _slots(rt, tb):
    s = rt.shape[0]
    tri = (np.arange(tb)[:, None] > np.arange(tb)[None, :]).astype(np.float32)
    return pl.pallas_call(
        _slots_kernel,
        grid=(2, s // tb),
        in_specs=[
            pl.BlockSpec((tb, LANES), lambda ps, b: (b, 0)),
            pl.BlockSpec((tb, tb), lambda ps, b: (0, 0)),
        ],
        out_specs=[
            pl.BlockSpec((tb, LANES), lambda ps, b: (b * ps, 0)),
            pl.BlockSpec((8, LANES), lambda ps, b: (0, 0)),
        ],
        out_shape=[
            jax.ShapeDtypeStruct((s, LANES), F32),
            jax.ShapeDtypeStruct((8, LANES), F32),
        ],
        scratch_shapes=[pltpu.VMEM((8, LANES), F32), pltpu.VMEM((8, LANES), F32)],
        compiler_params=_cparams(("arbitrary", "arbitrary")),
        name="moe_slots",
    )(rt, jnp.asarray(tri, BF16))


def _dispatch(rt, n_tok):
    a = n_tok * TOP_K
    nb = -(-a // MOE_BLOCK) + N_EXPERTS
    pos_f, cnt = _slots(rt, 512)
    pos = pos_f[:, :TOP_K].astype(jnp.int32).reshape(-1)
    counts = cnt[0, :N_EXPERTS].astype(jnp.int32)
    nblk = (counts + MOE_BLOCK - 1) // MOE_BLOCK
    bends = jnp.cumsum(nblk)
    bidx = jnp.arange(nb, dtype=jnp.int32)
    block_e = jnp.minimum(jnp.sum((bends[None, :] <= bidx[:, None]).astype(jnp.int32), axis=1), N_EXPERTS - 1)
    nused = bends[-1].astype(jnp.int32)
    nvalid = jnp.clip(counts[block_e] - (bidx - (bends - nblk)[block_e]) * MOE_BLOCK, 0, MOE_BLOCK)
    nvalid = jnp.where(bidx < nused, nvalid, 0).astype(jnp.int32)
    a_idx = jnp.arange(a, dtype=jnp.int32)
    sidx = jnp.arange(nb * MOE_BLOCK, dtype=jnp.int32)
    spare = a + ((sidx // MOE_BLOCK) % 2) * MOE_BLOCK + sidx % MOE_BLOCK
    slot_dst = spare.at[pos].set((a_idx % TOP_K) * n_tok + a_idx // TOP_K,
                                 unique_indices=True, mode="promise_in_bounds")
    return block_e, nvalid, slot_dst, nused.reshape(1)


def kernel(x, p, positions, attn_norm, w_in, q_norm, w_uq, kv_norm, w_ukv, mla_norm, hg_lb_logits, hg_norm, w_out, ffn_norm, w_router_group, b_router_group, w_router_expert, b_router_expert, w_exp_gate, w_exp_up, w_exp_down, ple_norm, w_ple_gate, b_ple_gate, w_ple_proj, final_norm):
    bsz, s, d = x.shape
    assert bsz == 1 and w_in.shape[0] == 1
    xt = x[0]

    inv_freq = 1.0 / (ROPE_THETA ** (jnp.arange(0, QK_ROPE, 2, dtype=F32) / QK_ROPE))
    ang = positions[0].astype(F32)[:, None] * inv_freq
    cos, sin = jnp.cos(ang), jnp.sin(ang)
    zpad = jnp.zeros((s, LANES - QK_ROPE), F32)
    cc = jnp.concatenate([cos, cos, zpad], axis=1)
    ss = jnp.concatenate([-sin, sin, zpad], axis=1)

    lb = jnp.cumsum(jax.nn.softmax(hg_lb_logits.astype(F32), axis=0), axis=0)[0][None, :]

    wi = w_in[0]
    kr0 = Q_LORA + KV_LORA
    half = QK_ROPE // 2
    w_all = jnp.concatenate(
        [wi[:, kr0 + QK_ROPE:], wi[:, :kr0 + QK_ROPE], wi[:, kr0 + half:kr0 + QK_ROPE], wi[:, kr0:kr0 + half],
         jnp.zeros((d, IN_TILE - D_LAT), F32)], axis=1).astype(BF16)
    wq3 = w_uq[0].reshape(Q_LORA, MLA_HEADS, QK_HEAD)
    wq_pad = jnp.concatenate(
        [wq3, wq3[:, :, QK_NOPE + half:], wq3[:, :, QK_NOPE:QK_NOPE + half]], axis=2
    ).reshape(Q_LORA, MLA_HEADS * QK_PAD).astype(BF16)
    wkv3 = w_ukv[0].reshape(KV_LORA, MLA_HEADS, QK_NOPE + V_HEAD)
    wknt = wkv3[:, :, :QK_NOPE].reshape(KV_LORA, MLA_HEADS * QK_NOPE).T.astype(BF16)
    wv = wkv3[:, :, QK_NOPE:].reshape(KV_LORA, D_MLA).astype(BF16)
    wo = w_out[0].astype(BF16)
    wr = jnp.concatenate(
        [w_router_group[0], w_router_expert[0], jnp.zeros((d, LANES - N_GROUPS - N_EXPERTS), F32)], axis=1)
    wr_hi = wr.astype(BF16)
    wr = jnp.concatenate([wr_hi, (wr - wr_hi.astype(F32)).astype(BF16)], axis=1)
    br = jnp.concatenate(
        [b_router_group[0], b_router_expert[0], jnp.zeros((LANES - N_GROUPS - N_EXPERTS,), F32)])[None, :]

    hg4, lat = _in_proj(xt, attn_norm, w_all, 1024, IN_TILE)
    q, kt, v = _mla_up(lat, q_norm, kv_norm, wq_pad, wknt, wv, cc, ss, 512)
    o_mla = _attention(q, kt, v, mla_norm, 512)
    mall, lv = _hgrn_tables()
    o_hg = _hgrn(hg4, lb, hg_norm[0].reshape(1, D_HG), mall, lv)

    h1, hn, rt = _out_route(xt, o_mla, o_hg, wo[:D_MLA], wo[D_MLA:], ffn_norm, wr, br, 512)
    block_e, nvalid, slot_dst, nused = _dispatch(rt, s)
    y = _moe(block_e, nvalid, slot_dst, nused, hn, w_exp_gate[0], w_exp_up[0], w_exp_down[0])

    out = _ple_final(h1, rt, y, p[0, 0], w_ple_gate[0].astype(BF16), b_ple_gate,
                     w_ple_proj[0].astype(BF16), ple_norm, final_norm[None, :], 512)
    return out[None]
```

```python
import functools

import jax
import jax.numpy as jnp
import numpy as np
from jax import lax
from jax.experimental import pallas as pl
from jax.experimental.pallas import tpu as pltpu

F32 = jnp.float32
BF16 = jnp.bfloat16

D_MODEL = 2048
PLE_DIM = 256
MLA_HEADS = 8
QK_NOPE = 128
QK_ROPE = 64
QK_HEAD = QK_NOPE + QK_ROPE
QK_PAD = 256
V_HEAD = 128
V_PAD = 2 * V_HEAD
QK_AHEAD = 2
PV_LAG = 1
LAG_LIMIT = 60.0
FLOOR_LIMIT = 100.0
Q_LORA = 512
KV_LORA = 256
ROPE_THETA = 10000.0
HG_HEADS = 8
HG_DK = 128
HG_DV = 128
HG_CHUNK = 64
D_MLA = MLA_HEADS * V_HEAD
D_HG = HG_HEADS * HG_DV
N_GROUPS = 8
EXPERTS_PER_GROUP = 8
N_EXPERTS = N_GROUPS * EXPERTS_PER_GROUP
TOP_K = 2
D_EXPERT = 512
EPS = 1e-6
LANES = 128
D_LAT = Q_LORA + KV_LORA + 2 * QK_ROPE
NEG_BIG = -1e30
LOG2E = 1.4426950408889634

MOE_BLOCK = 320
ROW_CHUNK = 64
ROW_DMA_PRIORITY = 0
HG_ROWS = 256
HG_LEVELS = (32, 16, 8, 4, 2, 1)
HG_MM_LEVELS = (2, 1)
HG_NH = 4


def _cparams(sem, vmem_mb=None, flags=None):
    kw = dict(dimension_semantics=sem)
    if vmem_mb is not None:
        kw["vmem_limit_bytes"] = vmem_mb * 1024 * 1024
    if flags is not None:
        kw["flags"] = flags
    return pltpu.CompilerParams(**kw)


def _rms(x, g):
    ms = jnp.mean(x * x, axis=-1, keepdims=True)
    return x * lax.rsqrt(ms + EPS) * g


def _sigmoid(x):
    return 1.0 / (1.0 + jnp.exp(-x))


def _pack_halves(x):
    n = x.shape[1] // 2
    bits = lax.bitcast_convert_type(x.astype(BF16).astype(F32), jnp.uint32)
    return bits[:, :n] | (bits[:, n:] >> 16)


def _unpack_halves(w):
    hi = lax.bitcast_convert_type(w & jnp.uint32(0xFFFF0000), F32)
    lo = lax.bitcast_convert_type(w << 16, F32)
    return jnp.concatenate([hi, lo], axis=1)


def _in_proj_kernel(x_ref, g_ref, whg_ref, wlat_ref, hg_ref, lat_ref, xn_ref):
    j = pl.program_id(1)
    last = pl.num_programs(1) - 1

    @pl.when(j == 0)
    def _():
        xn_ref[...] = _rms(x_ref[...], g_ref[...]).astype(BF16)

    @pl.when(j < last)
    def _():
        hg_ref[...] = jnp.dot(xn_ref[...], whg_ref[...], preferred_element_type=F32).astype(hg_ref.dtype)

    @pl.when(j == last)
    def _():
        lat_ref[...] = jnp.dot(xn_ref[...], wlat_ref[...], preferred_element_type=F32)


def _in_proj(x, gain, w_hg, w_lat, tm, tn):
    s, d = x.shape
    nt = w_hg.shape[1] // tn
    nl = w_lat.shape[1]
    hg_tile = lambda i, j: (i, jnp.minimum(j, nt - 1))
    return pl.pallas_call(
        _in_proj_kernel,
        grid=(s // tm, nt + 1),
        in_specs=[
            pl.BlockSpec((tm, d), lambda i, j: (i, 0)),
            pl.BlockSpec((1, d), lambda i, j: (0, 0)),
            pl.BlockSpec((d, tn), lambda i, j: (0, jnp.minimum(j, nt - 1))),
            pl.BlockSpec((d, nl), lambda i, j: (0, 0), pipeline_mode=pl.Buffered(1)),
        ],
        out_specs=[
            pl.BlockSpec((tm, tn), hg_tile),
            pl.BlockSpec((tm, nl), lambda i, j: (i, 0)),
        ],
        out_shape=[
            jax.ShapeDtypeStruct((s, nt * tn), BF16),
            jax.ShapeDtypeStruct((s, nl), F32),
        ],
        scratch_shapes=[pltpu.VMEM((tm, d), BF16)],
        compiler_params=_cparams(("parallel", "arbitrary"), 56),
        name="in_proj",
    )(x, gain, w_hg, w_lat)


def _mla_up_kernel(lat_ref, qn_ref, kvn_ref, wq_ref, wknt_ref, wv_ref, cc_ref, ss_ref,
                   q_ref, kt_ref, v_ref):
    lat = lat_ref[...]
    cq = _rms(lat[:, :Q_LORA], qn_ref[...]).astype(BF16)
    ckv = _rms(lat[:, Q_LORA:Q_LORA + KV_LORA], kvn_ref[...]).astype(BF16)
    kp = lat[:, Q_LORA + KV_LORA:D_LAT]
    cc = cc_ref[...]
    ss = ss_ref[...]
    scale = QK_HEAD ** -0.5 * LOG2E

    q = jnp.dot(cq, wq_ref[...], preferred_element_type=F32)
    for h in range(MLA_HEADS):
        base = h * QK_PAD
        q_ref[:, base:base + QK_NOPE] = (q[:, base:base + QK_NOPE] * scale).astype(BF16)
        r = q[:, base + QK_NOPE:base + QK_PAD]
        r = (r * cc + pltpu.roll(r, QK_ROPE, 1) * ss) * scale
        q_ref[:, base + QK_NOPE:base + QK_PAD] = r.astype(BF16)

    kpe_t = (kp * cc + pltpu.roll(kp, QK_ROPE, 1) * ss).T.astype(BF16)
    kn_t = lax.dot_general(wknt_ref[...], ckv, (((1,), (1,)), ((), ())), preferred_element_type=F32).astype(BF16)
    for h in range(MLA_HEADS):
        base = h * QK_PAD
        kt_ref[base:base + QK_NOPE, :] = kn_t[h * QK_NOPE:(h + 1) * QK_NOPE]
        kt_ref[base + QK_NOPE:base + QK_PAD, :] = kpe_t
    v_ref[...] = jnp.dot(ckv, wv_ref[...], preferred_element_type=F32).astype(BF16)


def _mla_up(lat, qn, kvn, wq, wknt, wv, cc, ss, tm):
    s = lat.shape[0]
    row = lambda i: (i, 0)
    full = lambda i: (0, 0)
    return pl.pallas_call(
        _mla_up_kernel,
        grid=(s // tm,),
        in_specs=[
            pl.BlockSpec((tm, D_LAT), row),
            pl.BlockSpec((1, Q_LORA), full),
            pl.BlockSpec((1, KV_LORA), full),
            pl.BlockSpec(wq.shape, full),
            pl.BlockSpec(wknt.shape, full),
            pl.BlockSpec(wv.shape, full),
            pl.BlockSpec((tm, LANES), row),
            pl.BlockSpec((tm, LANES), row),
        ],
        out_specs=[
            pl.BlockSpec((tm, MLA_HEADS * QK_PAD), row),
            pl.BlockSpec((MLA_HEADS * QK_PAD, tm), lambda i: (0, i)),
            pl.BlockSpec((tm, D_MLA), row),
        ],
        out_shape=[
            jax.ShapeDtypeStruct((s, MLA_HEADS * QK_PAD), BF16),
            jax.ShapeDtypeStruct((MLA_HEADS * QK_PAD, s), BF16),
            jax.ShapeDtypeStruct((s, D_MLA), BF16),
        ],
        compiler_params=_cparams(("parallel",), 48),
        name="mla_up",
    )(lat, qn, kvn, wq, wknt, wv, cc, ss)


def _attn_kernel(it_ref, jt_ref, q_ref, kt_ref, v_ref, g_ref, o_ref, acc_ref, m_ref, redo_ref, *, tb):
    t = pl.program_id(0)
    i = it_ref[t]
    j = jt_ref[t]
    src = t % 2
    dst = 1 - src
    ntile = tb // LANES

    @pl.when(j == 0)
    def _():
        m_ref[...] = jnp.zeros(m_ref.shape, F32)
        acc_ref[src] = jnp.zeros(acc_ref.shape[1:], F32)

    def scores(h):
        return jnp.dot(q_ref[:, h * QK_PAD:(h + 1) * QK_PAD], kt_ref[h * QK_PAD:(h + 1) * QK_PAD, :],
                       preferred_element_type=F32)

    ones = jnp.ones((tb, V_PAD - V_HEAD), BF16)

    def values(h):
        return jnp.concatenate([v_ref[:, h * V_HEAD:(h + 1) * V_HEAD], ones], axis=1)

    def key_minus_query():
        return (lax.broadcasted_iota(jnp.int32, (tb, tb), 1) - lax.broadcasted_iota(jnp.int32, (tb, tb), 0))

    def lagged(masked, first=False):
        if masked:
            keep = key_minus_query() <= (jnp.where(i == 0, 0, tb) if first else 0)
        ahead = [scores(h) for h in range(min(QK_AHEAD, MLA_HEADS))]
        pending = []
        worst = None
        least = None

        def accumulate(h, p):
            acc_ref[dst, h] = acc_ref[src, h] + jnp.dot(p, values(h), preferred_element_type=F32)

        for h in range(MLA_HEADS):
            s = ahead.pop(0)
            if h + QK_AHEAD < MLA_HEADS:
                ahead.append(scores(h + QK_AHEAD))
            d = s - jnp.concatenate([m_ref[h]] * ntile, axis=1)
            if first:
                low = jnp.min((jnp.where(keep, d, 0.0) if masked else d).reshape(tb // 8, 8, tb), axis=0)
                least = low if least is None else jnp.minimum(least, low)
            if masked:
                d = jnp.where(keep, d, NEG_BIG)
            top = jnp.max(d.reshape(tb // 8, 8, tb), axis=0)
            worst = top if worst is None else jnp.maximum(worst, top)
            pending.append((h, jnp.exp2(d).astype(BF16)))
            if len(pending) > PV_LAG:
                accumulate(*pending.pop(0))
        for item in pending:
            accumulate(*item)
        redo = jnp.max(worst) > LAG_LIMIT
        if first:
            redo = redo | (jnp.min(least) < -FLOOR_LIMIT)
        redo_ref[0] = redo.astype(jnp.int32)

    def exact():
        keep = key_minus_query() <= jnp.where(j == i, 0, tb)
        ahead = [scores(h) for h in range(min(QK_AHEAD, MLA_HEADS))]
        for h in range(MLA_HEADS):
            s = jnp.where(keep, ahead.pop(0), NEG_BIG)
            if h + QK_AHEAD < MLA_HEADS:
                ahead.append(scores(h + QK_AHEAD))
            m_prev = jnp.where(j == 0, NEG_BIG, m_ref[h])
            m_new = jnp.maximum(m_prev, jnp.max(s, axis=1, keepdims=True))
            alpha = jnp.exp2(m_prev - m_new)
            p = jnp.exp2(s - jnp.concatenate([m_new] * ntile, axis=1)).astype(BF16)
            m_ref[h] = m_new
            acc_ref[dst, h] = (acc_ref[src, h] * jnp.concatenate([alpha] * (V_PAD // LANES), axis=1)
                               + jnp.dot(p, values(h), preferred_element_type=F32))

    @pl.when(j == 0)
    def _():
        lagged(True, first=True)

    @pl.when((j > 0) & (j < i))
    def _():
        lagged(False)

    @pl.when((j > 0) & (j == i))
    def _():
        lagged(True)

    @pl.when(redo_ref[0] != 0)
    def _():
        exact()

    @pl.when(j == i)
    def _():
        outs = [acc_ref[dst, h, :, :V_HEAD] / acc_ref[dst, h, :, V_HEAD:] for h in range(MLA_HEADS)]
        o_ref[...] = _rms(jnp.concatenate(outs, axis=1), g_ref[...]).astype(o_ref.dtype)


def _attention(q, kt, v, gain, tb):
    s = q.shape[0]
    nq = s // tb
    it = np.concatenate([np.full(i + 1, i, np.int32) for i in range(nq)])
    jt = np.concatenate([np.arange(i + 1, dtype=np.int32) for i in range(nq)])
    grid_spec = pltpu.PrefetchScalarGridSpec(
        num_scalar_prefetch=2,
        grid=(it.shape[0],),
        in_specs=[
            pl.BlockSpec((tb, MLA_HEADS * QK_PAD), lambda t, it, jt: (it[t], 0)),
            pl.BlockSpec((MLA_HEADS * QK_PAD, tb), lambda t, it, jt: (0, jt[t])),
            pl.BlockSpec((tb, D_MLA), lambda t, it, jt: (jt[t], 0)),
            pl.BlockSpec((1, D_MLA), lambda t, it, jt: (0, 0)),
        ],
        out_specs=pl.BlockSpec((tb, D_MLA), lambda t, it, jt: (it[t], 0)),
        scratch_shapes=[
            pltpu.VMEM((2, MLA_HEADS, tb, V_PAD), F32),
            pltpu.VMEM((MLA_HEADS, tb, LANES), F32),
            pltpu.SMEM((1,), jnp.int32),
        ],
    )
    return pl.pallas_call(
        functools.partial(_attn_kernel, tb=tb),
        grid_spec=grid_spec,
        out_shape=jax.ShapeDtypeStruct((s, D_MLA), BF16),
        compiler_params=_cparams(("arbitrary",), 48),
        name="mla_attention",
    )(jnp.asarray(it), jnp.asarray(jt), q, kt, v, gain)


def _hgrn_tables():
    n = HG_ROWS
    r = np.arange(n)
    c = np.arange(n)
    same = (r[:, None] // HG_CHUNK) == (c[None, :] // HG_CHUNK)

    def rows_upto(idx):
        return (same & (c[None, :] <= idx[:, None])).astype(np.float32)

    blocks = [rows_upto(r)]
    for m in HG_MM_LEVELS:
        blocks.append(rows_upto((r // (2 * m)) * (2 * m) + m))
    mall = np.concatenate(blocks, axis=0)

    x = r[:, None] ^ c[None, :]
    lv = np.full((n, n), -1, np.int32)
    for li, m in enumerate(HG_LEVELS):
        lv = np.where(same & (r[:, None] > c[None, :]) & (x >= m) & (x < 2 * m), li, lv)
    lv = np.where(r[:, None] == c[None, :], len(HG_LEVELS), lv)
    return jnp.asarray(mall, BF16), jnp.asarray(lv, jnp.int32)


def _hgrn_kernel(q_ref, f_ref, i_ref, g_ref, lb_ref, gn_ref, mall_ref, lv_ref, o_ref, st_ref):
    t = pl.program_id(1)
    n = HG_ROWS

    @pl.when(t == 0)
    def _():
        st_ref[...] = jnp.zeros(st_ref.shape, F32)

    rowid = lax.broadcasted_iota(jnp.int32, (n, 1), 0)
    lv = lv_ref[...]

    def head(hh):
        cols = slice(hh * HG_DK, (hh + 1) * HG_DK)
        q_in = q_ref[:, cols].astype(F32)
        qs = q_in * _sigmoid(q_in)
        lb = lb_ref[:, cols]
        f = lb + (1.0 - lb) * _sigmoid(f_ref[:, cols].astype(F32))
        kk = 1.0 - f
        logf = jnp.log(f)
        iv = i_ref[:, cols]

        l1 = logf.astype(BF16)
        l2 = (logf - l1.astype(F32)).astype(BF16)
        parts = jnp.dot(mall_ref[...], jnp.concatenate([l1, l2], axis=1), preferred_element_type=F32)
        yield
        bc = parts[:, :HG_DK] + parts[:, HG_DK:]
        b = bc[:n]

        def anchor(period, row):
            b3 = b.reshape(n // period, period, HG_DK)
            return jnp.broadcast_to(b3[:, row:row + 1, :], b3.shape).reshape(n, HG_DK)

        a = jnp.zeros((n, n), F32)
        for li, m in enumerate(HG_LEVELS):
            if m in HG_MM_LEVELS:
                k = 1 + HG_MM_LEVELS.index(m)
                c = bc[k * n:(k + 1) * n]
            else:
                c = anchor(2 * m, m)
            e = jnp.exp(-jnp.abs(b - c))
            upper = (rowid & (2 * m - 1)) >= m
            x = (jnp.where(upper, qs, kk) * e).astype(BF16)
            p = lax.dot_general(x, x, (((1,), (1,)), ((), ())), preferred_element_type=F32)
            yield
            a = jnp.where(lv == li, p, a)
        a = jnp.where(lv == len(HG_LEVELS), jnp.sum(qs * kk, axis=-1, keepdims=True), a)
        o = jnp.dot(a.astype(BF16), iv, preferred_element_type=F32)
        yield

        blast = anchor(HG_CHUNK, HG_CHUNK - 1)
        qd = (qs * jnp.exp(b)).astype(BF16)
        kd = kk * jnp.exp(blast - b)
        ivt = iv.astype(F32).T.astype(BF16)
        st = st_ref[hh]
        outs = []
        for ci in range(n // HG_CHUNK):
            lo = ci * HG_CHUNK
            inter = lax.dot_general(qd[lo:lo + HG_CHUNK], st.astype(BF16), (((1,), (1,)), ((), ())),
                                    preferred_element_type=F32)
            kd_c = jnp.where((rowid >= lo) & (rowid < lo + HG_CHUNK), kd, 0.0).astype(BF16)
            upd = jnp.dot(ivt, kd_c, preferred_element_type=F32)
            yield
            outs.append(o[lo:lo + HG_CHUNK] + inter)
            st = jnp.exp(blast[lo:lo + 1]) * st + upd
        st_ref[hh] = st
        o = jnp.concatenate(outs, axis=0)

        g_in = g_ref[:, cols].astype(F32)
        o_ref[:, cols] = (_rms(o, gn_ref[:, cols]) * (g_in * _sigmoid(g_in))).astype(o_ref.dtype)

    progs = [head(hh) for hh in range(HG_NH)]
    while progs:
        progs = [pr for pr in progs if next(pr, "done") != "done"]


def _hgrn(hg4, lb, gn, mall, lv):
    s = hg4.shape[0]
    groups = HG_HEADS // HG_NH
    width = HG_NH * HG_DK

    def col(c):
        return lambda hh, t: (t, c * groups + hh)

    head = lambda hh, t: (0, hh)
    const = lambda hh, t: (0, 0)
    return pl.pallas_call(
        _hgrn_kernel,
        grid=(groups, s // HG_ROWS),
        in_specs=[
            pl.BlockSpec((HG_ROWS, width), col(0)),
            pl.BlockSpec((HG_ROWS, width), col(1)),
            pl.BlockSpec((HG_ROWS, width), col(2)),
            pl.BlockSpec((HG_ROWS, width), col(3)),
            pl.BlockSpec((1, width), head),
            pl.BlockSpec((1, width), head),
            pl.BlockSpec(mall.shape, const),
            pl.BlockSpec(lv.shape, const),
        ],
        out_specs=pl.BlockSpec((HG_ROWS, width), lambda hh, t: (t, hh)),
        out_shape=jax.ShapeDtypeStruct((s, D_HG), BF16),
        scratch_shapes=[pltpu.VMEM((HG_NH, HG_DV, HG_DK), F32)],
        compiler_params=_cparams(("parallel", "arbitrary"), 32),
        name="hgrn2",
    )(hg4, hg4, hg4, hg4, lb, gn, mall, lv)


def _out_route_kernel(x_ref, om_ref, oh_ref, wa_ref, wb_ref, g_ref, wr_ref, br_ref,
                      h_ref, hn_ref, rt_ref):
    h1 = (x_ref[...]
          + jnp.dot(om_ref[...], wa_ref[...], preferred_element_type=F32)
          + jnp.dot(oh_ref[...], wb_ref[...], preferred_element_type=F32))
    h_ref[...] = h1
    hn = _rms(h1, g_ref[...])
    hn_ref[...] = _pack_halves(hn)
    hn_hi = hn.astype(BF16)
    hn_lo = (hn - hn_hi.astype(F32)).astype(BF16)
    hh = jnp.dot(hn_hi, wr_ref[...], preferred_element_type=F32)
    lh = jnp.dot(hn_lo, wr_ref[:, :LANES], preferred_element_type=F32)
    logits = hh[:, :LANES] + (hh[:, LANES:] + lh) + br_ref[...]

    lane = lax.broadcasted_iota(jnp.int32, logits.shape, 1)
    lanef = lane.astype(F32)
    ninf = -jnp.inf
    big = float(LANES)

    is_g = lane < N_GROUPS
    gl = jnp.where(is_g, logits, ninf)
    gmax = jnp.max(gl, axis=-1, keepdims=True)
    gsum = jnp.sum(jnp.where(is_g, jnp.exp(gl - gmax), 0.0), axis=-1, keepdims=True)
    g_w = 1.0 / gsum
    g_idx = jnp.min(jnp.where(gl == gmax, lanef, big), axis=-1, keepdims=True)

    e_lane = lane - N_GROUPS
    in_grp = (e_lane >= 0) & (e_lane < N_EXPERTS) & ((e_lane >> 3).astype(F32) == g_idx)
    el = jnp.where(in_grp, logits, ninf)
    emax = jnp.max(el, axis=-1, keepdims=True)
    esum = jnp.sum(jnp.where(in_grp, jnp.exp(el - emax), 0.0), axis=-1, keepdims=True)
    i1 = jnp.min(jnp.where(el == emax, lanef, big), axis=-1, keepdims=True)
    el2 = jnp.where(lanef == i1, ninf, el)
    emax2 = jnp.max(el2, axis=-1, keepdims=True)
    i2 = jnp.min(jnp.where(el2 == emax2, lanef, big), axis=-1, keepdims=True)
    p1 = 1.0 / esum
    p2 = jnp.exp(emax2 - emax) / esum
    w1 = g_w * p1 / (p1 + p2)
    w2 = g_w * p2 / (p1 + p2)

    rt = jnp.where(lane == 0, i1 - N_GROUPS,
                   jnp.where(lane == 1, i2 - N_GROUPS,
                             jnp.where(lane == 2, w1, jnp.where(lane == 3, w2, 0.0))))
    rt_ref[...] = rt


def _out_route(x, o_mla, o_hg, wa, wb, gain, wr, br, tm):
    s, d = x.shape
    row = lambda i: (i, 0)
    full = lambda i: (0, 0)
    return pl.pallas_call(
        _out_route_kernel,
        grid=(s // tm,),
        in_specs=[
            pl.BlockSpec((tm, d), row),
            pl.BlockSpec((tm, D_MLA), row),
            pl.BlockSpec((tm, D_HG), row),
            pl.BlockSpec(wa.shape, full, pipeline_mode=pl.Buffered(1)),
            pl.BlockSpec(wb.shape, full, pipeline_mode=pl.Buffered(1)),
            pl.BlockSpec((1, d), full),
            pl.BlockSpec(wr.shape, full, pipeline_mode=pl.Buffered(1)),
            pl.BlockSpec((1, LANES), full),
        ],
        out_specs=[
            pl.BlockSpec((tm, d), row),
            pl.BlockSpec((tm, d // 2), row),
            pl.BlockSpec((tm, LANES), row),
        ],
        out_shape=[
            jax.ShapeDtypeStruct((s, d), F32),
            jax.ShapeDtypeStruct((s, d // 2), jnp.uint32),
            jax.ShapeDtypeStruct((s, LANES), F32),
        ],
        compiler_params=_cparams(("parallel",), 56),
        name="out_proj_route",
    )(x, o_mla, o_hg, wa, wb, gain, wr, br)


def _moe_kernel(be_ref, nv_ref, dst_ref, nu_ref, hn_hbm, wg_hbm, wu_hbm, wd_hbm, y_hbm,
                xbuf, xsem, obuf, osem, wgb, wub, wdb, wsem, wslot, *, n_tok):
    i = pl.program_id(0)
    nused = nu_ref[0]
    e = be_ref[i]
    active = i < nused
    first = (i == 0) | (e != be_ref[jnp.maximum(i - 1, 0)])

    def chunks(blk, fn):
        nv = nv_ref[blk]
        for lo in range(0, MOE_BLOCK, ROW_CHUNK):
            pl.when(lo < nv)(functools.partial(fn, lo))

    def gather_start(blk, sl):
        def issue(lo):
            for r in range(lo, lo + ROW_CHUNK):
                tok = dst_ref[blk * MOE_BLOCK + r] & (n_tok - 1)
                pltpu.make_async_copy(hn_hbm.at[pl.ds(tok, 1), :], xbuf.at[sl, pl.ds(r, 1), :],
                                      xsem.at[sl]).start(priority=ROW_DMA_PRIORITY)
        chunks(blk, issue)

    def gather_wait(blk, sl):
        chunks(blk, lambda lo: pltpu.make_async_copy(
            hn_hbm.at[pl.ds(0, ROW_CHUNK), :], xbuf.at[sl, pl.ds(lo, ROW_CHUNK), :], xsem.at[sl]).wait())

    def scatter_start(blk, sl):
        def issue(lo):
            for r in range(lo, lo + ROW_CHUNK):
                dst = dst_ref[blk * MOE_BLOCK + r]
                pltpu.make_async_copy(obuf.at[sl, pl.ds(r, 1), :], y_hbm.at[pl.ds(dst, 1), :],
                                      osem.at[sl]).start(priority=ROW_DMA_PRIORITY)
        chunks(blk, issue)

    def scatter_wait(blk, sl):
        chunks(blk, lambda lo: pltpu.make_async_copy(
            obuf.at[sl, pl.ds(lo, ROW_CHUNK), :], y_hbm.at[pl.ds(0, ROW_CHUNK), :], osem.at[sl]).wait())

    def spare_copy(sl):
        return pltpu.make_async_copy(obuf.at[sl], y_hbm.at[pl.ds(TOP_K * n_tok + sl * MOE_BLOCK, MOE_BLOCK), :],
                                     osem.at[sl])

    def weight_copies(ex, sl):
        return (pltpu.make_async_copy(wg_hbm.at[ex], wgb.at[sl], wsem.at[sl, 0]),
                pltpu.make_async_copy(wu_hbm.at[ex], wub.at[sl], wsem.at[sl, 1]),
                pltpu.make_async_copy(wd_hbm.at[ex], wdb.at[sl], wsem.at[sl, 2]))

    @pl.when((i == 0) & active)
    def _():
        wslot[0] = 1
        for cp in weight_copies(e, 0):
            cp.start(priority=1 - ROW_DMA_PRIORITY)
        xbuf[...] = jnp.zeros(xbuf.shape, jnp.uint32)
        gather_start(0, 0)
        obuf[...] = jnp.zeros(obuf.shape, jnp.uint32)
        for sl in (0, 1):
            spare_copy(sl).start()
        for sl in (0, 1):
            spare_copy(sl).wait()

    @pl.when(active & first)
    def _():
        sl = 1 - wslot[0]
        wslot[0] = sl
        nxt = lax.while_loop(lambda j: (j < nused) & (be_ref[jnp.minimum(j, nused - 1)] == e),
                             lambda j: j + 1, i + 1)

        @pl.when(nxt < nused)
        def _():
            for cp in weight_copies(be_ref[nxt], 1 - sl):
                cp.start(priority=1 - ROW_DMA_PRIORITY)

        for cp in weight_copies(e, sl):
            cp.wait()

    def block(xs):
        sl = wslot[0]

        @pl.when(i >= 2)
        def _():
            scatter_wait(i - 2, xs)

        gather_wait(i, xs)

        @pl.when(i + 1 < nused)
        def _():
            gather_start(i + 1, 1 - xs)

        x = _unpack_halves(xbuf[xs]).astype(BF16)
        g = jnp.dot(x, wgb[sl].astype(BF16), preferred_element_type=F32)
        u = jnp.dot(x, wub[sl].astype(BF16), preferred_element_type=F32)
        hmid = (g * _sigmoid(g) * u).astype(BF16)
        obuf[xs] = _pack_halves(jnp.dot(hmid, wdb[sl].astype(BF16), preferred_element_type=F32))
        scatter_start(i, xs)

        @pl.when(i == nused - 1)
        def _():
            @pl.when(i >= 1)
            def _():
                scatter_wait(i - 1, 1 - xs)

            scatter_wait(i, xs)

    for s_ in (0, 1):
        pl.when(active & (i % 2 == s_))(functools.partial(block, s_))


def _moe(block_e, nvalid, slot_dst, nused, hn, wg, wu, wd):
    nb = block_e.shape[0]
    n_tok, dw = hn.shape
    d = 2 * dw
    assert n_tok & (n_tok - 1) == 0
    grid_spec = pltpu.PrefetchScalarGridSpec(
        num_scalar_prefetch=4,
        grid=(nb,),
        in_specs=[pl.BlockSpec(memory_space=pl.ANY)] * 4,
        out_specs=pl.BlockSpec(memory_space=pl.ANY),
        scratch_shapes=[
            pltpu.VMEM((2, MOE_BLOCK, dw), jnp.uint32),
            pltpu.SemaphoreType.DMA((2,)),
            pltpu.VMEM((2, MOE_BLOCK, dw), jnp.uint32),
            pltpu.SemaphoreType.DMA((2,)),
            pltpu.VMEM((2, d, D_EXPERT), F32),
            pltpu.VMEM((2, d, D_EXPERT), F32),
            pltpu.VMEM((2, D_EXPERT, d), F32),
            pltpu.SemaphoreType.DMA((2, 3)),
            pltpu.SMEM((1,), jnp.int32),
        ],
    )
    return pl.pallas_call(
        functools.partial(_moe_kernel, n_tok=n_tok),
        grid_spec=grid_spec,
        out_shape=jax.ShapeDtypeStruct((TOP_K * n_tok + 2 * MOE_BLOCK, dw), jnp.uint32),
        compiler_params=_cparams(("arbitrary",), 56),
        name="moe_experts",
    )(block_e, nvalid, slot_dst, nused, hn, wg, wu, wd)


def _ple_kernel(h_ref, rt_ref, y0_ref, y1_ref, p_ref, wg_ref, bg_ref, wp_ref, gp_ref, gf_ref, o_ref):
    rt = rt_ref[...]
    h2 = h_ref[...] + (rt[:, TOP_K:TOP_K + 1] * _unpack_halves(y0_ref[...])
                       + rt[:, TOP_K + 1:TOP_K + 2] * _unpack_halves(y1_ref[...]))
    hn = _rms(h2, gp_ref[...]).astype(BF16)
    gate = _sigmoid(jnp.dot(hn, wg_ref[...], preferred_element_type=F32) + bg_ref[...])
    pe = jnp.dot(p_ref[...].astype(BF16), wp_ref[...], preferred_element_type=F32)
    h3 = h2 + gate * pe
    o_ref[...] = _rms(h3, gf_ref[...])


def _ple_final(h1, rt, y, p, wg, bg, wp, gp, gf, tm):
    s, d = h1.shape
    nblk = s // tm
    row = lambda i: (i, 0)
    full = lambda i: (0, 0)
    return pl.pallas_call(
        _ple_kernel,
        grid=(nblk,),
        in_specs=[
            pl.BlockSpec((tm, d), row),
            pl.BlockSpec((tm, LANES), row),
            pl.BlockSpec((tm, d // 2), row),
            pl.BlockSpec((tm, d // 2), lambda i: (i + nblk, 0)),
            pl.BlockSpec((tm, PLE_DIM), row),
            pl.BlockSpec(wg.shape, full, pipeline_mode=pl.Buffered(1)),
            pl.BlockSpec((1, d), full),
            pl.BlockSpec(wp.shape, full, pipeline_mode=pl.Buffered(1)),
            pl.BlockSpec((1, d), full),
            pl.BlockSpec((1, d), full),
        ],
        out_specs=pl.BlockSpec((tm, d), row),
        out_shape=jax.ShapeDtypeStruct((s, d), F32),
        compiler_params=_cparams(("parallel",), 56),
        name="ple_final",
    )(h1, rt, y, y, p, wg, bg, wp, gp, gf)


def _slots_kernel(rt_ref, tri_ref, pos_ref, cnt_ref, carry_ref, base_ref):
    ps = pl.program_id(0)
    b = pl.program_id(1)
    rt = rt_ref[...]
    lane = lax.broadcasted_iota(jnp.int32, rt.shape, 1)
    lanef = lane.astype(F32)
    oh = [(lanef == rt[:, kk:kk + 1]).astype(F32) for kk in range(TOP_K)]
    both = oh[0] + oh[1]
    colsum = jnp.sum(both, axis=0, keepdims=True)

    @pl.when((ps == 0) & (b == 0))
    def _():
        cnt_ref[...] = jnp.zeros(cnt_ref.shape, F32)

    @pl.when(ps == 0)
    def _():
        cnt_ref[...] = cnt_ref[...] + colsum

    @pl.when((ps == 1) & (b == 0))
    def _():
        blocks = jnp.floor((cnt_ref[...] + (MOE_BLOCK - 0.5)) / MOE_BLOCK)
        r = lax.broadcasted_iota(jnp.int32, (LANES, LANES), 0)
        c = lax.broadcasted_iota(jnp.int32, (LANES, LANES), 1)
        before = (r < c).astype(F32)
        base_ref[...] = jnp.dot(blocks * MOE_BLOCK, before, preferred_element_type=F32,
                                precision=lax.Precision.HIGHEST)
        carry_ref[...] = jnp.zeros(carry_ref.shape, F32)

    @pl.when(ps == 1)
    def _():
        earlier = jnp.dot(tri_ref[...], both.astype(BF16), preferred_element_type=F32)
        row = earlier + (base_ref[0:1, :] + carry_ref[0:1, :])
        s0 = jnp.sum(oh[0] * row, axis=-1, keepdims=True)
        s1 = jnp.sum(oh[1] * row, axis=-1, keepdims=True)
        pos_ref[...] = jnp.where(lane == 0, s0, jnp.where(lane == 1, s1, 0.0))
        carry_ref[...] = carry_ref[...] + colsum


def _slots(rt, tb):
    s = rt.shape[0]
    tri = (np.arange(tb)[:, None] > np.arange(tb)[None, :]).astype(np.float32)
    return pl.pallas_call(
        _slots_kernel,
        grid=(2, s // tb),
        in_specs=[
            pl.BlockSpec((tb, LANES), lambda ps, b: (b, 0)),
            pl.BlockSpec((tb, tb), lambda ps, b: (0, 0)),
        ],
        out_specs=[
            pl.BlockSpec((tb, LANES), lambda ps, b: (b * ps, 0)),
            pl.BlockSpec((8, LANES), lambda ps, b: (0, 0)),
        ],
        out_shape=[
            jax.ShapeDtypeStruct((s, LANES), F32),
            jax.ShapeDtypeStruct((8, LANES), F32),
        ],
        scratch_shapes=[pltpu.VMEM((8, LANES), F32), pltpu.VMEM((8, LANES), F32)],
        compiler_params=_cparams(("arbitrary", "arbitrary")),
        name="moe_slots",
    )(rt, jnp.asarray(tri, BF16))


def _dispatch(rt, n_tok):
    a = n_tok * TOP_K
    nb = -(-a // MOE_BLOCK) + N_EXPERTS
    pos_f, cnt = _slots(rt, 512)
    pos = pos_f[:, :TOP_K].astype(jnp.int32).reshape(-1)
    counts = cnt[0, :N_EXPERTS].astype(jnp.int32)
    nblk = (counts + MOE_BLOCK - 1) // MOE_BLOCK
    bends = jnp.cumsum(nblk)
    bidx = jnp.arange(nb, dtype=jnp.int32)
    block_e = jnp.minimum(jnp.sum((bends[None, :] <= bidx[:, None]).astype(jnp.int32), axis=1), N_EXPERTS - 1)
    nused = bends[-1].astype(jnp.int32)
    nvalid = jnp.clip(counts[block_e] - (bidx - (bends - nblk)[block_e]) * MOE_BLOCK, 0, MOE_BLOCK)
    nvalid = jnp.where(bidx < nused, nvalid, 0).astype(jnp.int32)
    a_idx = jnp.arange(a, dtype=jnp.int32)
    sidx = jnp.arange(nb * MOE_BLOCK, dtype=jnp.int32)
    spare = a + ((sidx // MOE_BLOCK) % 2) * MOE_BLOCK + sidx % MOE_BLOCK
    slot_dst = spare.at[pos].set((a_idx % TOP_K) * n_tok + a_idx // TOP_K,
                                 unique_indices=True, mode="promise_in_bounds")
    return block_e, nvalid, slot_dst, nused.reshape(1)


def kernel(x, p, positions, attn_norm, w_in, q_norm, w_uq, kv_norm, w_ukv, mla_norm, hg_lb_logits, hg_norm, w_out, ffn_norm, w_router_group, b_router_group, w_router_expert, b_router_expert, w_exp_gate, w_exp_up, w_exp_down, ple_norm, w_ple_gate, b_ple_gate, w_ple_proj, final_norm):
    bsz, s, d = x.shape
    assert bsz == 1 and w_in.shape[0] == 1
    xt = x[0]

    inv_freq = 1.0 / (ROPE_THETA ** (jnp.arange(0, QK_ROPE, 2, dtype=F32) / QK_ROPE))
    ang = positions[0].astype(F32)[:, None] * inv_freq
    cos, sin = jnp.cos(ang), jnp.sin(ang)
    zpad = jnp.zeros((s, LANES - QK_ROPE), F32)
    cc = jnp.concatenate([cos, cos, zpad], axis=1)
    ss = jnp.concatenate([-sin, sin, zpad], axis=1)

    lb = jnp.cumsum(jax.nn.softmax(hg_lb_logits.astype(F32), axis=0), axis=0)[0][None, :]

    wi = w_in[0]
    kr0 = Q_LORA + KV_LORA
    half = QK_ROPE // 2
    w_lat = jnp.concatenate(
        [wi[:, :kr0 + QK_ROPE], wi[:, kr0 + half:kr0 + QK_ROPE], wi[:, kr0:kr0 + half]], axis=1).astype(BF16)
    w_hg = wi[:, kr0 + QK_ROPE:].astype(BF16)
    wq3 = w_uq[0].reshape(Q_LORA, MLA_HEADS, QK_HEAD)
    wq_pad = jnp.concatenate(
        [wq3, wq3[:, :, QK_NOPE + half:], wq3[:, :, QK_NOPE:QK_NOPE + half]], axis=2
    ).reshape(Q_LORA, MLA_HEADS * QK_PAD).astype(BF16)
    wkv3 = w_ukv[0].reshape(KV_LORA, MLA_HEADS, QK_NOPE + V_HEAD)
    wknt = wkv3[:, :, :QK_NOPE].reshape(KV_LORA, MLA_HEADS * QK_NOPE).T.astype(BF16)
    wv = wkv3[:, :, QK_NOPE:].reshape(KV_LORA, D_MLA).astype(BF16)
    wo = w_out[0].astype(BF16)
    wr = jnp.concatenate(
        [w_router_group[0], w_router_expert[0], jnp.zeros((d, LANES - N_GROUPS - N_EXPERTS), F32)], axis=1)
    wr_hi = wr.astype(BF16)
    wr = jnp.concatenate([wr_hi, (wr - wr_hi.astype(F32)).astype(BF16)], axis=1)
    br = jnp.concatenate(
        [b_router_group[0], b_router_expert[0], jnp.zeros((LANES - N_GROUPS - N_EXPERTS,), F32)])[None, :]

    hg4, lat = _in_proj(xt, attn_norm, w_hg, w_lat, 1024, 1024)
    q, kt, v = _mla_up(lat, q_norm, kv_norm, wq_pad, wknt, wv, cc, ss, 512)
    o_mla = _attention(q, kt, v, mla_norm, 512)
    mall, lv = _hgrn_tables()
    o_hg = _hgrn(hg4, lb, hg_norm[0].reshape(1, D_HG), mall, lv)

    h1, hn, rt = _out_route(xt, o_mla, o_hg, wo[:D_MLA], wo[D_MLA:], ffn_norm, wr, br, 512)
    block_e, nvalid, slot_dst, nused = _dispatch(rt, s)
    y = _moe(block_e, nvalid, slot_dst, nused, hn, w_exp_gate[0], w_exp_up[0], w_exp_down[0])

    out = _ple_final(h1, rt, y, p[0, 0], w_ple_gate[0].astype(BF16), b_ple_gate,
                     w_ple_proj[0].astype(BF16), ple_norm, final_norm[None, :], 512)
    return out[None]
```

```python
import functools

import jax
import jax.numpy as jnp
import numpy as np
from jax import lax
from jax.experimental import pallas as pl
from jax.experimental.pallas import tpu as pltpu

F32 = jnp.float32
BF16 = jnp.bfloat16

D_MODEL = 2048
PLE_DIM = 256
MLA_HEADS = 8
QK_NOPE = 128
QK_ROPE = 64
QK_HEAD = QK_NOPE + QK_ROPE
QK_PAD = 256
V_HEAD = 128
V_PAD = 2 * V_HEAD
QK_AHEAD = 2
PV_LAG = 1
LAG_LIMIT = 60.0
FLOOR_LIMIT = 100.0
Q_LORA = 512
KV_LORA = 256
ROPE_THETA = 10000.0
HG_HEADS = 8
HG_DK = 128
HG_DV = 128
HG_CHUNK = 64
D_MLA = MLA_HEADS * V_HEAD
D_HG = HG_HEADS * HG_DV
N_GROUPS = 8
EXPERTS_PER_GROUP = 8
N_EXPERTS = N_GROUPS * EXPERTS_PER_GROUP
TOP_K = 2
D_EXPERT = 512
EPS = 1e-6
LANES = 128
D_LAT = Q_LORA + KV_LORA + 2 * QK_ROPE
NEG_BIG = -1e30
LOG2E = 1.4426950408889634

MOE_BLOCK = 320
ROW_CHUNK = 32
ROW_DMA_PRIORITY = 0
HG_ROWS = 256
HG_LEVELS = (32, 16, 8, 4, 2, 1)
HG_MM_LEVELS = (2, 1)
HG_NH = 4


def _cparams(sem, vmem_mb=None, flags=None):
    kw = dict(dimension_semantics=sem)
    if vmem_mb is not None:
        kw["vmem_limit_bytes"] = vmem_mb * 1024 * 1024
    if flags is not None:
        kw["flags"] = flags
    return pltpu.CompilerParams(**kw)


def _rms(x, g):
    ms = jnp.mean(x * x, axis=-1, keepdims=True)
    return x * lax.rsqrt(ms + EPS) * g


def _sigmoid(x):
    return 1.0 / (1.0 + jnp.exp(-x))


def _pack_halves(x):
    n = x.shape[1] // 2
    bits = lax.bitcast_convert_type(x.astype(BF16).astype(F32), jnp.uint32)
    return bits[:, :n] | (bits[:, n:] >> 16)


def _unpack_halves(w):
    hi = lax.bitcast_convert_type(w & jnp.uint32(0xFFFF0000), F32)
    lo = lax.bitcast_convert_type(w << 16, F32)
    return jnp.concatenate([hi, lo], axis=1)


def _in_proj_kernel(x_ref, g_ref, whg_ref, wlat_ref, hg_ref, lat_ref, xn_ref):
    j = pl.program_id(1)
    last = pl.num_programs(1) - 1

    @pl.when(j == 0)
    def _():
        xn_ref[...] = _rms(x_ref[...], g_ref[...]).astype(BF16)

    @pl.when(j < last)
    def _():
        hg_ref[...] = jnp.dot(xn_ref[...], whg_ref[...], preferred_element_type=F32).astype(hg_ref.dtype)

    @pl.when(j == last)
    def _():
        lat_ref[...] = jnp.dot(xn_ref[...], wlat_ref[...], preferred_element_type=F32)


def _in_proj(x, gain, w_hg, w_lat, tm, tn):
    s, d = x.shape
    nt = w_hg.shape[1] // tn
    nl = w_lat.shape[1]
    hg_tile = lambda i, j: (i, jnp.minimum(j, nt - 1))
    return pl.pallas_call(
        _in_proj_kernel,
        grid=(s // tm, nt + 1),
        in_specs=[
            pl.BlockSpec((tm, d), lambda i, j: (i, 0)),
            pl.BlockSpec((1, d), lambda i, j: (0, 0)),
            pl.BlockSpec((d, tn), lambda i, j: (0, jnp.minimum(j, nt - 1))),
            pl.BlockSpec((d, nl), lambda i, j: (0, 0), pipeline_mode=pl.Buffered(1)),
        ],
        out_specs=[
            pl.BlockSpec((tm, tn), hg_tile),
            pl.BlockSpec((tm, nl), lambda i, j: (i, 0)),
        ],
        out_shape=[
            jax.ShapeDtypeStruct((s, nt * tn), BF16),
            jax.ShapeDtypeStruct((s, nl), F32),
        ],
        scratch_shapes=[pltpu.VMEM((tm, d), BF16)],
        compiler_params=_cparams(("parallel", "arbitrary"), 56),
        name="in_proj",
    )(x, gain, w_hg, w_lat)


def _mla_up_kernel(lat_ref, qn_ref, kvn_ref, wq_ref, wknt_ref, wv_ref, cc_ref, ss_ref,
                   q_ref, kt_ref, v_ref):
    lat = lat_ref[...]
    cq = _rms(lat[:, :Q_LORA], qn_ref[...]).astype(BF16)
    ckv = _rms(lat[:, Q_LORA:Q_LORA + KV_LORA], kvn_ref[...]).astype(BF16)
    kp = lat[:, Q_LORA + KV_LORA:D_LAT]
    cc = cc_ref[...]
    ss = ss_ref[...]
    scale = QK_HEAD ** -0.5 * LOG2E

    q = jnp.dot(cq, wq_ref[...], preferred_element_type=F32)
    for h in range(MLA_HEADS):
        base = h * QK_PAD
        q_ref[:, base:base + QK_NOPE] = (q[:, base:base + QK_NOPE] * scale).astype(BF16)
        r = q[:, base + QK_NOPE:base + QK_PAD]
        r = (r * cc + pltpu.roll(r, QK_ROPE, 1) * ss) * scale
        q_ref[:, base + QK_NOPE:base + QK_PAD] = r.astype(BF16)

    kpe_t = (kp * cc + pltpu.roll(kp, QK_ROPE, 1) * ss).T.astype(BF16)
    kn_t = lax.dot_general(wknt_ref[...], ckv, (((1,), (1,)), ((), ())), preferred_element_type=F32).astype(BF16)
    for h in range(MLA_HEADS):
        base = h * QK_PAD
        kt_ref[base:base + QK_NOPE, :] = kn_t[h * QK_NOPE:(h + 1) * QK_NOPE]
        kt_ref[base + QK_NOPE:base + QK_PAD, :] = kpe_t
    v_ref[...] = jnp.dot(ckv, wv_ref[...], preferred_element_type=F32).astype(BF16)


def _mla_up(lat, qn, kvn, wq, wknt, wv, cc, ss, tm):
    s = lat.shape[0]
    row = lambda i: (i, 0)
    full = lambda i: (0, 0)
    return pl.pallas_call(
        _mla_up_kernel,
        grid=(s // tm,),
        in_specs=[
            pl.BlockSpec((tm, D_LAT), row),
            pl.BlockSpec((1, Q_LORA), full),
            pl.BlockSpec((1, KV_LORA), full),
            pl.BlockSpec(wq.shape, full),
            pl.BlockSpec(wknt.shape, full),
            pl.BlockSpec(wv.shape, full),
            pl.BlockSpec((tm, LANES), row),
            pl.BlockSpec((tm, LANES), row),
        ],
        out_specs=[
            pl.BlockSpec((tm, MLA_HEADS * QK_PAD), row),
            pl.BlockSpec((MLA_HEADS * QK_PAD, tm), lambda i: (0, i)),
            pl.BlockSpec((tm, D_MLA), row),
        ],
        out_shape=[
            jax.ShapeDtypeStruct((s, MLA_HEADS * QK_PAD), BF16),
            jax.ShapeDtypeStruct((MLA_HEADS * QK_PAD, s), BF16),
            jax.ShapeDtypeStruct((s, D_MLA), BF16),
        ],
        compiler_params=_cparams(("parallel",), 48),
        name="mla_up",
    )(lat, qn, kvn, wq, wknt, wv, cc, ss)


def _attn_kernel(it_ref, jt_ref, q_ref, kt_ref, v_ref, g_ref, o_ref, acc_ref, m_ref, redo_ref, *, tb):
    t = pl.program_id(0)
    i = it_ref[t]
    j = jt_ref[t]
    src = t % 2
    dst = 1 - src
    ntile = tb // LANES

    @pl.when(j == 0)
    def _():
        m_ref[...] = jnp.zeros(m_ref.shape, F32)
        acc_ref[src] = jnp.zeros(acc_ref.shape[1:], F32)

    def scores(h):
        return jnp.dot(q_ref[:, h * QK_PAD:(h + 1) * QK_PAD], kt_ref[h * QK_PAD:(h + 1) * QK_PAD, :],
                       preferred_element_type=F32)

    ones = jnp.ones((tb, V_PAD - V_HEAD), BF16)

    def values(h):
        return jnp.concatenate([v_ref[:, h * V_HEAD:(h + 1) * V_HEAD], ones], axis=1)

    def key_minus_query():
        return (lax.broadcasted_iota(jnp.int32, (tb, tb), 1) - lax.broadcasted_iota(jnp.int32, (tb, tb), 0))

    def lagged(masked, first=False):
        if masked:
            keep = key_minus_query() <= (jnp.where(i == 0, 0, tb) if first else 0)
        ahead = [scores(h) for h in range(min(QK_AHEAD, MLA_HEADS))]
        pending = []
        worst = None
        least = None

        def accumulate(h, p):
            acc_ref[dst, h] = acc_ref[src, h] + jnp.dot(p, values(h), preferred_element_type=F32)

        for h in range(MLA_HEADS):
            s = ahead.pop(0)
            if h + QK_AHEAD < MLA_HEADS:
                ahead.append(scores(h + QK_AHEAD))
            d = s - jnp.concatenate([m_ref[h]] * ntile, axis=1)
            if first:
                low = jnp.min((jnp.where(keep, d, 0.0) if masked else d).reshape(tb // 8, 8, tb), axis=0)
                least = low if least is None else jnp.minimum(least, low)
            if masked:
                d = jnp.where(keep, d, NEG_BIG)
            top = jnp.max(d.reshape(tb // 8, 8, tb), axis=0)
            worst = top if worst is None else jnp.maximum(worst, top)
            pending.append((h, jnp.exp2(d).astype(BF16)))
            if len(pending) > PV_LAG:
                accumulate(*pending.pop(0))
        for item in pending:
            accumulate(*item)
        redo = jnp.max(worst) > LAG_LIMIT
        if first:
            redo = redo | (jnp.min(least) < -FLOOR_LIMIT)
        redo_ref[0] = redo.astype(jnp.int32)

    def exact():
        keep = key_minus_query() <= jnp.where(j == i, 0, tb)
        ahead = [scores(h) for h in range(min(QK_AHEAD, MLA_HEADS))]
        for h in range(MLA_HEADS):
            s = jnp.where(keep, ahead.pop(0), NEG_BIG)
            if h + QK_AHEAD < MLA_HEADS:
                ahead.append(scores(h + QK_AHEAD))
            m_prev = jnp.where(j == 0, NEG_BIG, m_ref[h])
            m_new = jnp.maximum(m_prev, jnp.max(s, axis=1, keepdims=True))
            alpha = jnp.exp2(m_prev - m_new)
            p = jnp.exp2(s - jnp.concatenate([m_new] * ntile, axis=1)).astype(BF16)
            m_ref[h] = m_new
            acc_ref[dst, h] = (acc_ref[src, h] * jnp.concatenate([alpha] * (V_PAD // LANES), axis=1)
                               + jnp.dot(p, values(h), preferred_element_type=F32))

    @pl.when(j == 0)
    def _():
        lagged(True, first=True)

    @pl.when((j > 0) & (j < i))
    def _():
        lagged(False)

    @pl.when((j > 0) & (j == i))
    def _():
        lagged(True)

    @pl.when(redo_ref[0] != 0)
    def _():
        exact()

    @pl.when(j == i)
    def _():
        outs = [acc_ref[dst, h, :, :V_HEAD] / acc_ref[dst, h, :, V_HEAD:] for h in range(MLA_HEADS)]
        o_ref[...] = _rms(jnp.concatenate(outs, axis=1), g_ref[...]).astype(o_ref.dtype)


def _attention(q, kt, v, gain, tb):
    s = q.shape[0]
    nq = s // tb
    it = np.concatenate([np.full(i + 1, i, np.int32) for i in range(nq)])
    jt = np.concatenate([np.arange(i + 1, dtype=np.int32) for i in range(nq)])
    grid_spec = pltpu.PrefetchScalarGridSpec(
        num_scalar_prefetch=2,
        grid=(it.shape[0],),
        in_specs=[
            pl.BlockSpec((tb, MLA_HEADS * QK_PAD), lambda t, it, jt: (it[t], 0)),
            pl.BlockSpec((MLA_HEADS * QK_PAD, tb), lambda t, it, jt: (0, jt[t])),
            pl.BlockSpec((tb, D_MLA), lambda t, it, jt: (jt[t], 0)),
            pl.BlockSpec((1, D_MLA), lambda t, it, jt: (0, 0)),
        ],
        out_specs=pl.BlockSpec((tb, D_MLA), lambda t, it, jt: (it[t], 0)),
        scratch_shapes=[
            pltpu.VMEM((2, MLA_HEADS, tb, V_PAD), F32),
            pltpu.VMEM((MLA_HEADS, tb, LANES), F32),
            pltpu.SMEM((1,), jnp.int32),
        ],
    )
    return pl.pallas_call(
        functools.partial(_attn_kernel, tb=tb),
        grid_spec=grid_spec,
        out_shape=jax.ShapeDtypeStruct((s, D_MLA), BF16),
        compiler_params=_cparams(("arbitrary",), 48),
        name="mla_attention",
    )(jnp.asarray(it), jnp.asarray(jt), q, kt, v, gain)


def _hgrn_tables():
    n = HG_ROWS
    r = np.arange(n)
    c = np.arange(n)
    same = (r[:, None] // HG_CHUNK) == (c[None, :] // HG_CHUNK)

    def rows_upto(idx):
        return (same & (c[None, :] <= idx[:, None])).astype(np.float32)

    blocks = [rows_upto(r)]
    for m in HG_MM_LEVELS:
        blocks.append(rows_upto((r // (2 * m)) * (2 * m) + m))
    mall = np.concatenate(blocks, axis=0)

    x = r[:, None] ^ c[None, :]
    lv = np.full((n, n), -1, np.int32)
    for li, m in enumerate(HG_LEVELS):
        lv = np.where(same & (r[:, None] > c[None, :]) & (x >= m) & (x < 2 * m), li, lv)
    lv = np.where(r[:, None] == c[None, :], len(HG_LEVELS), lv)
    return jnp.asarray(mall, BF16), jnp.asarray(lv, jnp.int32)


def _hgrn_kernel(q_ref, f_ref, i_ref, g_ref, lb_ref, gn_ref, mall_ref, lv_ref, o_ref, st_ref):
    t = pl.program_id(1)
    n = HG_ROWS

    @pl.when(t == 0)
    def _():
        st_ref[...] = jnp.zeros(st_ref.shape, F32)

    rowid = lax.broadcasted_iota(jnp.int32, (n, 1), 0)
    lv = lv_ref[...]

    def head(hh):
        cols = slice(hh * HG_DK, (hh + 1) * HG_DK)
        q_in = q_ref[:, cols].astype(F32)
        qs = q_in * _sigmoid(q_in)
        lb = lb_ref[:, cols]
        f = lb + (1.0 - lb) * _sigmoid(f_ref[:, cols].astype(F32))
        kk = 1.0 - f
        logf = jnp.log(f)
        iv = i_ref[:, cols]

        l1 = logf.astype(BF16)
        l2 = (logf - l1.astype(F32)).astype(BF16)
        parts = jnp.dot(mall_ref[...], jnp.concatenate([l1, l2], axis=1), preferred_element_type=F32)
        yield
        bc = parts[:, :HG_DK] + parts[:, HG_DK:]
        b = bc[:n]

        def anchor(period, row):
            b3 = b.reshape(n // period, period, HG_DK)
            return jnp.broadcast_to(b3[:, row:row + 1, :], b3.shape).reshape(n, HG_DK)

        a = jnp.zeros((n, n), F32)
        for li, m in enumerate(HG_LEVELS):
            if m in HG_MM_LEVELS:
                k = 1 + HG_MM_LEVELS.index(m)
                c = bc[k * n:(k + 1) * n]
            else:
                c = anchor(2 * m, m)
            e = jnp.exp(-jnp.abs(b - c))
            upper = (rowid & (2 * m - 1)) >= m
            x = (jnp.where(upper, qs, kk) * e).astype(BF16)
            p = lax.dot_general(x, x, (((1,), (1,)), ((), ())), preferred_element_type=F32)
            yield
            a = jnp.where(lv == li, p, a)
        a = jnp.where(lv == len(HG_LEVELS), jnp.sum(qs * kk, axis=-1, keepdims=True), a)
        o = jnp.dot(a.astype(BF16), iv, preferred_element_type=F32)
        yield

        blast = anchor(HG_CHUNK, HG_CHUNK - 1)
        qd = (qs * jnp.exp(b)).astype(BF16)
        kd = kk * jnp.exp(blast - b)
        ivt = iv.astype(F32).T.astype(BF16)
        st = st_ref[hh]
        outs = []
        for ci in range(n // HG_CHUNK):
            lo = ci * HG_CHUNK
            inter = lax.dot_general(qd[lo:lo + HG_CHUNK], st.astype(BF16), (((1,), (1,)), ((), ())),
                                    preferred_element_type=F32)
            kd_c = jnp.where((rowid >= lo) & (rowid < lo + HG_CHUNK), kd, 0.0).astype(BF16)
            upd = jnp.dot(ivt, kd_c, preferred_element_type=F32)
            yield
            outs.append(o[lo:lo + HG_CHUNK] + inter)
            st = jnp.exp(blast[lo:lo + 1]) * st + upd
        st_ref[hh] = st
        o = jnp.concatenate(outs, axis=0)

        g_in = g_ref[:, cols].astype(F32)
        o_ref[:, cols] = (_rms(o, gn_ref[:, cols]) * (g_in * _sigmoid(g_in))).astype(o_ref.dtype)

    progs = [head(hh) for hh in range(HG_NH)]
    while progs:
        progs = [pr for pr in progs if next(pr, "done") != "done"]


def _hgrn(hg4, lb, gn, mall, lv):
    s = hg4.shape[0]
    groups = HG_HEADS // HG_NH
    width = HG_NH * HG_DK

    def col(c):
        return lambda hh, t: (t, c * groups + hh)

    head = lambda hh, t: (0, hh)
    const = lambda hh, t: (0, 0)
    return pl.pallas_call(
        _hgrn_kernel,
        grid=(groups, s // HG_ROWS),
        in_specs=[
            pl.BlockSpec((HG_ROWS, width), col(0)),
            pl.BlockSpec((HG_ROWS, width), col(1)),
            pl.BlockSpec((HG_ROWS, width), col(2)),
            pl.BlockSpec((HG_ROWS, width), col(3)),
            pl.BlockSpec((1, width), head),
            pl.BlockSpec((1, width), head),
            pl.BlockSpec(mall.shape, const),
            pl.BlockSpec(lv.shape, const),
        ],
        out_specs=pl.BlockSpec((HG_ROWS, width), lambda hh, t: (t, hh)),
        out_shape=jax.ShapeDtypeStruct((s, D_HG), BF16),
        scratch_shapes=[pltpu.VMEM((HG_NH, HG_DV, HG_DK), F32)],
        compiler_params=_cparams(("parallel", "arbitrary"), 32),
        name="hgrn2",
    )(hg4, hg4, hg4, hg4, lb, gn, mall, lv)


def _out_route_kernel(x_ref, om_ref, oh_ref, wa_ref, wb_ref, g_ref, wr_ref, br_ref,
                      h_ref, hn_ref, rt_ref):
    h1 = (x_ref[...]
          + jnp.dot(om_ref[...], wa_ref[...], preferred_element_type=F32)
          + jnp.dot(oh_ref[...], wb_ref[...], preferred_element_type=F32))
    h_ref[...] = h1
    hn = _rms(h1, g_ref[...])
    hn_ref[...] = _pack_halves(hn)
    hn_hi = hn.astype(BF16)
    hn_lo = (hn - hn_hi.astype(F32)).astype(BF16)
    hh = jnp.dot(hn_hi, wr_ref[...], preferred_element_type=F32)
    lh = jnp.dot(hn_lo, wr_ref[:, :LANES], preferred_element_type=F32)
    logits = hh[:, :LANES] + (hh[:, LANES:] + lh) + br_ref[...]

    lane = lax.broadcasted_iota(jnp.int32, logits.shape, 1)
    lanef = lane.astype(F32)
    ninf = -jnp.inf
    big = float(LANES)

    is_g = lane < N_GROUPS
    gl = jnp.where(is_g, logits, ninf)
    gmax = jnp.max(gl, axis=-1, keepdims=True)
    gsum = jnp.sum(jnp.where(is_g, jnp.exp(gl - gmax), 0.0), axis=-1, keepdims=True)
    g_w = 1.0 / gsum
    g_idx = jnp.min(jnp.where(gl == gmax, lanef, big), axis=-1, keepdims=True)

    e_lane = lane - N_GROUPS
    in_grp = (e_lane >= 0) & (e_lane < N_EXPERTS) & ((e_lane >> 3).astype(F32) == g_idx)
    el = jnp.where(in_grp, logits, ninf)
    emax = jnp.max(el, axis=-1, keepdims=True)
    esum = jnp.sum(jnp.where(in_grp, jnp.exp(el - emax), 0.0), axis=-1, keepdims=True)
    i1 = jnp.min(jnp.where(el == emax, lanef, big), axis=-1, keepdims=True)
    el2 = jnp.where(lanef == i1, ninf, el)
    emax2 = jnp.max(el2, axis=-1, keepdims=True)
    i2 = jnp.min(jnp.where(el2 == emax2, lanef, big), axis=-1, keepdims=True)
    p1 = 1.0 / esum
    p2 = jnp.exp(emax2 - emax) / esum
    w1 = g_w * p1 / (p1 + p2)
    w2 = g_w * p2 / (p1 + p2)

    rt = jnp.where(lane == 0, i1 - N_GROUPS,
                   jnp.where(lane == 1, i2 - N_GROUPS,
                             jnp.where(lane == 2, w1, jnp.where(lane == 3, w2, 0.0))))
    rt_ref[...] = rt


def _out_route(x, o_mla, o_hg, wa, wb, gain, wr, br, tm):
    s, d = x.shape
    row = lambda i: (i, 0)
    full = lambda i: (0, 0)
    return pl.pallas_call(
        _out_route_kernel,
        grid=(s // tm,),
        in_specs=[
            pl.BlockSpec((tm, d), row),
            pl.BlockSpec((tm, D_MLA), row),
            pl.BlockSpec((tm, D_HG), row),
            pl.BlockSpec(wa.shape, full, pipeline_mode=pl.Buffered(1)),
            pl.BlockSpec(wb.shape, full, pipeline_mode=pl.Buffered(1)),
            pl.BlockSpec((1, d), full),
            pl.BlockSpec(wr.shape, full, pipeline_mode=pl.Buffered(1)),
            pl.BlockSpec((1, LANES), full),
        ],
        out_specs=[
            pl.BlockSpec((tm, d), row),
            pl.BlockSpec((tm, d // 2), row),
            pl.BlockSpec((tm, LANES), row),
        ],
        out_shape=[
            jax.ShapeDtypeStruct((s, d), F32),
            jax.ShapeDtypeStruct((s, d // 2), jnp.uint32),
            jax.ShapeDtypeStruct((s, LANES), F32),
        ],
        compiler_params=_cparams(("parallel",), 56),
        name="out_proj_route",
    )(x, o_mla, o_hg, wa, wb, gain, wr, br)


def _moe_kernel(be_ref, nv_ref, dst_ref, nu_ref, hn_hbm, wg_hbm, wu_hbm, wd_hbm, y_hbm,
                xbuf, xsem, obuf, osem, wgb, wub, wdb, wsem, wslot, *, n_tok):
    i = pl.program_id(0)
    nused = nu_ref[0]
    e = be_ref[i]
    active = i < nused
    first = (i == 0) | (e != be_ref[jnp.maximum(i - 1, 0)])

    def chunks(blk, fn):
        nv = nv_ref[blk]
        for lo in range(0, MOE_BLOCK, ROW_CHUNK):
            pl.when(lo < nv)(functools.partial(fn, lo))

    def gather_start(blk, sl):
        def issue(lo):
            for r in range(lo, lo + ROW_CHUNK):
                tok = dst_ref[blk * MOE_BLOCK + r] & (n_tok - 1)
                pltpu.make_async_copy(hn_hbm.at[pl.ds(tok, 1), :], xbuf.at[sl, pl.ds(r, 1), :],
                                      xsem.at[sl]).start(priority=ROW_DMA_PRIORITY)
        chunks(blk, issue)

    def gather_wait(blk, sl):
        chunks(blk, lambda lo: pltpu.make_async_copy(
            hn_hbm.at[pl.ds(0, ROW_CHUNK), :], xbuf.at[sl, pl.ds(lo, ROW_CHUNK), :], xsem.at[sl]).wait())

    def scatter_start(blk, sl):
        def issue(lo):
            for r in range(lo, lo + ROW_CHUNK):
                dst = dst_ref[blk * MOE_BLOCK + r]
                pltpu.make_async_copy(obuf.at[sl, pl.ds(r, 1), :], y_hbm.at[pl.ds(dst, 1), :],
                                      osem.at[sl]).start(priority=ROW_DMA_PRIORITY)
        chunks(blk, issue)

    def scatter_wait(blk, sl):
        chunks(blk, lambda lo: pltpu.make_async_copy(
            obuf.at[sl, pl.ds(lo, ROW_CHUNK), :], y_hbm.at[pl.ds(0, ROW_CHUNK), :], osem.at[sl]).wait())

    def spare_copy(sl):
        return pltpu.make_async_copy(obuf.at[sl], y_hbm.at[pl.ds(TOP_K * n_tok + sl * MOE_BLOCK, MOE_BLOCK), :],
                                     osem.at[sl])

    def weight_copies(ex, sl):
        return (pltpu.make_async_copy(wg_hbm.at[ex], wgb.at[sl], wsem.at[sl, 0]),
                pltpu.make_async_copy(wu_hbm.at[ex], wub.at[sl], wsem.at[sl, 1]),
                pltpu.make_async_copy(wd_hbm.at[ex], wdb.at[sl], wsem.at[sl, 2]))

    @pl.when((i == 0) & active)
    def _():
        wslot[0] = 1
        for cp in weight_copies(e, 0):
            cp.start(priority=1 - ROW_DMA_PRIORITY)
        xbuf[...] = jnp.zeros(xbuf.shape, jnp.uint32)
        gather_start(0, 0)
        obuf[...] = jnp.zeros(obuf.shape, jnp.uint32)
        for sl in (0, 1):
            spare_copy(sl).start()
        for sl in (0, 1):
            spare_copy(sl).wait()

    @pl.when(active & first)
    def _():
        sl = 1 - wslot[0]
        wslot[0] = sl
        nxt = lax.while_loop(lambda j: (j < nused) & (be_ref[jnp.minimum(j, nused - 1)] == e),
                             lambda j: j + 1, i + 1)

        @pl.when(nxt < nused)
        def _():
            for cp in weight_copies(be_ref[nxt], 1 - sl):
                cp.start(priority=1 - ROW_DMA_PRIORITY)

        for cp in weight_copies(e, sl):
            cp.wait()

    def block(xs):
        sl = wslot[0]

        @pl.when(i >= 2)
        def _():
            scatter_wait(i - 2, xs)

        gather_wait(i, xs)

        @pl.when(i + 1 < nused)
        def _():
            gather_start(i + 1, 1 - xs)

        x = _unpack_halves(xbuf[xs]).astype(BF16)
        g = jnp.dot(x, wgb[sl].astype(BF16), preferred_element_type=F32)
        u = jnp.dot(x, wub[sl].astype(BF16), preferred_element_type=F32)
        hmid = (g * _sigmoid(g) * u).astype(BF16)
        obuf[xs] = _pack_halves(jnp.dot(hmid, wdb[sl].astype(BF16), preferred_element_type=F32))
        scatter_start(i, xs)

        @pl.when(i == nused - 1)
        def _():
            @pl.when(i >= 1)
            def _():
                scatter_wait(i - 1, 1 - xs)

            scatter_wait(i, xs)

    for s_ in (0, 1):
        pl.when(active & (i % 2 == s_))(functools.partial(block, s_))


def _moe(block_e, nvalid, slot_dst, nused, hn, wg, wu, wd):
    nb = block_e.shape[0]
    n_tok, dw = hn.shape
    d = 2 * dw
    assert n_tok & (n_tok - 1) == 0
    grid_spec = pltpu.PrefetchScalarGridSpec(
        num_scalar_prefetch=4,
        grid=(nb,),
        in_specs=[pl.BlockSpec(memory_space=pl.ANY)] * 4,
        out_specs=pl.BlockSpec(memory_space=pl.ANY),
        scratch_shapes=[
            pltpu.VMEM((2, MOE_BLOCK, dw), jnp.uint32),
            pltpu.SemaphoreType.DMA((2,)),
            pltpu.VMEM((2, MOE_BLOCK, dw), jnp.uint32),
            pltpu.SemaphoreType.DMA((2,)),
            pltpu.VMEM((2, d, D_EXPERT), F32),
            pltpu.VMEM((2, d, D_EXPERT), F32),
            pltpu.VMEM((2, D_EXPERT, d), F32),
            pltpu.SemaphoreType.DMA((2, 3)),
            pltpu.SMEM((1,), jnp.int32),
        ],
    )
    return pl.pallas_call(
        functools.partial(_moe_kernel, n_tok=n_tok),
        grid_spec=grid_spec,
        out_shape=jax.ShapeDtypeStruct((TOP_K * n_tok + 2 * MOE_BLOCK, dw), jnp.uint32),
        compiler_params=_cparams(("arbitrary",), 56),
        name="moe_experts",
    )(block_e, nvalid, slot_dst, nused, hn, wg, wu, wd)


def _ple_kernel(h_ref, rt_ref, y0_ref, y1_ref, p_ref, wg_ref, bg_ref, wp_ref, gp_ref, gf_ref, o_ref):
    rt = rt_ref[...]
    h2 = h_ref[...] + (rt[:, TOP_K:TOP_K + 1] * _unpack_halves(y0_ref[...])
                       + rt[:, TOP_K + 1:TOP_K + 2] * _unpack_halves(y1_ref[...]))
    hn = _rms(h2, gp_ref[...]).astype(BF16)
    gate = _sigmoid(jnp.dot(hn, wg_ref[...], preferred_element_type=F32) + bg_ref[...])
    pe = jnp.dot(p_ref[...].astype(BF16), wp_ref[...], preferred_element_type=F32)
    h3 = h2 + gate * pe
    o_ref[...] = _rms(h3, gf_ref[...])


def _ple_final(h1, rt, y, p, wg, bg, wp, gp, gf, tm):
    s, d = h1.shape
    nblk = s // tm
    row = lambda i: (i, 0)
    full = lambda i: (0, 0)
    return pl.pallas_call(
        _ple_kernel,
        grid=(nblk,),
        in_specs=[
            pl.BlockSpec((tm, d), row),
            pl.BlockSpec((tm, LANES), row),
            pl.BlockSpec((tm, d // 2), row),
            pl.BlockSpec((tm, d // 2), lambda i: (i + nblk, 0)),
            pl.BlockSpec((tm, PLE_DIM), row),
            pl.BlockSpec(wg.shape, full, pipeline_mode=pl.Buffered(1)),
            pl.BlockSpec((1, d), full),
            pl.BlockSpec(wp.shape, full, pipeline_mode=pl.Buffered(1)),
            pl.BlockSpec((1, d), full),
            pl.BlockSpec((1, d), full),
        ],
        out_specs=pl.BlockSpec((tm, d), row),
        out_shape=jax.ShapeDtypeStruct((s, d), F32),
        compiler_params=_cparams(("parallel",), 56),
        name="ple_final",
    )(h1, rt, y, y, p, wg, bg, wp, gp, gf)


def _slots_kernel(rt_ref, tri_ref, pos_ref, cnt_ref, carry_ref, base_ref):
    ps = pl.program_id(0)
    b = pl.program_id(1)
    rt = rt_ref[...]
    lane = lax.broadcasted_iota(jnp.int32, rt.shape, 1)
    lanef = lane.astype(F32)
    oh = [(lanef == rt[:, kk:kk + 1]).astype(F32) for kk in range(TOP_K)]
    both = oh[0] + oh[1]
    colsum = jnp.sum(both, axis=0, keepdims=True)

    @pl.when((ps == 0) & (b == 0))
    def _():
        cnt_ref[...] = jnp.zeros(cnt_ref.shape, F32)

    @pl.when(ps == 0)
    def _():
        cnt_ref[...] = cnt_ref[...] + colsum

    @pl.when((ps == 1) & (b == 0))
    def _():
        blocks = jnp.floor((cnt_ref[...] + (MOE_BLOCK - 0.5)) / MOE_BLOCK)
        r = lax.broadcasted_iota(jnp.int32, (LANES, LANES), 0)
        c = lax.broadcasted_iota(jnp.int32, (LANES, LANES), 1)
        before = (r < c).astype(F32)
        base_ref[...] = jnp.dot(blocks * MOE_BLOCK, before, preferred_element_type=F32,
                                precision=lax.Precision.HIGHEST)
        carry_ref[...] = jnp.zeros(carry_ref.shape, F32)

    @pl.when(ps == 1)
    def _():
        earlier = jnp.dot(tri_ref[...], both.astype(BF16), preferred_element_type=F32)
        row = earlier + (base_ref[0:1, :] + carry_ref[0:1, :])
        s0 = jnp.sum(oh[0] * row, axis=-1, keepdims=True)
        s1 = jnp.sum(oh[1] * row, axis=-1, keepdims=True)
        pos_ref[...] = jnp.where(lane == 0, s0, jnp.where(lane == 1, s1, 0.0))
        carry_ref[...] = carry_ref[...] + colsum


def _slots(rt, tb):
    s = rt.shape[0]
    tri = (np.arange(tb)[:, None] > np.arange(tb)[None, :]).astype(np.float32)
    return pl.pallas_call(
        _slots_kernel,
        grid=(2, s // tb),
        in_specs=[
            pl.BlockSpec((tb, LANES), lambda ps, b: (b, 0)),
            pl.BlockSpec((tb, tb), lambda ps, b: (0, 0)),
        ],
        out_specs=[
            pl.BlockSpec((tb, LANES), lambda ps, b: (b * ps, 0)),
            pl.BlockSpec((8, LANES), lambda ps, b: (0, 0)),
        ],
        out_shape=[
            jax.ShapeDtypeStruct((s, LANES), F32),
            jax.ShapeDtypeStruct((8, LANES), F32),
        ],
        scratch_shapes=[pltpu.VMEM((8, LANES), F32), pltpu.VMEM((8, LANES), F32)],
        compiler_params=_cparams(("arbitrary", "arbitrary")),
        name="moe_slots",
    )(rt, jnp.asarray(tri, BF16))


def _dispatch(rt, n_tok):
    a = n_tok * TOP_K
    nb = -(-a // MOE_BLOCK) + N_EXPERTS
    pos_f, cnt = _slots(rt, 1024)
    pos = pos_f[:, :TOP_K].astype(jnp.int32).reshape(-1)
    counts = cnt[0, :N_EXPERTS].astype(jnp.int32)
    nblk = (counts + MOE_BLOCK - 1) // MOE_BLOCK
    bends = jnp.cumsum(nblk)
    bidx = jnp.arange(nb, dtype=jnp.int32)
    block_e = jnp.minimum(jnp.sum((bends[None, :] <= bidx[:, None]).astype(jnp.int32), axis=1), N_EXPERTS - 1)
    nused = bends[-1].astype(jnp.int32)
    nvalid = jnp.clip(counts[block_e] - (bidx - (bends - nblk)[block_e]) * MOE_BLOCK, 0, MOE_BLOCK)
    nvalid = jnp.where(bidx < nused, nvalid, 0).astype(jnp.int32)
    a_idx = jnp.arange(a, dtype=jnp.int32)
    sidx = jnp.arange(nb * MOE_BLOCK, dtype=jnp.int32)
    spare = a + ((sidx // MOE_BLOCK) % 2) * MOE_BLOCK + sidx % MOE_BLOCK
    slot_dst = spare.at[pos].set((a_idx % TOP_K) * n_tok + a_idx // TOP_K,
                                 unique_indices=True, mode="promise_in_bounds")
    return block_e, nvalid, slot_dst, nused.reshape(1)


def kernel(x, p, positions, attn_norm, w_in, q_norm, w_uq, kv_norm, w_ukv, mla_norm, hg_lb_logits, hg_norm, w_out, ffn_norm, w_router_group, b_router_group, w_router_expert, b_router_expert, w_exp_gate, w_exp_up, w_exp_down, ple_norm, w_ple_gate, b_ple_gate, w_ple_proj, final_norm):
    bsz, s, d = x.shape
    assert bsz == 1 and w_in.shape[0] == 1
    xt = x[0]

    inv_freq = 1.0 / (ROPE_THETA ** (jnp.arange(0, QK_ROPE, 2, dtype=F32) / QK_ROPE))
    ang = positions[0].astype(F32)[:, None] * inv_freq
    cos, sin = jnp.cos(ang), jnp.sin(ang)
    zpad = jnp.zeros((s, LANES - QK_ROPE), F32)
    cc = jnp.concatenate([cos, cos, zpad], axis=1)
    ss = jnp.concatenate([-sin, sin, zpad], axis=1)

    lb = jnp.cumsum(jax.nn.softmax(hg_lb_logits.astype(F32), axis=0), axis=0)[0][None, :]

    wi = w_in[0]
    kr0 = Q_LORA + KV_LORA
    half = QK_ROPE // 2
    w_lat = jnp.concatenate(
        [wi[:, :kr0 + QK_ROPE], wi[:, kr0 + half:kr0 + QK_ROPE], wi[:, kr0:kr0 + half]], axis=1).astype(BF16)
    w_hg = wi[:, kr0 + QK_ROPE:].astype(BF16)
    wq3 = w_uq[0].reshape(Q_LORA, MLA_HEADS, QK_HEAD)
    wq_pad = jnp.concatenate(
        [wq3, wq3[:, :, QK_NOPE + half:], wq3[:, :, QK_NOPE:QK_NOPE + half]], axis=2
    ).reshape(Q_LORA, MLA_HEADS * QK_PAD).astype(BF16)
    wkv3 = w_ukv[0].reshape(KV_LORA, MLA_HEADS, QK_NOPE + V_HEAD)
    wknt = wkv3[:, :, :QK_NOPE].reshape(KV_LORA, MLA_HEADS * QK_NOPE).T.astype(BF16)
    wv = wkv3[:, :, QK_NOPE:].reshape(KV_LORA, D_MLA).astype(BF16)
    wo = w_out[0].astype(BF16)
    wr = jnp.concatenate(
        [w_router_group[0], w_router_expert[0], jnp.zeros((d, LANES - N_GROUPS - N_EXPERTS), F32)], axis=1)
    wr_hi = wr.astype(BF16)
    wr = jnp.concatenate([wr_hi, (wr - wr_hi.astype(F32)).astype(BF16)], axis=1)
    br = jnp.concatenate(
        [b_router_group[0], b_router_expert[0], jnp.zeros((LANES - N_GROUPS - N_EXPERTS,), F32)])[None, :]

    hg4, lat = _in_proj(xt, attn_norm, w_hg, w_lat, 1024, 1024)
    q, kt, v = _mla_up(lat, q_norm, kv_norm, wq_pad, wknt, wv, cc, ss, 512)
    o_mla = _attention(q, kt, v, mla_norm, 512)
    mall, lv = _hgrn_tables()
    o_hg = _hgrn(hg4, lb, hg_norm[0].reshape(1, D_HG), mall, lv)

    h1, hn, rt = _out_route(xt, o_mla, o_hg, wo[:D_MLA], wo[D_MLA:], ffn_norm, wr, br, 512)
    block_e, nvalid, slot_dst, nused = _dispatch(rt, s)
    y = _moe(block_e, nvalid, slot_dst, nused, hn, w_exp_gate[0], w_exp_up[0], w_exp_down[0])

    out = _ple_final(h1, rt, y, p[0, 0], w_ple_gate[0].astype(BF16), b_ple_gate,
                     w_ple_proj[0].astype(BF16), ple_norm, final_norm[None, :], 512)
    return out[None]
```

```python
import functools

import jax
import jax.numpy as jnp
import numpy as np
from jax import lax
from jax.experimental import pallas as pl
from jax.experimental.pallas import tpu as pltpu

F32 = jnp.float32
BF16 = jnp.bfloat16

D_MODEL = 2048
PLE_DIM = 256
MLA_HEADS = 8
QK_NOPE = 128
QK_ROPE = 64
QK_HEAD = QK_NOPE + QK_ROPE
QK_PAD = 256
V_HEAD = 128
V_PAD = 2 * V_HEAD
QK_AHEAD = 2
PV_LAG = 1
LAG_LIMIT = 60.0
FLOOR_LIMIT = 100.0
Q_LORA = 512
KV_LORA = 256
ROPE_THETA = 10000.0
HG_HEADS = 8
HG_DK = 128
HG_DV = 128
HG_CHUNK = 64
D_MLA = MLA_HEADS * V_HEAD
D_HG = HG_HEADS * HG_DV
N_GROUPS = 8
EXPERTS_PER_GROUP = 8
N_EXPERTS = N_GROUPS * EXPERTS_PER_GROUP
TOP_K = 2
D_EXPERT = 512
EPS = 1e-6
LANES = 128
D_LAT = Q_LORA + KV_LORA + 2 * QK_ROPE
NEG_BIG = -1e30
LOG2E = 1.4426950408889634

MOE_BLOCK = 320
ROW_CHUNK = 32
ROW_DMA_PRIORITY = 0
HG_ROWS = 256
HG_LEVELS = (32, 16, 8, 4, 2, 1)
HG_MM_LEVELS = (2, 1)
HG_NH = 4


def _cparams(sem, vmem_mb=None, flags=None):
    kw = dict(dimension_semantics=sem)
    if vmem_mb is not None:
        kw["vmem_limit_bytes"] = vmem_mb * 1024 * 1024
    if flags is not None:
        kw["flags"] = flags
    return pltpu.CompilerParams(**kw)


def _rms(x, g):
    ms = jnp.mean(x * x, axis=-1, keepdims=True)
    return x * lax.rsqrt(ms + EPS) * g


def _sigmoid(x):
    return 1.0 / (1.0 + jnp.exp(-x))


def _pack_halves(x):
    n = x.shape[1] // 2
    bits = lax.bitcast_convert_type(x.astype(BF16).astype(F32), jnp.uint32)
    return bits[:, :n] | (bits[:, n:] >> 16)


def _unpack_halves(w):
    hi = lax.bitcast_convert_type(w & jnp.uint32(0xFFFF0000), F32)
    lo = lax.bitcast_convert_type(w << 16, F32)
    return jnp.concatenate([hi, lo], axis=1)


def _in_proj_kernel(x_ref, g_ref, whg_ref, wlat_ref, hg_ref, lat_ref, xn_ref):
    j = pl.program_id(1)
    last = pl.num_programs(1) - 1

    @pl.when(j == 0)
    def _():
        xn_ref[...] = _rms(x_ref[...], g_ref[...]).astype(BF16)

    @pl.when(j < last)
    def _():
        hg_ref[...] = jnp.dot(xn_ref[...], whg_ref[...], preferred_element_type=F32).astype(hg_ref.dtype)

    @pl.when(j == last)
    def _():
        lat_ref[...] = jnp.dot(xn_ref[...], wlat_ref[...], preferred_element_type=F32)


def _in_proj(x, gain, w_hg, w_lat, tm, tn):
    s, d = x.shape
    nt = w_hg.shape[1] // tn
    nl = w_lat.shape[1]
    hg_tile = lambda i, j: (i, jnp.minimum(j, nt - 1))
    return pl.pallas_call(
        _in_proj_kernel,
        grid=(s // tm, nt + 1),
        in_specs=[
            pl.BlockSpec((tm, d), lambda i, j: (i, 0)),
            pl.BlockSpec((1, d), lambda i, j: (0, 0)),
            pl.BlockSpec((d, tn), lambda i, j: (0, jnp.minimum(j, nt - 1))),
            pl.BlockSpec((d, nl), lambda i, j: (0, 0), pipeline_mode=pl.Buffered(1)),
        ],
        out_specs=[
            pl.BlockSpec((tm, tn), hg_tile),
            pl.BlockSpec((tm, nl), lambda i, j: (i, 0)),
        ],
        out_shape=[
            jax.ShapeDtypeStruct((s, nt * tn), BF16),
            jax.ShapeDtypeStruct((s, nl), F32),
        ],
        scratch_shapes=[pltpu.VMEM((tm, d), BF16)],
        compiler_params=_cparams(("parallel", "arbitrary"), 56),
        name="in_proj",
    )(x, gain, w_hg, w_lat)


def _mla_up_kernel(lat_ref, qn_ref, kvn_ref, wq_ref, wknt_ref, wv_ref, cc_ref, ss_ref,
                   q_ref, kt_ref, v_ref):
    lat = lat_ref[...]
    cq = _rms(lat[:, :Q_LORA], qn_ref[...]).astype(BF16)
    ckv = _rms(lat[:, Q_LORA:Q_LORA + KV_LORA], kvn_ref[...]).astype(BF16)
    kp = lat[:, Q_LORA + KV_LORA:D_LAT]
    cc = cc_ref[...]
    ss = ss_ref[...]
    scale = QK_HEAD ** -0.5 * LOG2E

    q = jnp.dot(cq, wq_ref[...], preferred_element_type=F32)
    for h in range(MLA_HEADS):
        base = h * QK_PAD
        q_ref[:, base:base + QK_NOPE] = (q[:, base:base + QK_NOPE] * scale).astype(BF16)
        r = q[:, base + QK_NOPE:base + QK_PAD]
        r = (r * cc + pltpu.roll(r, QK_ROPE, 1) * ss) * scale
        q_ref[:, base + QK_NOPE:base + QK_PAD] = r.astype(BF16)

    kpe_t = (kp * cc + pltpu.roll(kp, QK_ROPE, 1) * ss).T.astype(BF16)
    kn_t = lax.dot_general(wknt_ref[...], ckv, (((1,), (1,)), ((), ())), preferred_element_type=F32).astype(BF16)
    for h in range(MLA_HEADS):
        base = h * QK_PAD
        kt_ref[base:base + QK_NOPE, :] = kn_t[h * QK_NOPE:(h + 1) * QK_NOPE]
        kt_ref[base + QK_NOPE:base + QK_PAD, :] = kpe_t
    v_ref[...] = jnp.dot(ckv, wv_ref[...], preferred_element_type=F32).astype(BF16)


def _mla_up(lat, qn, kvn, wq, wknt, wv, cc, ss, tm):
    s = lat.shape[0]
    row = lambda i: (i, 0)
    full = lambda i: (0, 0)
    return pl.pallas_call(
        _mla_up_kernel,
        grid=(s // tm,),
        in_specs=[
            pl.BlockSpec((tm, D_LAT), row),
            pl.BlockSpec((1, Q_LORA), full),
            pl.BlockSpec((1, KV_LORA), full),
            pl.BlockSpec(wq.shape, full),
            pl.BlockSpec(wknt.shape, full),
            pl.BlockSpec(wv.shape, full),
            pl.BlockSpec((tm, LANES), row),
            pl.BlockSpec((tm, LANES), row),
        ],
        out_specs=[
            pl.BlockSpec((tm, MLA_HEADS * QK_PAD), row),
            pl.BlockSpec((MLA_HEADS * QK_PAD, tm), lambda i: (0, i)),
            pl.BlockSpec((tm, D_MLA), row),
        ],
        out_shape=[
            jax.ShapeDtypeStruct((s, MLA_HEADS * QK_PAD), BF16),
            jax.ShapeDtypeStruct((MLA_HEADS * QK_PAD, s), BF16),
            jax.ShapeDtypeStruct((s, D_MLA), BF16),
        ],
        compiler_params=_cparams(("parallel",), 48),
        name="mla_up",
    )(lat, qn, kvn, wq, wknt, wv, cc, ss)


def _attn_kernel(it_ref, jt_ref, q_ref, kt_ref, v_ref, g_ref, o_ref, acc_ref, m_ref, redo_ref, *, tb):
    t = pl.program_id(0)
    i = it_ref[t]
    j = jt_ref[t]
    src = t % 2
    dst = 1 - src
    ntile = tb // LANES

    @pl.when(j == 0)
    def _():
        m_ref[...] = jnp.zeros(m_ref.shape, F32)
        acc_ref[src] = jnp.zeros(acc_ref.shape[1:], F32)

    def scores(h):
        return jnp.dot(q_ref[:, h * QK_PAD:(h + 1) * QK_PAD], kt_ref[h * QK_PAD:(h + 1) * QK_PAD, :],
                       preferred_element_type=F32)

    ones = jnp.ones((tb, V_PAD - V_HEAD), BF16)

    def values(h):
        return jnp.concatenate([v_ref[:, h * V_HEAD:(h + 1) * V_HEAD], ones], axis=1)

    def key_minus_query():
        return (lax.broadcasted_iota(jnp.int32, (tb, tb), 1) - lax.broadcasted_iota(jnp.int32, (tb, tb), 0))

    def lagged(masked, first=False):
        if masked:
            keep = key_minus_query() <= (jnp.where(i == 0, 0, tb) if first else 0)
        ahead = [scores(h) for h in range(min(QK_AHEAD, MLA_HEADS))]
        pending = []
        worst = None
        least = None

        def accumulate(h, p):
            acc_ref[dst, h] = acc_ref[src, h] + jnp.dot(p, values(h), preferred_element_type=F32)

        for h in range(MLA_HEADS):
            s = ahead.pop(0)
            if h + QK_AHEAD < MLA_HEADS:
                ahead.append(scores(h + QK_AHEAD))
            d = s - jnp.concatenate([m_ref[h]] * ntile, axis=1)
            if first:
                low = jnp.min((jnp.where(keep, d, 0.0) if masked else d).reshape(tb // 8, 8, tb), axis=0)
                least = low if least is None else jnp.minimum(least, low)
            if masked:
                d = jnp.where(keep, d, NEG_BIG)
            top = jnp.max(d.reshape(tb // 8, 8, tb), axis=0)
            worst = top if worst is None else jnp.maximum(worst, top)
            pending.append((h, jnp.exp2(d).astype(BF16)))
            if len(pending) > PV_LAG:
                accumulate(*pending.pop(0))
        for item in pending:
            accumulate(*item)
        redo = jnp.max(worst) > LAG_LIMIT
        if first:
            redo = redo | (jnp.min(least) < -FLOOR_LIMIT)
        redo_ref[0] = redo.astype(jnp.int32)

    def exact():
        keep = key_minus_query() <= jnp.where(j == i, 0, tb)
        ahead = [scores(h) for h in range(min(QK_AHEAD, MLA_HEADS))]
        for h in range(MLA_HEADS):
            s = jnp.where(keep, ahead.pop(0), NEG_BIG)
            if h + QK_AHEAD < MLA_HEADS:
                ahead.append(scores(h + QK_AHEAD))
            m_prev = jnp.where(j == 0, NEG_BIG, m_ref[h])
            m_new = jnp.maximum(m_prev, jnp.max(s, axis=1, keepdims=True))
            alpha = jnp.exp2(m_prev - m_new)
            p = jnp.exp2(s - jnp.concatenate([m_new] * ntile, axis=1)).astype(BF16)
            m_ref[h] = m_new
            acc_ref[dst, h] = (acc_ref[src, h] * jnp.concatenate([alpha] * (V_PAD // LANES), axis=1)
                               + jnp.dot(p, values(h), preferred_element_type=F32))

    @pl.when(j == 0)
    def _():
        lagged(True, first=True)

    @pl.when((j > 0) & (j < i))
    def _():
        lagged(False)

    @pl.when((j > 0) & (j == i))
    def _():
        lagged(True)

    @pl.when(redo_ref[0] != 0)
    def _():
        exact()

    @pl.when(j == i)
    def _():
        outs = [acc_ref[dst, h, :, :V_HEAD] / acc_ref[dst, h, :, V_HEAD:] for h in range(MLA_HEADS)]
        o_ref[...] = _rms(jnp.concatenate(outs, axis=1), g_ref[...]).astype(o_ref.dtype)


def _attention(q, kt, v, gain, tb):
    s = q.shape[0]
    nq = s // tb
    it = np.concatenate([np.full(i + 1, i, np.int32) for i in range(nq)])
    jt = np.concatenate([np.arange(i + 1, dtype=np.int32) for i in range(nq)])
    grid_spec = pltpu.PrefetchScalarGridSpec(
        num_scalar_prefetch=2,
        grid=(it.shape[0],),
        in_specs=[
            pl.BlockSpec((tb, MLA_HEADS * QK_PAD), lambda t, it, jt: (it[t], 0)),
            pl.BlockSpec((MLA_HEADS * QK_PAD, tb), lambda t, it, jt: (0, jt[t])),
            pl.BlockSpec((tb, D_MLA), lambda t, it, jt: (jt[t], 0)),
            pl.BlockSpec((1, D_MLA), lambda t, it, jt: (0, 0)),
        ],
        out_specs=pl.BlockSpec((tb, D_MLA), lambda t, it, jt: (it[t], 0)),
        scratch_shapes=[
            pltpu.VMEM((2, MLA_HEADS, tb, V_PAD), F32),
            pltpu.VMEM((MLA_HEADS, tb, LANES), F32),
            pltpu.SMEM((1,), jnp.int32),
        ],
    )
    return pl.pallas_call(
        functools.partial(_attn_kernel, tb=tb),
        grid_spec=grid_spec,
        out_shape=jax.ShapeDtypeStruct((s, D_MLA), BF16),
        compiler_params=_cparams(("arbitrary",), 48),
        name="mla_attention",
    )(jnp.asarray(it), jnp.asarray(jt), q, kt, v, gain)


def _hgrn_tables():
    n = HG_ROWS
    r = np.arange(n)
    c = np.arange(n)
    same = (r[:, None] // HG_CHUNK) == (c[None, :] // HG_CHUNK)

    def rows_upto(idx):
        return (same & (c[None, :] <= idx[:, None])).astype(np.float32)

    blocks = [rows_upto(r)]
    for m in HG_MM_LEVELS:
        blocks.append(rows_upto((r // (2 * m)) * (2 * m) + m))
    mall = np.concatenate(blocks, axis=0)

    x = r[:, None] ^ c[None, :]
    lv = np.full((n, n), -1, np.int32)
    for li, m in enumerate(HG_LEVELS):
        lv = np.where(same & (r[:, None] > c[None, :]) & (x >= m) & (x < 2 * m), li, lv)
    lv = np.where(r[:, None] == c[None, :], len(HG_LEVELS), lv)
    return jnp.asarray(mall, BF16), jnp.asarray(lv, jnp.int32)


def _hgrn_kernel(q_ref, f_ref, i_ref, g_ref, lb_ref, gn_ref, mall_ref, lv_ref, o_ref, st_ref):
    t = pl.program_id(1)
    n = HG_ROWS

    @pl.when(t == 0)
    def _():
        st_ref[...] = jnp.zeros(st_ref.shape, F32)

    rowid = lax.broadcasted_iota(jnp.int32, (n, 1), 0)
    lv = lv_ref[...]

    def head(hh):
        cols = slice(hh * HG_DK, (hh + 1) * HG_DK)
        q_in = q_ref[:, cols].astype(F32)
        qs = q_in * _sigmoid(q_in)
        lb = lb_ref[:, cols]
        f = lb + (1.0 - lb) * _sigmoid(f_ref[:, cols].astype(F32))
        kk = 1.0 - f
        logf = jnp.log(f)
        iv = i_ref[:, cols]

        l1 = logf.astype(BF16)
        l2 = (logf - l1.astype(F32)).astype(BF16)
        parts = jnp.dot(mall_ref[...], jnp.concatenate([l1, l2], axis=1), preferred_element_type=F32)
        yield
        bc = parts[:, :HG_DK] + parts[:, HG_DK:]
        b = bc[:n]

        def anchor(period, row):
            b3 = b.reshape(n // period, period, HG_DK)
            return jnp.broadcast_to(b3[:, row:row + 1, :], b3.shape).reshape(n, HG_DK)

        a = jnp.zeros((n, n), F32)
        for li, m in enumerate(HG_LEVELS):
            if m in HG_MM_LEVELS:
                k = 1 + HG_MM_LEVELS.index(m)
                c = bc[k * n:(k + 1) * n]
            else:
                c = anchor(2 * m, m)
            e = jnp.exp(-jnp.abs(b - c))
            upper = (rowid & (2 * m - 1)) >= m
            x = (jnp.where(upper, qs, kk) * e).astype(BF16)
            p = lax.dot_general(x, x, (((1,), (1,)), ((), ())), preferred_element_type=F32)
            yield
            a = jnp.where(lv == li, p, a)
        a = jnp.where(lv == len(HG_LEVELS), jnp.sum(qs * kk, axis=-1, keepdims=True), a)
        o = jnp.dot(a.astype(BF16), iv, preferred_element_type=F32)
        yield

        blast = anchor(HG_CHUNK, HG_CHUNK - 1)
        qd = (qs * jnp.exp(b)).astype(BF16)
        kd = kk * jnp.exp(blast - b)
        ivt = iv.astype(F32).T.astype(BF16)
        st = st_ref[hh]
        outs = []
        for ci in range(n // HG_CHUNK):
            lo = ci * HG_CHUNK
            inter = lax.dot_general(qd[lo:lo + HG_CHUNK], st.astype(BF16), (((1,), (1,)), ((), ())),
                                    preferred_element_type=F32)
            kd_c = jnp.where((rowid >= lo) & (rowid < lo + HG_CHUNK), kd, 0.0).astype(BF16)
            upd = jnp.dot(ivt, kd_c, preferred_element_type=F32)
            yield
            outs.append(o[lo:lo + HG_CHUNK] + inter)
            st = jnp.exp(blast[lo:lo + 1]) * st + upd
        st_ref[hh] = st
        o = jnp.concatenate(outs, axis=0)

        g_in = g_ref[:, cols].astype(F32)
        o_ref[:, cols] = (_rms(o, gn_ref[:, cols]) * (g_in * _sigmoid(g_in))).astype(o_ref.dtype)

    progs = [head(hh) for hh in range(HG_NH)]
    while progs:
        progs = [pr for pr in progs if next(pr, "done") != "done"]


def _hgrn(hg4, lb, gn, mall, lv):
    s = hg4.shape[0]
    groups = HG_HEADS // HG_NH
    width = HG_NH * HG_DK

    def col(c):
        return lambda hh, t: (t, c * groups + hh)

    head = lambda hh, t: (0, hh)
    const = lambda hh, t: (0, 0)
    return pl.pallas_call(
        _hgrn_kernel,
        grid=(groups, s // HG_ROWS),
        in_specs=[
            pl.BlockSpec((HG_ROWS, width), col(0)),
            pl.BlockSpec((HG_ROWS, width), col(1)),
            pl.BlockSpec((HG_ROWS, width), col(2)),
            pl.BlockSpec((HG_ROWS, width), col(3)),
            pl.BlockSpec((1, width), head),
            pl.BlockSpec((1, width), head),
            pl.BlockSpec(mall.shape, const),
            pl.BlockSpec(lv.shape, const),
        ],
        out_specs=pl.BlockSpec((HG_ROWS, width), lambda hh, t: (t, hh)),
        out_shape=jax.ShapeDtypeStruct((s, D_HG), BF16),
        scratch_shapes=[pltpu.VMEM((HG_NH, HG_DV, HG_DK), F32)],
        compiler_params=_cparams(("parallel", "arbitrary"), 32),
        name="hgrn2",
    )(hg4, hg4, hg4, hg4, lb, gn, mall, lv)


def _out_route_kernel(x_ref, om_ref, oh_ref, wa_ref, wb_ref, g_ref, wr_ref, br_ref,
                      h_ref, hn_ref, rt_ref):
    h1 = (x_ref[...]
          + jnp.dot(om_ref[...], wa_ref[...], preferred_element_type=F32)
          + jnp.dot(oh_ref[...], wb_ref[...], preferred_element_type=F32))
    h_ref[...] = h1
    hn = _rms(h1, g_ref[...])
    hn_ref[...] = _pack_halves(hn)
    hn_hi = hn.astype(BF16)
    hn_lo = (hn - hn_hi.astype(F32)).astype(BF16)
    hh = jnp.dot(hn_hi, wr_ref[...], preferred_element_type=F32)
    lh = jnp.dot(hn_lo, wr_ref[:, :LANES], preferred_element_type=F32)
    logits = hh[:, :LANES] + (hh[:, LANES:] + lh) + br_ref[...]

    lane = lax.broadcasted_iota(jnp.int32, logits.shape, 1)
    lanef = lane.astype(F32)
    ninf = -jnp.inf
    big = float(LANES)

    is_g = lane < N_GROUPS
    gl = jnp.where(is_g, logits, ninf)
    gmax = jnp.max(gl, axis=-1, keepdims=True)
    gsum = jnp.sum(jnp.where(is_g, jnp.exp(gl - gmax), 0.0), axis=-1, keepdims=True)
    g_w = 1.0 / gsum
    g_idx = jnp.min(jnp.where(gl == gmax, lanef, big), axis=-1, keepdims=True)

    e_lane = lane - N_GROUPS
    in_grp = (e_lane >= 0) & (e_lane < N_EXPERTS) & ((e_lane >> 3).astype(F32) == g_idx)
    el = jnp.where(in_grp, logits, ninf)
    emax = jnp.max(el, axis=-1, keepdims=True)
    esum = jnp.sum(jnp.where(in_grp, jnp.exp(el - emax), 0.0), axis=-1, keepdims=True)
    i1 = jnp.min(jnp.where(el == emax, lanef, big), axis=-1, keepdims=True)
    el2 = jnp.where(lanef == i1, ninf, el)
    emax2 = jnp.max(el2, axis=-1, keepdims=True)
    i2 = jnp.min(jnp.where(el2 == emax2, lanef, big), axis=-1, keepdims=True)
    p1 = 1.0 / esum
    p2 = jnp.exp(emax2 - emax) / esum
    w1 = g_w * p1 / (p1 + p2)
    w2 = g_w * p2 / (p1 + p2)

    rt = jnp.where(lane == 0, i1 - N_GROUPS,
                   jnp.where(lane == 1, i2 - N_GROUPS,
                             jnp.where(lane == 2, w1, jnp.where(lane == 3, w2, 0.0))))
    rt_ref[...] = rt


def _out_route(x, o_mla, o_hg, wa, wb, gain, wr, br, tm):
    s, d = x.shape
    row = lambda i: (i, 0)
    full = lambda i: (0, 0)
    return pl.pallas_call(
        _out_route_kernel,
        grid=(s // tm,),
        in_specs=[
            pl.BlockSpec((tm, d), row),
            pl.BlockSpec((tm, D_MLA), row),
            pl.BlockSpec((tm, D_HG), row),
            pl.BlockSpec(wa.shape, full, pipeline_mode=pl.Buffered(1)),
            pl.BlockSpec(wb.shape, full, pipeline_mode=pl.Buffered(1)),
            pl.BlockSpec((1, d), full),
            pl.BlockSpec(wr.shape, full, pipeline_mode=pl.Buffered(1)),
            pl.BlockSpec((1, LANES), full),
        ],
        out_specs=[
            pl.BlockSpec((tm, d), row),
            pl.BlockSpec((tm, d // 2), row),
            pl.BlockSpec((tm, LANES), row),
        ],
        out_shape=[
            jax.ShapeDtypeStruct((s, d), F32),
            jax.ShapeDtypeStruct((s, d // 2), jnp.uint32),
            jax.ShapeDtypeStruct((s, LANES), F32),
        ],
        compiler_params=_cparams(("parallel",), 56),
        name="out_proj_route",
    )(x, o_mla, o_hg, wa, wb, gain, wr, br)


def _moe_kernel(be_ref, nv_ref, dst_ref, nu_ref, hn_hbm, wg_hbm, wu_hbm, wd_hbm, y_hbm,
                xbuf, xsem, obuf, osem, wgb, wub, wdb, wsem, wslot, *, n_tok):
    i = pl.program_id(0)
    nused = nu_ref[0]
    e = be_ref[i]
    active = i < nused
    first = (i == 0) | (e != be_ref[jnp.maximum(i - 1, 0)])

    def chunks(blk, fn):
        nv = nv_ref[blk]
        for lo in range(0, MOE_BLOCK, ROW_CHUNK):
            pl.when(lo < nv)(functools.partial(fn, lo))

    def gather_start(blk, sl):
        def issue(lo):
            for r in range(lo, lo + ROW_CHUNK):
                tok = dst_ref[blk * MOE_BLOCK + r] & (n_tok - 1)
                pltpu.make_async_copy(hn_hbm.at[pl.ds(tok, 1), :], xbuf.at[sl, pl.ds(r, 1), :],
                                      xsem.at[sl]).start(priority=ROW_DMA_PRIORITY)
        chunks(blk, issue)

    def gather_wait(blk, sl):
        chunks(blk, lambda lo: pltpu.make_async_copy(
            hn_hbm.at[pl.ds(0, ROW_CHUNK), :], xbuf.at[sl, pl.ds(lo, ROW_CHUNK), :], xsem.at[sl]).wait())

    def scatter_start(blk, sl):
        def issue(lo):
            for r in range(lo, lo + ROW_CHUNK):
                dst = dst_ref[blk * MOE_BLOCK + r]
                pltpu.make_async_copy(obuf.at[sl, pl.ds(r, 1), :], y_hbm.at[pl.ds(dst, 1), :],
                                      osem.at[sl]).start(priority=ROW_DMA_PRIORITY)
        chunks(blk, issue)

    def scatter_wait(blk, sl):
        chunks(blk, lambda lo: pltpu.make_async_copy(
            obuf.at[sl, pl.ds(lo, ROW_CHUNK), :], y_hbm.at[pl.ds(0, ROW_CHUNK), :], osem.at[sl]).wait())

    def spare_copy(sl):
        return pltpu.make_async_copy(obuf.at[sl], y_hbm.at[pl.ds(TOP_K * n_tok + sl * MOE_BLOCK, MOE_BLOCK), :],
                                     osem.at[sl])

    def weight_copies(ex, sl):
        return (pltpu.make_async_copy(wg_hbm.at[ex], wgb.at[sl], wsem.at[sl, 0]),
                pltpu.make_async_copy(wu_hbm.at[ex], wub.at[sl], wsem.at[sl, 1]),
                pltpu.make_async_copy(wd_hbm.at[ex], wdb.at[sl], wsem.at[sl, 2]))

    @pl.when((i == 0) & active)
    def _():
        wslot[0] = 1
        for cp in weight_copies(e, 0):
            cp.start(priority=1 - ROW_DMA_PRIORITY)
        xbuf[...] = jnp.zeros(xbuf.shape, jnp.uint32)
        gather_start(0, 0)
        obuf[...] = jnp.zeros(obuf.shape, jnp.uint32)
        for sl in (0, 1):
            spare_copy(sl).start()
        for sl in (0, 1):
            spare_copy(sl).wait()

    @pl.when(active & first)
    def _():
        sl = 1 - wslot[0]
        wslot[0] = sl
        nxt = lax.while_loop(lambda j: (j < nused) & (be_ref[jnp.minimum(j, nused - 1)] == e),
                             lambda j: j + 1, i + 1)

        @pl.when(nxt < nused)
        def _():
            for cp in weight_copies(be_ref[nxt], 1 - sl):
                cp.start(priority=1 - ROW_DMA_PRIORITY)

        for cp in weight_copies(e, sl):
            cp.wait()

    def block(xs):
        sl = wslot[0]

        @pl.when(i >= 2)
        def _():
            scatter_wait(i - 2, xs)

        gather_wait(i, xs)

        @pl.when(i + 1 < nused)
        def _():
            gather_start(i + 1, 1 - xs)

        x = _unpack_halves(xbuf[xs]).astype(BF16)
        g = jnp.dot(x, wgb[sl].astype(BF16), preferred_element_type=F32)
        u = jnp.dot(x, wub[sl].astype(BF16), preferred_element_type=F32)
        hmid = (g * _sigmoid(g) * u).astype(BF16)
        obuf[xs] = _pack_halves(jnp.dot(hmid, wdb[sl].astype(BF16), preferred_element_type=F32))
        scatter_start(i, xs)

        @pl.when(i == nused - 1)
        def _():
            @pl.when(i >= 1)
            def _():
                scatter_wait(i - 1, 1 - xs)

            scatter_wait(i, xs)

    for s_ in (0, 1):
        pl.when(active & (i % 2 == s_))(functools.partial(block, s_))


def _moe(block_e, nvalid, slot_dst, nused, hn, wg, wu, wd):
    nb = block_e.shape[0]
    n_tok, dw = hn.shape
    d = 2 * dw
    assert n_tok & (n_tok - 1) == 0
    grid_spec = pltpu.PrefetchScalarGridSpec(
        num_scalar_prefetch=4,
        grid=(nb,),
        in_specs=[pl.BlockSpec(memory_space=pl.ANY)] * 4,
        out_specs=pl.BlockSpec(memory_space=pl.ANY),
        scratch_shapes=[
            pltpu.VMEM((2, MOE_BLOCK, dw), jnp.uint32),
            pltpu.SemaphoreType.DMA((2,)),
            pltpu.VMEM((2, MOE_BLOCK, dw), jnp.uint32),
            pltpu.SemaphoreType.DMA((2,)),
            pltpu.VMEM((2, d, D_EXPERT), F32),
            pltpu.VMEM((2, d, D_EXPERT), F32),
            pltpu.VMEM((2, D_EXPERT, d), F32),
            pltpu.SemaphoreType.DMA((2, 3)),
            pltpu.SMEM((1,), jnp.int32),
        ],
    )
    return pl.pallas_call(
        functools.partial(_moe_kernel, n_tok=n_tok),
        grid_spec=grid_spec,
        out_shape=jax.ShapeDtypeStruct((TOP_K * n_tok + 2 * MOE_BLOCK, dw), jnp.uint32),
        compiler_params=_cparams(("arbitrary",), 56),
        name="moe_experts",
    )(block_e, nvalid, slot_dst, nused, hn, wg, wu, wd)


def _ple_kernel(h_ref, rt_ref, y0_ref, y1_ref, p_ref, wg_ref, bg_ref, wp_ref, gp_ref, gf_ref, o_ref):
    rt = rt_ref[...]
    h2 = h_ref[...] + (rt[:, TOP_K:TOP_K + 1] * _unpack_halves(y0_ref[...])
                       + rt[:, TOP_K + 1:TOP_K + 2] * _unpack_halves(y1_ref[...]))
    hn = _rms(h2, gp_ref[...]).astype(BF16)
    gate = _sigmoid(jnp.dot(hn, wg_ref[...], preferred_element_type=F32) + bg_ref[...])
    pe = jnp.dot(p_ref[...].astype(BF16), wp_ref[...], preferred_element_type=F32)
    h3 = h2 + gate * pe
    o_ref[...] = _rms(h3, gf_ref[...])


def _ple_final(h1, rt, y, p, wg, bg, wp, gp, gf, tm):
    s, d = h1.shape
    nblk = s // tm
    row = lambda i: (i, 0)
    full = lambda i: (0, 0)
    return pl.pallas_call(
        _ple_kernel,
        grid=(nblk,),
        in_specs=[
            pl.BlockSpec((tm, d), row),
            pl.BlockSpec((tm, LANES), row),
            pl.BlockSpec((tm, d // 2), row),
            pl.BlockSpec((tm, d // 2), lambda i: (i + nblk, 0)),
            pl.BlockSpec((tm, PLE_DIM), row),
            pl.BlockSpec(wg.shape, full, pipeline_mode=pl.Buffered(1)),
            pl.BlockSpec((1, d), full),
            pl.BlockSpec(wp.shape, full, pipeline_mode=pl.Buffered(1)),
            pl.BlockSpec((1, d), full),
            pl.BlockSpec((1, d), full),
        ],
        out_specs=pl.BlockSpec((tm, d), row),
        out_shape=jax.ShapeDtypeStruct((s, d), F32),
        compiler_params=_cparams(("parallel",), 56),
        name="ple_final",
    )(h1, rt, y, y, p, wg, bg, wp, gp, gf)


def _slots_kernel(rt_ref, tri_ref, tab_ref, cnt_ref, carry_ref, base_ref, acc_ref, *, n_tok):
    ps = pl.program_id(0)
    b = pl.program_id(1)
    tb = rt_ref.shape[0]
    n_rows = tab_ref.shape[0]
    rt = rt_ref[...]
    lane = lax.broadcasted_iota(jnp.int32, rt.shape, 1)
    lanef = lane.astype(F32)
    oh = [(lanef == rt[:, kk:kk + 1]).astype(F32) for kk in range(TOP_K)]
    both = oh[0] + oh[1]
    colsum = jnp.sum(both, axis=0, keepdims=True)

    @pl.when((ps == 0) & (b == 0))
    def _():
        cnt_ref[...] = jnp.zeros(cnt_ref.shape, F32)

    @pl.when(ps == 0)
    def _():
        cnt_ref[...] = cnt_ref[...] + colsum

    @pl.when((ps == 1) & (b == 0))
    def _():
        blocks = jnp.floor((cnt_ref[...] + (MOE_BLOCK - 0.5)) / MOE_BLOCK)
        r = lax.broadcasted_iota(jnp.int32, (LANES, LANES), 0)
        c = lax.broadcasted_iota(jnp.int32, (LANES, LANES), 1)
        before = (r < c).astype(F32)
        base_ref[...] = jnp.dot(blocks * MOE_BLOCK, before, preferred_element_type=F32,
                                precision=lax.Precision.HIGHEST)
        carry_ref[...] = jnp.zeros(carry_ref.shape, F32)
        acc_ref[...] = jnp.zeros(acc_ref.shape, F32)

    @pl.when(ps == 1)
    def _():
        earlier = jnp.dot(tri_ref[...], both.astype(BF16), preferred_element_type=F32)
        row = earlier + (base_ref[0:1, :] + carry_ref[0:1, :])
        tok = (b * tb + lax.broadcasted_iota(jnp.int32, (tb, 1), 0)).astype(F32)
        rows_f = lax.broadcasted_iota(jnp.int32, (tb, n_rows), 1).astype(F32)
        for kk in range(TOP_K):
            slot = jnp.sum(oh[kk] * row, axis=-1, keepdims=True)
            srow = jnp.floor(slot * (1.0 / LANES))
            scol = slot - LANES * srow
            val = kk * n_tok + tok
            vhi = jnp.floor(val * (1.0 / LANES))
            vlo = val - LANES * vhi
            at_row = (rows_f == srow).astype(BF16)
            at_col = lanef == scol
            x = jnp.concatenate([jnp.where(at_col, vhi, 0.0), jnp.where(at_col, vlo, 0.0),
                                 at_col.astype(F32)], axis=1).astype(BF16)
            acc_ref[...] += lax.dot_general(at_row, x, (((0,), (0,)), ((), ())), preferred_element_type=F32)
        carry_ref[...] = carry_ref[...] + colsum

    @pl.when((ps == 1) & (b == pl.num_programs(1) - 1))
    def _():
        acc = acc_ref[...]
        sidx = (lax.broadcasted_iota(jnp.int32, (n_rows, LANES), 0) * LANES
                + lax.broadcasted_iota(jnp.int32, (n_rows, LANES), 1)).astype(F32)
        blk = jnp.floor((sidx + 0.5) / MOE_BLOCK)
        spare = TOP_K * n_tok + (blk - 2.0 * jnp.floor(blk * 0.5)) * MOE_BLOCK + (sidx - MOE_BLOCK * blk)
        filled = acc[:, :LANES] * LANES + acc[:, LANES:2 * LANES]
        tab_ref[...] = jnp.where(acc[:, 2 * LANES:] > 0.5, filled, spare).astype(jnp.int32)


def _slots(rt, tb, n_slots):
    s = rt.shape[0]
    n_rows = -(-n_slots // (LANES * LANES)) * LANES
    tri = (np.arange(tb)[:, None] > np.arange(tb)[None, :]).astype(np.float32)
    return pl.pallas_call(
        functools.partial(_slots_kernel, n_tok=s),
        grid=(2, s // tb),
        in_specs=[
            pl.BlockSpec((tb, LANES), lambda ps, b: (b, 0)),
            pl.BlockSpec((tb, tb), lambda ps, b: (0, 0)),
        ],
        out_specs=[
            pl.BlockSpec((n_rows, LANES), lambda ps, b: (0, 0)),
            pl.BlockSpec((8, LANES), lambda ps, b: (0, 0)),
        ],
        out_shape=[
            jax.ShapeDtypeStruct((n_rows, LANES), jnp.int32),
            jax.ShapeDtypeStruct((8, LANES), F32),
        ],
        scratch_shapes=[pltpu.VMEM((8, LANES), F32), pltpu.VMEM((8, LANES), F32),
                        pltpu.VMEM((n_rows, 3 * LANES), F32)],
        compiler_params=_cparams(("arbitrary", "arbitrary")),
        name="moe_slots",
    )(rt, jnp.asarray(tri, BF16))


def _dispatch(rt, n_tok):
    a = n_tok * TOP_K
    nb = -(-a // MOE_BLOCK) + N_EXPERTS
    tab, cnt = _slots(rt, 1024, nb * MOE_BLOCK)
    slot_dst = tab.reshape(-1)[:nb * MOE_BLOCK]
    counts = cnt[0, :N_EXPERTS].astype(jnp.int32)
    nblk = (counts + MOE_BLOCK - 1) // MOE_BLOCK
    bends = jnp.cumsum(nblk)
    bidx = jnp.arange(nb, dtype=jnp.int32)
    block_e = jnp.minimum(jnp.sum((bends[None, :] <= bidx[:, None]).astype(jnp.int32), axis=1), N_EXPERTS - 1)
    nused = bends[-1].astype(jnp.int32)
    nvalid = jnp.clip(counts[block_e] - (bidx - (bends - nblk)[block_e]) * MOE_BLOCK, 0, MOE_BLOCK)
    nvalid = jnp.where(bidx < nused, nvalid, 0).astype(jnp.int32)
    return block_e, nvalid, slot_dst, nused.reshape(1)


def kernel(x, p, positions, attn_norm, w_in, q_norm, w_uq, kv_norm, w_ukv, mla_norm, hg_lb_logits, hg_norm, w_out, ffn_norm, w_router_group, b_router_group, w_router_expert, b_router_expert, w_exp_gate, w_exp_up, w_exp_down, ple_norm, w_ple_gate, b_ple_gate, w_ple_proj, final_norm):
    bsz, s, d = x.shape
    assert bsz == 1 and w_in.shape[0] == 1
    xt = x[0]

    inv_freq = 1.0 / (ROPE_THETA ** (jnp.arange(0, QK_ROPE, 2, dtype=F32) / QK_ROPE))
    ang = positions[0].astype(F32)[:, None] * inv_freq
    cos, sin = jnp.cos(ang), jnp.sin(ang)
    zpad = jnp.zeros((s, LANES - QK_ROPE), F32)
    cc = jnp.concatenate([cos, cos, zpad], axis=1)
    ss = jnp.concatenate([-sin, sin, zpad], axis=1)

    lb = jnp.cumsum(jax.nn.softmax(hg_lb_logits.astype(F32), axis=0), axis=0)[0][None, :]

    wi = w_in[0]
    kr0 = Q_LORA + KV_LORA
    half = QK_ROPE // 2
    w_lat = jnp.concatenate(
        [wi[:, :kr0 + QK_ROPE], wi[:, kr0 + half:kr0 + QK_ROPE], wi[:, kr0:kr0 + half]], axis=1).astype(BF16)
    w_hg = wi[:, kr0 + QK_ROPE:].astype(BF16)
    wq3 = w_uq[0].reshape(Q_LORA, MLA_HEADS, QK_HEAD)
    wq_pad = jnp.concatenate(
        [wq3, wq3[:, :, QK_NOPE + half:], wq3[:, :, QK_NOPE:QK_NOPE + half]], axis=2
    ).reshape(Q_LORA, MLA_HEADS * QK_PAD).astype(BF16)
    wkv3 = w_ukv[0].reshape(KV_LORA, MLA_HEADS, QK_NOPE + V_HEAD)
    wknt = wkv3[:, :, :QK_NOPE].reshape(KV_LORA, MLA_HEADS * QK_NOPE).T.astype(BF16)
    wv = wkv3[:, :, QK_NOPE:].reshape(KV_LORA, D_MLA).astype(BF16)
    wo = w_out[0].astype(BF16)
    wr = jnp.concatenate(
        [w_router_group[0], w_router_expert[0], jnp.zeros((d, LANES - N_GROUPS - N_EXPERTS), F32)], axis=1)
    wr_hi = wr.astype(BF16)
    wr = jnp.concatenate([wr_hi, (wr - wr_hi.astype(F32)).astype(BF16)], axis=1)
    br = jnp.concatenate(
        [b_router_group[0], b_router_expert[0], jnp.zeros((LANES - N_GROUPS - N_EXPERTS,), F32)])[None, :]

    hg4, lat = _in_proj(xt, attn_norm, w_hg, w_lat, 1024, 1024)
    q, kt, v = _mla_up(lat, q_norm, kv_norm, wq_pad, wknt, wv, cc, ss, 512)
    o_mla = _attention(q, kt, v, mla_norm, 512)
    mall, lv = _hgrn_tables()
    o_hg = _hgrn(hg4, lb, hg_norm[0].reshape(1, D_HG), mall, lv)

    h1, hn, rt = _out_route(xt, o_mla, o_hg, wo[:D_MLA], wo[D_MLA:], ffn_norm, wr, br, 512)
    block_e, nvalid, slot_dst, nused = _dispatch(rt, s)
    y = _moe(block_e, nvalid, slot_dst, nused, hn, w_exp_gate[0], w_exp_up[0], w_exp_down[0])

    out = _ple_final(h1, rt, y, p[0, 0], w_ple_gate[0].astype(BF16), b_ple_gate,
                     w_ple_proj[0].astype(BF16), ple_norm, final_norm[None, :], 512)
    return out[None]
```

```python
import functools

import jax
import jax.numpy as jnp
import numpy as np
from jax import lax
from jax.experimental import pallas as pl
from jax.experimental.pallas import tpu as pltpu

F32 = jnp.float32
BF16 = jnp.bfloat16

PLE_DIM = 256
MLA_HEADS = 8
QK_NOPE = 128
QK_ROPE = 64
QK_HEAD = QK_NOPE + QK_ROPE
QK_PAD = 256
V_HEAD = 128
V_PAD = 2 * V_HEAD
QK_AHEAD = 2
PV_LAG = 1
LAG_LIMIT = 60.0
FLOOR_LIMIT = 100.0
Q_LORA = 512
KV_LORA = 256
ROPE_THETA = 10000.0
HG_HEADS = 8
HG_DK = 128
HG_DV = 128
HG_CHUNK = 64
D_MLA = MLA_HEADS * V_HEAD
D_HG = HG_HEADS * HG_DV
N_GROUPS = 8
EXPERTS_PER_GROUP = 8
N_EXPERTS = N_GROUPS * EXPERTS_PER_GROUP
EPG_LOG2 = EXPERTS_PER_GROUP.bit_length() - 1
assert 1 << EPG_LOG2 == EXPERTS_PER_GROUP
TOP_K = 2
D_EXPERT = 512
EPS = 1e-6
LANES = 128
D_LAT = Q_LORA + KV_LORA + 2 * QK_ROPE
NEG_BIG = -1e30
LOG2E = 1.4426950408889634

MOE_BLOCK = 320
ROW_CHUNK = 32
ROW_DMA_PRIORITY = 0
HG_ROWS = 256
HG_LEVELS = (32, 16, 8, 4, 2, 1)
HG_MM_LEVELS = (2, 1)
HG_NH = 4

IN_PROJ_TM = 1024
IN_PROJ_TN = 1024
MLA_UP_TM = 512
ATTN_BLOCK = 512
OUT_ROUTE_TM = 512
SLOTS_TM = 1024
PLE_TM = 512
VMEM_MB_HGRN = 32
VMEM_MB_MID = 48
VMEM_MB_LARGE = 56


def _cparams(sem, vmem_mb=None):
    kw = dict(dimension_semantics=sem)
    if vmem_mb is not None:
        kw["vmem_limit_bytes"] = vmem_mb * 1024 * 1024
    return pltpu.CompilerParams(**kw)


def _rms(x, g):
    ms = jnp.mean(x * x, axis=-1, keepdims=True)
    return x * lax.rsqrt(ms + EPS) * g


def _sigmoid(x):
    return 1.0 / (1.0 + jnp.exp(-x))


def _pack_halves(x):
    n = x.shape[1] // 2
    bits = lax.bitcast_convert_type(x.astype(BF16).astype(F32), jnp.uint32)
    return bits[:, :n] | (bits[:, n:] >> 16)


def _unpack_halves(w):
    hi = lax.bitcast_convert_type(w & jnp.uint32(0xFFFF0000), F32)
    lo = lax.bitcast_convert_type(w << 16, F32)
    return jnp.concatenate([hi, lo], axis=1)


def _in_proj_kernel(x_ref, g_ref, whg_ref, wlat_ref, hg_ref, lat_ref, xn_ref):
    j = pl.program_id(1)
    last = pl.num_programs(1) - 1

    @pl.when(j == 0)
    def _():
        xn_ref[...] = _rms(x_ref[...], g_ref[...]).astype(BF16)

    @pl.when(j < last)
    def _():
        hg_ref[...] = jnp.dot(xn_ref[...], whg_ref[...], preferred_element_type=F32).astype(hg_ref.dtype)

    @pl.when(j == last)
    def _():
        lat_ref[...] = jnp.dot(xn_ref[...], wlat_ref[...], preferred_element_type=F32)


def _in_proj(x, gain, w_hg, w_lat, tm, tn):
    s, d = x.shape
    nt = w_hg.shape[1] // tn
    nl = w_lat.shape[1]
    hg_tile = lambda i, j: (i, jnp.minimum(j, nt - 1))
    return pl.pallas_call(
        _in_proj_kernel,
        grid=(s // tm, nt + 1),
        in_specs=[
            pl.BlockSpec((tm, d), lambda i, j: (i, 0)),
            pl.BlockSpec((1, d), lambda i, j: (0, 0)),
            pl.BlockSpec((d, tn), lambda i, j: (0, jnp.minimum(j, nt - 1))),
            pl.BlockSpec((d, nl), lambda i, j: (0, 0), pipeline_mode=pl.Buffered(1)),
        ],
        out_specs=[
            pl.BlockSpec((tm, tn), hg_tile),
            pl.BlockSpec((tm, nl), lambda i, j: (i, 0)),
        ],
        out_shape=[
            jax.ShapeDtypeStruct((s, nt * tn), BF16),
            jax.ShapeDtypeStruct((s, nl), F32),
        ],
        scratch_shapes=[pltpu.VMEM((tm, d), BF16)],
        compiler_params=_cparams(("parallel", "arbitrary"), VMEM_MB_LARGE),
        name="in_proj",
    )(x, gain, w_hg, w_lat)


def _mla_up_kernel(lat_ref, qn_ref, kvn_ref, wq_ref, wknt_ref, wv_ref, cc_ref, ss_ref,
                   q_ref, kt_ref, v_ref):
    lat = lat_ref[...]
    cq = _rms(lat[:, :Q_LORA], qn_ref[...]).astype(BF16)
    ckv = _rms(lat[:, Q_LORA:Q_LORA + KV_LORA], kvn_ref[...]).astype(BF16)
    kp = lat[:, Q_LORA + KV_LORA:D_LAT]
    cc = cc_ref[...]
    ss = ss_ref[...]
    scale = QK_HEAD ** -0.5 * LOG2E

    q = jnp.dot(cq, wq_ref[...], preferred_element_type=F32)
    for h in range(MLA_HEADS):
        base = h * QK_PAD
        q_ref[:, base:base + QK_NOPE] = (q[:, base:base + QK_NOPE] * scale).astype(BF16)
        r = q[:, base + QK_NOPE:base + QK_PAD]
        r = (r * cc + pltpu.roll(r, QK_ROPE, 1) * ss) * scale
        q_ref[:, base + QK_NOPE:base + QK_PAD] = r.astype(BF16)

    kpe_t = (kp * cc + pltpu.roll(kp, QK_ROPE, 1) * ss).T.astype(BF16)
    kn_t = lax.dot_general(wknt_ref[...], ckv, (((1,), (1,)), ((), ())), preferred_element_type=F32).astype(BF16)
    for h in range(MLA_HEADS):
        base = h * QK_PAD
        kt_ref[base:base + QK_NOPE, :] = kn_t[h * QK_NOPE:(h + 1) * QK_NOPE]
        kt_ref[base + QK_NOPE:base + QK_PAD, :] = kpe_t
    v_ref[...] = jnp.dot(ckv, wv_ref[...], preferred_element_type=F32).astype(BF16)


def _mla_up(lat, qn, kvn, wq, wknt, wv, cc, ss, tm):
    s = lat.shape[0]
    row = lambda i: (i, 0)
    full = lambda i: (0, 0)
    return pl.pallas_call(
        _mla_up_kernel,
        grid=(s // tm,),
        in_specs=[
            pl.BlockSpec((tm, D_LAT), row),
            pl.BlockSpec((1, Q_LORA), full),
            pl.BlockSpec((1, KV_LORA), full),
            pl.BlockSpec(wq.shape, full),
            pl.BlockSpec(wknt.shape, full),
            pl.BlockSpec(wv.shape, full),
            pl.BlockSpec((tm, LANES), row),
            pl.BlockSpec((tm, LANES), row),
        ],
        out_specs=[
            pl.BlockSpec((tm, MLA_HEADS * QK_PAD), row),
            pl.BlockSpec((MLA_HEADS * QK_PAD, tm), lambda i: (0, i)),
            pl.BlockSpec((tm, D_MLA), row),
        ],
        out_shape=[
            jax.ShapeDtypeStruct((s, MLA_HEADS * QK_PAD), BF16),
            jax.ShapeDtypeStruct((MLA_HEADS * QK_PAD, s), BF16),
            jax.ShapeDtypeStruct((s, D_MLA), BF16),
        ],
        compiler_params=_cparams(("parallel",), VMEM_MB_MID),
        name="mla_up",
    )(lat, qn, kvn, wq, wknt, wv, cc, ss)


def _attn_kernel(it_ref, jt_ref, q_ref, kt_ref, v_ref, g_ref, o_ref, acc_ref, m_ref, redo_ref, *, tb):
    t = pl.program_id(0)
    i = it_ref[t]
    j = jt_ref[t]
    src = t % 2
    dst = 1 - src
    ntile = tb // LANES

    @pl.when(j == 0)
    def _():
        m_ref[...] = jnp.zeros(m_ref.shape, F32)
        acc_ref[src] = jnp.zeros(acc_ref.shape[1:], F32)

    def scores(h):
        return jnp.dot(q_ref[:, h * QK_PAD:(h + 1) * QK_PAD], kt_ref[h * QK_PAD:(h + 1) * QK_PAD, :],
                       preferred_element_type=F32)

    ones = jnp.ones((tb, V_PAD - V_HEAD), BF16)

    def values(h):
        return jnp.concatenate([v_ref[:, h * V_HEAD:(h + 1) * V_HEAD], ones], axis=1)

    def key_minus_query():
        return (lax.broadcasted_iota(jnp.int32, (tb, tb), 1) - lax.broadcasted_iota(jnp.int32, (tb, tb), 0))

    def lagged(masked, first=False):
        if masked:
            keep = key_minus_query() <= (jnp.where(i == 0, 0, tb) if first else 0)
        ahead = [scores(h) for h in range(min(QK_AHEAD, MLA_HEADS))]
        pending = []
        worst = None
        least = None

        def accumulate(h, p):
            acc_ref[dst, h] = acc_ref[src, h] + jnp.dot(p, values(h), preferred_element_type=F32)

        for h in range(MLA_HEADS):
            s = ahead.pop(0)
            if h + QK_AHEAD < MLA_HEADS:
                ahead.append(scores(h + QK_AHEAD))
            d = s - jnp.concatenate([m_ref[h]] * ntile, axis=1)
            if first:
                low = jnp.min((jnp.where(keep, d, 0.0) if masked else d).reshape(tb // 8, 8, tb), axis=0)
                least = low if least is None else jnp.minimum(least, low)
            if masked:
                d = jnp.where(keep, d, NEG_BIG)
            top = jnp.max(d.reshape(tb // 8, 8, tb), axis=0)
            worst = top if worst is None else jnp.maximum(worst, top)
            pending.append((h, jnp.exp2(d).astype(BF16)))
            if len(pending) > PV_LAG:
                accumulate(*pending.pop(0))
        for item in pending:
            accumulate(*item)
        redo = jnp.max(worst) > LAG_LIMIT
        if first:
            redo = redo | (jnp.min(least) < -FLOOR_LIMIT)
        redo_ref[0] = redo.astype(jnp.int32)

    def exact():
        keep = key_minus_query() <= jnp.where(j == i, 0, tb)
        ahead = [scores(h) for h in range(min(QK_AHEAD, MLA_HEADS))]
        for h in range(MLA_HEADS):
            s = jnp.where(keep, ahead.pop(0), NEG_BIG)
            if h + QK_AHEAD < MLA_HEADS:
                ahead.append(scores(h + QK_AHEAD))
            m_prev = jnp.where(j == 0, NEG_BIG, m_ref[h])
            m_new = jnp.maximum(m_prev, jnp.max(s, axis=1, keepdims=True))
            alpha = jnp.exp2(m_prev - m_new)
            p = jnp.exp2(s - jnp.concatenate([m_new] * ntile, axis=1)).astype(BF16)
            m_ref[h] = m_new
            acc_ref[dst, h] = (acc_ref[src, h] * jnp.concatenate([alpha] * (V_PAD // LANES), axis=1)
                               + jnp.dot(p, values(h), preferred_element_type=F32))

    @pl.when(j == 0)
    def _():
        lagged(True, first=True)

    @pl.when((j > 0) & (j < i))
    def _():
        lagged(False)

    @pl.when((j > 0) & (j == i))
    def _():
        lagged(True)

    @pl.when(redo_ref[0] != 0)
    def _():
        exact()

    @pl.when(j == i)
    def _():
        outs = [acc_ref[dst, h, :, :V_HEAD] / acc_ref[dst, h, :, V_HEAD:] for h in range(MLA_HEADS)]
        o_ref[...] = _rms(jnp.concatenate(outs, axis=1), g_ref[...]).astype(o_ref.dtype)


def _attention(q, kt, v, gain, tb):
    s = q.shape[0]
    nq = s // tb
    it = np.concatenate([np.full(i + 1, i, np.int32) for i in range(nq)])
    jt = np.concatenate([np.arange(i + 1, dtype=np.int32) for i in range(nq)])
    grid_spec = pltpu.PrefetchScalarGridSpec(
        num_scalar_prefetch=2,
        grid=(it.shape[0],),
        in_specs=[
            pl.BlockSpec((tb, MLA_HEADS * QK_PAD), lambda t, it, jt: (it[t], 0)),
            pl.BlockSpec((MLA_HEADS * QK_PAD, tb), lambda t, it, jt: (0, jt[t])),
            pl.BlockSpec((tb, D_MLA), lambda t, it, jt: (jt[t], 0)),
            pl.BlockSpec((1, D_MLA), lambda t, it, jt: (0, 0)),
        ],
        out_specs=pl.BlockSpec((tb, D_MLA), lambda t, it, jt: (it[t], 0)),
        scratch_shapes=[
            pltpu.VMEM((2, MLA_HEADS, tb, V_PAD), F32),
            pltpu.VMEM((MLA_HEADS, tb, LANES), F32),
            pltpu.SMEM((1,), jnp.int32),
        ],
    )
    return pl.pallas_call(
        functools.partial(_attn_kernel, tb=tb),
        grid_spec=grid_spec,
        out_shape=jax.ShapeDtypeStruct((s, D_MLA), BF16),
        compiler_params=_cparams(("arbitrary",), VMEM_MB_MID),
        name="mla_attention",
    )(jnp.asarray(it), jnp.asarray(jt), q, kt, v, gain)


def _hgrn_tables():
    n = HG_ROWS
    r = np.arange(n)
    c = np.arange(n)
    same = (r[:, None] // HG_CHUNK) == (c[None, :] // HG_CHUNK)

    def rows_upto(idx):
        return (same & (c[None, :] <= idx[:, None])).astype(np.float32)

    blocks = [rows_upto(r)]
    for m in HG_MM_LEVELS:
        blocks.append(rows_upto((r // (2 * m)) * (2 * m) + m))
    mall = np.concatenate(blocks, axis=0)

    x = r[:, None] ^ c[None, :]
    lv = np.full((n, n), -1, np.int32)
    for li, m in enumerate(HG_LEVELS):
        lv = np.where(same & (r[:, None] > c[None, :]) & (x >= m) & (x < 2 * m), li, lv)
    lv = np.where(r[:, None] == c[None, :], len(HG_LEVELS), lv)
    return jnp.asarray(mall, BF16), jnp.asarray(lv, jnp.int32)


def _hgrn_kernel(q_ref, f_ref, i_ref, g_ref, lb_ref, gn_ref, mall_ref, lv_ref, o_ref, st_ref):
    t = pl.program_id(1)
    n = HG_ROWS

    @pl.when(t == 0)
    def _():
        st_ref[...] = jnp.zeros(st_ref.shape, F32)

    rowid = lax.broadcasted_iota(jnp.int32, (n, 1), 0)
    lv = lv_ref[...]

    def head(hh):
        cols = slice(hh * HG_DK, (hh + 1) * HG_DK)
        q_in = q_ref[:, cols].astype(F32)
        qs = q_in * _sigmoid(q_in)
        lb = lb_ref[:, cols]
        f = lb + (1.0 - lb) * _sigmoid(f_ref[:, cols].astype(F32))
        kk = 1.0 - f
        logf = jnp.log(f)
        iv = i_ref[:, cols]

        l1 = logf.astype(BF16)
        l2 = (logf - l1.astype(F32)).astype(BF16)
        parts = jnp.dot(mall_ref[...], jnp.concatenate([l1, l2], axis=1), preferred_element_type=F32)
        yield
        bc = parts[:, :HG_DK] + parts[:, HG_DK:]
        b = bc[:n]

        def anchor(period, row):
            b3 = b.reshape(n // period, period, HG_DK)
            return jnp.broadcast_to(b3[:, row:row + 1, :], b3.shape).reshape(n, HG_DK)

        a = jnp.zeros((n, n), F32)
        for li, m in enumerate(HG_LEVELS):
            if m in HG_MM_LEVELS:
                k = 1 + HG_MM_LEVELS.index(m)
                c = bc[k * n:(k + 1) * n]
            else:
                c = anchor(2 * m, m)
            e = jnp.exp(-jnp.abs(b - c))
            upper = (rowid & (2 * m - 1)) >= m
            x = (jnp.where(upper, qs, kk) * e).astype(BF16)
            p = lax.dot_general(x, x, (((1,), (1,)), ((), ())), preferred_element_type=F32)
            yield
            a = jnp.where(lv == li, p, a)
        a = jnp.where(lv == len(HG_LEVELS), jnp.sum(qs * kk, axis=-1, keepdims=True), a)
        o = jnp.dot(a.astype(BF16), iv, preferred_element_type=F32)
        yield

        blast = anchor(HG_CHUNK, HG_CHUNK - 1)
        qd = (qs * jnp.exp(b)).astype(BF16)
        kd = kk * jnp.exp(blast - b)
        ivt = iv.astype(F32).T.astype(BF16)
        st = st_ref[hh]
        outs = []
        for ci in range(n // HG_CHUNK):
            lo = ci * HG_CHUNK
            inter = lax.dot_general(qd[lo:lo + HG_CHUNK], st.astype(BF16), (((1,), (1,)), ((), ())),
                                    preferred_element_type=F32)
            kd_c = jnp.where((rowid >= lo) & (rowid < lo + HG_CHUNK), kd, 0.0).astype(BF16)
            upd = jnp.dot(ivt, kd_c, preferred_element_type=F32)
            yield
            outs.append(o[lo:lo + HG_CHUNK] + inter)
            st = jnp.exp(blast[lo:lo + 1]) * st + upd
        st_ref[hh] = st
        o = jnp.concatenate(outs, axis=0)

        g_in = g_ref[:, cols].astype(F32)
        o_ref[:, cols] = (_rms(o, gn_ref[:, cols]) * (g_in * _sigmoid(g_in))).astype(o_ref.dtype)

    progs = [head(hh) for hh in range(HG_NH)]
    while progs:
        progs = [pr for pr in progs if next(pr, "done") != "done"]


def _hgrn(hg4, lb, gn, mall, lv):
    s = hg4.shape[0]
    groups = HG_HEADS // HG_NH
    width = HG_NH * HG_DK

    def col(c):
        return lambda hh, t: (t, c * groups + hh)

    head = lambda hh, t: (0, hh)
    const = lambda hh, t: (0, 0)
    return pl.pallas_call(
        _hgrn_kernel,
        grid=(groups, s // HG_ROWS),
        in_specs=[
            pl.BlockSpec((HG_ROWS, width), col(0)),
            pl.BlockSpec((HG_ROWS, width), col(1)),
            pl.BlockSpec((HG_ROWS, width), col(2)),
            pl.BlockSpec((HG_ROWS, width), col(3)),
            pl.BlockSpec((1, width), head),
            pl.BlockSpec((1, width), head),
            pl.BlockSpec(mall.shape, const),
            pl.BlockSpec(lv.shape, const),
        ],
        out_specs=pl.BlockSpec((HG_ROWS, width), lambda hh, t: (t, hh)),
        out_shape=jax.ShapeDtypeStruct((s, D_HG), BF16),
        scratch_shapes=[pltpu.VMEM((HG_NH, HG_DV, HG_DK), F32)],
        compiler_params=_cparams(("parallel", "arbitrary"), VMEM_MB_HGRN),
        name="hgrn2",
    )(hg4, hg4, hg4, hg4, lb, gn, mall, lv)


def _out_route_kernel(x_ref, om_ref, oh_ref, wa_ref, wb_ref, g_ref, wr_ref, br_ref,
                      h_ref, hn_ref, rt_ref):
    h1 = (x_ref[...]
          + jnp.dot(om_ref[...], wa_ref[...], preferred_element_type=F32)
          + jnp.dot(oh_ref[...], wb_ref[...], preferred_element_type=F32))
    h_ref[...] = h1
    hn = _rms(h1, g_ref[...])
    hn_ref[...] = _pack_halves(hn)
    hn_hi = hn.astype(BF16)
    hn_lo = (hn - hn_hi.astype(F32)).astype(BF16)
    hh = jnp.dot(hn_hi, wr_ref[...], preferred_element_type=F32)
    lh = jnp.dot(hn_lo, wr_ref[:, :LANES], preferred_element_type=F32)
    logits = hh[:, :LANES] + (hh[:, LANES:] + lh) + br_ref[...]

    lane = lax.broadcasted_iota(jnp.int32, logits.shape, 1)
    lanef = lane.astype(F32)
    ninf = -jnp.inf
    big = float(LANES)

    is_g = lane < N_GROUPS
    gl = jnp.where(is_g, logits, ninf)
    gmax = jnp.max(gl, axis=-1, keepdims=True)
    gsum = jnp.sum(jnp.where(is_g, jnp.exp(gl - gmax), 0.0), axis=-1, keepdims=True)
    g_w = 1.0 / gsum
    g_idx = jnp.min(jnp.where(gl == gmax, lanef, big), axis=-1, keepdims=True)

    e_lane = lane - N_GROUPS
    in_grp = (e_lane >= 0) & (e_lane < N_EXPERTS) & ((e_lane >> EPG_LOG2).astype(F32) == g_idx)
    el = jnp.where(in_grp, logits, ninf)
    emax = jnp.max(el, axis=-1, keepdims=True)
    esum = jnp.sum(jnp.where(in_grp, jnp.exp(el - emax), 0.0), axis=-1, keepdims=True)
    i1 = jnp.min(jnp.where(el == emax, lanef, big), axis=-1, keepdims=True)
    el2 = jnp.where(lanef == i1, ninf, el)
    emax2 = jnp.max(el2, axis=-1, keepdims=True)
    i2 = jnp.min(jnp.where(el2 == emax2, lanef, big), axis=-1, keepdims=True)
    p1 = 1.0 / esum
    p2 = jnp.exp(emax2 - emax) / esum
    w1 = g_w * p1 / (p1 + p2)
    w2 = g_w * p2 / (p1 + p2)

    rt = jnp.where(lane == 0, i1 - N_GROUPS,
                   jnp.where(lane == 1, i2 - N_GROUPS,
                             jnp.where(lane == 2, w1, jnp.where(lane == 3, w2, 0.0))))
    rt_ref[...] = rt


def _out_route(x, o_mla, o_hg, wa, wb, gain, wr, br, tm):
    s, d = x.shape
    row = lambda i: (i, 0)
    full = lambda i: (0, 0)
    return pl.pallas_call(
        _out_route_kernel,
        grid=(s // tm,),
        in_specs=[
            pl.BlockSpec((tm, d), row),
            pl.BlockSpec((tm, D_MLA), row),
            pl.BlockSpec((tm, D_HG), row),
            pl.BlockSpec(wa.shape, full, pipeline_mode=pl.Buffered(1)),
            pl.BlockSpec(wb.shape, full, pipeline_mode=pl.Buffered(1)),
            pl.BlockSpec((1, d), full),
            pl.BlockSpec(wr.shape, full, pipeline_mode=pl.Buffered(1)),
            pl.BlockSpec((1, LANES), full),
        ],
        out_specs=[
            pl.BlockSpec((tm, d), row),
            pl.BlockSpec((tm, d // 2), row),
            pl.BlockSpec((tm, LANES), row),
        ],
        out_shape=[
            jax.ShapeDtypeStruct((s, d), F32),
            jax.ShapeDtypeStruct((s, d // 2), jnp.uint32),
            jax.ShapeDtypeStruct((s, LANES), F32),
        ],
        compiler_params=_cparams(("parallel",), VMEM_MB_LARGE),
        name="out_proj_route",
    )(x, o_mla, o_hg, wa, wb, gain, wr, br)


def _moe_kernel(be_ref, nv_ref, dst_ref, nu_ref, hn_hbm, wg_hbm, wu_hbm, wd_hbm, y_hbm,
                xbuf, xsem, obuf, osem, wgb, wub, wdb, wsem, wslot, *, n_tok):
    i = pl.program_id(0)
    nused = nu_ref[0]
    e = be_ref[i]
    active = i < nused
    first = (i == 0) | (e != be_ref[jnp.maximum(i - 1, 0)])

    def chunks(blk, fn):
        nv = nv_ref[blk]
        for lo in range(0, MOE_BLOCK, ROW_CHUNK):
            pl.when(lo < nv)(functools.partial(fn, lo))

    def gather_start(blk, sl):
        def issue(lo):
            for r in range(lo, lo + ROW_CHUNK):
                tok = dst_ref[blk * MOE_BLOCK + r] & (n_tok - 1)
                pltpu.make_async_copy(hn_hbm.at[pl.ds(tok, 1), :], xbuf.at[sl, pl.ds(r, 1), :],
                                      xsem.at[sl]).start(priority=ROW_DMA_PRIORITY)
        chunks(blk, issue)

    def gather_wait(blk, sl):
        chunks(blk, lambda lo: pltpu.make_async_copy(
            hn_hbm.at[pl.ds(0, ROW_CHUNK), :], xbuf.at[sl, pl.ds(lo, ROW_CHUNK), :], xsem.at[sl]).wait())

    def scatter_start(blk, sl):
        def issue(lo):
            for r in range(lo, lo + ROW_CHUNK):
                dst = dst_ref[blk * MOE_BLOCK + r]
                pltpu.make_async_copy(obuf.at[sl, pl.ds(r, 1), :], y_hbm.at[pl.ds(dst, 1), :],
                                      osem.at[sl]).start(priority=ROW_DMA_PRIORITY)
        chunks(blk, issue)

    def scatter_wait(blk, sl):
        chunks(blk, lambda lo: pltpu.make_async_copy(
            obuf.at[sl, pl.ds(lo, ROW_CHUNK), :], y_hbm.at[pl.ds(0, ROW_CHUNK), :], osem.at[sl]).wait())

    def spare_copy(sl):
        return pltpu.make_async_copy(obuf.at[sl], y_hbm.at[pl.ds(TOP_K * n_tok + sl * MOE_BLOCK, MOE_BLOCK), :],
                                     osem.at[sl])

    def weight_copies(ex, sl):
        return (pltpu.make_async_copy(wg_hbm.at[ex], wgb.at[sl], wsem.at[sl, 0]),
                pltpu.make_async_copy(wu_hbm.at[ex], wub.at[sl], wsem.at[sl, 1]),
                pltpu.make_async_copy(wd_hbm.at[ex], wdb.at[sl], wsem.at[sl, 2]))

    @pl.when((i == 0) & active)
    def _():
        wslot[0] = 1
        for cp in weight_copies(e, 0):
            cp.start(priority=1 - ROW_DMA_PRIORITY)
        xbuf[...] = jnp.zeros(xbuf.shape, jnp.uint32)
        gather_start(0, 0)
        obuf[...] = jnp.zeros(obuf.shape, jnp.uint32)
        for sl in (0, 1):
            spare_copy(sl).start()
        for sl in (0, 1):
            spare_copy(sl).wait()

    @pl.when(active & first)
    def _():
        sl = 1 - wslot[0]
        wslot[0] = sl
        nxt = lax.while_loop(lambda j: (j < nused) & (be_ref[jnp.minimum(j, nused - 1)] == e),
                             lambda j: j + 1, i + 1)

        @pl.when(nxt < nused)
        def _():
            for cp in weight_copies(be_ref[nxt], 1 - sl):
                cp.start(priority=1 - ROW_DMA_PRIORITY)

        for cp in weight_copies(e, sl):
            cp.wait()

    def block(xs):
        sl = wslot[0]

        @pl.when(i >= 2)
        def _():
            scatter_wait(i - 2, xs)

        gather_wait(i, xs)

        @pl.when(i + 1 < nused)
        def _():
            gather_start(i + 1, 1 - xs)

        x = _unpack_halves(xbuf[xs]).astype(BF16)
        g = jnp.dot(x, wgb[sl].astype(BF16), preferred_element_type=F32)
        u = jnp.dot(x, wub[sl].astype(BF16), preferred_element_type=F32)
        hmid = (g * _sigmoid(g) * u).astype(BF16)
        obuf[xs] = _pack_halves(jnp.dot(hmid, wdb[sl].astype(BF16), preferred_element_type=F32))
        scatter_start(i, xs)

        @pl.when(i == nused - 1)
        def _():
            @pl.when(i >= 1)
            def _():
                scatter_wait(i - 1, 1 - xs)

            scatter_wait(i, xs)

    for s_ in (0, 1):
        pl.when(active & (i % 2 == s_))(functools.partial(block, s_))


def _moe(block_e, nvalid, slot_dst, nused, hn, wg, wu, wd):
    nb = block_e.shape[0]
    n_tok, dw = hn.shape
    d = 2 * dw
    assert n_tok & (n_tok - 1) == 0
    grid_spec = pltpu.PrefetchScalarGridSpec(
        num_scalar_prefetch=4,
        grid=(nb,),
        in_specs=[pl.BlockSpec(memory_space=pl.ANY)] * 4,
        out_specs=pl.BlockSpec(memory_space=pl.ANY),
        scratch_shapes=[
            pltpu.VMEM((2, MOE_BLOCK, dw), jnp.uint32),
            pltpu.SemaphoreType.DMA((2,)),
            pltpu.VMEM((2, MOE_BLOCK, dw), jnp.uint32),
            pltpu.SemaphoreType.DMA((2,)),
            pltpu.VMEM((2, d, D_EXPERT), F32),
            pltpu.VMEM((2, d, D_EXPERT), F32),
            pltpu.VMEM((2, D_EXPERT, d), F32),
            pltpu.SemaphoreType.DMA((2, 3)),
            pltpu.SMEM((1,), jnp.int32),
        ],
    )
    return pl.pallas_call(
        functools.partial(_moe_kernel, n_tok=n_tok),
        grid_spec=grid_spec,
        out_shape=jax.ShapeDtypeStruct((TOP_K * n_tok + 2 * MOE_BLOCK, dw), jnp.uint32),
        compiler_params=_cparams(("arbitrary",), VMEM_MB_LARGE),
        name="moe_experts",
    )(block_e, nvalid, slot_dst, nused, hn, wg, wu, wd)


def _ple_kernel(h_ref, rt_ref, y0_ref, y1_ref, p_ref, wg_ref, bg_ref, wp_ref, gp_ref, gf_ref, o_ref):
    rt = rt_ref[...]
    h2 = h_ref[...] + (rt[:, TOP_K:TOP_K + 1] * _unpack_halves(y0_ref[...])
                       + rt[:, TOP_K + 1:TOP_K + 2] * _unpack_halves(y1_ref[...]))
    hn = _rms(h2, gp_ref[...]).astype(BF16)
    gate = _sigmoid(jnp.dot(hn, wg_ref[...], preferred_element_type=F32) + bg_ref[...])
    pe = jnp.dot(p_ref[...].astype(BF16), wp_ref[...], preferred_element_type=F32)
    h3 = h2 + gate * pe
    o_ref[...] = _rms(h3, gf_ref[...])


def _ple_final(h1, rt, y, p, wg, bg, wp, gp, gf, tm):
    s, d = h1.shape
    nblk = s // tm
    row = lambda i: (i, 0)
    full = lambda i: (0, 0)
    return pl.pallas_call(
        _ple_kernel,
        grid=(nblk,),
        in_specs=[
            pl.BlockSpec((tm, d), row),
            pl.BlockSpec((tm, LANES), row),
            pl.BlockSpec((tm, d // 2), row),
            pl.BlockSpec((tm, d // 2), lambda i: (i + nblk, 0)),
            pl.BlockSpec((tm, PLE_DIM), row),
            pl.BlockSpec(wg.shape, full, pipeline_mode=pl.Buffered(1)),
            pl.BlockSpec((1, d), full),
            pl.BlockSpec(wp.shape, full, pipeline_mode=pl.Buffered(1)),
            pl.BlockSpec((1, d), full),
            pl.BlockSpec((1, d), full),
        ],
        out_specs=pl.BlockSpec((tm, d), row),
        out_shape=jax.ShapeDtypeStruct((s, d), F32),
        compiler_params=_cparams(("parallel",), VMEM_MB_LARGE),
        name="ple_final",
    )(h1, rt, y, y, p, wg, bg, wp, gp, gf)


def _slots_kernel(rt_ref, tri_ref, tab_ref, cnt_ref, carry_ref, base_ref, acc_ref, *, n_tok):
    ps = pl.program_id(0)
    b = pl.program_id(1)
    tb = rt_ref.shape[0]
    n_rows = tab_ref.shape[0]
    rt = rt_ref[...]
    lane = lax.broadcasted_iota(jnp.int32, rt.shape, 1)
    lanef = lane.astype(F32)
    oh = [(lanef == rt[:, kk:kk + 1]).astype(F32) for kk in range(TOP_K)]
    both = oh[0] + oh[1]
    colsum = jnp.sum(both, axis=0, keepdims=True)

    @pl.when((ps == 0) & (b == 0))
    def _():
        cnt_ref[...] = jnp.zeros(cnt_ref.shape, F32)

    @pl.when(ps == 0)
    def _():
        cnt_ref[...] = cnt_ref[...] + colsum

    @pl.when((ps == 1) & (b == 0))
    def _():
        blocks = jnp.floor((cnt_ref[...] + (MOE_BLOCK - 0.5)) / MOE_BLOCK)
        r = lax.broadcasted_iota(jnp.int32, (LANES, LANES), 0)
        c = lax.broadcasted_iota(jnp.int32, (LANES, LANES), 1)
        before = (r < c).astype(F32)
        base_ref[...] = jnp.dot(blocks * MOE_BLOCK, before, preferred_element_type=F32,
                                precision=lax.Precision.HIGHEST)
        carry_ref[...] = jnp.zeros(carry_ref.shape, F32)
        acc_ref[...] = jnp.zeros(acc_ref.shape, F32)

    @pl.when(ps == 1)
    def _():
        earlier = jnp.dot(tri_ref[...], both.astype(BF16), preferred_element_type=F32)
        row = earlier + (base_ref[0:1, :] + carry_ref[0:1, :])
        tok = (b * tb + lax.broadcasted_iota(jnp.int32, (tb, 1), 0)).astype(F32)
        rows_f = lax.broadcasted_iota(jnp.int32, (tb, n_rows), 1).astype(F32)
        for kk in range(TOP_K):
            slot = jnp.sum(oh[kk] * row, axis=-1, keepdims=True)
            srow = jnp.floor(slot * (1.0 / LANES))
            scol = slot - LANES * srow
            val = kk * n_tok + tok
            vhi = jnp.floor(val * (1.0 / LANES))
            vlo = val - LANES * vhi
            at_row = (rows_f == srow).astype(BF16)
            at_col = lanef == scol
            x = jnp.concatenate([jnp.where(at_col, vhi, 0.0), jnp.where(at_col, vlo, 0.0),
                                 at_col.astype(F32)], axis=1).astype(BF16)
            acc_ref[...] += lax.dot_general(at_row, x, (((0,), (0,)), ((), ())), preferred_element_type=F32)
        carry_ref[...] = carry_ref[...] + colsum

    @pl.when((ps == 1) & (b == pl.num_programs(1) - 1))
    def _():
        acc = acc_ref[...]
        sidx = (lax.broadcasted_iota(jnp.int32, (n_rows, LANES), 0) * LANES
                + lax.broadcasted_iota(jnp.int32, (n_rows, LANES), 1)).astype(F32)
        blk = jnp.floor((sidx + 0.5) / MOE_BLOCK)
        spare = TOP_K * n_tok + (blk - 2.0 * jnp.floor(blk * 0.5)) * MOE_BLOCK + (sidx - MOE_BLOCK * blk)
        filled = acc[:, :LANES] * LANES + acc[:, LANES:2 * LANES]
        tab_ref[...] = jnp.where(acc[:, 2 * LANES:] > 0.5, filled, spare).astype(jnp.int32)


def _slots(rt, tb, n_slots):
    s = rt.shape[0]
    n_rows = -(-n_slots // (LANES * LANES)) * LANES
    tri = (np.arange(tb)[:, None] > np.arange(tb)[None, :]).astype(np.float32)
    return pl.pallas_call(
        functools.partial(_slots_kernel, n_tok=s),
        grid=(2, s // tb),
        in_specs=[
            pl.BlockSpec((tb, LANES), lambda ps, b: (b, 0)),
            pl.BlockSpec((tb, tb), lambda ps, b: (0, 0)),
        ],
        out_specs=[
            pl.BlockSpec((n_rows, LANES), lambda ps, b: (0, 0)),
            pl.BlockSpec((8, LANES), lambda ps, b: (0, 0)),
        ],
        out_shape=[
            jax.ShapeDtypeStruct((n_rows, LANES), jnp.int32),
            jax.ShapeDtypeStruct((8, LANES), F32),
        ],
        scratch_shapes=[pltpu.VMEM((8, LANES), F32), pltpu.VMEM((8, LANES), F32),
                        pltpu.VMEM((n_rows, 3 * LANES), F32)],
        compiler_params=_cparams(("arbitrary", "arbitrary")),
        name="moe_slots",
    )(rt, jnp.asarray(tri, BF16))


def _dispatch(rt, n_tok):
    a = n_tok * TOP_K
    nb = -(-a // MOE_BLOCK) + N_EXPERTS
    tab, cnt = _slots(rt, SLOTS_TM, nb * MOE_BLOCK)
    slot_dst = tab.reshape(-1)[:nb * MOE_BLOCK]
    counts = cnt[0, :N_EXPERTS].astype(jnp.int32)
    nblk = (counts + MOE_BLOCK - 1) // MOE_BLOCK
    bends = jnp.cumsum(nblk)
    bidx = jnp.arange(nb, dtype=jnp.int32)
    block_e = jnp.minimum(jnp.sum((bends[None, :] <= bidx[:, None]).astype(jnp.int32), axis=1), N_EXPERTS - 1)
    nused = bends[-1].astype(jnp.int32)
    nvalid = jnp.clip(counts[block_e] - (bidx - (bends - nblk)[block_e]) * MOE_BLOCK, 0, MOE_BLOCK)
    nvalid = jnp.where(bidx < nused, nvalid, 0).astype(jnp.int32)
    return block_e, nvalid, slot_dst, nused.reshape(1)


def kernel(x, p, positions, attn_norm, w_in, q_norm, w_uq, kv_norm, w_ukv, mla_norm, hg_lb_logits, hg_norm, w_out, ffn_norm, w_router_group, b_router_group, w_router_expert, b_router_expert, w_exp_gate, w_exp_up, w_exp_down, ple_norm, w_ple_gate, b_ple_gate, w_ple_proj, final_norm):
    bsz, s, d = x.shape
    assert bsz == 1 and w_in.shape[0] == 1
    xt = x[0]

    inv_freq = 1.0 / (ROPE_THETA ** (jnp.arange(0, QK_ROPE, 2, dtype=F32) / QK_ROPE))
    ang = positions[0].astype(F32)[:, None] * inv_freq
    cos, sin = jnp.cos(ang), jnp.sin(ang)
    zpad = jnp.zeros((s, LANES - QK_ROPE), F32)
    cc = jnp.concatenate([cos, cos, zpad], axis=1)
    ss = jnp.concatenate([-sin, sin, zpad], axis=1)

    lb = jnp.cumsum(jax.nn.softmax(hg_lb_logits.astype(F32), axis=0), axis=0)[0][None, :]

    wi = w_in[0]
    kr0 = Q_LORA + KV_LORA
    half = QK_ROPE // 2
    w_lat = jnp.concatenate(
        [wi[:, :kr0 + QK_ROPE], wi[:, kr0 + half:kr0 + QK_ROPE], wi[:, kr0:kr0 + half]], axis=1).astype(BF16)
    w_hg = wi[:, kr0 + QK_ROPE:].astype(BF16)
    wq3 = w_uq[0].reshape(Q_LORA, MLA_HEADS, QK_HEAD)
    wq_pad = jnp.concatenate(
        [wq3, wq3[:, :, QK_NOPE + half:], wq3[:, :, QK_NOPE:QK_NOPE + half]], axis=2
    ).reshape(Q_LORA, MLA_HEADS * QK_PAD).astype(BF16)
    wkv3 = w_ukv[0].reshape(KV_LORA, MLA_HEADS, QK_NOPE + V_HEAD)
    wknt = wkv3[:, :, :QK_NOPE].reshape(KV_LORA, MLA_HEADS * QK_NOPE).T.astype(BF16)
    wv = wkv3[:, :, QK_NOPE:].reshape(KV_LORA, D_MLA).astype(BF16)
    wo = w_out[0].astype(BF16)
    wr = jnp.concatenate(
        [w_router_group[0], w_router_expert[0], jnp.zeros((d, LANES - N_GROUPS - N_EXPERTS), F32)], axis=1)
    wr_hi = wr.astype(BF16)
    wr = jnp.concatenate([wr_hi, (wr - wr_hi.astype(F32)).astype(BF16)], axis=1)
    br = jnp.concatenate(
        [b_router_group[0], b_router_expert[0], jnp.zeros((LANES - N_GROUPS - N_EXPERTS,), F32)])[None, :]

    hg4, lat = _in_proj(xt, attn_norm, w_hg, w_lat, IN_PROJ_TM, IN_PROJ_TN)
    q, kt, v = _mla_up(lat, q_norm, kv_norm, wq_pad, wknt, wv, cc, ss, MLA_UP_TM)
    o_mla = _attention(q, kt, v, mla_norm, ATTN_BLOCK)
    mall, lv = _hgrn_tables()
    o_hg = _hgrn(hg4, lb, hg_norm[0].reshape(1, D_HG), mall, lv)

    h1, hn, rt = _out_route(xt, o_mla, o_hg, wo[:D_MLA], wo[D_MLA:], ffn_norm, wr, br, OUT_ROUTE_TM)
    block_e, nvalid, slot_dst, nused = _dispatch(rt, s)
    y = _moe(block_e, nvalid, slot_dst, nused, hn, w_exp_gate[0], w_exp_up[0], w_exp_down[0])

    out = _ple_final(h1, rt, y, p[0, 0], w_ple_gate[0].astype(BF16), b_ple_gate,
                     w_ple_proj[0].astype(BF16), ple_norm, final_norm[None, :], PLE_TM)
    return out[None]
```

```python
import functools

import jax
import jax.numpy as jnp
import numpy as np
from jax import lax
from jax.experimental import pallas as pl
from jax.experimental.pallas import tpu as pltpu

F32 = jnp.float32
BF16 = jnp.bfloat16

PLE_DIM = 256
MLA_HEADS = 8
QK_NOPE = 128
QK_ROPE = 64
QK_HEAD = QK_NOPE + QK_ROPE
QK_PAD = 256
V_HEAD = 128
V_PAD = 2 * V_HEAD
QK_AHEAD = 2
PV_LAG = 1
LAG_LIMIT = 60.0
FLOOR_LIMIT = 100.0
Q_LORA = 512
KV_LORA = 256
ROPE_THETA = 10000.0
HG_HEADS = 8
HG_DK = 128
HG_DV = 128
HG_CHUNK = 64
D_MLA = MLA_HEADS * V_HEAD
D_HG = HG_HEADS * HG_DV
N_GROUPS = 8
EXPERTS_PER_GROUP = 8
N_EXPERTS = N_GROUPS * EXPERTS_PER_GROUP
EPG_LOG2 = EXPERTS_PER_GROUP.bit_length() - 1
assert 1 << EPG_LOG2 == EXPERTS_PER_GROUP
TOP_K = 2
D_EXPERT = 512
EPS = 1e-6
LANES = 128
D_LAT = Q_LORA + KV_LORA + 2 * QK_ROPE
NEG_BIG = -1e30
LOG2E = 1.4426950408889634

MOE_BLOCK = 320
ROW_CHUNK = 32
ROW_DMA_PRIORITY = 0
HG_ROWS = 256
HG_LEVELS = (32, 16, 8, 4, 2, 1)
HG_MM_LEVELS = (2, 1)
HG_NH = 4

IN_PROJ_TM = 1024
IN_PROJ_TN = 1024
MLA_UP_TM = 512
ATTN_BLOCK = 512
OUT_ROUTE_TM = 512
SLOTS_TM = 1024
PLE_TM = 512
VMEM_MB_HGRN = 32
VMEM_MB_MID = 48
VMEM_MB_LARGE = 56


def _cparams(sem, vmem_mb=None):
    kw = dict(dimension_semantics=sem)
    if vmem_mb is not None:
        kw["vmem_limit_bytes"] = vmem_mb * 1024 * 1024
    return pltpu.CompilerParams(**kw)


def _rms(x, g):
    ms = jnp.mean(x * x, axis=-1, keepdims=True)
    return x * lax.rsqrt(ms + EPS) * g


def _sigmoid(x):
    return 1.0 / (1.0 + jnp.exp(-x))


def _pack_halves(x):
    n = x.shape[1] // 2
    bits = lax.bitcast_convert_type(x.astype(BF16).astype(F32), jnp.uint32)
    return bits[:, :n] | (bits[:, n:] >> 16)


def _unpack_halves(w):
    hi = lax.bitcast_convert_type(w & jnp.uint32(0xFFFF0000), F32)
    lo = lax.bitcast_convert_type(w << 16, F32)
    return jnp.concatenate([hi, lo], axis=1)


def _in_proj_kernel(x_ref, g_ref, whg_ref, wlat_ref, hg_ref, lat_ref, xn_ref):
    j = pl.program_id(1)
    last = pl.num_programs(1) - 1

    @pl.when(j == 0)
    def _():
        xn_ref[...] = _rms(x_ref[...], g_ref[...]).astype(BF16)

    @pl.when(j < last)
    def _():
        hg_ref[...] = jnp.dot(xn_ref[...], whg_ref[...], preferred_element_type=F32).astype(hg_ref.dtype)

    @pl.when(j == last)
    def _():
        lat_ref[...] = jnp.dot(xn_ref[...], wlat_ref[...], preferred_element_type=F32)


def _in_proj(x, gain, w_hg, w_lat, tm, tn):
    s, d = x.shape
    nt = w_hg.shape[1] // tn
    nl = w_lat.shape[1]
    hg_tile = lambda i, j: (i, jnp.minimum(j, nt - 1))
    return pl.pallas_call(
        _in_proj_kernel,
        grid=(s // tm, nt + 1),
        in_specs=[
            pl.BlockSpec((tm, d), lambda i, j: (i, 0)),
            pl.BlockSpec((1, d), lambda i, j: (0, 0)),
            pl.BlockSpec((d, tn), lambda i, j: (0, jnp.minimum(j, nt - 1))),
            pl.BlockSpec((d, nl), lambda i, j: (0, 0), pipeline_mode=pl.Buffered(1)),
        ],
        out_specs=[
            pl.BlockSpec((tm, tn), hg_tile),
            pl.BlockSpec((tm, nl), lambda i, j: (i, 0)),
        ],
        out_shape=[
            jax.ShapeDtypeStruct((s, nt * tn), BF16),
            jax.ShapeDtypeStruct((s, nl), F32),
        ],
        scratch_shapes=[pltpu.VMEM((tm, d), BF16)],
        compiler_params=_cparams(("parallel", "arbitrary"), VMEM_MB_LARGE),
        name="in_proj",
    )(x, gain, w_hg, w_lat)


def _mla_up_kernel(lat_ref, qn_ref, kvn_ref, wq_ref, wknt_ref, wv_ref, cc_ref, ss_ref,
                   q_ref, kt_ref, v_ref):
    lat = lat_ref[...]
    cq = _rms(lat[:, :Q_LORA], qn_ref[...]).astype(BF16)
    ckv = _rms(lat[:, Q_LORA:Q_LORA + KV_LORA], kvn_ref[...]).astype(BF16)
    kp = lat[:, Q_LORA + KV_LORA:D_LAT]
    cc = cc_ref[...]
    ss = ss_ref[...]
    scale = QK_HEAD ** -0.5 * LOG2E

    q = jnp.dot(cq, wq_ref[...], preferred_element_type=F32)
    for h in range(MLA_HEADS):
        base = h * QK_PAD
        q_ref[:, base:base + QK_NOPE] = (q[:, base:base + QK_NOPE] * scale).astype(BF16)
        r = q[:, base + QK_NOPE:base + QK_PAD]
        r = (r * cc + pltpu.roll(r, QK_ROPE, 1) * ss) * scale
        q_ref[:, base + QK_NOPE:base + QK_PAD] = r.astype(BF16)

    kpe_t = (kp * cc + pltpu.roll(kp, QK_ROPE, 1) * ss).T.astype(BF16)
    kn_t = lax.dot_general(wknt_ref[...], ckv, (((1,), (1,)), ((), ())), preferred_element_type=F32).astype(BF16)
    for h in range(MLA_HEADS):
        base = h * QK_PAD
        kt_ref[base:base + QK_NOPE, :] = kn_t[h * QK_NOPE:(h + 1) * QK_NOPE]
        kt_ref[base + QK_NOPE:base + QK_PAD, :] = kpe_t
    v_ref[...] = jnp.dot(ckv, wv_ref[...], preferred_element_type=F32).astype(BF16)


def _mla_up(lat, qn, kvn, wq, wknt, wv, cc, ss, tm):
    s = lat.shape[0]
    row = lambda i: (i, 0)
    full = lambda i: (0, 0)
    return pl.pallas_call(
        _mla_up_kernel,
        grid=(s // tm,),
        in_specs=[
            pl.BlockSpec((tm, D_LAT), row),
            pl.BlockSpec((1, Q_LORA), full),
            pl.BlockSpec((1, KV_LORA), full),
            pl.BlockSpec(wq.shape, full),
            pl.BlockSpec(wknt.shape, full),
            pl.BlockSpec(wv.shape, full),
            pl.BlockSpec((tm, LANES), row),
            pl.BlockSpec((tm, LANES), row),
        ],
        out_specs=[
            pl.BlockSpec((tm, MLA_HEADS * QK_PAD), row),
            pl.BlockSpec((MLA_HEADS * QK_PAD, tm), lambda i: (0, i)),
            pl.BlockSpec((tm, D_MLA), row),
        ],
        out_shape=[
            jax.ShapeDtypeStruct((s, MLA_HEADS * QK_PAD), BF16),
            jax.ShapeDtypeStruct((MLA_HEADS * QK_PAD, s), BF16),
            jax.ShapeDtypeStruct((s, D_MLA), BF16),
        ],
        compiler_params=_cparams(("parallel",), VMEM_MB_MID),
        name="mla_up",
    )(lat, qn, kvn, wq, wknt, wv, cc, ss)


def _attn_kernel(it_ref, jt_ref, q_ref, kt_ref, v_ref, g_ref, o_ref, acc_ref, m_ref, redo_ref, *, tb):
    t = pl.program_id(0)
    i = it_ref[t]
    j = jt_ref[t]
    src = t % 2
    dst = 1 - src
    ntile = tb // LANES

    @pl.when(j == 0)
    def _():
        m_ref[...] = jnp.zeros(m_ref.shape, F32)
        acc_ref[src] = jnp.zeros(acc_ref.shape[1:], F32)

    def scores(h):
        return jnp.dot(q_ref[:, h * QK_PAD:(h + 1) * QK_PAD], kt_ref[h * QK_PAD:(h + 1) * QK_PAD, :],
                       preferred_element_type=F32)

    ones = jnp.ones((tb, V_PAD - V_HEAD), BF16)

    def values(h):
        return jnp.concatenate([v_ref[:, h * V_HEAD:(h + 1) * V_HEAD], ones], axis=1)

    def key_minus_query():
        return (lax.broadcasted_iota(jnp.int32, (tb, tb), 1) - lax.broadcasted_iota(jnp.int32, (tb, tb), 0))

    def lagged(masked, first=False):
        if masked:
            keep = key_minus_query() <= (jnp.where(i == 0, 0, tb) if first else 0)
        ahead = [scores(h) for h in range(min(QK_AHEAD, MLA_HEADS))]
        pending = []
        worst = None
        least = None

        def accumulate(h, p):
            acc_ref[dst, h] = acc_ref[src, h] + jnp.dot(p, values(h), preferred_element_type=F32)

        for h in range(MLA_HEADS):
            s = ahead.pop(0)
            if h + QK_AHEAD < MLA_HEADS:
                ahead.append(scores(h + QK_AHEAD))
            d = s - jnp.concatenate([m_ref[h]] * ntile, axis=1)
            if first:
                low = jnp.min((jnp.where(keep, d, 0.0) if masked else d).reshape(tb // 8, 8, tb), axis=0)
                least = low if least is None else jnp.minimum(least, low)
            if masked:
                d = jnp.where(keep, d, NEG_BIG)
            top = jnp.max(d.reshape(tb // 8, 8, tb), axis=0)
            worst = top if worst is None else jnp.maximum(worst, top)
            pending.append((h, jnp.exp2(d).astype(BF16)))
            if len(pending) > PV_LAG:
                accumulate(*pending.pop(0))
        for item in pending:
            accumulate(*item)
        redo = jnp.max(worst) > LAG_LIMIT
        if first:
            redo = redo | (jnp.min(least) < -FLOOR_LIMIT)
        redo_ref[0] = redo.astype(jnp.int32)

    def exact():
        keep = key_minus_query() <= jnp.where(j == i, 0, tb)
        ahead = [scores(h) for h in range(min(QK_AHEAD, MLA_HEADS))]
        for h in range(MLA_HEADS):
            s = jnp.where(keep, ahead.pop(0), NEG_BIG)
            if h + QK_AHEAD < MLA_HEADS:
                ahead.append(scores(h + QK_AHEAD))
            m_prev = jnp.where(j == 0, NEG_BIG, m_ref[h])
            m_new = jnp.maximum(m_prev, jnp.max(s, axis=1, keepdims=True))
            alpha = jnp.exp2(m_prev - m_new)
            p = jnp.exp2(s - jnp.concatenate([m_new] * ntile, axis=1)).astype(BF16)
            m_ref[h] = m_new
            acc_ref[dst, h] = (acc_ref[src, h] * jnp.concatenate([alpha] * (V_PAD // LANES), axis=1)
                               + jnp.dot(p, values(h), preferred_element_type=F32))

    @pl.when(j == 0)
    def _():
        lagged(True, first=True)

    @pl.when((j > 0) & (j < i))
    def _():
        lagged(False)

    @pl.when((j > 0) & (j == i))
    def _():
        lagged(True)

    @pl.when(redo_ref[0] != 0)
    def _():
        exact()

    @pl.when(j == i)
    def _():
        outs = [acc_ref[dst, h, :, :V_HEAD] / acc_ref[dst, h, :, V_HEAD:] for h in range(MLA_HEADS)]
        o_ref[...] = _rms(jnp.concatenate(outs, axis=1), g_ref[...]).astype(o_ref.dtype)


def _attention(q, kt, v, gain, tb):
    s = q.shape[0]
    nq = s // tb
    it = np.concatenate([np.full(i + 1, i, np.int32) for i in range(nq)])
    jt = np.concatenate([np.arange(i + 1, dtype=np.int32) for i in range(nq)])
    grid_spec = pltpu.PrefetchScalarGridSpec(
        num_scalar_prefetch=2,
        grid=(it.shape[0],),
        in_specs=[
            pl.BlockSpec((tb, MLA_HEADS * QK_PAD), lambda t, it, jt: (it[t], 0)),
            pl.BlockSpec((MLA_HEADS * QK_PAD, tb), lambda t, it, jt: (0, jt[t])),
            pl.BlockSpec((tb, D_MLA), lambda t, it, jt: (jt[t], 0)),
            pl.BlockSpec((1, D_MLA), lambda t, it, jt: (0, 0)),
        ],
        out_specs=pl.BlockSpec((tb, D_MLA), lambda t, it, jt: (it[t], 0)),
        scratch_shapes=[
            pltpu.VMEM((2, MLA_HEADS, tb, V_PAD), F32),
            pltpu.VMEM((MLA_HEADS, tb, LANES), F32),
            pltpu.SMEM((1,), jnp.int32),
        ],
    )
    return pl.pallas_call(
        functools.partial(_attn_kernel, tb=tb),
        grid_spec=grid_spec,
        out_shape=jax.ShapeDtypeStruct((s, D_MLA), BF16),
        compiler_params=_cparams(("arbitrary",), VMEM_MB_MID),
        name="mla_attention",
    )(jnp.asarray(it), jnp.asarray(jt), q, kt, v, gain)


def _hgrn_tables():
    n = HG_ROWS
    r = np.arange(n)
    c = np.arange(n)
    same = (r[:, None] // HG_CHUNK) == (c[None, :] // HG_CHUNK)

    def rows_upto(idx):
        return (same & (c[None, :] <= idx[:, None])).astype(np.float32)

    blocks = [rows_upto(r)]
    for m in HG_MM_LEVELS:
        blocks.append(rows_upto((r // (2 * m)) * (2 * m) + m))
    mall = np.concatenate(blocks, axis=0)

    x = r[:, None] ^ c[None, :]
    lv = np.full((n, n), -1, np.int32)
    for li, m in enumerate(HG_LEVELS):
        lv = np.where(same & (r[:, None] > c[None, :]) & (x >= m) & (x < 2 * m), li, lv)
    lv = np.where(r[:, None] == c[None, :], len(HG_LEVELS), lv)
    return jnp.asarray(mall, BF16), jnp.asarray(lv, jnp.int32)


def _hgrn_kernel(q_ref, f_ref, i_ref, g_ref, lb_ref, gn_ref, mall_ref, lv_ref, o_ref, st_ref):
    t = pl.program_id(1)
    n = HG_ROWS

    @pl.when(t == 0)
    def _():
        st_ref[...] = jnp.zeros(st_ref.shape, F32)

    rowid = lax.broadcasted_iota(jnp.int32, (n, 1), 0)
    lv = lv_ref[...]

    def head(hh):
        cols = slice(hh * HG_DK, (hh + 1) * HG_DK)
        q_in = q_ref[:, cols].astype(F32)
        qs = q_in * _sigmoid(q_in)
        lb = lb_ref[:, cols]
        f = lb + (1.0 - lb) * _sigmoid(f_ref[:, cols].astype(F32))
        kk = 1.0 - f
        logf = jnp.log(f)
        iv = i_ref[:, cols]

        l1 = logf.astype(BF16)
        l2 = (logf - l1.astype(F32)).astype(BF16)
        parts = jnp.dot(mall_ref[...], jnp.concatenate([l1, l2], axis=1), preferred_element_type=F32)
        yield
        bc = parts[:, :HG_DK] + parts[:, HG_DK:]
        b = bc[:n]

        def anchor(period, row):
            b3 = b.reshape(n // period, period, HG_DK)
            return jnp.broadcast_to(b3[:, row:row + 1, :], b3.shape).reshape(n, HG_DK)

        a = jnp.zeros((n, n), F32)
        for li, m in enumerate(HG_LEVELS):
            if m in HG_MM_LEVELS:
                k = 1 + HG_MM_LEVELS.index(m)
                c = bc[k * n:(k + 1) * n]
            else:
                c = anchor(2 * m, m)
            e = jnp.exp(-jnp.abs(b - c))
            upper = (rowid & (2 * m - 1)) >= m
            x = (jnp.where(upper, qs, kk) * e).astype(BF16)
            p = lax.dot_general(x, x, (((1,), (1,)), ((), ())), preferred_element_type=F32)
            yield
            a = jnp.where(lv == li, p, a)
        a = jnp.where(lv == len(HG_LEVELS), jnp.sum(qs * kk, axis=-1, keepdims=True), a)
        o = jnp.dot(a.astype(BF16), iv, preferred_element_type=F32)
        yield

        blast = anchor(HG_CHUNK, HG_CHUNK - 1)
        qd = (qs * jnp.exp(b)).astype(BF16)
        kd = kk * jnp.exp(blast - b)
        ivt = iv.astype(F32).T.astype(BF16)
        st = st_ref[hh]
        outs = []
        for ci in range(n // HG_CHUNK):
            lo = ci * HG_CHUNK
            inter = lax.dot_general(qd[lo:lo + HG_CHUNK], st.astype(BF16), (((1,), (1,)), ((), ())),
                                    preferred_element_type=F32)
            kd_c = jnp.where((rowid >= lo) & (rowid < lo + HG_CHUNK), kd, 0.0).astype(BF16)
            upd = jnp.dot(ivt, kd_c, preferred_element_type=F32)
            yield
            outs.append(o[lo:lo + HG_CHUNK] + inter)
            st = jnp.exp(blast[lo:lo + 1]) * st + upd
        st_ref[hh] = st
        o = jnp.concatenate(outs, axis=0)

        g_in = g_ref[:, cols].astype(F32)
        o_ref[:, cols] = (_rms(o, gn_ref[:, cols]) * (g_in * _sigmoid(g_in))).astype(o_ref.dtype)

    progs = [head(hh) for hh in range(HG_NH)]
    while progs:
        progs = [pr for pr in progs if next(pr, "done") != "done"]


def _hgrn(hg4, lb, gn, mall, lv):
    s = hg4.shape[0]
    groups = HG_HEADS // HG_NH
    width = HG_NH * HG_DK

    def col(c):
        return lambda hh, t: (t, c * groups + hh)

    head = lambda hh, t: (0, hh)
    const = lambda hh, t: (0, 0)
    return pl.pallas_call(
        _hgrn_kernel,
        grid=(groups, s // HG_ROWS),
        in_specs=[
            pl.BlockSpec((HG_ROWS, width), col(0)),
            pl.BlockSpec((HG_ROWS, width), col(1)),
            pl.BlockSpec((HG_ROWS, width), col(2)),
            pl.BlockSpec((HG_ROWS, width), col(3)),
            pl.BlockSpec((1, width), head),
            pl.BlockSpec((1, width), head),
            pl.BlockSpec(mall.shape, const),
            pl.BlockSpec(lv.shape, const),
        ],
        out_specs=pl.BlockSpec((HG_ROWS, width), lambda hh, t: (t, hh)),
        out_shape=jax.ShapeDtypeStruct((s, D_HG), BF16),
        scratch_shapes=[pltpu.VMEM((HG_NH, HG_DV, HG_DK), F32)],
        compiler_params=_cparams(("parallel", "arbitrary"), VMEM_MB_HGRN),
        name="hgrn2",
    )(hg4, hg4, hg4, hg4, lb, gn, mall, lv)


def _out_route_kernel(x_ref, om_ref, oh_ref, wa_ref, wb_ref, g_ref, wr_ref, br_ref,
                      h_ref, hn_ref, rt_ref):
    h1 = (x_ref[...]
          + jnp.dot(om_ref[...], wa_ref[...], preferred_element_type=F32)
          + jnp.dot(oh_ref[...], wb_ref[...], preferred_element_type=F32))
    h_ref[...] = h1.astype(h_ref.dtype)
    hn = _rms(h1, g_ref[...])
    hn_ref[...] = _pack_halves(hn)
    hn_hi = hn.astype(BF16)
    hn_lo = (hn - hn_hi.astype(F32)).astype(BF16)
    hh = jnp.dot(hn_hi, wr_ref[...], preferred_element_type=F32)
    lh = jnp.dot(hn_lo, wr_ref[:, :LANES], preferred_element_type=F32)
    logits = hh[:, :LANES] + (hh[:, LANES:] + lh) + br_ref[...]

    lane = lax.broadcasted_iota(jnp.int32, logits.shape, 1)
    lanef = lane.astype(F32)
    ninf = -jnp.inf
    big = float(LANES)

    is_g = lane < N_GROUPS
    gl = jnp.where(is_g, logits, ninf)
    gmax = jnp.max(gl, axis=-1, keepdims=True)
    gsum = jnp.sum(jnp.where(is_g, jnp.exp(gl - gmax), 0.0), axis=-1, keepdims=True)
    g_w = 1.0 / gsum
    g_idx = jnp.min(jnp.where(gl == gmax, lanef, big), axis=-1, keepdims=True)

    e_lane = lane - N_GROUPS
    in_grp = (e_lane >= 0) & (e_lane < N_EXPERTS) & ((e_lane >> EPG_LOG2).astype(F32) == g_idx)
    el = jnp.where(in_grp, logits, ninf)
    emax = jnp.max(el, axis=-1, keepdims=True)
    esum = jnp.sum(jnp.where(in_grp, jnp.exp(el - emax), 0.0), axis=-1, keepdims=True)
    i1 = jnp.min(jnp.where(el == emax, lanef, big), axis=-1, keepdims=True)
    el2 = jnp.where(lanef == i1, ninf, el)
    emax2 = jnp.max(el2, axis=-1, keepdims=True)
    i2 = jnp.min(jnp.where(el2 == emax2, lanef, big), axis=-1, keepdims=True)
    p1 = 1.0 / esum
    p2 = jnp.exp(emax2 - emax) / esum
    w1 = g_w * p1 / (p1 + p2)
    w2 = g_w * p2 / (p1 + p2)

    rt = jnp.where(lane == 0, i1 - N_GROUPS,
                   jnp.where(lane == 1, i2 - N_GROUPS,
                             jnp.where(lane == 2, w1, jnp.where(lane == 3, w2, 0.0))))
    rt_ref[...] = rt


def _out_route(x, o_mla, o_hg, wa, wb, gain, wr, br, tm):
    s, d = x.shape
    row = lambda i: (i, 0)
    full = lambda i: (0, 0)
    return pl.pallas_call(
        _out_route_kernel,
        grid=(s // tm,),
        in_specs=[
            pl.BlockSpec((tm, d), row),
            pl.BlockSpec((tm, D_MLA), row),
            pl.BlockSpec((tm, D_HG), row),
            pl.BlockSpec(wa.shape, full, pipeline_mode=pl.Buffered(1)),
            pl.BlockSpec(wb.shape, full, pipeline_mode=pl.Buffered(1)),
            pl.BlockSpec((1, d), full),
            pl.BlockSpec(wr.shape, full, pipeline_mode=pl.Buffered(1)),
            pl.BlockSpec((1, LANES), full),
        ],
        out_specs=[
            pl.BlockSpec((tm, d), row),
            pl.BlockSpec((tm, d // 2), row),
            pl.BlockSpec((tm, LANES), row),
        ],
        out_shape=[
            jax.ShapeDtypeStruct((s, d), BF16),
            jax.ShapeDtypeStruct((s, d // 2), jnp.uint32),
            jax.ShapeDtypeStruct((s, LANES), F32),
        ],
        compiler_params=_cparams(("parallel",), VMEM_MB_LARGE),
        name="out_proj_route",
    )(x, o_mla, o_hg, wa, wb, gain, wr, br)


def _moe_kernel(be_ref, nv_ref, dst_ref, nu_ref, hn_hbm, wg_hbm, wu_hbm, wd_hbm, y_hbm,
                xbuf, xsem, obuf, osem, wgb, wub, wdb, wsem, wslot, *, n_tok):
    i = pl.program_id(0)
    nused = nu_ref[0]
    e = be_ref[i]
    active = i < nused
    first = (i == 0) | (e != be_ref[jnp.maximum(i - 1, 0)])

    def chunks(blk, fn):
        nv = nv_ref[blk]
        for lo in range(0, MOE_BLOCK, ROW_CHUNK):
            pl.when(lo < nv)(functools.partial(fn, lo))

    def gather_start(blk, sl):
        def issue(lo):
            for r in range(lo, lo + ROW_CHUNK):
                tok = dst_ref[blk * MOE_BLOCK + r] & (n_tok - 1)
                pltpu.make_async_copy(hn_hbm.at[pl.ds(tok, 1), :], xbuf.at[sl, pl.ds(r, 1), :],
                                      xsem.at[sl]).start(priority=ROW_DMA_PRIORITY)
        chunks(blk, issue)

    def gather_wait(blk, sl):
        chunks(blk, lambda lo: pltpu.make_async_copy(
            hn_hbm.at[pl.ds(0, ROW_CHUNK), :], xbuf.at[sl, pl.ds(lo, ROW_CHUNK), :], xsem.at[sl]).wait())

    def scatter_start(blk, sl):
        def issue(lo):
            for r in range(lo, lo + ROW_CHUNK):
                dst = dst_ref[blk * MOE_BLOCK + r]
                pltpu.make_async_copy(obuf.at[sl, pl.ds(r, 1), :], y_hbm.at[pl.ds(dst, 1), :],
                                      osem.at[sl]).start(priority=ROW_DMA_PRIORITY)
        chunks(blk, issue)

    def scatter_wait(blk, sl):
        chunks(blk, lambda lo: pltpu.make_async_copy(
            obuf.at[sl, pl.ds(lo, ROW_CHUNK), :], y_hbm.at[pl.ds(0, ROW_CHUNK), :], osem.at[sl]).wait())

    def spare_copy(sl):
        return pltpu.make_async_copy(obuf.at[sl], y_hbm.at[pl.ds(TOP_K * n_tok + sl * MOE_BLOCK, MOE_BLOCK), :],
                                     osem.at[sl])

    def weight_copies(ex, sl):
        return (pltpu.make_async_copy(wg_hbm.at[ex], wgb.at[sl], wsem.at[sl, 0]),
                pltpu.make_async_copy(wu_hbm.at[ex], wub.at[sl], wsem.at[sl, 1]),
                pltpu.make_async_copy(wd_hbm.at[ex], wdb.at[sl], wsem.at[sl, 2]))

    @pl.when((i == 0) & active)
    def _():
        wslot[0] = 1
        for cp in weight_copies(e, 0):
            cp.start(priority=1 - ROW_DMA_PRIORITY)
        xbuf[...] = jnp.zeros(xbuf.shape, jnp.uint32)
        gather_start(0, 0)
        obuf[...] = jnp.zeros(obuf.shape, jnp.uint32)
        for sl in (0, 1):
            spare_copy(sl).start()
        for sl in (0, 1):
            spare_copy(sl).wait()

    @pl.when(active & first)
    def _():
        sl = 1 - wslot[0]
        wslot[0] = sl
        nxt = lax.while_loop(lambda j: (j < nused) & (be_ref[jnp.minimum(j, nused - 1)] == e),
                             lambda j: j + 1, i + 1)

        @pl.when(nxt < nused)
        def _():
            for cp in weight_copies(be_ref[nxt], 1 - sl):
                cp.start(priority=1 - ROW_DMA_PRIORITY)

        for cp in weight_copies(e, sl):
            cp.wait()

    def block(xs):
        sl = wslot[0]

        @pl.when(i >= 2)
        def _():
            scatter_wait(i - 2, xs)

        gather_wait(i, xs)

        @pl.when(i + 1 < nused)
        def _():
            gather_start(i + 1, 1 - xs)

        x = _unpack_halves(xbuf[xs]).astype(BF16)
        g = jnp.dot(x, wgb[sl].astype(BF16), preferred_element_type=F32)
        u = jnp.dot(x, wub[sl].astype(BF16), preferred_element_type=F32)
        hmid = (g * _sigmoid(g) * u).astype(BF16)
        obuf[xs] = _pack_halves(jnp.dot(hmid, wdb[sl].astype(BF16), preferred_element_type=F32))
        scatter_start(i, xs)

        @pl.when(i == nused - 1)
        def _():
            @pl.when(i >= 1)
            def _():
                scatter_wait(i - 1, 1 - xs)

            scatter_wait(i, xs)

    for s_ in (0, 1):
        pl.when(active & (i % 2 == s_))(functools.partial(block, s_))


def _moe(block_e, nvalid, slot_dst, nused, hn, wg, wu, wd):
    nb = block_e.shape[0]
    n_tok, dw = hn.shape
    d = 2 * dw
    assert n_tok & (n_tok - 1) == 0
    grid_spec = pltpu.PrefetchScalarGridSpec(
        num_scalar_prefetch=4,
        grid=(nb,),
        in_specs=[pl.BlockSpec(memory_space=pl.ANY)] * 4,
        out_specs=pl.BlockSpec(memory_space=pl.ANY),
        scratch_shapes=[
            pltpu.VMEM((2, MOE_BLOCK, dw), jnp.uint32),
            pltpu.SemaphoreType.DMA((2,)),
            pltpu.VMEM((2, MOE_BLOCK, dw), jnp.uint32),
            pltpu.SemaphoreType.DMA((2,)),
            pltpu.VMEM((2, d, D_EXPERT), F32),
            pltpu.VMEM((2, d, D_EXPERT), F32),
            pltpu.VMEM((2, D_EXPERT, d), F32),
            pltpu.SemaphoreType.DMA((2, 3)),
            pltpu.SMEM((1,), jnp.int32),
        ],
    )
    return pl.pallas_call(
        functools.partial(_moe_kernel, n_tok=n_tok),
        grid_spec=grid_spec,
        out_shape=jax.ShapeDtypeStruct((TOP_K * n_tok + 2 * MOE_BLOCK, dw), jnp.uint32),
        compiler_params=_cparams(("arbitrary",), VMEM_MB_LARGE),
        name="moe_experts",
    )(block_e, nvalid, slot_dst, nused, hn, wg, wu, wd)


def _ple_kernel(h_ref, rt_ref, y0_ref, y1_ref, p_ref, wg_ref, bg_ref, wp_ref, gp_ref, gf_ref, o_ref):
    rt = rt_ref[...]
    h2 = h_ref[...].astype(F32) + (rt[:, TOP_K:TOP_K + 1] * _unpack_halves(y0_ref[...])
                       + rt[:, TOP_K + 1:TOP_K + 2] * _unpack_halves(y1_ref[...]))
    hn = _rms(h2, gp_ref[...]).astype(BF16)
    gate = _sigmoid(jnp.dot(hn, wg_ref[...], preferred_element_type=F32) + bg_ref[...])
    pe = jnp.dot(p_ref[...].astype(BF16), wp_ref[...], preferred_element_type=F32)
    h3 = h2 + gate * pe
    o_ref[...] = _rms(h3, gf_ref[...])


def _ple_final(h1, rt, y, p, wg, bg, wp, gp, gf, tm):
    s, d = h1.shape
    nblk = s // tm
    row = lambda i: (i, 0)
    full = lambda i: (0, 0)
    return pl.pallas_call(
        _ple_kernel,
        grid=(nblk,),
        in_specs=[
            pl.BlockSpec((tm, d), row),
            pl.BlockSpec((tm, LANES), row),
            pl.BlockSpec((tm, d // 2), row),
            pl.BlockSpec((tm, d // 2), lambda i: (i + nblk, 0)),
            pl.BlockSpec((tm, PLE_DIM), row),
            pl.BlockSpec(wg.shape, full, pipeline_mode=pl.Buffered(1)),
            pl.BlockSpec((1, d), full),
            pl.BlockSpec(wp.shape, full, pipeline_mode=pl.Buffered(1)),
            pl.BlockSpec((1, d), full),
            pl.BlockSpec((1, d), full),
        ],
        out_specs=pl.BlockSpec((tm, d), row),
        out_shape=jax.ShapeDtypeStruct((s, d), F32),
        compiler_params=_cparams(("parallel",), VMEM_MB_LARGE),
        name="ple_final",
    )(h1, rt, y, y, p, wg, bg, wp, gp, gf)


def _slots_kernel(rt_ref, tri_ref, tab_ref, cnt_ref, carry_ref, base_ref, acc_ref, *, n_tok):
    ps = pl.program_id(0)
    b = pl.program_id(1)
    tb = rt_ref.shape[0]
    n_rows = tab_ref.shape[0]
    rt = rt_ref[...]
    lane = lax.broadcasted_iota(jnp.int32, rt.shape, 1)
    lanef = lane.astype(F32)
    oh = [(lanef == rt[:, kk:kk + 1]).astype(F32) for kk in range(TOP_K)]
    both = oh[0] + oh[1]
    colsum = jnp.sum(both, axis=0, keepdims=True)

    @pl.when((ps == 0) & (b == 0))
    def _():
        cnt_ref[...] = jnp.zeros(cnt_ref.shape, F32)

    @pl.when(ps == 0)
    def _():
        cnt_ref[...] = cnt_ref[...] + colsum

    @pl.when((ps == 1) & (b == 0))
    def _():
        blocks = jnp.floor((cnt_ref[...] + (MOE_BLOCK - 0.5)) / MOE_BLOCK)
        r = lax.broadcasted_iota(jnp.int32, (LANES, LANES), 0)
        c = lax.broadcasted_iota(jnp.int32, (LANES, LANES), 1)
        before = (r < c).astype(F32)
        base_ref[...] = jnp.dot(blocks * MOE_BLOCK, before, preferred_element_type=F32,
                                precision=lax.Precision.HIGHEST)
        carry_ref[...] = jnp.zeros(carry_ref.shape, F32)
        acc_ref[...] = jnp.zeros(acc_ref.shape, F32)

    @pl.when(ps == 1)
    def _():
        earlier = jnp.dot(tri_ref[...], both.astype(BF16), preferred_element_type=F32)
        row = earlier + (base_ref[0:1, :] + carry_ref[0:1, :])
        tok = (b * tb + lax.broadcasted_iota(jnp.int32, (tb, 1), 0)).astype(F32)
        rows_f = lax.broadcasted_iota(jnp.int32, (tb, n_rows), 1).astype(F32)
        for kk in range(TOP_K):
            slot = jnp.sum(oh[kk] * row, axis=-1, keepdims=True)
            srow = jnp.floor(slot * (1.0 / LANES))
            scol = slot - LANES * srow
            val = kk * n_tok + tok
            vhi = jnp.floor(val * (1.0 / LANES))
            vlo = val - LANES * vhi
            at_row = (rows_f == srow).astype(BF16)
            at_col = lanef == scol
            x = jnp.concatenate([jnp.where(at_col, vhi, 0.0), jnp.where(at_col, vlo, 0.0),
                                 at_col.astype(F32)], axis=1).astype(BF16)
            acc_ref[...] += lax.dot_general(at_row, x, (((0,), (0,)), ((), ())), preferred_element_type=F32)
        carry_ref[...] = carry_ref[...] + colsum

    @pl.when((ps == 1) & (b == pl.num_programs(1) - 1))
    def _():
        acc = acc_ref[...]
        sidx = (lax.broadcasted_iota(jnp.int32, (n_rows, LANES), 0) * LANES
                + lax.broadcasted_iota(jnp.int32, (n_rows, LANES), 1)).astype(F32)
        blk = jnp.floor((sidx + 0.5) / MOE_BLOCK)
        spare = TOP_K * n_tok + (blk - 2.0 * jnp.floor(blk * 0.5)) * MOE_BLOCK + (sidx - MOE_BLOCK * blk)
        filled = acc[:, :LANES] * LANES + acc[:, LANES:2 * LANES]
        tab_ref[...] = jnp.where(acc[:, 2 * LANES:] > 0.5, filled, spare).astype(jnp.int32)


def _slots(rt, tb, n_slots):
    s = rt.shape[0]
    n_rows = -(-n_slots // (LANES * LANES)) * LANES
    tri = (np.arange(tb)[:, None] > np.arange(tb)[None, :]).astype(np.float32)
    return pl.pallas_call(
        functools.partial(_slots_kernel, n_tok=s),
        grid=(2, s // tb),
        in_specs=[
            pl.BlockSpec((tb, LANES), lambda ps, b: (b, 0)),
            pl.BlockSpec((tb, tb), lambda ps, b: (0, 0)),
        ],
        out_specs=[
            pl.BlockSpec((n_rows, LANES), lambda ps, b: (0, 0)),
            pl.BlockSpec((8, LANES), lambda ps, b: (0, 0)),
        ],
        out_shape=[
            jax.ShapeDtypeStruct((n_rows, LANES), jnp.int32),
            jax.ShapeDtypeStruct((8, LANES), F32),
        ],
        scratch_shapes=[pltpu.VMEM((8, LANES), F32), pltpu.VMEM((8, LANES), F32),
                        pltpu.VMEM((n_rows, 3 * LANES), F32)],
        compiler_params=_cparams(("arbitrary", "arbitrary")),
        name="moe_slots",
    )(rt, jnp.asarray(tri, BF16))


def _dispatch(rt, n_tok):
    a = n_tok * TOP_K
    nb = -(-a // MOE_BLOCK) + N_EXPERTS
    tab, cnt = _slots(rt, SLOTS_TM, nb * MOE_BLOCK)
    slot_dst = tab.reshape(-1)[:nb * MOE_BLOCK]
    counts = cnt[0, :N_EXPERTS].astype(jnp.int32)
    nblk = (counts + MOE_BLOCK - 1) // MOE_BLOCK
    bends = jnp.cumsum(nblk)
    bidx = jnp.arange(nb, dtype=jnp.int32)
    block_e = jnp.minimum(jnp.sum((bends[None, :] <= bidx[:, None]).astype(jnp.int32), axis=1), N_EXPERTS - 1)
    nused = bends[-1].astype(jnp.int32)
    nvalid = jnp.clip(counts[block_e] - (bidx - (bends - nblk)[block_e]) * MOE_BLOCK, 0, MOE_BLOCK)
    nvalid = jnp.where(bidx < nused, nvalid, 0).astype(jnp.int32)
    return block_e, nvalid, slot_dst, nused.reshape(1)


def kernel(x, p, positions, attn_norm, w_in, q_norm, w_uq, kv_norm, w_ukv, mla_norm, hg_lb_logits, hg_norm, w_out, ffn_norm, w_router_group, b_router_group, w_router_expert, b_router_expert, w_exp_gate, w_exp_up, w_exp_down, ple_norm, w_ple_gate, b_ple_gate, w_ple_proj, final_norm):
    bsz, s, d = x.shape
    assert bsz == 1 and w_in.shape[0] == 1
    xt = x[0]

    inv_freq = 1.0 / (ROPE_THETA ** (jnp.arange(0, QK_ROPE, 2, dtype=F32) / QK_ROPE))
    ang = positions[0].astype(F32)[:, None] * inv_freq
    cos, sin = jnp.cos(ang), jnp.sin(ang)
    zpad = jnp.zeros((s, LANES - QK_ROPE), F32)
    cc = jnp.concatenate([cos, cos, zpad], axis=1)
    ss = jnp.concatenate([-sin, sin, zpad], axis=1)

    lb = jnp.cumsum(jax.nn.softmax(hg_lb_logits.astype(F32), axis=0), axis=0)[0][None, :]

    wi = w_in[0]
    kr0 = Q_LORA + KV_LORA
    half = QK_ROPE // 2
    w_lat = jnp.concatenate(
        [wi[:, :kr0 + QK_ROPE], wi[:, kr0 + half:kr0 + QK_ROPE], wi[:, kr0:kr0 + half]], axis=1).astype(BF16)
    w_hg = wi[:, kr0 + QK_ROPE:].astype(BF16)
    wq3 = w_uq[0].reshape(Q_LORA, MLA_HEADS, QK_HEAD)
    wq_pad = jnp.concatenate(
        [wq3, wq3[:, :, QK_NOPE + half:], wq3[:, :, QK_NOPE:QK_NOPE + half]], axis=2
    ).reshape(Q_LORA, MLA_HEADS * QK_PAD).astype(BF16)
    wkv3 = w_ukv[0].reshape(KV_LORA, MLA_HEADS, QK_NOPE + V_HEAD)
    wknt = wkv3[:, :, :QK_NOPE].reshape(KV_LORA, MLA_HEADS * QK_NOPE).T.astype(BF16)
    wv = wkv3[:, :, QK_NOPE:].reshape(KV_LORA, D_MLA).astype(BF16)
    wo = w_out[0].astype(BF16)
    wr = jnp.concatenate(
        [w_router_group[0], w_router_expert[0], jnp.zeros((d, LANES - N_GROUPS - N_EXPERTS), F32)], axis=1)
    wr_hi = wr.astype(BF16)
    wr = jnp.concatenate([wr_hi, (wr - wr_hi.astype(F32)).astype(BF16)], axis=1)
    br = jnp.concatenate(
        [b_router_group[0], b_router_expert[0], jnp.zeros((LANES - N_GROUPS - N_EXPERTS,), F32)])[None, :]

    hg4, lat = _in_proj(xt, attn_norm, w_hg, w_lat, IN_PROJ_TM, IN_PROJ_TN)
    q, kt, v = _mla_up(lat, q_norm, kv_norm, wq_pad, wknt, wv, cc, ss, MLA_UP_TM)
    o_mla = _attention(q, kt, v, mla_norm, ATTN_BLOCK)
    mall, lv = _hgrn_tables()
    o_hg = _hgrn(hg4, lb, hg_norm[0].reshape(1, D_HG), mall, lv)

    h1, hn, rt = _out_route(xt, o_mla, o_hg, wo[:D_MLA], wo[D_MLA:], ffn_norm, wr, br, OUT_ROUTE_TM)
    block_e, nvalid, slot_dst, nused = _dispatch(rt, s)
    y = _moe(block_e, nvalid, slot_dst, nused, hn, w_exp_gate[0], w_exp_up[0], w_exp_down[0])

    out = _ple_final(h1, rt, y, p[0, 0], w_ple_gate[0].astype(BF16), b_ple_gate,
                     w_ple_proj[0].astype(BF16), ple_norm, final_norm[None, :], PLE_TM)
    return out[None]
```

```python
import functools

import jax
import jax.numpy as jnp
import numpy as np
from jax import lax
from jax.experimental import pallas as pl
from jax.experimental.pallas import tpu as pltpu

F32 = jnp.float32
BF16 = jnp.bfloat16

PLE_DIM = 256
MLA_HEADS = 8
QK_NOPE = 128
QK_ROPE = 64
QK_HEAD = QK_NOPE + QK_ROPE
QK_PAD = 256
V_HEAD = 128
V_PAD = 2 * V_HEAD
QK_AHEAD = 2
PV_LAG = 1
LAG_LIMIT = 60.0
FLOOR_LIMIT = 100.0
Q_LORA = 512
KV_LORA = 256
ROPE_THETA = 10000.0
HG_HEADS = 8
HG_DK = 128
HG_DV = 128
HG_CHUNK = 64
D_MLA = MLA_HEADS * V_HEAD
D_HG = HG_HEADS * HG_DV
N_GROUPS = 8
EXPERTS_PER_GROUP = 8
N_EXPERTS = N_GROUPS * EXPERTS_PER_GROUP
EPG_LOG2 = EXPERTS_PER_GROUP.bit_length() - 1
assert 1 << EPG_LOG2 == EXPERTS_PER_GROUP
TOP_K = 2
D_EXPERT = 512
EPS = 1e-6
LANES = 128
D_LAT = Q_LORA + KV_LORA + 2 * QK_ROPE
NEG_BIG = -1e30
LOG2E = 1.4426950408889634

MOE_BLOCK = 320
ROW_CHUNK = 32
ROW_DMA_PRIORITY = 0
HG_ROWS = 256
HG_LEVELS = (32, 16, 8, 4, 2, 1)
HG_MM_LEVELS = (2, 1)
HG_NH = 4

IN_PROJ_TM = 1024
IN_PROJ_TN = 1024
MLA_UP_TM = 512
ATTN_BLOCK = 512
OUT_ROUTE_TM = 512
SLOTS_TM = 1024
PLE_TM = 512
VMEM_MB_HGRN = 32
VMEM_MB_MID = 48
VMEM_MB_LARGE = 56


def _cparams(sem, vmem_mb=None):
    kw = dict(dimension_semantics=sem)
    if vmem_mb is not None:
        kw["vmem_limit_bytes"] = vmem_mb * 1024 * 1024
    return pltpu.CompilerParams(**kw)


def _rms(x, g):
    ms = jnp.mean(x * x, axis=-1, keepdims=True)
    return x * lax.rsqrt(ms + EPS) * g


def _sigmoid(x):
    return 1.0 / (1.0 + jnp.exp(-x))


def _pack_halves(x):
    n = x.shape[1] // 2
    bits = lax.bitcast_convert_type(x.astype(BF16).astype(F32), jnp.uint32)
    return bits[:, :n] | (bits[:, n:] >> 16)


def _unpack_halves(w):
    hi = lax.bitcast_convert_type(w & jnp.uint32(0xFFFF0000), F32)
    lo = lax.bitcast_convert_type(w << 16, F32)
    return jnp.concatenate([hi, lo], axis=1)


def _in_proj_kernel(x_ref, g_ref, whg_ref, wlat_ref, hg_ref, lat_ref, xn_ref):
    j = pl.program_id(1)
    last = pl.num_programs(1) - 1

    @pl.when(j == 0)
    def _():
        xn_ref[...] = _rms(x_ref[...], g_ref[...]).astype(BF16)

    @pl.when(j < last)
    def _():
        hg_ref[...] = jnp.dot(xn_ref[...], whg_ref[...], preferred_element_type=F32).astype(hg_ref.dtype)

    @pl.when(j == last)
    def _():
        lat_ref[...] = jnp.dot(xn_ref[...], wlat_ref[...], preferred_element_type=F32)


def _in_proj(x, gain, w_hg, w_lat, tm, tn):
    s, d = x.shape
    nt = w_hg.shape[1] // tn
    nl = w_lat.shape[1]
    hg_tile = lambda i, j: (i, jnp.minimum(j, nt - 1))
    return pl.pallas_call(
        _in_proj_kernel,
        grid=(s // tm, nt + 1),
        in_specs=[
            pl.BlockSpec((tm, d), lambda i, j: (i, 0)),
            pl.BlockSpec((1, d), lambda i, j: (0, 0)),
            pl.BlockSpec((d, tn), lambda i, j: (0, jnp.minimum(j, nt - 1))),
            pl.BlockSpec((d, nl), lambda i, j: (0, 0), pipeline_mode=pl.Buffered(1)),
        ],
        out_specs=[
            pl.BlockSpec((tm, tn), hg_tile),
            pl.BlockSpec((tm, nl), lambda i, j: (i, 0)),
        ],
        out_shape=[
            jax.ShapeDtypeStruct((s, nt * tn), BF16),
            jax.ShapeDtypeStruct((s, nl), F32),
        ],
        scratch_shapes=[pltpu.VMEM((tm, d), BF16)],
        compiler_params=_cparams(("parallel", "arbitrary"), VMEM_MB_LARGE),
        name="in_proj",
    )(x, gain, w_hg, w_lat)


def _mla_up_kernel(lat_ref, qn_ref, kvn_ref, wq_ref, wknt_ref, wv_ref, cc_ref, ss_ref,
                   q_ref, kt_ref, v_ref):
    lat = lat_ref[...]
    cq = _rms(lat[:, :Q_LORA], qn_ref[...]).astype(BF16)
    ckv = _rms(lat[:, Q_LORA:Q_LORA + KV_LORA], kvn_ref[...]).astype(BF16)
    kp = lat[:, Q_LORA + KV_LORA:D_LAT]
    cc = cc_ref[...]
    ss = ss_ref[...]
    scale = QK_HEAD ** -0.5 * LOG2E

    q = jnp.dot(cq, wq_ref[...], preferred_element_type=F32)
    for h in range(MLA_HEADS):
        base = h * QK_PAD
        q_ref[:, base:base + QK_NOPE] = (q[:, base:base + QK_NOPE] * scale).astype(BF16)
        r = q[:, base + QK_NOPE:base + QK_PAD]
        r = (r * cc + pltpu.roll(r, QK_ROPE, 1) * ss) * scale
        q_ref[:, base + QK_NOPE:base + QK_PAD] = r.astype(BF16)

    kpe_t = (kp * cc + pltpu.roll(kp, QK_ROPE, 1) * ss).T.astype(BF16)
    kn_t = lax.dot_general(wknt_ref[...], ckv, (((1,), (1,)), ((), ())), preferred_element_type=F32).astype(BF16)
    for h in range(MLA_HEADS):
        base = h * QK_PAD
        kt_ref[base:base + QK_NOPE, :] = kn_t[h * QK_NOPE:(h + 1) * QK_NOPE]
        kt_ref[base + QK_NOPE:base + QK_PAD, :] = kpe_t
    v = jnp.dot(ckv, wv_ref[...], preferred_element_type=F32).astype(BF16)
    ones = jnp.ones((v.shape[0], V_PAD - V_HEAD), BF16)
    for h in range(MLA_HEADS):
        v_ref[:, h * V_PAD:h * V_PAD + V_HEAD] = v[:, h * V_HEAD:(h + 1) * V_HEAD]
        v_ref[:, h * V_PAD + V_HEAD:(h + 1) * V_PAD] = ones


def _mla_up(lat, qn, kvn, wq, wknt, wv, cc, ss, tm):
    s = lat.shape[0]
    row = lambda i: (i, 0)
    full = lambda i: (0, 0)
    return pl.pallas_call(
        _mla_up_kernel,
        grid=(s // tm,),
        in_specs=[
            pl.BlockSpec((tm, D_LAT), row),
            pl.BlockSpec((1, Q_LORA), full),
            pl.BlockSpec((1, KV_LORA), full),
            pl.BlockSpec(wq.shape, full),
            pl.BlockSpec(wknt.shape, full),
            pl.BlockSpec(wv.shape, full),
            pl.BlockSpec((tm, LANES), row),
            pl.BlockSpec((tm, LANES), row),
        ],
        out_specs=[
            pl.BlockSpec((tm, MLA_HEADS * QK_PAD), row),
            pl.BlockSpec((MLA_HEADS * QK_PAD, tm), lambda i: (0, i)),
            pl.BlockSpec((tm, MLA_HEADS * V_PAD), row),
        ],
        out_shape=[
            jax.ShapeDtypeStruct((s, MLA_HEADS * QK_PAD), BF16),
            jax.ShapeDtypeStruct((MLA_HEADS * QK_PAD, s), BF16),
            jax.ShapeDtypeStruct((s, MLA_HEADS * V_PAD), BF16),
        ],
        compiler_params=_cparams(("parallel",), VMEM_MB_MID),
        name="mla_up",
    )(lat, qn, kvn, wq, wknt, wv, cc, ss)


def _attn_kernel(it_ref, jt_ref, q_ref, kt_ref, v_ref, g_ref, o_ref, acc_ref, m_ref, redo_ref, *, tb):
    t = pl.program_id(0)
    i = it_ref[t]
    j = jt_ref[t]
    src = t % 2
    dst = 1 - src
    ntile = tb // LANES

    @pl.when(j == 0)
    def _():
        m_ref[...] = jnp.zeros(m_ref.shape, F32)
        acc_ref[src] = jnp.zeros(acc_ref.shape[1:], F32)

    def scores(h):
        return jnp.dot(q_ref[:, h * QK_PAD:(h + 1) * QK_PAD], kt_ref[h * QK_PAD:(h + 1) * QK_PAD, :],
                       preferred_element_type=F32)

    def values(h):
        return v_ref[:, h * V_PAD:(h + 1) * V_PAD]

    def key_minus_query():
        return (lax.broadcasted_iota(jnp.int32, (tb, tb), 1) - lax.broadcasted_iota(jnp.int32, (tb, tb), 0))

    def lagged(masked, first=False):
        if masked:
            keep = key_minus_query() <= (jnp.where(i == 0, 0, tb) if first else 0)
        ahead = [scores(h) for h in range(min(QK_AHEAD, MLA_HEADS))]
        pending = []
        worst = None
        least = None

        def accumulate(h, p):
            acc_ref[dst, h] = acc_ref[src, h] + jnp.dot(p, values(h), preferred_element_type=F32)

        for h in range(MLA_HEADS):
            s = ahead.pop(0)
            if h + QK_AHEAD < MLA_HEADS:
                ahead.append(scores(h + QK_AHEAD))
            d = s - jnp.concatenate([m_ref[h]] * ntile, axis=1)
            if first:
                low = jnp.min((jnp.where(keep, d, 0.0) if masked else d).reshape(tb // 8, 8, tb), axis=0)
                least = low if least is None else jnp.minimum(least, low)
            if masked:
                d = jnp.where(keep, d, NEG_BIG)
            top = jnp.max(d.reshape(tb // 8, 8, tb), axis=0)
            worst = top if worst is None else jnp.maximum(worst, top)
            pending.append((h, jnp.exp2(d).astype(BF16)))
            if len(pending) > PV_LAG:
                accumulate(*pending.pop(0))
        for item in pending:
            accumulate(*item)
        redo = jnp.max(worst) > LAG_LIMIT
        if first:
            redo = redo | (jnp.min(least) < -FLOOR_LIMIT)
        redo_ref[0] = redo.astype(jnp.int32)

    def exact():
        keep = key_minus_query() <= jnp.where(j == i, 0, tb)
        ahead = [scores(h) for h in range(min(QK_AHEAD, MLA_HEADS))]
        for h in range(MLA_HEADS):
            s = jnp.where(keep, ahead.pop(0), NEG_BIG)
            if h + QK_AHEAD < MLA_HEADS:
                ahead.append(scores(h + QK_AHEAD))
            m_prev = jnp.where(j == 0, NEG_BIG, m_ref[h])
            m_new = jnp.maximum(m_prev, jnp.max(s, axis=1, keepdims=True))
            alpha = jnp.exp2(m_prev - m_new)
            p = jnp.exp2(s - jnp.concatenate([m_new] * ntile, axis=1)).astype(BF16)
            m_ref[h] = m_new
            acc_ref[dst, h] = (acc_ref[src, h] * jnp.concatenate([alpha] * (V_PAD // LANES), axis=1)
                               + jnp.dot(p, values(h), preferred_element_type=F32))

    @pl.when(j == 0)
    def _():
        lagged(True, first=True)

    @pl.when((j > 0) & (j < i))
    def _():
        lagged(False)

    @pl.when((j > 0) & (j == i))
    def _():
        lagged(True)

    @pl.when(redo_ref[0] != 0)
    def _():
        exact()

    @pl.when(j == i)
    def _():
        outs = [acc_ref[dst, h, :, :V_HEAD] / acc_ref[dst, h, :, V_HEAD:] for h in range(MLA_HEADS)]
        o_ref[...] = _rms(jnp.concatenate(outs, axis=1), g_ref[...]).astype(o_ref.dtype)


def _attention(q, kt, v, gain, tb):
    s = q.shape[0]
    nq = s // tb
    it = np.concatenate([np.full(i + 1, i, np.int32) for i in range(nq)])
    jt = np.concatenate([np.arange(i + 1, dtype=np.int32) for i in range(nq)])
    grid_spec = pltpu.PrefetchScalarGridSpec(
        num_scalar_prefetch=2,
        grid=(it.shape[0],),
        in_specs=[
            pl.BlockSpec((tb, MLA_HEADS * QK_PAD), lambda t, it, jt: (it[t], 0)),
            pl.BlockSpec((MLA_HEADS * QK_PAD, tb), lambda t, it, jt: (0, jt[t])),
            pl.BlockSpec((tb, MLA_HEADS * V_PAD), lambda t, it, jt: (jt[t], 0)),
            pl.BlockSpec((1, D_MLA), lambda t, it, jt: (0, 0)),
        ],
        out_specs=pl.BlockSpec((tb, D_MLA), lambda t, it, jt: (it[t], 0)),
        scratch_shapes=[
            pltpu.VMEM((2, MLA_HEADS, tb, V_PAD), F32),
            pltpu.VMEM((MLA_HEADS, tb, LANES), F32),
            pltpu.SMEM((1,), jnp.int32),
        ],
    )
    return pl.pallas_call(
        functools.partial(_attn_kernel, tb=tb),
        grid_spec=grid_spec,
        out_shape=jax.ShapeDtypeStruct((s, D_MLA), BF16),
        compiler_params=_cparams(("arbitrary",), VMEM_MB_MID),
        name="mla_attention",
    )(jnp.asarray(it), jnp.asarray(jt), q, kt, v, gain)


def _hgrn_tables():
    n = HG_ROWS
    r = np.arange(n)
    c = np.arange(n)
    same = (r[:, None] // HG_CHUNK) == (c[None, :] // HG_CHUNK)

    def rows_upto(idx):
        return (same & (c[None, :] <= idx[:, None])).astype(np.float32)

    blocks = [rows_upto(r)]
    for m in HG_MM_LEVELS:
        blocks.append(rows_upto((r // (2 * m)) * (2 * m) + m))
    mall = np.concatenate(blocks, axis=0)

    x = r[:, None] ^ c[None, :]
    lv = np.full((n, n), -1, np.int32)
    for li, m in enumerate(HG_LEVELS):
        lv = np.where(same & (r[:, None] > c[None, :]) & (x >= m) & (x < 2 * m), li, lv)
    lv = np.where(r[:, None] == c[None, :], len(HG_LEVELS), lv)
    return jnp.asarray(mall, BF16), jnp.asarray(lv, jnp.int32)


def _hgrn_kernel(q_ref, f_ref, i_ref, g_ref, lb_ref, gn_ref, mall_ref, lv_ref, o_ref, st_ref):
    t = pl.program_id(1)
    n = HG_ROWS

    @pl.when(t == 0)
    def _():
        st_ref[...] = jnp.zeros(st_ref.shape, F32)

    rowid = lax.broadcasted_iota(jnp.int32, (n, 1), 0)
    lv = lv_ref[...]

    def head(hh):
        cols = slice(hh * HG_DK, (hh + 1) * HG_DK)
        q_in = q_ref[:, cols].astype(F32)
        qs = q_in * _sigmoid(q_in)
        lb = lb_ref[:, cols]
        f = lb + (1.0 - lb) * _sigmoid(f_ref[:, cols].astype(F32))
        kk = 1.0 - f
        logf = jnp.log(f)
        iv = i_ref[:, cols]

        l1 = logf.astype(BF16)
        l2 = (logf - l1.astype(F32)).astype(BF16)
        parts = jnp.dot(mall_ref[...], jnp.concatenate([l1, l2], axis=1), preferred_element_type=F32)
        yield
        bc = parts[:, :HG_DK] + parts[:, HG_DK:]
        b = bc[:n]

        def anchor(period, row):
            b3 = b.reshape(n // period, period, HG_DK)
            return jnp.broadcast_to(b3[:, row:row + 1, :], b3.shape).reshape(n, HG_DK)

        a = jnp.zeros((n, n), F32)
        for li, m in enumerate(HG_LEVELS):
            if m in HG_MM_LEVELS:
                k = 1 + HG_MM_LEVELS.index(m)
                c = bc[k * n:(k + 1) * n]
            else:
                c = anchor(2 * m, m)
            e = jnp.exp(-jnp.abs(b - c))
            upper = (rowid & (2 * m - 1)) >= m
            x = (jnp.where(upper, qs, kk) * e).astype(BF16)
            p = lax.dot_general(x, x, (((1,), (1,)), ((), ())), preferred_element_type=F32)
            yield
            a = jnp.where(lv == li, p, a)
        a = jnp.where(lv == len(HG_LEVELS), jnp.sum(qs * kk, axis=-1, keepdims=True), a)
        o = jnp.dot(a.astype(BF16), iv, preferred_element_type=F32)
        yield

        blast = anchor(HG_CHUNK, HG_CHUNK - 1)
        qd = (qs * jnp.exp(b)).astype(BF16)
        kd = kk * jnp.exp(blast - b)
        ivt = iv.astype(F32).T.astype(BF16)
        kd_by_chunk = jnp.concatenate(
            [jnp.where((rowid >= lo) & (rowid < lo + HG_CHUNK), kd, 0.0) for lo in range(0, n, HG_CHUNK)],
            axis=1).astype(BF16)
        upd_all = jnp.dot(ivt, kd_by_chunk, preferred_element_type=F32)
        yield
        st = st_ref[hh]
        outs = []
        for ci in range(n // HG_CHUNK):
            lo = ci * HG_CHUNK
            inter = lax.dot_general(qd[lo:lo + HG_CHUNK], st.astype(BF16), (((1,), (1,)), ((), ())),
                                    preferred_element_type=F32)
            yield
            outs.append(o[lo:lo + HG_CHUNK] + inter)
            st = jnp.exp(blast[lo:lo + 1]) * st + upd_all[:, ci * HG_DK:(ci + 1) * HG_DK]
        st_ref[hh] = st
        o = jnp.concatenate(outs, axis=0)

        g_in = g_ref[:, cols].astype(F32)
        o_ref[:, cols] = (_rms(o, gn_ref[:, cols]) * (g_in * _sigmoid(g_in))).astype(o_ref.dtype)

    progs = [head(hh) for hh in range(HG_NH)]
    while progs:
        progs = [pr for pr in progs if next(pr, "done") != "done"]


def _hgrn(hg4, lb, gn, mall, lv):
    s = hg4.shape[0]
    groups = HG_HEADS // HG_NH
    width = HG_NH * HG_DK

    def col(c):
        return lambda hh, t: (t, c * groups + hh)

    head = lambda hh, t: (0, hh)
    const = lambda hh, t: (0, 0)
    return pl.pallas_call(
        _hgrn_kernel,
        grid=(groups, s // HG_ROWS),
        in_specs=[
            pl.BlockSpec((HG_ROWS, width), col(0)),
            pl.BlockSpec((HG_ROWS, width), col(1)),
            pl.BlockSpec((HG_ROWS, width), col(2)),
            pl.BlockSpec((HG_ROWS, width), col(3)),
            pl.BlockSpec((1, width), head),
            pl.BlockSpec((1, width), head),
            pl.BlockSpec(mall.shape, const),
            pl.BlockSpec(lv.shape, const),
        ],
        out_specs=pl.BlockSpec((HG_ROWS, width), lambda hh, t: (t, hh)),
        out_shape=jax.ShapeDtypeStruct((s, D_HG), BF16),
        scratch_shapes=[pltpu.VMEM((HG_NH, HG_DV, HG_DK), F32)],
        compiler_params=_cparams(("parallel", "arbitrary"), VMEM_MB_HGRN),
        name="hgrn2",
    )(hg4, hg4, hg4, hg4, lb, gn, mall, lv)


def _out_route_kernel(x_ref, om_ref, oh_ref, wa_ref, wb_ref, g_ref, wr_ref, br_ref,
                      h_ref, hn_ref, rt_ref):
    h1 = (x_ref[...]
          + jnp.dot(om_ref[...], wa_ref[...], preferred_element_type=F32)
          + jnp.dot(oh_ref[...], wb_ref[...], preferred_element_type=F32))
    h_ref[...] = h1
    hn = _rms(h1, g_ref[...])
    hn_ref[...] = _pack_halves(hn)
    hn_hi = hn.astype(BF16)
    hn_lo = (hn - hn_hi.astype(F32)).astype(BF16)
    hh = jnp.dot(hn_hi, wr_ref[...], preferred_element_type=F32)
    lh = jnp.dot(hn_lo, wr_ref[:, :LANES], preferred_element_type=F32)
    logits = hh[:, :LANES] + (hh[:, LANES:] + lh) + br_ref[...]

    lane = lax.broadcasted_iota(jnp.int32, logits.shape, 1)
    lanef = lane.astype(F32)
    ninf = -jnp.inf
    big = float(LANES)

    is_g = lane < N_GROUPS
    gl = jnp.where(is_g, logits, ninf)
    gmax = jnp.max(gl, axis=-1, keepdims=True)
    gsum = jnp.sum(jnp.where(is_g, jnp.exp(gl - gmax), 0.0), axis=-1, keepdims=True)
    g_w = 1.0 / gsum
    g_idx = jnp.min(jnp.where(gl == gmax, lanef, big), axis=-1, keepdims=True)

    e_lane = lane - N_GROUPS
    in_grp = (e_lane >= 0) & (e_lane < N_EXPERTS) & ((e_lane >> EPG_LOG2).astype(F32) == g_idx)
    el = jnp.where(in_grp, logits, ninf)
    emax = jnp.max(el, axis=-1, keepdims=True)
    esum = jnp.sum(jnp.where(in_grp, jnp.exp(el - emax), 0.0), axis=-1, keepdims=True)
    i1 = jnp.min(jnp.where(el == emax, lanef, big), axis=-1, keepdims=True)
    el2 = jnp.where(lanef == i1, ninf, el)
    emax2 = jnp.max(el2, axis=-1, keepdims=True)
    i2 = jnp.min(jnp.where(el2 == emax2, lanef, big), axis=-1, keepdims=True)
    p1 = 1.0 / esum
    p2 = jnp.exp(emax2 - emax) / esum
    w1 = g_w * p1 / (p1 + p2)
    w2 = g_w * p2 / (p1 + p2)

    rt = jnp.where(lane == 0, i1 - N_GROUPS,
                   jnp.where(lane == 1, i2 - N_GROUPS,
                             jnp.where(lane == 2, w1, jnp.where(lane == 3, w2, 0.0))))
    rt_ref[...] = rt


def _out_route(x, o_mla, o_hg, wa, wb, gain, wr, br, tm):
    s, d = x.shape
    row = lambda i: (i, 0)
    full = lambda i: (0, 0)
    return pl.pallas_call(
        _out_route_kernel,
        grid=(s // tm,),
        in_specs=[
            pl.BlockSpec((tm, d), row),
            pl.BlockSpec((tm, D_MLA), row),
            pl.BlockSpec((tm, D_HG), row),
            pl.BlockSpec(wa.shape, full, pipeline_mode=pl.Buffered(1)),
            pl.BlockSpec(wb.shape, full, pipeline_mode=pl.Buffered(1)),
            pl.BlockSpec((1, d), full),
            pl.BlockSpec(wr.shape, full, pipeline_mode=pl.Buffered(1)),
            pl.BlockSpec((1, LANES), full),
        ],
        out_specs=[
            pl.BlockSpec((tm, d), row),
            pl.BlockSpec((tm, d // 2), row),
            pl.BlockSpec((tm, LANES), row),
        ],
        out_shape=[
            jax.ShapeDtypeStruct((s, d), F32),
            jax.ShapeDtypeStruct((s, d // 2), jnp.uint32),
            jax.ShapeDtypeStruct((s, LANES), F32),
        ],
        compiler_params=_cparams(("parallel",), VMEM_MB_LARGE),
        name="out_proj_route",
    )(x, o_mla, o_hg, wa, wb, gain, wr, br)


def _moe_kernel(be_ref, nv_ref, dst_ref, nu_ref, hn_hbm, wg_hbm, wu_hbm, wd_hbm, y_hbm,
                xbuf, xsem, obuf, osem, wgb, wub, wdb, wsem, wslot, *, n_tok):
    i = pl.program_id(0)
    nused = nu_ref[0]
    e = be_ref[i]
    active = i < nused
    first = (i == 0) | (e != be_ref[jnp.maximum(i - 1, 0)])

    def chunks(blk, fn):
        nv = nv_ref[blk]
        for lo in range(0, MOE_BLOCK, ROW_CHUNK):
            pl.when(lo < nv)(functools.partial(fn, lo))

    def gather_start(blk, sl):
        def issue(lo):
            for r in range(lo, lo + ROW_CHUNK):
                tok = dst_ref[blk * MOE_BLOCK + r] & (n_tok - 1)
                pltpu.make_async_copy(hn_hbm.at[pl.ds(tok, 1), :], xbuf.at[sl, pl.ds(r, 1), :],
                                      xsem.at[sl]).start(priority=ROW_DMA_PRIORITY)
        chunks(blk, issue)

    def gather_wait(blk, sl):
        chunks(blk, lambda lo: pltpu.make_async_copy(
            hn_hbm.at[pl.ds(0, ROW_CHUNK), :], xbuf.at[sl, pl.ds(lo, ROW_CHUNK), :], xsem.at[sl]).wait())

    def scatter_start(blk, sl):
        def issue(lo):
            for r in range(lo, lo + ROW_CHUNK):
                dst = dst_ref[blk * MOE_BLOCK + r]
                pltpu.make_async_copy(obuf.at[sl, pl.ds(r, 1), :], y_hbm.at[pl.ds(dst, 1), :],
                                      osem.at[sl]).start(priority=ROW_DMA_PRIORITY)
        chunks(blk, issue)

    def scatter_wait(blk, sl):
        chunks(blk, lambda lo: pltpu.make_async_copy(
            obuf.at[sl, pl.ds(lo, ROW_CHUNK), :], y_hbm.at[pl.ds(0, ROW_CHUNK), :], osem.at[sl]).wait())

    def spare_copy(sl):
        return pltpu.make_async_copy(obuf.at[sl], y_hbm.at[pl.ds(TOP_K * n_tok + sl * MOE_BLOCK, MOE_BLOCK), :],
                                     osem.at[sl])

    def weight_copies(ex, sl):
        return (pltpu.make_async_copy(wg_hbm.at[ex], wgb.at[sl], wsem.at[sl, 0]),
                pltpu.make_async_copy(wu_hbm.at[ex], wub.at[sl], wsem.at[sl, 1]),
                pltpu.make_async_copy(wd_hbm.at[ex], wdb.at[sl], wsem.at[sl, 2]))

    @pl.when((i == 0) & active)
    def _():
        wslot[0] = 1
        for cp in weight_copies(e, 0):
            cp.start(priority=1 - ROW_DMA_PRIORITY)
        xbuf[...] = jnp.zeros(xbuf.shape, jnp.uint32)
        gather_start(0, 0)
        obuf[...] = jnp.zeros(obuf.shape, jnp.uint32)
        for sl in (0, 1):
            spare_copy(sl).start()
        for sl in (0, 1):
            spare_copy(sl).wait()

    @pl.when(active & first)
    def _():
        sl = 1 - wslot[0]
        wslot[0] = sl
        nxt = lax.while_loop(lambda j: (j < nused) & (be_ref[jnp.minimum(j, nused - 1)] == e),
                             lambda j: j + 1, i + 1)

        @pl.when(nxt < nused)
        def _():
            for cp in weight_copies(be_ref[nxt], 1 - sl):
                cp.start(priority=1 - ROW_DMA_PRIORITY)

        for cp in weight_copies(e, sl):
            cp.wait()

    def block(xs):
        sl = wslot[0]

        @pl.when(i >= 2)
        def _():
            scatter_wait(i - 2, xs)

        gather_wait(i, xs)

        @pl.when(i + 1 < nused)
        def _():
            gather_start(i + 1, 1 - xs)

        x = _unpack_halves(xbuf[xs]).astype(BF16)
        g = jnp.dot(x, wgb[sl].astype(BF16), preferred_element_type=F32)
        u = jnp.dot(x, wub[sl].astype(BF16), preferred_element_type=F32)
        hmid = (g * _sigmoid(g) * u).astype(BF16)
        obuf[xs] = _pack_halves(jnp.dot(hmid, wdb[sl].astype(BF16), preferred_element_type=F32))
        scatter_start(i, xs)

        @pl.when(i == nused - 1)
        def _():
            @pl.when(i >= 1)
            def _():
                scatter_wait(i - 1, 1 - xs)

            scatter_wait(i, xs)

    for s_ in (0, 1):
        pl.when(active & (i % 2 == s_))(functools.partial(block, s_))


def _moe(block_e, nvalid, slot_dst, nused, hn, wg, wu, wd):
    nb = block_e.shape[0]
    n_tok, dw = hn.shape
    d = 2 * dw
    assert n_tok & (n_tok - 1) == 0
    grid_spec = pltpu.PrefetchScalarGridSpec(
        num_scalar_prefetch=4,
        grid=(nb,),
        in_specs=[pl.BlockSpec(memory_space=pl.ANY)] * 4,
        out_specs=pl.BlockSpec(memory_space=pl.ANY),
        scratch_shapes=[
            pltpu.VMEM((2, MOE_BLOCK, dw), jnp.uint32),
            pltpu.SemaphoreType.DMA((2,)),
            pltpu.VMEM((2, MOE_BLOCK, dw), jnp.uint32),
            pltpu.SemaphoreType.DMA((2,)),
            pltpu.VMEM((2, d, D_EXPERT), F32),
            pltpu.VMEM((2, d, D_EXPERT), F32),
            pltpu.VMEM((2, D_EXPERT, d), F32),
            pltpu.SemaphoreType.DMA((2, 3)),
            pltpu.SMEM((1,), jnp.int32),
        ],
    )
    return pl.pallas_call(
        functools.partial(_moe_kernel, n_tok=n_tok),
        grid_spec=grid_spec,
        out_shape=jax.ShapeDtypeStruct((TOP_K * n_tok + 2 * MOE_BLOCK, dw), jnp.uint32),
        compiler_params=_cparams(("arbitrary",), VMEM_MB_LARGE),
        name="moe_experts",
    )(block_e, nvalid, slot_dst, nused, hn, wg, wu, wd)


def _ple_kernel(h_ref, rt_ref, y0_ref, y1_ref, p_ref, wg_ref, bg_ref, wp_ref, gp_ref, gf_ref, o_ref):
    rt = rt_ref[...]
    h2 = h_ref[...] + (rt[:, TOP_K:TOP_K + 1] * _unpack_halves(y0_ref[...])
                       + rt[:, TOP_K + 1:TOP_K + 2] * _unpack_halves(y1_ref[...]))
    hn = _rms(h2, gp_ref[...]).astype(BF16)
    gate = _sigmoid(jnp.dot(hn, wg_ref[...], preferred_element_type=F32) + bg_ref[...])
    pe = jnp.dot(p_ref[...].astype(BF16), wp_ref[...], preferred_element_type=F32)
    h3 = h2 + gate * pe
    o_ref[...] = _rms(h3, gf_ref[...])


def _ple_final(h1, rt, y, p, wg, bg, wp, gp, gf, tm):
    s, d = h1.shape
    nblk = s // tm
    row = lambda i: (i, 0)
    full = lambda i: (0, 0)
    return pl.pallas_call(
        _ple_kernel,
        grid=(nblk,),
        in_specs=[
            pl.BlockSpec((tm, d), row),
            pl.BlockSpec((tm, LANES), row),
            pl.BlockSpec((tm, d // 2), row),
            pl.BlockSpec((tm, d // 2), lambda i: (i + nblk, 0)),
            pl.BlockSpec((tm, PLE_DIM), row),
            pl.BlockSpec(wg.shape, full, pipeline_mode=pl.Buffered(1)),
            pl.BlockSpec((1, d), full),
            pl.BlockSpec(wp.shape, full, pipeline_mode=pl.Buffered(1)),
            pl.BlockSpec((1, d), full),
            pl.BlockSpec((1, d), full),
        ],
        out_specs=pl.BlockSpec((tm, d), row),
        out_shape=jax.ShapeDtypeStruct((s, d), F32),
        compiler_params=_cparams(("parallel",), VMEM_MB_LARGE),
        name="ple_final",
    )(h1, rt, y, y, p, wg, bg, wp, gp, gf)


def _slots_kernel(rt_ref, tri_ref, tab_ref, cnt_ref, carry_ref, base_ref, acc_ref, *, n_tok):
    ps = pl.program_id(0)
    b = pl.program_id(1)
    tb = rt_ref.shape[0]
    n_rows = tab_ref.shape[0]
    rt = rt_ref[...]
    lane = lax.broadcasted_iota(jnp.int32, rt.shape, 1)
    lanef = lane.astype(F32)
    oh = [(lanef == rt[:, kk:kk + 1]).astype(F32) for kk in range(TOP_K)]
    both = oh[0] + oh[1]
    colsum = jnp.sum(both, axis=0, keepdims=True)

    @pl.when((ps == 0) & (b == 0))
    def _():
        cnt_ref[...] = jnp.zeros(cnt_ref.shape, F32)

    @pl.when(ps == 0)
    def _():
        cnt_ref[...] = cnt_ref[...] + colsum

    @pl.when((ps == 1) & (b == 0))
    def _():
        blocks = jnp.floor((cnt_ref[...] + (MOE_BLOCK - 0.5)) / MOE_BLOCK)
        r = lax.broadcasted_iota(jnp.int32, (LANES, LANES), 0)
        c = lax.broadcasted_iota(jnp.int32, (LANES, LANES), 1)
        before = (r < c).astype(F32)
        base_ref[...] = jnp.dot(blocks * MOE_BLOCK, before, preferred_element_type=F32,
                                precision=lax.Precision.HIGHEST)
        carry_ref[...] = jnp.zeros(carry_ref.shape, F32)
        acc_ref[...] = jnp.zeros(acc_ref.shape, F32)

    @pl.when(ps == 1)
    def _():
        earlier = jnp.dot(tri_ref[...], both.astype(BF16), preferred_element_type=F32)
        row = earlier + (base_ref[0:1, :] + carry_ref[0:1, :])
        tok = (b * tb + lax.broadcasted_iota(jnp.int32, (tb, 1), 0)).astype(F32)
        rows_f = lax.broadcasted_iota(jnp.int32, (tb, n_rows), 1).astype(F32)
        for kk in range(TOP_K):
            slot = jnp.sum(oh[kk] * row, axis=-1, keepdims=True)
            srow = jnp.floor(slot * (1.0 / LANES))
            scol = slot - LANES * srow
            val = kk * n_tok + tok
            vhi = jnp.floor(val * (1.0 / LANES))
            vlo = val - LANES * vhi
            at_row = (rows_f == srow).astype(BF16)
            at_col = lanef == scol
            x = jnp.concatenate([jnp.where(at_col, vhi, 0.0), jnp.where(at_col, vlo, 0.0),
                                 at_col.astype(F32)], axis=1).astype(BF16)
            acc_ref[...] += lax.dot_general(at_row, x, (((0,), (0,)), ((), ())), preferred_element_type=F32)
        carry_ref[...] = carry_ref[...] + colsum

    @pl.when((ps == 1) & (b == pl.num_programs(1) - 1))
    def _():
        acc = acc_ref[...]
        sidx = (lax.broadcasted_iota(jnp.int32, (n_rows, LANES), 0) * LANES
                + lax.broadcasted_iota(jnp.int32, (n_rows, LANES), 1)).astype(F32)
        blk = jnp.floor((sidx + 0.5) / MOE_BLOCK)
        spare = TOP_K * n_tok + (blk - 2.0 * jnp.floor(blk * 0.5)) * MOE_BLOCK + (sidx - MOE_BLOCK * blk)
        filled = acc[:, :LANES] * LANES + acc[:, LANES:2 * LANES]
        tab_ref[...] = jnp.where(acc[:, 2 * LANES:] > 0.5, filled, spare).astype(jnp.int32)


def _slots(rt, tb, n_slots):
    s = rt.shape[0]
    n_rows = -(-n_slots // (LANES * LANES)) * LANES
    tri = (np.arange(tb)[:, None] > np.arange(tb)[None, :]).astype(np.float32)
    return pl.pallas_call(
        functools.partial(_slots_kernel, n_tok=s),
        grid=(2, s // tb),
        in_specs=[
            pl.BlockSpec((tb, LANES), lambda ps, b: (b, 0)),
            pl.BlockSpec((tb, tb), lambda ps, b: (0, 0)),
        ],
        out_specs=[
            pl.BlockSpec((n_rows, LANES), lambda ps, b: (0, 0)),
            pl.BlockSpec((8, LANES), lambda ps, b: (0, 0)),
        ],
        out_shape=[
            jax.ShapeDtypeStruct((n_rows, LANES), jnp.int32),
            jax.ShapeDtypeStruct((8, LANES), F32),
        ],
        scratch_shapes=[pltpu.VMEM((8, LANES), F32), pltpu.VMEM((8, LANES), F32),
                        pltpu.VMEM((n_rows, 3 * LANES), F32)],
        compiler_params=_cparams(("arbitrary", "arbitrary")),
        name="moe_slots",
    )(rt, jnp.asarray(tri, BF16))


def _dispatch(rt, n_tok):
    a = n_tok * TOP_K
    nb = -(-a // MOE_BLOCK) + N_EXPERTS
    tab, cnt = _slots(rt, SLOTS_TM, nb * MOE_BLOCK)
    slot_dst = tab.reshape(-1)[:nb * MOE_BLOCK]
    counts = cnt[0, :N_EXPERTS].astype(jnp.int32)
    nblk = (counts + MOE_BLOCK - 1) // MOE_BLOCK
    bends = jnp.cumsum(nblk)
    bidx = jnp.arange(nb, dtype=jnp.int32)
    block_e = jnp.minimum(jnp.sum((bends[None, :] <= bidx[:, None]).astype(jnp.int32), axis=1), N_EXPERTS - 1)
    nused = bends[-1].astype(jnp.int32)
    nvalid = jnp.clip(counts[block_e] - (bidx - (bends - nblk)[block_e]) * MOE_BLOCK, 0, MOE_BLOCK)
    nvalid = jnp.where(bidx < nused, nvalid, 0).astype(jnp.int32)
    return block_e, nvalid, slot_dst, nused.reshape(1)


def kernel(x, p, positions, attn_norm, w_in, q_norm, w_uq, kv_norm, w_ukv, mla_norm, hg_lb_logits, hg_norm, w_out, ffn_norm, w_router_group, b_router_group, w_router_expert, b_router_expert, w_exp_gate, w_exp_up, w_exp_down, ple_norm, w_ple_gate, b_ple_gate, w_ple_proj, final_norm):
    bsz, s, d = x.shape
    assert bsz == 1 and w_in.shape[0] == 1
    xt = x[0]

    inv_freq = 1.0 / (ROPE_THETA ** (jnp.arange(0, QK_ROPE, 2, dtype=F32) / QK_ROPE))
    ang = positions[0].astype(F32)[:, None] * inv_freq
    cos, sin = jnp.cos(ang), jnp.sin(ang)
    zpad = jnp.zeros((s, LANES - QK_ROPE), F32)
    cc = jnp.concatenate([cos, cos, zpad], axis=1)
    ss = jnp.concatenate([-sin, sin, zpad], axis=1)

    lb = jnp.cumsum(jax.nn.softmax(hg_lb_logits.astype(F32), axis=0), axis=0)[0][None, :]

    wi = w_in[0]
    kr0 = Q_LORA + KV_LORA
    half = QK_ROPE // 2
    w_lat = jnp.concatenate(
        [wi[:, :kr0 + QK_ROPE], wi[:, kr0 + half:kr0 + QK_ROPE], wi[:, kr0:kr0 + half]], axis=1).astype(BF16)
    w_hg = wi[:, kr0 + QK_ROPE:].astype(BF16)
    wq3 = w_uq[0].reshape(Q_LORA, MLA_HEADS, QK_HEAD)
    wq_pad = jnp.concatenate(
        [wq3, wq3[:, :, QK_NOPE + half:], wq3[:, :, QK_NOPE:QK_NOPE + half]], axis=2
    ).reshape(Q_LORA, MLA_HEADS * QK_PAD).astype(BF16)
    wkv3 = w_ukv[0].reshape(KV_LORA, MLA_HEADS, QK_NOPE + V_HEAD)
    wknt = wkv3[:, :, :QK_NOPE].reshape(KV_LORA, MLA_HEADS * QK_NOPE).T.astype(BF16)
    wv = wkv3[:, :, QK_NOPE:].reshape(KV_LORA, D_MLA).astype(BF16)
    wo = w_out[0].astype(BF16)
    wr = jnp.concatenate(
        [w_router_group[0], w_router_expert[0], jnp.zeros((d, LANES - N_GROUPS - N_EXPERTS), F32)], axis=1)
    wr_hi = wr.astype(BF16)
    wr = jnp.concatenate([wr_hi, (wr - wr_hi.astype(F32)).astype(BF16)], axis=1)
    br = jnp.concatenate(
        [b_router_group[0], b_router_expert[0], jnp.zeros((LANES - N_GROUPS - N_EXPERTS,), F32)])[None, :]

    hg4, lat = _in_proj(xt, attn_norm, w_hg, w_lat, IN_PROJ_TM, IN_PROJ_TN)
    q, kt, v = _mla_up(lat, q_norm, kv_norm, wq_pad, wknt, wv, cc, ss, MLA_UP_TM)
    o_mla = _attention(q, kt, v, mla_norm, ATTN_BLOCK)
    mall, lv = _hgrn_tables()
    o_hg = _hgrn(hg4, lb, hg_norm[0].reshape(1, D_HG), mall, lv)

    h1, hn, rt = _out_route(xt, o_mla, o_hg, wo[:D_MLA], wo[D_MLA:], ffn_norm, wr, br, OUT_ROUTE_TM)
    block_e, nvalid, slot_dst, nused = _dispatch(rt, s)
    y = _moe(block_e, nvalid, slot_dst, nused, hn, w_exp_gate[0], w_exp_up[0], w_exp_down[0])

    out = _ple_final(h1, rt, y, p[0, 0], w_ple_gate[0].astype(BF16), b_ple_gate,
                     w_ple_proj[0].astype(BF16), ple_norm, final_norm[None, :], PLE_TM)
    return out[None]
```

```python
import functools

import jax
import jax.numpy as jnp
import numpy as np
from jax import lax
from jax.experimental import pallas as pl
from jax.experimental.pallas import tpu as pltpu

F32 = jnp.float32
BF16 = jnp.bfloat16

PLE_DIM = 256
MLA_HEADS = 8
QK_NOPE = 128
QK_ROPE = 64
QK_HEAD = QK_NOPE + QK_ROPE
QK_PAD = 256
V_HEAD = 128
V_PAD = 2 * V_HEAD
QK_AHEAD = 2
PV_LAG = 1
LAG_LIMIT = 60.0
FLOOR_LIMIT = 100.0
Q_LORA = 512
KV_LORA = 256
ROPE_THETA = 10000.0
HG_HEADS = 8
HG_DK = 128
HG_DV = 128
HG_CHUNK = 64
D_MLA = MLA_HEADS * V_HEAD
D_HG = HG_HEADS * HG_DV
N_GROUPS = 8
EXPERTS_PER_GROUP = 8
N_EXPERTS = N_GROUPS * EXPERTS_PER_GROUP
EPG_LOG2 = EXPERTS_PER_GROUP.bit_length() - 1
assert 1 << EPG_LOG2 == EXPERTS_PER_GROUP
TOP_K = 2
D_EXPERT = 512
EPS = 1e-6
LANES = 128
D_LAT = Q_LORA + KV_LORA + 2 * QK_ROPE
NEG_BIG = -1e30
LOG2E = 1.4426950408889634

MOE_BLOCK = 320
ROW_CHUNK = 32
ROW_DMA_PRIORITY = 0
HG_ROWS = 256
HG_LEVELS = (32, 16, 8, 4, 2, 1)
HG_MM_LEVELS = (2, 1)
HG_NH = 4

IN_PROJ_TM = 1024
IN_PROJ_TN = 1024
MLA_UP_TM = 512
ATTN_BLOCK = 512
OUT_ROUTE_TM = 512
OUT_ROUTE_PARTS = 2
SLOTS_TM = 1024
PLE_TM = 512
PLE_PARTS = 2
VMEM_MB_HGRN = 32
VMEM_MB_MID = 48
VMEM_MB_LARGE = 56


def _cparams(sem, vmem_mb=None):
    kw = dict(dimension_semantics=sem)
    if vmem_mb is not None:
        kw["vmem_limit_bytes"] = vmem_mb * 1024 * 1024
    return pltpu.CompilerParams(**kw)


def _rms(x, g):
    ms = jnp.mean(x * x, axis=-1, keepdims=True)
    return x * lax.rsqrt(ms + EPS) * g


def _sigmoid(x):
    return 1.0 / (1.0 + jnp.exp(-x))


def _pack_halves(x):
    n = x.shape[1] // 2
    bits = lax.bitcast_convert_type(x.astype(BF16).astype(F32), jnp.uint32)
    return bits[:, :n] | (bits[:, n:] >> 16)


def _unpack_halves(w):
    hi = lax.bitcast_convert_type(w & jnp.uint32(0xFFFF0000), F32)
    lo = lax.bitcast_convert_type(w << 16, F32)
    return jnp.concatenate([hi, lo], axis=1)


def _in_proj_kernel(x_ref, g_ref, whg_ref, wlat_ref, hg_ref, lat_ref, xn_ref):
    j = pl.program_id(1)
    last = pl.num_programs(1) - 1

    @pl.when(j == 0)
    def _():
        xn_ref[...] = _rms(x_ref[...], g_ref[...]).astype(BF16)

    @pl.when(j < last)
    def _():
        hg_ref[...] = jnp.dot(xn_ref[...], whg_ref[...], preferred_element_type=F32).astype(hg_ref.dtype)

    @pl.when(j == last)
    def _():
        lat_ref[...] = jnp.dot(xn_ref[...], wlat_ref[...], preferred_element_type=F32)


def _in_proj(x, gain, w_hg, w_lat, tm, tn):
    s, d = x.shape
    nt = w_hg.shape[1] // tn
    nl = w_lat.shape[1]
    hg_tile = lambda i, j: (i, jnp.minimum(j, nt - 1))
    return pl.pallas_call(
        _in_proj_kernel,
        grid=(s // tm, nt + 1),
        in_specs=[
            pl.BlockSpec((tm, d), lambda i, j: (i, 0)),
            pl.BlockSpec((1, d), lambda i, j: (0, 0)),
            pl.BlockSpec((d, tn), lambda i, j: (0, jnp.minimum(j, nt - 1))),
            pl.BlockSpec((d, nl), lambda i, j: (0, 0), pipeline_mode=pl.Buffered(1)),
        ],
        out_specs=[
            pl.BlockSpec((tm, tn), hg_tile),
            pl.BlockSpec((tm, nl), lambda i, j: (i, 0)),
        ],
        out_shape=[
            jax.ShapeDtypeStruct((s, nt * tn), BF16),
            jax.ShapeDtypeStruct((s, nl), F32),
        ],
        scratch_shapes=[pltpu.VMEM((tm, d), BF16)],
        compiler_params=_cparams(("parallel", "arbitrary"), VMEM_MB_LARGE),
        name="in_proj",
    )(x, gain, w_hg, w_lat)


def _mla_up_kernel(lat_ref, qn_ref, kvn_ref, wq_ref, wknt_ref, wv_ref, cc_ref, ss_ref,
                   q_ref, kt_ref, v_ref):
    lat = lat_ref[...]
    cq = _rms(lat[:, :Q_LORA], qn_ref[...]).astype(BF16)
    ckv = _rms(lat[:, Q_LORA:Q_LORA + KV_LORA], kvn_ref[...]).astype(BF16)
    kp = lat[:, Q_LORA + KV_LORA:D_LAT]
    cc = cc_ref[...]
    ss = ss_ref[...]
    scale = QK_HEAD ** -0.5 * LOG2E

    q = jnp.dot(cq, wq_ref[...], preferred_element_type=F32)
    for h in range(MLA_HEADS):
        base = h * QK_PAD
        q_ref[:, base:base + QK_NOPE] = (q[:, base:base + QK_NOPE] * scale).astype(BF16)
        r = q[:, base + QK_NOPE:base + QK_PAD]
        r = (r * cc + pltpu.roll(r, QK_ROPE, 1) * ss) * scale
        q_ref[:, base + QK_NOPE:base + QK_PAD] = r.astype(BF16)

    kpe_t = (kp * cc + pltpu.roll(kp, QK_ROPE, 1) * ss).T.astype(BF16)
    kn_t = lax.dot_general(wknt_ref[...], ckv, (((1,), (1,)), ((), ())), preferred_element_type=F32).astype(BF16)
    for h in range(MLA_HEADS):
        base = h * QK_PAD
        kt_ref[base:base + QK_NOPE, :] = kn_t[h * QK_NOPE:(h + 1) * QK_NOPE]
        kt_ref[base + QK_NOPE:base + QK_PAD, :] = kpe_t
    v = jnp.dot(ckv, wv_ref[...], preferred_element_type=F32).astype(BF16)
    ones = jnp.ones((v.shape[0], V_PAD - V_HEAD), BF16)
    for h in range(MLA_HEADS):
        v_ref[:, h * V_PAD:h * V_PAD + V_HEAD] = v[:, h * V_HEAD:(h + 1) * V_HEAD]
        v_ref[:, h * V_PAD + V_HEAD:(h + 1) * V_PAD] = ones


def _mla_up(lat, qn, kvn, wq, wknt, wv, cc, ss, tm):
    s = lat.shape[0]
    row = lambda i: (i, 0)
    full = lambda i: (0, 0)
    return pl.pallas_call(
        _mla_up_kernel,
        grid=(s // tm,),
        in_specs=[
            pl.BlockSpec((tm, D_LAT), row),
            pl.BlockSpec((1, Q_LORA), full),
            pl.BlockSpec((1, KV_LORA), full),
            pl.BlockSpec(wq.shape, full),
            pl.BlockSpec(wknt.shape, full),
            pl.BlockSpec(wv.shape, full),
            pl.BlockSpec((tm, LANES), row),
            pl.BlockSpec((tm, LANES), row),
        ],
        out_specs=[
            pl.BlockSpec((tm, MLA_HEADS * QK_PAD), row),
            pl.BlockSpec((MLA_HEADS * QK_PAD, tm), lambda i: (0, i)),
            pl.BlockSpec((tm, MLA_HEADS * V_PAD), row),
        ],
        out_shape=[
            jax.ShapeDtypeStruct((s, MLA_HEADS * QK_PAD), BF16),
            jax.ShapeDtypeStruct((MLA_HEADS * QK_PAD, s), BF16),
            jax.ShapeDtypeStruct((s, MLA_HEADS * V_PAD), BF16),
        ],
        compiler_params=_cparams(("parallel",), VMEM_MB_MID),
        name="mla_up",
    )(lat, qn, kvn, wq, wknt, wv, cc, ss)


def _attn_kernel(it_ref, jt_ref, q_ref, kt_ref, v_ref, g_ref, o_ref, acc_ref, m_ref, redo_ref, *, tb):
    t = pl.program_id(0)
    i = it_ref[t]
    j = jt_ref[t]
    src = t % 2
    dst = 1 - src
    ntile = tb // LANES

    @pl.when(j == 0)
    def _():
        m_ref[...] = jnp.zeros(m_ref.shape, F32)
        acc_ref[src] = jnp.zeros(acc_ref.shape[1:], F32)

    def scores(h):
        return jnp.dot(q_ref[:, h * QK_PAD:(h + 1) * QK_PAD], kt_ref[h * QK_PAD:(h + 1) * QK_PAD, :],
                       preferred_element_type=F32)

    def values(h):
        return v_ref[:, h * V_PAD:(h + 1) * V_PAD]

    def key_minus_query():
        return (lax.broadcasted_iota(jnp.int32, (tb, tb), 1) - lax.broadcasted_iota(jnp.int32, (tb, tb), 0))

    def lagged(masked, first=False):
        if masked:
            keep = key_minus_query() <= (jnp.where(i == 0, 0, tb) if first else 0)
        ahead = [scores(h) for h in range(min(QK_AHEAD, MLA_HEADS))]
        pending = []
        worst = None
        least = None

        def accumulate(h, p):
            acc_ref[dst, h] = acc_ref[src, h] + jnp.dot(p, values(h), preferred_element_type=F32)

        for h in range(MLA_HEADS):
            s = ahead.pop(0)
            if h + QK_AHEAD < MLA_HEADS:
                ahead.append(scores(h + QK_AHEAD))
            d = s - jnp.concatenate([m_ref[h]] * ntile, axis=1)
            if first:
                low = jnp.min((jnp.where(keep, d, 0.0) if masked else d).reshape(tb // 8, 8, tb), axis=0)
                least = low if least is None else jnp.minimum(least, low)
            if masked:
                d = jnp.where(keep, d, NEG_BIG)
            top = jnp.max(d.reshape(tb // 8, 8, tb), axis=0)
            worst = top if worst is None else jnp.maximum(worst, top)
            pending.append((h, jnp.exp2(d).astype(BF16)))
            if len(pending) > PV_LAG:
                accumulate(*pending.pop(0))
        for item in pending:
            accumulate(*item)
        redo = jnp.max(worst) > LAG_LIMIT
        if first:
            redo = redo | (jnp.min(least) < -FLOOR_LIMIT)
        redo_ref[0] = redo.astype(jnp.int32)

    def exact():
        keep = key_minus_query() <= jnp.where(j == i, 0, tb)
        ahead = [scores(h) for h in range(min(QK_AHEAD, MLA_HEADS))]
        for h in range(MLA_HEADS):
            s = jnp.where(keep, ahead.pop(0), NEG_BIG)
            if h + QK_AHEAD < MLA_HEADS:
                ahead.append(scores(h + QK_AHEAD))
            m_prev = jnp.where(j == 0, NEG_BIG, m_ref[h])
            m_new = jnp.maximum(m_prev, jnp.max(s, axis=1, keepdims=True))
            alpha = jnp.exp2(m_prev - m_new)
            p = jnp.exp2(s - jnp.concatenate([m_new] * ntile, axis=1)).astype(BF16)
            m_ref[h] = m_new
            acc_ref[dst, h] = (acc_ref[src, h] * jnp.concatenate([alpha] * (V_PAD // LANES), axis=1)
                               + jnp.dot(p, values(h), preferred_element_type=F32))

    @pl.when(j == 0)
    def _():
        lagged(True, first=True)

    @pl.when((j > 0) & (j < i))
    def _():
        lagged(False)

    @pl.when((j > 0) & (j == i))
    def _():
        lagged(True)

    @pl.when(redo_ref[0] != 0)
    def _():
        exact()

    @pl.when(j == i)
    def _():
        outs = [acc_ref[dst, h, :, :V_HEAD] / acc_ref[dst, h, :, V_HEAD:] for h in range(MLA_HEADS)]
        o_ref[...] = _rms(jnp.concatenate(outs, axis=1), g_ref[...]).astype(o_ref.dtype)


def _attention(q, kt, v, gain, tb):
    s = q.shape[0]
    nq = s // tb
    it = np.concatenate([np.full(i + 1, i, np.int32) for i in range(nq)])
    jt = np.concatenate([np.arange(i + 1, dtype=np.int32) for i in range(nq)])
    grid_spec = pltpu.PrefetchScalarGridSpec(
        num_scalar_prefetch=2,
        grid=(it.shape[0],),
        in_specs=[
            pl.BlockSpec((tb, MLA_HEADS * QK_PAD), lambda t, it, jt: (it[t], 0)),
            pl.BlockSpec((MLA_HEADS * QK_PAD, tb), lambda t, it, jt: (0, jt[t])),
            pl.BlockSpec((tb, MLA_HEADS * V_PAD), lambda t, it, jt: (jt[t], 0)),
            pl.BlockSpec((1, D_MLA), lambda t, it, jt: (0, 0)),
        ],
        out_specs=pl.BlockSpec((tb, D_MLA), lambda t, it, jt: (it[t], 0)),
        scratch_shapes=[
            pltpu.VMEM((2, MLA_HEADS, tb, V_PAD), F32),
            pltpu.VMEM((MLA_HEADS, tb, LANES), F32),
            pltpu.SMEM((1,), jnp.int32),
        ],
    )
    return pl.pallas_call(
        functools.partial(_attn_kernel, tb=tb),
        grid_spec=grid_spec,
        out_shape=jax.ShapeDtypeStruct((s, D_MLA), BF16),
        compiler_params=_cparams(("arbitrary",), VMEM_MB_MID),
        name="mla_attention",
    )(jnp.asarray(it), jnp.asarray(jt), q, kt, v, gain)


def _hgrn_tables():
    n = HG_ROWS
    r = np.arange(n)
    c = np.arange(n)
    same = (r[:, None] // HG_CHUNK) == (c[None, :] // HG_CHUNK)

    def rows_upto(idx):
        return (same & (c[None, :] <= idx[:, None])).astype(np.float32)

    blocks = [rows_upto(r)]
    for m in HG_MM_LEVELS:
        blocks.append(rows_upto((r // (2 * m)) * (2 * m) + m))
    mall = np.concatenate(blocks, axis=0)

    x = r[:, None] ^ c[None, :]
    lv = np.full((n, n), -1, np.int32)
    for li, m in enumerate(HG_LEVELS):
        lv = np.where(same & (r[:, None] > c[None, :]) & (x >= m) & (x < 2 * m), li, lv)
    lv = np.where(r[:, None] == c[None, :], len(HG_LEVELS), lv)
    return jnp.asarray(mall, BF16), jnp.asarray(lv, jnp.int32)


def _hgrn_kernel(q_ref, f_ref, i_ref, g_ref, lb_ref, gn_ref, mall_ref, lv_ref, o_ref, st_ref):
    t = pl.program_id(1)
    n = HG_ROWS

    @pl.when(t == 0)
    def _():
        st_ref[...] = jnp.zeros(st_ref.shape, F32)

    rowid = lax.broadcasted_iota(jnp.int32, (n, 1), 0)
    lv = lv_ref[...]

    def head(hh):
        cols = slice(hh * HG_DK, (hh + 1) * HG_DK)
        q_in = q_ref[:, cols].astype(F32)
        qs = q_in * _sigmoid(q_in)
        lb = lb_ref[:, cols]
        f = lb + (1.0 - lb) * _sigmoid(f_ref[:, cols].astype(F32))
        kk = 1.0 - f
        logf = jnp.log(f)
        iv = i_ref[:, cols]

        l1 = logf.astype(BF16)
        l2 = (logf - l1.astype(F32)).astype(BF16)
        parts = jnp.dot(mall_ref[...], jnp.concatenate([l1, l2], axis=1), preferred_element_type=F32)
        yield
        bc = parts[:, :HG_DK] + parts[:, HG_DK:]
        b = bc[:n]

        def anchor(period, row):
            b3 = b.reshape(n // period, period, HG_DK)
            return jnp.broadcast_to(b3[:, row:row + 1, :], b3.shape).reshape(n, HG_DK)

        a = jnp.zeros((n, n), F32)
        for li, m in enumerate(HG_LEVELS):
            if m in HG_MM_LEVELS:
                k = 1 + HG_MM_LEVELS.index(m)
                c = bc[k * n:(k + 1) * n]
            else:
                c = anchor(2 * m, m)
            e = jnp.exp(-jnp.abs(b - c))
            upper = (rowid & (2 * m - 1)) >= m
            x = (jnp.where(upper, qs, kk) * e).astype(BF16)
            p = lax.dot_general(x, x, (((1,), (1,)), ((), ())), preferred_element_type=F32)
            yield
            a = jnp.where(lv == li, p, a)
        a = jnp.where(lv == len(HG_LEVELS), jnp.sum(qs * kk, axis=-1, keepdims=True), a)
        o = jnp.dot(a.astype(BF16), iv, preferred_element_type=F32)
        yield

        blast = anchor(HG_CHUNK, HG_CHUNK - 1)
        qd = (qs * jnp.exp(b)).astype(BF16)
        kd = kk * jnp.exp(blast - b)
        ivt = iv.astype(F32).T.astype(BF16)
        kd_by_chunk = jnp.concatenate(
            [jnp.where((rowid >= lo) & (rowid < lo + HG_CHUNK), kd, 0.0) for lo in range(0, n, HG_CHUNK)],
            axis=1).astype(BF16)
        upd_all = jnp.dot(ivt, kd_by_chunk, preferred_element_type=F32)
        yield
        st = st_ref[hh]
        outs = []
        for ci in range(n // HG_CHUNK):
            lo = ci * HG_CHUNK
            inter = lax.dot_general(qd[lo:lo + HG_CHUNK], st.astype(BF16), (((1,), (1,)), ((), ())),
                                    preferred_element_type=F32)
            yield
            outs.append(o[lo:lo + HG_CHUNK] + inter)
            st = jnp.exp(blast[lo:lo + 1]) * st + upd_all[:, ci * HG_DK:(ci + 1) * HG_DK]
        st_ref[hh] = st
        o = jnp.concatenate(outs, axis=0)

        g_in = g_ref[:, cols].astype(F32)
        o_ref[:, cols] = (_rms(o, gn_ref[:, cols]) * (g_in * _sigmoid(g_in))).astype(o_ref.dtype)

    progs = [head(hh) for hh in range(HG_NH)]
    while progs:
        progs = [pr for pr in progs if next(pr, "done") != "done"]


def _hgrn(hg4, lb, gn, mall, lv):
    s = hg4.shape[0]
    groups = HG_HEADS // HG_NH
    width = HG_NH * HG_DK

    def col(c):
        return lambda hh, t: (t, c * groups + hh)

    head = lambda hh, t: (0, hh)
    const = lambda hh, t: (0, 0)
    return pl.pallas_call(
        _hgrn_kernel,
        grid=(groups, s // HG_ROWS),
        in_specs=[
            pl.BlockSpec((HG_ROWS, width), col(0)),
            pl.BlockSpec((HG_ROWS, width), col(1)),
            pl.BlockSpec((HG_ROWS, width), col(2)),
            pl.BlockSpec((HG_ROWS, width), col(3)),
            pl.BlockSpec((1, width), head),
            pl.BlockSpec((1, width), head),
            pl.BlockSpec(mall.shape, const),
            pl.BlockSpec(lv.shape, const),
        ],
        out_specs=pl.BlockSpec((HG_ROWS, width), lambda hh, t: (t, hh)),
        out_shape=jax.ShapeDtypeStruct((s, D_HG), BF16),
        scratch_shapes=[pltpu.VMEM((HG_NH, HG_DV, HG_DK), F32)],
        compiler_params=_cparams(("parallel", "arbitrary"), VMEM_MB_HGRN),
        name="hgrn2",
    )(hg4, hg4, hg4, hg4, lb, gn, mall, lv)


def _out_route_kernel(x_ref, om_ref, oh_ref, wa_ref, wb_ref, g_ref, wr_ref, br_ref,
                      h_ref, hn_ref, rt_ref):
    tm = x_ref.shape[0]
    sub = tm // OUT_ROUTE_PARTS
    progs = [_out_route_part(pl.ds(k * sub, sub), x_ref, om_ref, oh_ref, wa_ref, wb_ref, g_ref, wr_ref, br_ref,
                             h_ref, hn_ref, rt_ref) for k in range(OUT_ROUTE_PARTS)]
    while progs:
        progs = [pr for pr in progs if next(pr, "done") != "done"]


def _out_route_part(rows, x_ref, om_ref, oh_ref, wa_ref, wb_ref, g_ref, wr_ref, br_ref, h_ref, hn_ref, rt_ref):
    h1 = (x_ref[rows, :]
          + jnp.dot(om_ref[rows, :], wa_ref[...], preferred_element_type=F32)
          + jnp.dot(oh_ref[rows, :], wb_ref[...], preferred_element_type=F32))
    yield
    h_ref[rows, :] = h1
    hn = _rms(h1, g_ref[...])
    hn_ref[rows, :] = _pack_halves(hn)
    hn_hi = hn.astype(BF16)
    hn_lo = (hn - hn_hi.astype(F32)).astype(BF16)
    hh = jnp.dot(hn_hi, wr_ref[...], preferred_element_type=F32)
    lh = jnp.dot(hn_lo, wr_ref[:, :LANES], preferred_element_type=F32)
    yield
    logits = hh[:, :LANES] + (hh[:, LANES:] + lh) + br_ref[...]

    lane = lax.broadcasted_iota(jnp.int32, logits.shape, 1)
    lanef = lane.astype(F32)
    ninf = -jnp.inf
    big = float(LANES)

    is_g = lane < N_GROUPS
    gl = jnp.where(is_g, logits, ninf)
    gmax = jnp.max(gl, axis=-1, keepdims=True)
    gsum = jnp.sum(jnp.where(is_g, jnp.exp(gl - gmax), 0.0), axis=-1, keepdims=True)
    g_w = 1.0 / gsum
    g_idx = jnp.min(jnp.where(gl == gmax, lanef, big), axis=-1, keepdims=True)

    e_lane = lane - N_GROUPS
    in_grp = (e_lane >= 0) & (e_lane < N_EXPERTS) & ((e_lane >> EPG_LOG2).astype(F32) == g_idx)
    el = jnp.where(in_grp, logits, ninf)
    emax = jnp.max(el, axis=-1, keepdims=True)
    esum = jnp.sum(jnp.where(in_grp, jnp.exp(el - emax), 0.0), axis=-1, keepdims=True)
    i1 = jnp.min(jnp.where(el == emax, lanef, big), axis=-1, keepdims=True)
    el2 = jnp.where(lanef == i1, ninf, el)
    emax2 = jnp.max(el2, axis=-1, keepdims=True)
    i2 = jnp.min(jnp.where(el2 == emax2, lanef, big), axis=-1, keepdims=True)
    p1 = 1.0 / esum
    p2 = jnp.exp(emax2 - emax) / esum
    w1 = g_w * p1 / (p1 + p2)
    w2 = g_w * p2 / (p1 + p2)

    rt = jnp.where(lane == 0, i1 - N_GROUPS,
                   jnp.where(lane == 1, i2 - N_GROUPS,
                             jnp.where(lane == 2, w1, jnp.where(lane == 3, w2, 0.0))))
    rt_ref[rows, :] = rt


def _out_route(x, o_mla, o_hg, wa, wb, gain, wr, br, tm):
    s, d = x.shape
    row = lambda i: (i, 0)
    full = lambda i: (0, 0)
    return pl.pallas_call(
        _out_route_kernel,
        grid=(s // tm,),
        in_specs=[
            pl.BlockSpec((tm, d), row),
            pl.BlockSpec((tm, D_MLA), row),
            pl.BlockSpec((tm, D_HG), row),
            pl.BlockSpec(wa.shape, full, pipeline_mode=pl.Buffered(1)),
            pl.BlockSpec(wb.shape, full, pipeline_mode=pl.Buffered(1)),
            pl.BlockSpec((1, d), full),
            pl.BlockSpec(wr.shape, full, pipeline_mode=pl.Buffered(1)),
            pl.BlockSpec((1, LANES), full),
        ],
        out_specs=[
            pl.BlockSpec((tm, d), row),
            pl.BlockSpec((tm, d // 2), row),
            pl.BlockSpec((tm, LANES), row),
        ],
        out_shape=[
            jax.ShapeDtypeStruct((s, d), F32),
            jax.ShapeDtypeStruct((s, d // 2), jnp.uint32),
            jax.ShapeDtypeStruct((s, LANES), F32),
        ],
        compiler_params=_cparams(("parallel",), VMEM_MB_LARGE),
        name="out_proj_route",
    )(x, o_mla, o_hg, wa, wb, gain, wr, br)


def _moe_kernel(be_ref, nv_ref, dst_ref, nu_ref, hn_hbm, wg_hbm, wu_hbm, wd_hbm, y_hbm,
                xbuf, xsem, obuf, osem, wgb, wub, wdb, wsem, wslot, *, n_tok):
    i = pl.program_id(0)
    nused = nu_ref[0]
    e = be_ref[i]
    active = i < nused
    first = (i == 0) | (e != be_ref[jnp.maximum(i - 1, 0)])

    def chunks(blk, fn):
        nv = nv_ref[blk]
        for lo in range(0, MOE_BLOCK, ROW_CHUNK):
            pl.when(lo < nv)(functools.partial(fn, lo))

    def gather_start(blk, sl):
        def issue(lo):
            for r in range(lo, lo + ROW_CHUNK):
                tok = dst_ref[blk * MOE_BLOCK + r] & (n_tok - 1)
                pltpu.make_async_copy(hn_hbm.at[pl.ds(tok, 1), :], xbuf.at[sl, pl.ds(r, 1), :],
                                      xsem.at[sl]).start(priority=ROW_DMA_PRIORITY)
        chunks(blk, issue)

    def gather_wait(blk, sl):
        chunks(blk, lambda lo: pltpu.make_async_copy(
            hn_hbm.at[pl.ds(0, ROW_CHUNK), :], xbuf.at[sl, pl.ds(lo, ROW_CHUNK), :], xsem.at[sl]).wait())

    def scatter_start(blk, sl):
        def issue(lo):
            for r in range(lo, lo + ROW_CHUNK):
                dst = dst_ref[blk * MOE_BLOCK + r]
                pltpu.make_async_copy(obuf.at[sl, pl.ds(r, 1), :], y_hbm.at[pl.ds(dst, 1), :],
                                      osem.at[sl]).start(priority=ROW_DMA_PRIORITY)
        chunks(blk, issue)

    def scatter_wait(blk, sl):
        chunks(blk, lambda lo: pltpu.make_async_copy(
            obuf.at[sl, pl.ds(lo, ROW_CHUNK), :], y_hbm.at[pl.ds(0, ROW_CHUNK), :], osem.at[sl]).wait())

    def spare_copy(sl):
        return pltpu.make_async_copy(obuf.at[sl], y_hbm.at[pl.ds(TOP_K * n_tok + sl * MOE_BLOCK, MOE_BLOCK), :],
                                     osem.at[sl])

    def weight_copies(ex, sl):
        return (pltpu.make_async_copy(wg_hbm.at[ex], wgb.at[sl], wsem.at[sl, 0]),
                pltpu.make_async_copy(wu_hbm.at[ex], wub.at[sl], wsem.at[sl, 1]),
                pltpu.make_async_copy(wd_hbm.at[ex], wdb.at[sl], wsem.at[sl, 2]))

    @pl.when((i == 0) & active)
    def _():
        wslot[0] = 1
        for cp in weight_copies(e, 0):
            cp.start(priority=1 - ROW_DMA_PRIORITY)
        xbuf[...] = jnp.zeros(xbuf.shape, jnp.uint32)
        gather_start(0, 0)
        obuf[...] = jnp.zeros(obuf.shape, jnp.uint32)
        for sl in (0, 1):
            spare_copy(sl).start()
        for sl in (0, 1):
            spare_copy(sl).wait()

    @pl.when(active & first)
    def _():
        sl = 1 - wslot[0]
        wslot[0] = sl
        nxt = lax.while_loop(lambda j: (j < nused) & (be_ref[jnp.minimum(j, nused - 1)] == e),
                             lambda j: j + 1, i + 1)

        @pl.when(nxt < nused)
        def _():
            for cp in weight_copies(be_ref[nxt], 1 - sl):
                cp.start(priority=1 - ROW_DMA_PRIORITY)

        for cp in weight_copies(e, sl):
            cp.wait()

    def block(xs):
        sl = wslot[0]

        @pl.when(i >= 2)
        def _():
            scatter_wait(i - 2, xs)

        gather_wait(i, xs)

        @pl.when(i + 1 < nused)
        def _():
            gather_start(i + 1, 1 - xs)

        x = _unpack_halves(xbuf[xs]).astype(BF16)
        g = jnp.dot(x, wgb[sl].astype(BF16), preferred_element_type=F32)
        u = jnp.dot(x, wub[sl].astype(BF16), preferred_element_type=F32)
        hmid = (g * _sigmoid(g) * u).astype(BF16)
        obuf[xs] = _pack_halves(jnp.dot(hmid, wdb[sl].astype(BF16), preferred_element_type=F32))
        scatter_start(i, xs)

        @pl.when(i == nused - 1)
        def _():
            @pl.when(i >= 1)
            def _():
                scatter_wait(i - 1, 1 - xs)

            scatter_wait(i, xs)

    for s_ in (0, 1):
        pl.when(active & (i % 2 == s_))(functools.partial(block, s_))


def _moe(block_e, nvalid, slot_dst, nused, hn, wg, wu, wd):
    nb = block_e.shape[0]
    n_tok, dw = hn.shape
    d = 2 * dw
    assert n_tok & (n_tok - 1) == 0
    grid_spec = pltpu.PrefetchScalarGridSpec(
        num_scalar_prefetch=4,
        grid=(nb,),
        in_specs=[pl.BlockSpec(memory_space=pl.ANY)] * 4,
        out_specs=pl.BlockSpec(memory_space=pl.ANY),
        scratch_shapes=[
            pltpu.VMEM((2, MOE_BLOCK, dw), jnp.uint32),
            pltpu.SemaphoreType.DMA((2,)),
            pltpu.VMEM((2, MOE_BLOCK, dw), jnp.uint32),
            pltpu.SemaphoreType.DMA((2,)),
            pltpu.VMEM((2, d, D_EXPERT), F32),
            pltpu.VMEM((2, d, D_EXPERT), F32),
            pltpu.VMEM((2, D_EXPERT, d), F32),
            pltpu.SemaphoreType.DMA((2, 3)),
            pltpu.SMEM((1,), jnp.int32),
        ],
    )
    return pl.pallas_call(
        functools.partial(_moe_kernel, n_tok=n_tok),
        grid_spec=grid_spec,
        out_shape=jax.ShapeDtypeStruct((TOP_K * n_tok + 2 * MOE_BLOCK, dw), jnp.uint32),
        compiler_params=_cparams(("arbitrary",), VMEM_MB_LARGE),
        name="moe_experts",
    )(block_e, nvalid, slot_dst, nused, hn, wg, wu, wd)


def _ple_kernel(h_ref, rt_ref, y0_ref, y1_ref, p_ref, wg_ref, bg_ref, wp_ref, gp_ref, gf_ref, o_ref):
    sub = h_ref.shape[0] // PLE_PARTS

    def part(rows):
        pe = jnp.dot(p_ref[rows, :].astype(BF16), wp_ref[...], preferred_element_type=F32)
        rt = rt_ref[rows, :]
        h2 = h_ref[rows, :] + (rt[:, TOP_K:TOP_K + 1] * _unpack_halves(y0_ref[rows, :])
                               + rt[:, TOP_K + 1:TOP_K + 2] * _unpack_halves(y1_ref[rows, :]))
        hn = _rms(h2, gp_ref[...]).astype(BF16)
        z = jnp.dot(hn, wg_ref[...], preferred_element_type=F32)
        yield
        h3 = h2 + _sigmoid(z + bg_ref[...]) * pe
        o_ref[rows, :] = _rms(h3, gf_ref[...])

    progs = [part(pl.ds(k * sub, sub)) for k in range(PLE_PARTS)]
    while progs:
        progs = [pr for pr in progs if next(pr, "done") != "done"]


def _ple_final(h1, rt, y, p, wg, bg, wp, gp, gf, tm):
    s, d = h1.shape
    nblk = s // tm
    row = lambda i: (i, 0)
    full = lambda i: (0, 0)
    return pl.pallas_call(
        _ple_kernel,
        grid=(nblk,),
        in_specs=[
            pl.BlockSpec((tm, d), row),
            pl.BlockSpec((tm, LANES), row),
            pl.BlockSpec((tm, d // 2), row),
            pl.BlockSpec((tm, d // 2), lambda i: (i + nblk, 0)),
            pl.BlockSpec((tm, PLE_DIM), row),
            pl.BlockSpec(wg.shape, full, pipeline_mode=pl.Buffered(1)),
            pl.BlockSpec((1, d), full),
            pl.BlockSpec(wp.shape, full, pipeline_mode=pl.Buffered(1)),
            pl.BlockSpec((1, d), full),
            pl.BlockSpec((1, d), full),
        ],
        out_specs=pl.BlockSpec((tm, d), row),
        out_shape=jax.ShapeDtypeStruct((s, d), F32),
        compiler_params=_cparams(("parallel",), VMEM_MB_LARGE),
        name="ple_final",
    )(h1, rt, y, y, p, wg, bg, wp, gp, gf)


def _slots_kernel(rt_ref, tri_ref, tab_ref, cnt_ref, carry_ref, base_ref, acc_ref, *, n_tok):
    ps = pl.program_id(0)
    b = pl.program_id(1)
    tb = rt_ref.shape[0]
    n_rows = tab_ref.shape[0]
    rt = rt_ref[...]
    lane = lax.broadcasted_iota(jnp.int32, rt.shape, 1)
    lanef = lane.astype(F32)
    oh = [(lanef == rt[:, kk:kk + 1]).astype(F32) for kk in range(TOP_K)]
    both = oh[0] + oh[1]
    colsum = jnp.sum(both, axis=0, keepdims=True)

    @pl.when((ps == 0) & (b == 0))
    def _():
        cnt_ref[...] = jnp.zeros(cnt_ref.shape, F32)

    @pl.when(ps == 0)
    def _():
        cnt_ref[...] = cnt_ref[...] + colsum

    @pl.when((ps == 1) & (b == 0))
    def _():
        blocks = jnp.floor((cnt_ref[...] + (MOE_BLOCK - 0.5)) / MOE_BLOCK)
        r = lax.broadcasted_iota(jnp.int32, (LANES, LANES), 0)
        c = lax.broadcasted_iota(jnp.int32, (LANES, LANES), 1)
        before = (r < c).astype(F32)
        base_ref[...] = jnp.dot(blocks * MOE_BLOCK, before, preferred_element_type=F32,
                                precision=lax.Precision.HIGHEST)
        carry_ref[...] = jnp.zeros(carry_ref.shape, F32)
        acc_ref[...] = jnp.zeros(acc_ref.shape, F32)

    @pl.when(ps == 1)
    def _():
        earlier = jnp.dot(tri_ref[...], both.astype(BF16), preferred_element_type=F32)
        row = earlier + (base_ref[0:1, :] + carry_ref[0:1, :])
        tok = (b * tb + lax.broadcasted_iota(jnp.int32, (tb, 1), 0)).astype(F32)
        rows_f = lax.broadcasted_iota(jnp.int32, (tb, n_rows), 1).astype(F32)
        for kk in range(TOP_K):
            slot = jnp.sum(oh[kk] * row, axis=-1, keepdims=True)
            srow = jnp.floor(slot * (1.0 / LANES))
            scol = slot - LANES * srow
            val = kk * n_tok + tok
            vhi = jnp.floor(val * (1.0 / LANES))
            vlo = val - LANES * vhi
            at_row = (rows_f == srow).astype(BF16)
            at_col = lanef == scol
            x = jnp.concatenate([jnp.where(at_col, vhi, 0.0), jnp.where(at_col, vlo, 0.0),
                                 at_col.astype(F32)], axis=1).astype(BF16)
            acc_ref[...] += lax.dot_general(at_row, x, (((0,), (0,)), ((), ())), preferred_element_type=F32)
        carry_ref[...] = carry_ref[...] + colsum

    @pl.when((ps == 1) & (b == pl.num_programs(1) - 1))
    def _():
        acc = acc_ref[...]
        sidx = (lax.broadcasted_iota(jnp.int32, (n_rows, LANES), 0) * LANES
                + lax.broadcasted_iota(jnp.int32, (n_rows, LANES), 1)).astype(F32)
        blk = jnp.floor((sidx + 0.5) / MOE_BLOCK)
        spare = TOP_K * n_tok + (blk - 2.0 * jnp.floor(blk * 0.5)) * MOE_BLOCK + (sidx - MOE_BLOCK * blk)
        filled = acc[:, :LANES] * LANES + acc[:, LANES:2 * LANES]
        tab_ref[...] = jnp.where(acc[:, 2 * LANES:] > 0.5, filled, spare).astype(jnp.int32)


def _slots(rt, tb, n_slots):
    s = rt.shape[0]
    n_rows = -(-n_slots // (LANES * LANES)) * LANES
    tri = (np.arange(tb)[:, None] > np.arange(tb)[None, :]).astype(np.float32)
    return pl.pallas_call(
        functools.partial(_slots_kernel, n_tok=s),
        grid=(2, s // tb),
        in_specs=[
            pl.BlockSpec((tb, LANES), lambda ps, b: (b, 0)),
            pl.BlockSpec((tb, tb), lambda ps, b: (0, 0)),
        ],
        out_specs=[
            pl.BlockSpec((n_rows, LANES), lambda ps, b: (0, 0)),
            pl.BlockSpec((8, LANES), lambda ps, b: (0, 0)),
        ],
        out_shape=[
            jax.ShapeDtypeStruct((n_rows, LANES), jnp.int32),
            jax.ShapeDtypeStruct((8, LANES), F32),
        ],
        scratch_shapes=[pltpu.VMEM((8, LANES), F32), pltpu.VMEM((8, LANES), F32),
                        pltpu.VMEM((n_rows, 3 * LANES), F32)],
        compiler_params=_cparams(("arbitrary", "arbitrary")),
        name="moe_slots",
    )(rt, jnp.asarray(tri, BF16))


def _dispatch(rt, n_tok):
    a = n_tok * TOP_K
    nb = -(-a // MOE_BLOCK) + N_EXPERTS
    tab, cnt = _slots(rt, SLOTS_TM, nb * MOE_BLOCK)
    slot_dst = tab.reshape(-1)[:nb * MOE_BLOCK]
    counts = cnt[0, :N_EXPERTS].astype(jnp.int32)
    nblk = (counts + MOE_BLOCK - 1) // MOE_BLOCK
    bends = jnp.cumsum(nblk)
    bidx = jnp.arange(nb, dtype=jnp.int32)
    block_e = jnp.minimum(jnp.sum((bends[None, :] <= bidx[:, None]).astype(jnp.int32), axis=1), N_EXPERTS - 1)
    nused = bends[-1].astype(jnp.int32)
    nvalid = jnp.clip(counts[block_e] - (bidx - (bends - nblk)[block_e]) * MOE_BLOCK, 0, MOE_BLOCK)
    nvalid = jnp.where(bidx < nused, nvalid, 0).astype(jnp.int32)
    return block_e, nvalid, slot_dst, nused.reshape(1)


def kernel(x, p, positions, attn_norm, w_in, q_norm, w_uq, kv_norm, w_ukv, mla_norm, hg_lb_logits, hg_norm, w_out, ffn_norm, w_router_group, b_router_group, w_router_expert, b_router_expert, w_exp_gate, w_exp_up, w_exp_down, ple_norm, w_ple_gate, b_ple_gate, w_ple_proj, final_norm):
    bsz, s, d = x.shape
    assert bsz == 1 and w_in.shape[0] == 1
    xt = x[0]

    inv_freq = 1.0 / (ROPE_THETA ** (jnp.arange(0, QK_ROPE, 2, dtype=F32) / QK_ROPE))
    ang = positions[0].astype(F32)[:, None] * inv_freq
    cos, sin = jnp.cos(ang), jnp.sin(ang)
    zpad = jnp.zeros((s, LANES - QK_ROPE), F32)
    cc = jnp.concatenate([cos, cos, zpad], axis=1)
    ss = jnp.concatenate([-sin, sin, zpad], axis=1)

    lb = jnp.cumsum(jax.nn.softmax(hg_lb_logits.astype(F32), axis=0), axis=0)[0][None, :]

    wi = w_in[0]
    kr0 = Q_LORA + KV_LORA
    half = QK_ROPE // 2
    w_lat = jnp.concatenate(
        [wi[:, :kr0 + QK_ROPE], wi[:, kr0 + half:kr0 + QK_ROPE], wi[:, kr0:kr0 + half]], axis=1).astype(BF16)
    w_hg = wi[:, kr0 + QK_ROPE:].astype(BF16)
    wq3 = w_uq[0].reshape(Q_LORA, MLA_HEADS, QK_HEAD)
    wq_pad = jnp.concatenate(
        [wq3, wq3[:, :, QK_NOPE + half:], wq3[:, :, QK_NOPE:QK_NOPE + half]], axis=2
    ).reshape(Q_LORA, MLA_HEADS * QK_PAD).astype(BF16)
    wkv3 = w_ukv[0].reshape(KV_LORA, MLA_HEADS, QK_NOPE + V_HEAD)
    wknt = wkv3[:, :, :QK_NOPE].reshape(KV_LORA, MLA_HEADS * QK_NOPE).T.astype(BF16)
    wv = wkv3[:, :, QK_NOPE:].reshape(KV_LORA, D_MLA).astype(BF16)
    wo = w_out[0].astype(BF16)
    wr = jnp.concatenate(
        [w_router_group[0], w_router_expert[0], jnp.zeros((d, LANES - N_GROUPS - N_EXPERTS), F32)], axis=1)
    wr_hi = wr.astype(BF16)
    wr = jnp.concatenate([wr_hi, (wr - wr_hi.astype(F32)).astype(BF16)], axis=1)
    br = jnp.concatenate(
        [b_router_group[0], b_router_expert[0], jnp.zeros((LANES - N_GROUPS - N_EXPERTS,), F32)])[None, :]

    hg4, lat = _in_proj(xt, attn_norm, w_hg, w_lat, IN_PROJ_TM, IN_PROJ_TN)
    q, kt, v = _mla_up(lat, q_norm, kv_norm, wq_pad, wknt, wv, cc, ss, MLA_UP_TM)
    o_mla = _attention(q, kt, v, mla_norm, ATTN_BLOCK)
    mall, lv = _hgrn_tables()
    o_hg = _hgrn(hg4, lb, hg_norm[0].reshape(1, D_HG), mall, lv)

    h1, hn, rt = _out_route(xt, o_mla, o_hg, wo[:D_MLA], wo[D_MLA:], ffn_norm, wr, br, OUT_ROUTE_TM)
    block_e, nvalid, slot_dst, nused = _dispatch(rt, s)
    y = _moe(block_e, nvalid, slot_dst, nused, hn, w_exp_gate[0], w_exp_up[0], w_exp_down[0])

    out = _ple_final(h1, rt, y, p[0, 0], w_ple_gate[0].astype(BF16), b_ple_gate,
                     w_ple_proj[0].astype(BF16), ple_norm, final_norm[None, :], PLE_TM)
    return out[None]
```

```python
import functools

import jax
import jax.numpy as jnp
import numpy as np
from jax import lax
from jax.experimental import pallas as pl
from jax.experimental.pallas import tpu as pltpu

F32 = jnp.float32
BF16 = jnp.bfloat16

PLE_DIM = 256
MLA_HEADS = 8
QK_NOPE = 128
QK_ROPE = 64
QK_HEAD = QK_NOPE + QK_ROPE
QK_PAD = 256
V_HEAD = 128
V_PAD = 2 * V_HEAD
QK_AHEAD = 2
PV_LAG = 1
LAG_LIMIT = 60.0
FLOOR_LIMIT = 100.0
Q_LORA = 512
KV_LORA = 256
ROPE_THETA = 10000.0
HG_HEADS = 8
HG_DK = 128
HG_DV = 128
HG_CHUNK = 64
D_MLA = MLA_HEADS * V_HEAD
D_HG = HG_HEADS * HG_DV
N_GROUPS = 8
EXPERTS_PER_GROUP = 8
N_EXPERTS = N_GROUPS * EXPERTS_PER_GROUP
EPG_LOG2 = EXPERTS_PER_GROUP.bit_length() - 1
assert 1 << EPG_LOG2 == EXPERTS_PER_GROUP
TOP_K = 2
D_EXPERT = 512
EPS = 1e-6
LANES = 128
D_LAT = Q_LORA + KV_LORA + 2 * QK_ROPE
NEG_BIG = -1e30
LOG2E = 1.4426950408889634

MOE_BLOCK = 320
ROW_CHUNK = 32
ROW_DMA_PRIORITY = 0
HG_ROWS = 256
HG_LEVELS = (32, 16, 8, 4, 2, 1)
HG_MM_LEVELS = (2, 1)
HG_NH = 4

IN_PROJ_TM = 1024
IN_PROJ_TN = 1024
MLA_UP_TM = 512
ATTN_BLOCK = 512
OUT_ROUTE_TM = 512
OUT_ROUTE_PARTS = 2
SLOTS_TM = 1024
PLE_TM = 512
PLE_PARTS = 2
VMEM_MB_HGRN = 32
VMEM_MB_MID = 48
VMEM_MB_LARGE = 56


def _cparams(sem, vmem_mb=None):
    kw = dict(dimension_semantics=sem)
    if vmem_mb is not None:
        kw["vmem_limit_bytes"] = vmem_mb * 1024 * 1024
    return pltpu.CompilerParams(**kw)


def _rms(x, g):
    ms = jnp.mean(x * x, axis=-1, keepdims=True)
    return x * lax.rsqrt(ms + EPS) * g


def _sigmoid(x):
    return 1.0 / (1.0 + jnp.exp(-x))


def _pack_halves(x):
    n = x.shape[1] // 2
    bits = lax.bitcast_convert_type(x.astype(BF16).astype(F32), jnp.uint32)
    return bits[:, :n] | (bits[:, n:] >> 16)


def _unpack_halves(w):
    hi = lax.bitcast_convert_type(w & jnp.uint32(0xFFFF0000), F32)
    lo = lax.bitcast_convert_type(w << 16, F32)
    return jnp.concatenate([hi, lo], axis=1)


def _in_proj_kernel(x_ref, g_ref, whg_ref, wlat_ref, hg_ref, lat_ref, xn_ref):
    j = pl.program_id(1)
    last = pl.num_programs(1) - 1

    @pl.when(j == 0)
    def _():
        xn_ref[...] = _rms(x_ref[...], g_ref[...]).astype(BF16)

    @pl.when(j < last)
    def _():
        hg_ref[...] = jnp.dot(xn_ref[...], whg_ref[...], preferred_element_type=F32).astype(hg_ref.dtype)

    @pl.when(j == last)
    def _():
        lat_ref[...] = jnp.dot(xn_ref[...], wlat_ref[...], preferred_element_type=F32)


def _in_proj(x, gain, w_hg, w_lat, tm, tn):
    s, d = x.shape
    nt = w_hg.shape[1] // tn
    nl = w_lat.shape[1]
    hg_tile = lambda i, j: (i, jnp.minimum(j, nt - 1))
    return pl.pallas_call(
        _in_proj_kernel,
        grid=(s // tm, nt + 1),
        in_specs=[
            pl.BlockSpec((tm, d), lambda i, j: (i, 0)),
            pl.BlockSpec((1, d), lambda i, j: (0, 0)),
            pl.BlockSpec((d, tn), lambda i, j: (0, jnp.minimum(j, nt - 1))),
            pl.BlockSpec((d, nl), lambda i, j: (0, 0), pipeline_mode=pl.Buffered(1)),
        ],
        out_specs=[
            pl.BlockSpec((tm, tn), hg_tile),
            pl.BlockSpec((tm, nl), lambda i, j: (i, 0)),
        ],
        out_shape=[
            jax.ShapeDtypeStruct((s, nt * tn), BF16),
            jax.ShapeDtypeStruct((s, nl), F32),
        ],
        scratch_shapes=[pltpu.VMEM((tm, d), BF16)],
        compiler_params=_cparams(("parallel", "arbitrary"), VMEM_MB_LARGE),
        name="in_proj",
    )(x, gain, w_hg, w_lat)


def _mla_up_kernel(lat_ref, qn_ref, kvn_ref, wq_ref, wknt_ref, wv_ref, cc_ref, ss_ref,
                   q_ref, kt_ref, v_ref):
    lat = lat_ref[...]
    cq = _rms(lat[:, :Q_LORA], qn_ref[...]).astype(BF16)
    ckv = _rms(lat[:, Q_LORA:Q_LORA + KV_LORA], kvn_ref[...]).astype(BF16)
    kp = lat[:, Q_LORA + KV_LORA:D_LAT]
    cc = cc_ref[...]
    ss = ss_ref[...]
    scale = QK_HEAD ** -0.5 * LOG2E

    q = jnp.dot(cq, wq_ref[...], preferred_element_type=F32)
    for h in range(MLA_HEADS):
        base = h * QK_PAD
        q_ref[:, base:base + QK_NOPE] = (q[:, base:base + QK_NOPE] * scale).astype(BF16)
        r = q[:, base + QK_NOPE:base + QK_PAD]
        r = (r * cc + pltpu.roll(r, QK_ROPE, 1) * ss) * scale
        q_ref[:, base + QK_NOPE:base + QK_PAD] = r.astype(BF16)

    kpe_t = (kp * cc + pltpu.roll(kp, QK_ROPE, 1) * ss).T.astype(BF16)
    kn_t = lax.dot_general(wknt_ref[...], ckv, (((1,), (1,)), ((), ())), preferred_element_type=F32).astype(BF16)
    for h in range(MLA_HEADS):
        base = h * QK_PAD
        kt_ref[base:base + QK_NOPE, :] = kn_t[h * QK_NOPE:(h + 1) * QK_NOPE]
        kt_ref[base + QK_NOPE:base + QK_PAD, :] = kpe_t
    v = jnp.dot(ckv, wv_ref[...], preferred_element_type=F32).astype(BF16)
    ones = jnp.ones((v.shape[0], V_PAD - V_HEAD), BF16)
    for h in range(MLA_HEADS):
        v_ref[:, h * V_PAD:h * V_PAD + V_HEAD] = v[:, h * V_HEAD:(h + 1) * V_HEAD]
        v_ref[:, h * V_PAD + V_HEAD:(h + 1) * V_PAD] = ones


def _mla_up(lat, qn, kvn, wq, wknt, wv, cc, ss, tm):
    s = lat.shape[0]
    row = lambda i: (i, 0)
    full = lambda i: (0, 0)
    return pl.pallas_call(
        _mla_up_kernel,
        grid=(s // tm,),
        in_specs=[
            pl.BlockSpec((tm, D_LAT), row),
            pl.BlockSpec((1, Q_LORA), full),
            pl.BlockSpec((1, KV_LORA), full),
            pl.BlockSpec(wq.shape, full),
            pl.BlockSpec(wknt.shape, full),
            pl.BlockSpec(wv.shape, full),
            pl.BlockSpec((tm, LANES), row),
            pl.BlockSpec((tm, LANES), row),
        ],
        out_specs=[
            pl.BlockSpec((tm, MLA_HEADS * QK_PAD), row),
            pl.BlockSpec((MLA_HEADS * QK_PAD, tm), lambda i: (0, i)),
            pl.BlockSpec((tm, MLA_HEADS * V_PAD), row),
        ],
        out_shape=[
            jax.ShapeDtypeStruct((s, MLA_HEADS * QK_PAD), BF16),
            jax.ShapeDtypeStruct((MLA_HEADS * QK_PAD, s), BF16),
            jax.ShapeDtypeStruct((s, MLA_HEADS * V_PAD), BF16),
        ],
        compiler_params=_cparams(("parallel",), VMEM_MB_MID),
        name="mla_up",
    )(lat, qn, kvn, wq, wknt, wv, cc, ss)


def _attn_kernel(it_ref, jt_ref, q_ref, kt_ref, v_ref, g_ref, o_ref, acc_ref, m_ref, redo_ref, *, tb):
    t = pl.program_id(0)
    i = it_ref[t]
    j = jt_ref[t]
    src = t % 2
    dst = 1 - src
    ntile = tb // LANES

    @pl.when(j == 0)
    def _():
        m_ref[...] = jnp.zeros(m_ref.shape, F32)
        acc_ref[src] = jnp.zeros(acc_ref.shape[1:], F32)

    def scores(h):
        return jnp.dot(q_ref[:, h * QK_PAD:(h + 1) * QK_PAD], kt_ref[h * QK_PAD:(h + 1) * QK_PAD, :],
                       preferred_element_type=F32)

    def values(h):
        return v_ref[:, h * V_PAD:(h + 1) * V_PAD]

    def key_minus_query():
        return (lax.broadcasted_iota(jnp.int32, (tb, tb), 1) - lax.broadcasted_iota(jnp.int32, (tb, tb), 0))

    def lagged(masked, first=False):
        if masked:
            keep = key_minus_query() <= (jnp.where(i == 0, 0, tb) if first else 0)
        ahead = [scores(h) for h in range(min(QK_AHEAD, MLA_HEADS))]
        pending = []
        worst = None
        least = None

        def accumulate(h, p):
            acc_ref[dst, h] = acc_ref[src, h] + jnp.dot(p, values(h), preferred_element_type=F32)

        for h in range(MLA_HEADS):
            s = ahead.pop(0)
            if h + QK_AHEAD < MLA_HEADS:
                ahead.append(scores(h + QK_AHEAD))
            d = s - jnp.concatenate([m_ref[h]] * ntile, axis=1)
            if first:
                low = jnp.min((jnp.where(keep, d, 0.0) if masked else d).reshape(tb // 8, 8, tb), axis=0)
                least = low if least is None else jnp.minimum(least, low)
            if masked:
                d = jnp.where(keep, d, NEG_BIG)
            top = jnp.max(d.reshape(tb // 8, 8, tb), axis=0)
            worst = top if worst is None else jnp.maximum(worst, top)
            pending.append((h, jnp.exp2(d).astype(BF16)))
            if len(pending) > PV_LAG:
                accumulate(*pending.pop(0))
        for item in pending:
            accumulate(*item)
        redo = jnp.max(worst) > LAG_LIMIT
        if first:
            redo = redo | (jnp.min(least) < -FLOOR_LIMIT)
        redo_ref[0] = redo.astype(jnp.int32)

    def exact():
        keep = key_minus_query() <= jnp.where(j == i, 0, tb)
        ahead = [scores(h) for h in range(min(QK_AHEAD, MLA_HEADS))]
        for h in range(MLA_HEADS):
            s = jnp.where(keep, ahead.pop(0), NEG_BIG)
            if h + QK_AHEAD < MLA_HEADS:
                ahead.append(scores(h + QK_AHEAD))
            m_prev = jnp.where(j == 0, NEG_BIG, m_ref[h])
            m_new = jnp.maximum(m_prev, jnp.max(s, axis=1, keepdims=True))
            alpha = jnp.exp2(m_prev - m_new)
            p = jnp.exp2(s - jnp.concatenate([m_new] * ntile, axis=1)).astype(BF16)
            m_ref[h] = m_new
            acc_ref[dst, h] = (acc_ref[src, h] * jnp.concatenate([alpha] * (V_PAD // LANES), axis=1)
                               + jnp.dot(p, values(h), preferred_element_type=F32))

    @pl.when(j == 0)
    def _():
        lagged(True, first=True)

    @pl.when((j > 0) & (j < i))
    def _():
        lagged(False)

    @pl.when((j > 0) & (j == i))
    def _():
        lagged(True)

    @pl.when(redo_ref[0] != 0)
    def _():
        exact()

    @pl.when(j == i)
    def _():
        outs = [acc_ref[dst, h, :, :V_HEAD] / acc_ref[dst, h, :, V_HEAD:] for h in range(MLA_HEADS)]
        o_ref[...] = _rms(jnp.concatenate(outs, axis=1), g_ref[...]).astype(o_ref.dtype)


def _attention(q, kt, v, gain, tb):
    s = q.shape[0]
    nq = s // tb
    it = np.concatenate([np.full(i + 1, i, np.int32) for i in range(nq)])
    jt = np.concatenate([np.arange(i + 1, dtype=np.int32) for i in range(nq)])
    grid_spec = pltpu.PrefetchScalarGridSpec(
        num_scalar_prefetch=2,
        grid=(it.shape[0],),
        in_specs=[
            pl.BlockSpec((tb, MLA_HEADS * QK_PAD), lambda t, it, jt: (it[t], 0)),
            pl.BlockSpec((MLA_HEADS * QK_PAD, tb), lambda t, it, jt: (0, jt[t])),
            pl.BlockSpec((tb, MLA_HEADS * V_PAD), lambda t, it, jt: (jt[t], 0)),
            pl.BlockSpec((1, D_MLA), lambda t, it, jt: (0, 0)),
        ],
        out_specs=pl.BlockSpec((tb, D_MLA), lambda t, it, jt: (it[t], 0)),
        scratch_shapes=[
            pltpu.VMEM((2, MLA_HEADS, tb, V_PAD), F32),
            pltpu.VMEM((MLA_HEADS, tb, LANES), F32),
            pltpu.SMEM((1,), jnp.int32),
        ],
    )
    return pl.pallas_call(
        functools.partial(_attn_kernel, tb=tb),
        grid_spec=grid_spec,
        out_shape=jax.ShapeDtypeStruct((s, D_MLA), BF16),
        compiler_params=_cparams(("arbitrary",), VMEM_MB_MID),
        name="mla_attention",
    )(jnp.asarray(it), jnp.asarray(jt), q, kt, v, gain)


def _hgrn_tables():
    n = HG_ROWS
    r = np.arange(n)
    c = np.arange(n)
    same = (r[:, None] // HG_CHUNK) == (c[None, :] // HG_CHUNK)

    def rows_upto(idx):
        return (same & (c[None, :] <= idx[:, None])).astype(np.float32)

    blocks = [rows_upto(r)]
    for m in HG_MM_LEVELS:
        blocks.append(rows_upto((r // (2 * m)) * (2 * m) + m))
    mall = np.concatenate(blocks, axis=0)

    x = r[:, None] ^ c[None, :]
    lv = np.full((n, n), -1, np.int32)
    for li, m in enumerate(HG_LEVELS):
        lv = np.where(same & (r[:, None] > c[None, :]) & (x >= m) & (x < 2 * m), li, lv)
    lv = np.where(r[:, None] == c[None, :], len(HG_LEVELS), lv)
    return jnp.asarray(mall, BF16), jnp.asarray(lv, jnp.int32)


def _hgrn_kernel(q_ref, f_ref, i_ref, g_ref, lb_ref, gn_ref, mall_ref, lv_ref, o_ref, st_ref):
    t = pl.program_id(1)
    n = HG_ROWS

    @pl.when(t == 0)
    def _():
        st_ref[...] = jnp.zeros(st_ref.shape, F32)

    rowid = lax.broadcasted_iota(jnp.int32, (n, 1), 0)
    lv = lv_ref[...]

    def head(hh):
        cols = slice(hh * HG_DK, (hh + 1) * HG_DK)
        q_in = q_ref[:, cols].astype(F32)
        qs = q_in * _sigmoid(q_in)
        lb = lb_ref[:, cols]
        f = lb + (1.0 - lb) * _sigmoid(f_ref[:, cols].astype(F32))
        kk = 1.0 - f
        logf = jnp.log(f)
        iv = i_ref[:, cols]

        l1 = logf.astype(BF16)
        l2 = (logf - l1.astype(F32)).astype(BF16)
        parts = jnp.dot(mall_ref[...], jnp.concatenate([l1, l2], axis=1), preferred_element_type=F32)
        yield
        bc = parts[:, :HG_DK] + parts[:, HG_DK:]
        b = bc[:n]

        def anchor(period, row):
            b3 = b.reshape(n // period, period, HG_DK)
            return jnp.broadcast_to(b3[:, row:row + 1, :], b3.shape).reshape(n, HG_DK)

        a = jnp.zeros((n, n), F32)
        for li, m in enumerate(HG_LEVELS):
            if m in HG_MM_LEVELS:
                k = 1 + HG_MM_LEVELS.index(m)
                c = bc[k * n:(k + 1) * n]
            else:
                c = anchor(2 * m, m)
            e = jnp.exp(-jnp.abs(b - c))
            upper = (rowid & (2 * m - 1)) >= m
            x = (jnp.where(upper, qs, kk) * e).astype(BF16)
            p = lax.dot_general(x, x, (((1,), (1,)), ((), ())), preferred_element_type=F32)
            yield
            a = jnp.where(lv == li, p, a)
        a = jnp.where(lv == len(HG_LEVELS), jnp.sum(qs * kk, axis=-1, keepdims=True), a)
        o = jnp.dot(a.astype(BF16), iv, preferred_element_type=F32)
        yield

        blast = anchor(HG_CHUNK, HG_CHUNK - 1)
        qd = (qs * jnp.exp(b)).astype(BF16)
        kd = kk * jnp.exp(blast - b)
        ivt = iv.astype(F32).T.astype(BF16)
        kd_by_chunk = jnp.concatenate(
            [jnp.where((rowid >= lo) & (rowid < lo + HG_CHUNK), kd, 0.0) for lo in range(0, n, HG_CHUNK)],
            axis=1).astype(BF16)
        upd_all = jnp.dot(ivt, kd_by_chunk, preferred_element_type=F32)
        yield
        st = st_ref[hh]
        outs = []
        for ci in range(n // HG_CHUNK):
            lo = ci * HG_CHUNK
            inter = lax.dot_general(qd[lo:lo + HG_CHUNK], st.astype(BF16), (((1,), (1,)), ((), ())),
                                    preferred_element_type=F32)
            yield
            outs.append(o[lo:lo + HG_CHUNK] + inter)
            st = jnp.exp(blast[lo:lo + 1]) * st + upd_all[:, ci * HG_DK:(ci + 1) * HG_DK]
        st_ref[hh] = st
        o = jnp.concatenate(outs, axis=0)

        g_in = g_ref[:, cols].astype(F32)
        o_ref[:, cols] = (_rms(o, gn_ref[:, cols]) * (g_in * _sigmoid(g_in))).astype(o_ref.dtype)

    progs = [head(hh) for hh in range(HG_NH)]
    while progs:
        progs = [pr for pr in progs if next(pr, "done") != "done"]


def _hgrn(hg4, lb, gn, mall, lv):
    s = hg4.shape[0]
    groups = HG_HEADS // HG_NH
    width = HG_NH * HG_DK

    def col(c):
        return lambda hh, t: (t, c * groups + hh)

    head = lambda hh, t: (0, hh)
    const = lambda hh, t: (0, 0)
    return pl.pallas_call(
        _hgrn_kernel,
        grid=(groups, s // HG_ROWS),
        in_specs=[
            pl.BlockSpec((HG_ROWS, width), col(0)),
            pl.BlockSpec((HG_ROWS, width), col(1)),
            pl.BlockSpec((HG_ROWS, width), col(2)),
            pl.BlockSpec((HG_ROWS, width), col(3)),
            pl.BlockSpec((1, width), head),
            pl.BlockSpec((1, width), head),
            pl.BlockSpec(mall.shape, const),
            pl.BlockSpec(lv.shape, const),
        ],
        out_specs=pl.BlockSpec((HG_ROWS, width), lambda hh, t: (t, hh)),
        out_shape=jax.ShapeDtypeStruct((s, D_HG), BF16),
        scratch_shapes=[pltpu.VMEM((HG_NH, HG_DV, HG_DK), F32)],
        compiler_params=_cparams(("parallel", "arbitrary"), VMEM_MB_HGRN),
        name="hgrn2",
    )(hg4, hg4, hg4, hg4, lb, gn, mall, lv)


def _out_route_kernel(x_ref, om_ref, oh_ref, wa_ref, wb_ref, g_ref, wr_ref, br_ref,
                      h_ref, hn_ref, rt_ref, cnt_ref):
    @pl.when(pl.program_id(0) == 0)
    def _():
        cnt_ref[...] = jnp.zeros(cnt_ref.shape, F32)

    tm = x_ref.shape[0]
    sub = tm // OUT_ROUTE_PARTS
    progs = [_out_route_part(pl.ds(k * sub, sub), x_ref, om_ref, oh_ref, wa_ref, wb_ref, g_ref, wr_ref, br_ref,
                             h_ref, hn_ref, rt_ref, cnt_ref) for k in range(OUT_ROUTE_PARTS)]
    while progs:
        progs = [pr for pr in progs if next(pr, "done") != "done"]


def _out_route_part(rows, x_ref, om_ref, oh_ref, wa_ref, wb_ref, g_ref, wr_ref, br_ref, h_ref, hn_ref, rt_ref,
                    cnt_ref):
    h1 = (x_ref[rows, :]
          + jnp.dot(om_ref[rows, :], wa_ref[...], preferred_element_type=F32)
          + jnp.dot(oh_ref[rows, :], wb_ref[...], preferred_element_type=F32))
    yield
    h_ref[rows, :] = h1
    hn = _rms(h1, g_ref[...])
    hn_ref[rows, :] = _pack_halves(hn)
    hn_hi = hn.astype(BF16)
    hn_lo = (hn - hn_hi.astype(F32)).astype(BF16)
    hh = jnp.dot(hn_hi, wr_ref[...], preferred_element_type=F32)
    lh = jnp.dot(hn_lo, wr_ref[:, :LANES], preferred_element_type=F32)
    yield
    logits = hh[:, :LANES] + (hh[:, LANES:] + lh) + br_ref[...]

    lane = lax.broadcasted_iota(jnp.int32, logits.shape, 1)
    lanef = lane.astype(F32)
    ninf = -jnp.inf
    big = float(LANES)

    is_g = lane < N_GROUPS
    gl = jnp.where(is_g, logits, ninf)
    gmax = jnp.max(gl, axis=-1, keepdims=True)
    gsum = jnp.sum(jnp.where(is_g, jnp.exp(gl - gmax), 0.0), axis=-1, keepdims=True)
    g_w = 1.0 / gsum
    g_idx = jnp.min(jnp.where(gl == gmax, lanef, big), axis=-1, keepdims=True)

    e_lane = lane - N_GROUPS
    in_grp = (e_lane >= 0) & (e_lane < N_EXPERTS) & ((e_lane >> EPG_LOG2).astype(F32) == g_idx)
    el = jnp.where(in_grp, logits, ninf)
    emax = jnp.max(el, axis=-1, keepdims=True)
    esum = jnp.sum(jnp.where(in_grp, jnp.exp(el - emax), 0.0), axis=-1, keepdims=True)
    i1 = jnp.min(jnp.where(el == emax, lanef, big), axis=-1, keepdims=True)
    el2 = jnp.where(lanef == i1, ninf, el)
    emax2 = jnp.max(el2, axis=-1, keepdims=True)
    i2 = jnp.min(jnp.where(el2 == emax2, lanef, big), axis=-1, keepdims=True)
    p1 = 1.0 / esum
    p2 = jnp.exp(emax2 - emax) / esum
    w1 = g_w * p1 / (p1 + p2)
    w2 = g_w * p2 / (p1 + p2)

    rt = jnp.where(lane == 0, i1 - N_GROUPS,
                   jnp.where(lane == 1, i2 - N_GROUPS,
                             jnp.where(lane == 2, w1, jnp.where(lane == 3, w2, 0.0))))
    rt_ref[rows, :] = rt
    chosen = (lanef == i1 - N_GROUPS).astype(F32) + (lanef == i2 - N_GROUPS).astype(F32)
    cnt_ref[...] += jnp.sum(chosen, axis=0, keepdims=True)


def _out_route(x, o_mla, o_hg, wa, wb, gain, wr, br, tm):
    s, d = x.shape
    row = lambda i: (i, 0)
    full = lambda i: (0, 0)
    return pl.pallas_call(
        _out_route_kernel,
        grid=(s // tm,),
        in_specs=[
            pl.BlockSpec((tm, d), row),
            pl.BlockSpec((tm, D_MLA), row),
            pl.BlockSpec((tm, D_HG), row),
            pl.BlockSpec(wa.shape, full, pipeline_mode=pl.Buffered(1)),
            pl.BlockSpec(wb.shape, full, pipeline_mode=pl.Buffered(1)),
            pl.BlockSpec((1, d), full),
            pl.BlockSpec(wr.shape, full, pipeline_mode=pl.Buffered(1)),
            pl.BlockSpec((1, LANES), full),
        ],
        out_specs=[
            pl.BlockSpec((tm, d), row),
            pl.BlockSpec((tm, d // 2), row),
            pl.BlockSpec((tm, LANES), row),
            pl.BlockSpec((8, LANES), full),
        ],
        out_shape=[
            jax.ShapeDtypeStruct((s, d), F32),
            jax.ShapeDtypeStruct((s, d // 2), jnp.uint32),
            jax.ShapeDtypeStruct((s, LANES), F32),
            jax.ShapeDtypeStruct((8, LANES), F32),
        ],
        compiler_params=_cparams(("arbitrary",), VMEM_MB_LARGE),
        name="out_proj_route",
    )(x, o_mla, o_hg, wa, wb, gain, wr, br)


def _moe_kernel(be_ref, nv_ref, dst_ref, nu_ref, hn_hbm, wg_hbm, wu_hbm, wd_hbm, y_hbm,
                xbuf, xsem, obuf, osem, wgb, wub, wdb, wsem, wslot, *, n_tok):
    i = pl.program_id(0)
    nused = nu_ref[0]
    e = be_ref[i]
    active = i < nused
    first = (i == 0) | (e != be_ref[jnp.maximum(i - 1, 0)])

    def chunks(blk, fn):
        nv = nv_ref[blk]
        for lo in range(0, MOE_BLOCK, ROW_CHUNK):
            pl.when(lo < nv)(functools.partial(fn, lo))

    def gather_start(blk, sl):
        def issue(lo):
            for r in range(lo, lo + ROW_CHUNK):
                tok = dst_ref[blk * MOE_BLOCK + r] & (n_tok - 1)
                pltpu.make_async_copy(hn_hbm.at[pl.ds(tok, 1), :], xbuf.at[sl, pl.ds(r, 1), :],
                                      xsem.at[sl]).start(priority=ROW_DMA_PRIORITY)
        chunks(blk, issue)

    def gather_wait(blk, sl):
        chunks(blk, lambda lo: pltpu.make_async_copy(
            hn_hbm.at[pl.ds(0, ROW_CHUNK), :], xbuf.at[sl, pl.ds(lo, ROW_CHUNK), :], xsem.at[sl]).wait())

    def scatter_start(blk, sl):
        def issue(lo):
            for r in range(lo, lo + ROW_CHUNK):
                dst = dst_ref[blk * MOE_BLOCK + r]
                pltpu.make_async_copy(obuf.at[sl, pl.ds(r, 1), :], y_hbm.at[pl.ds(dst, 1), :],
                                      osem.at[sl]).start(priority=ROW_DMA_PRIORITY)
        chunks(blk, issue)

    def scatter_wait(blk, sl):
        chunks(blk, lambda lo: pltpu.make_async_copy(
            obuf.at[sl, pl.ds(lo, ROW_CHUNK), :], y_hbm.at[pl.ds(0, ROW_CHUNK), :], osem.at[sl]).wait())

    def spare_copy(sl):
        return pltpu.make_async_copy(obuf.at[sl], y_hbm.at[pl.ds(TOP_K * n_tok + sl * MOE_BLOCK, MOE_BLOCK), :],
                                     osem.at[sl])

    def weight_copies(ex, sl):
        return (pltpu.make_async_copy(wg_hbm.at[ex], wgb.at[sl], wsem.at[sl, 0]),
                pltpu.make_async_copy(wu_hbm.at[ex], wub.at[sl], wsem.at[sl, 1]),
                pltpu.make_async_copy(wd_hbm.at[ex], wdb.at[sl], wsem.at[sl, 2]))

    @pl.when((i == 0) & active)
    def _():
        wslot[0] = 1
        for cp in weight_copies(e, 0):
            cp.start(priority=1 - ROW_DMA_PRIORITY)
        xbuf[...] = jnp.zeros(xbuf.shape, jnp.uint32)
        gather_start(0, 0)
        obuf[...] = jnp.zeros(obuf.shape, jnp.uint32)
        for sl in (0, 1):
            spare_copy(sl).start()
        for sl in (0, 1):
            spare_copy(sl).wait()

    @pl.when(active & first)
    def _():
        sl = 1 - wslot[0]
        wslot[0] = sl
        nxt = lax.while_loop(lambda j: (j < nused) & (be_ref[jnp.minimum(j, nused - 1)] == e),
                             lambda j: j + 1, i + 1)

        @pl.when(nxt < nused)
        def _():
            for cp in weight_copies(be_ref[nxt], 1 - sl):
                cp.start(priority=1 - ROW_DMA_PRIORITY)

        for cp in weight_copies(e, sl):
            cp.wait()

    def block(xs):
        sl = wslot[0]

        @pl.when(i >= 2)
        def _():
            scatter_wait(i - 2, xs)

        gather_wait(i, xs)

        @pl.when(i + 1 < nused)
        def _():
            gather_start(i + 1, 1 - xs)

        x = _unpack_halves(xbuf[xs]).astype(BF16)
        g = jnp.dot(x, wgb[sl].astype(BF16), preferred_element_type=F32)
        u = jnp.dot(x, wub[sl].astype(BF16), preferred_element_type=F32)
        hmid = (g * _sigmoid(g) * u).astype(BF16)
        obuf[xs] = _pack_halves(jnp.dot(hmid, wdb[sl].astype(BF16), preferred_element_type=F32))
        scatter_start(i, xs)

        @pl.when(i == nused - 1)
        def _():
            @pl.when(i >= 1)
            def _():
                scatter_wait(i - 1, 1 - xs)

            scatter_wait(i, xs)

    for s_ in (0, 1):
        pl.when(active & (i % 2 == s_))(functools.partial(block, s_))


def _moe(block_e, nvalid, slot_dst, nused, hn, wg, wu, wd):
    nb = block_e.shape[0]
    n_tok, dw = hn.shape
    d = 2 * dw
    assert n_tok & (n_tok - 1) == 0
    grid_spec = pltpu.PrefetchScalarGridSpec(
        num_scalar_prefetch=4,
        grid=(nb,),
        in_specs=[pl.BlockSpec(memory_space=pl.ANY)] * 4,
        out_specs=pl.BlockSpec(memory_space=pl.ANY),
        scratch_shapes=[
            pltpu.VMEM((2, MOE_BLOCK, dw), jnp.uint32),
            pltpu.SemaphoreType.DMA((2,)),
            pltpu.VMEM((2, MOE_BLOCK, dw), jnp.uint32),
            pltpu.SemaphoreType.DMA((2,)),
            pltpu.VMEM((2, d, D_EXPERT), F32),
            pltpu.VMEM((2, d, D_EXPERT), F32),
            pltpu.VMEM((2, D_EXPERT, d), F32),
            pltpu.SemaphoreType.DMA((2, 3)),
            pltpu.SMEM((1,), jnp.int32),
        ],
    )
    return pl.pallas_call(
        functools.partial(_moe_kernel, n_tok=n_tok),
        grid_spec=grid_spec,
        out_shape=jax.ShapeDtypeStruct((TOP_K * n_tok + 2 * MOE_BLOCK, dw), jnp.uint32),
        compiler_params=_cparams(("arbitrary",), VMEM_MB_LARGE),
        name="moe_experts",
    )(block_e, nvalid, slot_dst, nused, hn, wg, wu, wd)


def _ple_kernel(h_ref, rt_ref, y0_ref, y1_ref, p_ref, wg_ref, bg_ref, wp_ref, gp_ref, gf_ref, o_ref):
    sub = h_ref.shape[0] // PLE_PARTS

    def part(rows):
        pe = jnp.dot(p_ref[rows, :].astype(BF16), wp_ref[...], preferred_element_type=F32)
        rt = rt_ref[rows, :]
        h2 = h_ref[rows, :] + (rt[:, TOP_K:TOP_K + 1] * _unpack_halves(y0_ref[rows, :])
                               + rt[:, TOP_K + 1:TOP_K + 2] * _unpack_halves(y1_ref[rows, :]))
        hn = _rms(h2, gp_ref[...]).astype(BF16)
        z = jnp.dot(hn, wg_ref[...], preferred_element_type=F32)
        yield
        h3 = h2 + _sigmoid(z + bg_ref[...]) * pe
        o_ref[rows, :] = _rms(h3, gf_ref[...])

    progs = [part(pl.ds(k * sub, sub)) for k in range(PLE_PARTS)]
    while progs:
        progs = [pr for pr in progs if next(pr, "done") != "done"]


def _ple_final(h1, rt, y, p, wg, bg, wp, gp, gf, tm):
    s, d = h1.shape
    nblk = s // tm
    row = lambda i: (i, 0)
    full = lambda i: (0, 0)
    return pl.pallas_call(
        _ple_kernel,
        grid=(nblk,),
        in_specs=[
            pl.BlockSpec((tm, d), row),
            pl.BlockSpec((tm, LANES), row),
            pl.BlockSpec((tm, d // 2), row),
            pl.BlockSpec((tm, d // 2), lambda i: (i + nblk, 0)),
            pl.BlockSpec((tm, PLE_DIM), row),
            pl.BlockSpec(wg.shape, full, pipeline_mode=pl.Buffered(1)),
            pl.BlockSpec((1, d), full),
            pl.BlockSpec(wp.shape, full, pipeline_mode=pl.Buffered(1)),
            pl.BlockSpec((1, d), full),
            pl.BlockSpec((1, d), full),
        ],
        out_specs=pl.BlockSpec((tm, d), row),
        out_shape=jax.ShapeDtypeStruct((s, d), F32),
        compiler_params=_cparams(("parallel",), VMEM_MB_LARGE),
        name="ple_final",
    )(h1, rt, y, y, p, wg, bg, wp, gp, gf)


def _slots_kernel(rt_ref, cnt_ref, tri_ref, tab_ref, carry_ref, base_ref, acc_ref, *, n_tok):
    b = pl.program_id(0)
    tb = rt_ref.shape[0]
    n_rows = tab_ref.shape[0]
    rt = rt_ref[...]
    lane = lax.broadcasted_iota(jnp.int32, rt.shape, 1)
    lanef = lane.astype(F32)
    oh = [(lanef == rt[:, kk:kk + 1]).astype(F32) for kk in range(TOP_K)]
    both = oh[0] + oh[1]

    @pl.when(b == 0)
    def _():
        blocks = jnp.floor((cnt_ref[...] + (MOE_BLOCK - 0.5)) / MOE_BLOCK)
        r = lax.broadcasted_iota(jnp.int32, (LANES, LANES), 0)
        c = lax.broadcasted_iota(jnp.int32, (LANES, LANES), 1)
        before = (r < c).astype(F32)
        base_ref[...] = jnp.dot(blocks * MOE_BLOCK, before, preferred_element_type=F32,
                                precision=lax.Precision.HIGHEST)
        carry_ref[...] = jnp.zeros(carry_ref.shape, F32)
        acc_ref[...] = jnp.zeros(acc_ref.shape, F32)

    earlier = jnp.dot(tri_ref[...], both.astype(BF16), preferred_element_type=F32)
    row = earlier + (base_ref[0:1, :] + carry_ref[0:1, :])
    tok = (b * tb + lax.broadcasted_iota(jnp.int32, (tb, 1), 0)).astype(F32)
    rows_f = lax.broadcasted_iota(jnp.int32, (tb, n_rows), 1).astype(F32)
    for kk in range(TOP_K):
        slot = jnp.sum(oh[kk] * row, axis=-1, keepdims=True)
        srow = jnp.floor(slot * (1.0 / LANES))
        scol = slot - LANES * srow
        val = kk * n_tok + tok
        vhi = jnp.floor(val * (1.0 / LANES))
        vlo = val - LANES * vhi
        at_row = (rows_f == srow).astype(BF16)
        at_col = lanef == scol
        x = jnp.concatenate([jnp.where(at_col, vhi, 0.0), jnp.where(at_col, vlo, 0.0),
                             at_col.astype(F32)], axis=1).astype(BF16)
        acc_ref[...] += lax.dot_general(at_row, x, (((0,), (0,)), ((), ())), preferred_element_type=F32)
    carry_ref[...] = carry_ref[...] + jnp.sum(both, axis=0, keepdims=True)

    @pl.when(b == pl.num_programs(0) - 1)
    def _():
        acc = acc_ref[...]
        sidx = (lax.broadcasted_iota(jnp.int32, (n_rows, LANES), 0) * LANES
                + lax.broadcasted_iota(jnp.int32, (n_rows, LANES), 1)).astype(F32)
        blk = jnp.floor((sidx + 0.5) / MOE_BLOCK)
        spare = TOP_K * n_tok + (blk - 2.0 * jnp.floor(blk * 0.5)) * MOE_BLOCK + (sidx - MOE_BLOCK * blk)
        filled = acc[:, :LANES] * LANES + acc[:, LANES:2 * LANES]
        tab_ref[...] = jnp.where(acc[:, 2 * LANES:] > 0.5, filled, spare).astype(jnp.int32)


def _slots(rt, cnt, tb, n_slots):
    s = rt.shape[0]
    n_rows = -(-n_slots // (LANES * LANES)) * LANES
    tri = (np.arange(tb)[:, None] > np.arange(tb)[None, :]).astype(np.float32)
    return pl.pallas_call(
        functools.partial(_slots_kernel, n_tok=s),
        grid=(s // tb,),
        in_specs=[
            pl.BlockSpec((tb, LANES), lambda b: (b, 0)),
            pl.BlockSpec((8, LANES), lambda b: (0, 0)),
            pl.BlockSpec((tb, tb), lambda b: (0, 0)),
        ],
        out_specs=pl.BlockSpec((n_rows, LANES), lambda b: (0, 0)),
        out_shape=jax.ShapeDtypeStruct((n_rows, LANES), jnp.int32),
        scratch_shapes=[pltpu.VMEM((8, LANES), F32), pltpu.VMEM((8, LANES), F32),
                        pltpu.VMEM((n_rows, 3 * LANES), F32)],
        compiler_params=_cparams(("arbitrary",)),
        name="moe_slots",
    )(rt, cnt, jnp.asarray(tri, BF16))


def _dispatch(rt, cnt, n_tok):
    a = n_tok * TOP_K
    nb = -(-a // MOE_BLOCK) + N_EXPERTS
    tab = _slots(rt, cnt, SLOTS_TM, nb * MOE_BLOCK)
    slot_dst = tab.reshape(-1)[:nb * MOE_BLOCK]
    counts = cnt[0, :N_EXPERTS].astype(jnp.int32)
    nblk = (counts + MOE_BLOCK - 1) // MOE_BLOCK
    bends = jnp.cumsum(nblk)
    bidx = jnp.arange(nb, dtype=jnp.int32)
    block_e = jnp.minimum(jnp.sum((bends[None, :] <= bidx[:, None]).astype(jnp.int32), axis=1), N_EXPERTS - 1)
    nused = bends[-1].astype(jnp.int32)
    nvalid = jnp.clip(counts[block_e] - (bidx - (bends - nblk)[block_e]) * MOE_BLOCK, 0, MOE_BLOCK)
    nvalid = jnp.where(bidx < nused, nvalid, 0).astype(jnp.int32)
    return block_e, nvalid, slot_dst, nused.reshape(1)


def kernel(x, p, positions, attn_norm, w_in, q_norm, w_uq, kv_norm, w_ukv, mla_norm, hg_lb_logits, hg_norm, w_out, ffn_norm, w_router_group, b_router_group, w_router_expert, b_router_expert, w_exp_gate, w_exp_up, w_exp_down, ple_norm, w_ple_gate, b_ple_gate, w_ple_proj, final_norm):
    bsz, s, d = x.shape
    assert bsz == 1 and w_in.shape[0] == 1
    xt = x[0]

    inv_freq = 1.0 / (ROPE_THETA ** (jnp.arange(0, QK_ROPE, 2, dtype=F32) / QK_ROPE))
    ang = positions[0].astype(F32)[:, None] * inv_freq
    cos, sin = jnp.cos(ang), jnp.sin(ang)
    zpad = jnp.zeros((s, LANES - QK_ROPE), F32)
    cc = jnp.concatenate([cos, cos, zpad], axis=1)
    ss = jnp.concatenate([-sin, sin, zpad], axis=1)

    lb = jnp.cumsum(jax.nn.softmax(hg_lb_logits.astype(F32), axis=0), axis=0)[0][None, :]

    wi = w_in[0]
    kr0 = Q_LORA + KV_LORA
    half = QK_ROPE // 2
    w_lat = jnp.concatenate(
        [wi[:, :kr0 + QK_ROPE], wi[:, kr0 + half:kr0 + QK_ROPE], wi[:, kr0:kr0 + half]], axis=1).astype(BF16)
    w_hg = wi[:, kr0 + QK_ROPE:].astype(BF16)
    wq3 = w_uq[0].reshape(Q_LORA, MLA_HEADS, QK_HEAD)
    wq_pad = jnp.concatenate(
        [wq3, wq3[:, :, QK_NOPE + half:], wq3[:, :, QK_NOPE:QK_NOPE + half]], axis=2
    ).reshape(Q_LORA, MLA_HEADS * QK_PAD).astype(BF16)
    wkv3 = w_ukv[0].reshape(KV_LORA, MLA_HEADS, QK_NOPE + V_HEAD)
    wknt = wkv3[:, :, :QK_NOPE].reshape(KV_LORA, MLA_HEADS * QK_NOPE).T.astype(BF16)
    wv = wkv3[:, :, QK_NOPE:].reshape(KV_LORA, D_MLA).astype(BF16)
    wo = w_out[0].astype(BF16)
    wr = jnp.concatenate(
        [w_router_group[0], w_router_expert[0], jnp.zeros((d, LANES - N_GROUPS - N_EXPERTS), F32)], axis=1)
    wr_hi = wr.astype(BF16)
    wr = jnp.concatenate([wr_hi, (wr - wr_hi.astype(F32)).astype(BF16)], axis=1)
    br = jnp.concatenate(
        [b_router_group[0], b_router_expert[0], jnp.zeros((LANES - N_GROUPS - N_EXPERTS,), F32)])[None, :]

    hg4, lat = _in_proj(xt, attn_norm, w_hg, w_lat, IN_PROJ_TM, IN_PROJ_TN)
    q, kt, v = _mla_up(lat, q_norm, kv_norm, wq_pad, wknt, wv, cc, ss, MLA_UP_TM)
    o_mla = _attention(q, kt, v, mla_norm, ATTN_BLOCK)
    mall, lv = _hgrn_tables()
    o_hg = _hgrn(hg4, lb, hg_norm[0].reshape(1, D_HG), mall, lv)

    h1, hn, rt, cnt = _out_route(xt, o_mla, o_hg, wo[:D_MLA], wo[D_MLA:], ffn_norm, wr, br, OUT_ROUTE_TM)
    block_e, nvalid, slot_dst, nused = _dispatch(rt, cnt, s)
    y = _moe(block_e, nvalid, slot_dst, nused, hn, w_exp_gate[0], w_exp_up[0], w_exp_down[0])

    out = _ple_final(h1, rt, y, p[0, 0], w_ple_gate[0].astype(BF16), b_ple_gate,
                     w_ple_proj[0].astype(BF16), ple_norm, final_norm[None, :], PLE_TM)
    return out[None]
```
